```python
import jax, jax.numpy as jnp
from jax import lax
import numpy as np

D_MODEL = 2048
BATCH = 8
SEQ = 8192
DEPTH = 4

CHUNK = 64
N_META = 16
Q_BLOCK = 128
MLA_HEADS = 8
MLA_Q_LORA = 512
MLA_KV_LORA = 512
MLA_NOPE = 128
MLA_ROPE = 64
MLA_V = 128
ROPE_THETA = 10000.0
FOX_HEADS = 8
FOX_HD = 128
FOX_W = FOX_HEADS * FOX_HD
FORGET_BIAS = 3.0
MIX_WIDTH = MLA_HEADS * MLA_V + FOX_W
D_FF = 5632
CONV_K = 3
EPS = 1e-6
NEG = -1e30
IN_SPLIT_SIZES = (MLA_Q_LORA, MLA_KV_LORA, MLA_ROPE, FOX_W, FOX_W, FOX_W, FOX_W, FOX_HEADS)
IN_COLS = sum(IN_SPLIT_SIZES)

kernel_name = "hybrid_mla_fox_convffn_trunk"


def rms_norm(x, g):
    xf = x.astype(jnp.float32)
    y = xf * lax.rsqrt(jnp.mean(xf * xf, axis=-1, keepdims=True) + EPS)
    return (y * g.astype(jnp.float32)).astype(x.dtype)


def apply_rope(x, cos, sin):
    xf = x.astype(jnp.float32)
    half = xf.shape[-1] // 2
    x1, x2 = xf[..., :half], xf[..., half:]
    out = jnp.concatenate([x1 * cos - x2 * sin, x2 * cos + x1 * sin], axis=-1)
    return out.astype(x.dtype)


def mla_mixer(c_q, c_kv, k_rope, g_q, g_kv, w_q_up, w_kv_up, cos, sin, chunk_id):
    B, L, _ = c_q.shape
    q = (rms_norm(c_q, g_q) @ w_q_up).reshape(B, L, MLA_HEADS, MLA_NOPE + MLA_ROPE)
    q_nope = q[..., :MLA_NOPE]
    q_rope = apply_rope(q[..., MLA_NOPE:], cos[None, :, None], sin[None, :, None])
    kv = (rms_norm(c_kv, g_kv) @ w_kv_up).reshape(B, L, MLA_HEADS, MLA_NOPE + MLA_V)
    k_nope, v = kv[..., :MLA_NOPE], kv[..., MLA_NOPE:]
    k_r = apply_rope(k_rope, cos[None], sin[None])
    scale = (MLA_NOPE + MLA_ROPE) ** -0.5

    def one_block(start):
        qn = lax.dynamic_slice_in_dim(q_nope, start, Q_BLOCK, axis=1)
        qr = lax.dynamic_slice_in_dim(q_rope, start, Q_BLOCK, axis=1)
        cq = lax.dynamic_slice_in_dim(chunk_id, start, Q_BLOCK)
        s = (jnp.einsum('bqhd,bkhd->bhqk', qn, k_nope)
             + jnp.einsum('bqhr,bkr->bhqk', qr, k_r)).astype(jnp.float32) * scale
        mask = chunk_id[None, :] <= cq[:, None]
        s = jnp.where(mask, s, NEG)
        p = jax.nn.softmax(s, axis=-1).astype(v.dtype)
        return jnp.einsum('bhqk,bkhd->bqhd', p, v)

    starts = jnp.arange(L // Q_BLOCK, dtype=jnp.int32) * Q_BLOCK
    out = lax.map(one_block, starts)
    return jnp.moveaxis(out, 0, 1).reshape(B, L, MLA_HEADS * MLA_V)


def fox_mixer(q, k, v, gate, f_logit, b_f, g_q, g_k, pos):
    B, L, _ = q.shape
    q = rms_norm(q.reshape(B, L, FOX_HEADS, FOX_HD), g_q)
    k = rms_norm(k.reshape(B, L, FOX_HEADS, FOX_HD), g_k)
    v = v.reshape(B, L, FOX_HEADS, FOX_HD)
    log_f = jax.nn.log_sigmoid(f_logit.astype(jnp.float32) + b_f.astype(jnp.float32))
    c = jnp.cumsum(log_f, axis=1).transpose(0, 2, 1)
    scale = FOX_HD ** -0.5

    def one_block(start):
        qb = lax.dynamic_slice_in_dim(q, start, Q_BLOCK, axis=1)
        cq = lax.dynamic_slice_in_dim(c, start, Q_BLOCK, axis=2)
        pq = lax.dynamic_slice_in_dim(pos, start, Q_BLOCK)
        s = jnp.einsum('bqhd,bkhd->bhqk', qb, k).astype(jnp.float32) * scale
        s = s + cq[..., :, None] - c[..., None, :]
        mask = pos[None, :] <= pq[:, None]
        s = jnp.where(mask, s, NEG)
        p = jax.nn.softmax(s, axis=-1).astype(v.dtype)
        return jnp.einsum('bhqk,bkhd->bqhd', p, v)

    starts = jnp.arange(L // Q_BLOCK, dtype=jnp.int32) * Q_BLOCK
    out = jnp.moveaxis(lax.map(one_block, starts), 0, 1).reshape(B, L, FOX_W)
    return out * jax.nn.sigmoid(gate)


def conv_ffn(x, w_up, w_conv, b_conv, w_down):
    L = x.shape[1]
    h = x @ w_up
    hp = jnp.pad(h, ((0, 0), (CONV_K - 1, 0), (0, 0)))
    h = b_conv + sum(w_conv[j] * hp[:, j:j + L] for j in range(CONV_K))
    gate, up = jnp.split(h, 2, axis=-1)
    return (jax.nn.gelu(gate, approximate=True) * up) @ w_down


def _fwd_setup_inputs(seed: int = 0) -> dict:
    key = jax.random.key(seed)
    ks = jax.random.split(key, 20)
    f32 = jnp.float32
    nrm = lambda k, shp, s: jax.random.normal(k, shp, f32) * s
    gain = lambda k, shp: 1.0 + 0.05 * jax.random.normal(k, shp, f32)
    return {
        "x": jax.random.normal(ks[0], (BATCH, SEQ, D_MODEL), f32),
        "meta_tokens": nrm(ks[1], (N_META, D_MODEL), 1.0),
        "ln_mix_pre": gain(ks[2], (DEPTH, D_MODEL)),
        "w_in": nrm(ks[3], (DEPTH, D_MODEL, IN_COLS), D_MODEL ** -0.5),
        "b_forget": FORGET_BIAS + 0.5 * jax.random.normal(ks[4], (DEPTH, FOX_HEADS), f32),
        "g_q_latent": gain(ks[5], (DEPTH, MLA_Q_LORA)),
        "g_kv_latent": gain(ks[6], (DEPTH, MLA_KV_LORA)),
        "w_q_up": nrm(ks[7], (DEPTH, MLA_Q_LORA, MLA_HEADS * (MLA_NOPE + MLA_ROPE)), MLA_Q_LORA ** -0.5),
        "w_kv_up": nrm(ks[8], (DEPTH, MLA_KV_LORA, MLA_HEADS * (MLA_NOPE + MLA_V)), MLA_KV_LORA ** -0.5),
        "g_fox_q": gain(ks[9], (DEPTH, FOX_HD)),
        "g_fox_k": gain(ks[10], (DEPTH, FOX_HD)),
        "w_out": nrm(ks[11], (DEPTH, MIX_WIDTH, D_MODEL), MIX_WIDTH ** -0.5),
        "ln_mix_post": gain(ks[12], (DEPTH, D_MODEL)),
        "ln_ffn_pre": gain(ks[13], (DEPTH, D_MODEL)),
        "w_ffn_up": nrm(ks[14], (DEPTH, D_MODEL, 2 * D_FF), D_MODEL ** -0.5),
        "w_ffn_conv": nrm(ks[15], (DEPTH, CONV_K, 2 * D_FF), CONV_K ** -0.5),
        "b_ffn_conv": nrm(ks[16], (DEPTH, 2 * D_FF), 0.02),
        "w_ffn_down": nrm(ks[17], (DEPTH, D_FF, D_MODEL), D_FF ** -0.5),
        "ln_ffn_post": gain(ks[18], (DEPTH, D_MODEL)),
    }


def _fwd_reference(x, meta_tokens, ln_mix_pre, w_in, b_forget, g_q_latent, g_kv_latent, w_q_up, w_kv_up,
              g_fox_q, g_fox_k, w_out, ln_mix_post, ln_ffn_pre, w_ffn_up, w_ffn_conv, b_ffn_conv,
              w_ffn_down, ln_ffn_post):
    B, S, D = x.shape
    L = N_META + S
    L_pad = -(-L // Q_BLOCK) * Q_BLOCK
    h = jnp.concatenate([
        jnp.broadcast_to(meta_tokens.astype(x.dtype)[None], (B, N_META, D)),
        x,
        jnp.zeros((B, L_pad - L, D), x.dtype)], axis=1)
    pos = jnp.arange(L_pad, dtype=jnp.int32)
    chunk_id = jnp.where(pos < N_META, 0,
                         jnp.where(pos < L, 1 + (pos - N_META) // CHUNK, 2 + S // CHUNK)).astype(jnp.int32)
    half = MLA_ROPE // 2
    inv_freq = ROPE_THETA ** (-jnp.arange(half, dtype=jnp.float32) / half)
    ang = pos.astype(jnp.float32)[:, None] * inv_freq[None, :]
    cos, sin = jnp.cos(ang), jnp.sin(ang)
    split_idx = [int(v) for v in np.cumsum(IN_SPLIT_SIZES)[:-1]]

    for l in range(DEPTH):
        hn = rms_norm(h, ln_mix_pre[l])
        c_q, c_kv, k_rope, fq, fk, fv, fg, ff = jnp.split(hn @ w_in[l], split_idx, axis=-1)
        a = mla_mixer(c_q, c_kv, k_rope, g_q_latent[l], g_kv_latent[l], w_q_up[l], w_kv_up[l],
                      cos, sin, chunk_id)
        b = fox_mixer(fq, fk, fv, fg, ff, b_forget[l], g_fox_q[l], g_fox_k[l], pos)
        mix = jnp.concatenate([a, b], axis=-1) @ w_out[l]
        h = h + rms_norm(mix, ln_mix_post[l])
        f = conv_ffn(rms_norm(h, ln_ffn_pre[l]), w_ffn_up[l], w_ffn_conv[l], b_ffn_conv[l], w_ffn_down[l])
        h = h + rms_norm(f, ln_ffn_post[l])

    return h[:, N_META:N_META + S]


import jax as _jax
import jax.numpy as _jnp

TWIN_FORMAT = 'train_step'
FWD_PARAMS = ['x', 'meta_tokens', 'ln_mix_pre', 'w_in', 'b_forget', 'g_q_latent', 'g_kv_latent', 'w_q_up', 'w_kv_up', 'g_fox_q', 'g_fox_k', 'w_out', 'ln_mix_post', 'ln_ffn_pre', 'w_ffn_up', 'w_ffn_conv', 'b_ffn_conv', 'w_ffn_down', 'ln_ffn_post']
TWIN_WEIGHTS = ['meta_tokens', 'ln_mix_pre', 'w_in', 'b_forget', 'g_q_latent', 'g_kv_latent', 'w_q_up', 'w_kv_up', 'g_fox_q', 'g_fox_k', 'w_out', 'ln_mix_post', 'ln_ffn_pre', 'w_ffn_up', 'w_ffn_conv', 'b_ffn_conv', 'w_ffn_down', 'ln_ffn_post']
TWIN_DIFF_INPUT = 'x'
TWIN_INPUTS = ['x', 'meta_tokens', 'ln_mix_pre', 'w_in', 'b_forget', 'g_q_latent', 'g_kv_latent', 'w_q_up', 'w_kv_up', 'g_fox_q', 'g_fox_k', 'w_out', 'ln_mix_post', 'ln_ffn_pre', 'w_ffn_up', 'w_ffn_conv', 'b_ffn_conv', 'w_ffn_down', 'ln_ffn_post', 'loss_target', 'm_meta_tokens', 'm_ln_mix_pre', 'm_w_in', 'm_b_forget', 'm_g_q_latent', 'm_g_kv_latent', 'm_w_q_up', 'm_w_kv_up', 'm_g_fox_q', 'm_g_fox_k', 'm_w_out', 'm_ln_mix_post', 'm_ln_ffn_pre', 'm_w_ffn_up', 'm_w_ffn_conv', 'm_b_ffn_conv', 'm_w_ffn_down', 'm_ln_ffn_post', 'v_meta_tokens', 'v_ln_mix_pre', 'v_w_in', 'v_b_forget', 'v_g_q_latent', 'v_g_kv_latent', 'v_w_q_up', 'v_w_kv_up', 'v_g_fox_q', 'v_g_fox_k', 'v_w_out', 'v_ln_mix_post', 'v_ln_ffn_pre', 'v_w_ffn_up', 'v_w_ffn_conv', 'v_b_ffn_conv', 'v_w_ffn_down', 'v_ln_ffn_post']
TWIN_OUTPUTS = ['loss', 'grad_x', 'grad_meta_tokens', 'grad_ln_mix_pre', 'grad_w_in', 'grad_b_forget', 'grad_g_q_latent', 'grad_g_kv_latent', 'grad_w_q_up', 'grad_w_kv_up', 'grad_g_fox_q', 'grad_g_fox_k', 'grad_w_out', 'grad_ln_mix_post', 'grad_ln_ffn_pre', 'grad_w_ffn_up', 'grad_w_ffn_conv', 'grad_b_ffn_conv', 'grad_w_ffn_down', 'grad_ln_ffn_post', 'delta_meta_tokens', 'delta_ln_mix_pre', 'delta_w_in', 'delta_b_forget', 'delta_g_q_latent', 'delta_g_kv_latent', 'delta_w_q_up', 'delta_w_kv_up', 'delta_g_fox_q', 'delta_g_fox_k', 'delta_w_out', 'delta_ln_mix_post', 'delta_ln_ffn_pre', 'delta_w_ffn_up', 'delta_w_ffn_conv', 'delta_b_ffn_conv', 'delta_w_ffn_down', 'delta_ln_ffn_post', 'new_m_meta_tokens', 'new_m_ln_mix_pre', 'new_m_w_in', 'new_m_b_forget', 'new_m_g_q_latent', 'new_m_g_kv_latent', 'new_m_w_q_up', 'new_m_w_kv_up', 'new_m_g_fox_q', 'new_m_g_fox_k', 'new_m_w_out', 'new_m_ln_mix_post', 'new_m_ln_ffn_pre', 'new_m_w_ffn_up', 'new_m_w_ffn_conv', 'new_m_b_ffn_conv', 'new_m_w_ffn_down', 'new_m_ln_ffn_post', 'new_v_meta_tokens', 'new_v_ln_mix_pre', 'new_v_w_in', 'new_v_b_forget', 'new_v_g_q_latent', 'new_v_g_kv_latent', 'new_v_w_q_up', 'new_v_w_kv_up', 'new_v_g_fox_q', 'new_v_g_fox_k', 'new_v_w_out', 'new_v_ln_mix_post', 'new_v_ln_ffn_pre', 'new_v_w_ffn_up', 'new_v_w_ffn_conv', 'new_v_b_ffn_conv', 'new_v_w_ffn_down', 'new_v_ln_ffn_post']
TWIN_LEAF_KINDS = {'loss': 'loss', 'grad_x': 'grad_x', 'grad_meta_tokens': 'grad_w', 'grad_ln_mix_pre': 'grad_w', 'grad_w_in': 'grad_w', 'grad_b_forget': 'grad_w', 'grad_g_q_latent': 'grad_w', 'grad_g_kv_latent': 'grad_w', 'grad_w_q_up': 'grad_w', 'grad_w_kv_up': 'grad_w', 'grad_g_fox_q': 'grad_w', 'grad_g_fox_k': 'grad_w', 'grad_w_out': 'grad_w', 'grad_ln_mix_post': 'grad_w', 'grad_ln_ffn_pre': 'grad_w', 'grad_w_ffn_up': 'grad_w', 'grad_w_ffn_conv': 'grad_w', 'grad_b_ffn_conv': 'grad_w', 'grad_w_ffn_down': 'grad_w', 'grad_ln_ffn_post': 'grad_w', 'delta_meta_tokens': 'delta_w', 'delta_ln_mix_pre': 'delta_w', 'delta_w_in': 'delta_w', 'delta_b_forget': 'delta_w', 'delta_g_q_latent': 'delta_w', 'delta_g_kv_latent': 'delta_w', 'delta_w_q_up': 'delta_w', 'delta_w_kv_up': 'delta_w', 'delta_g_fox_q': 'delta_w', 'delta_g_fox_k': 'delta_w', 'delta_w_out': 'delta_w', 'delta_ln_mix_post': 'delta_w', 'delta_ln_ffn_pre': 'delta_w', 'delta_w_ffn_up': 'delta_w', 'delta_w_ffn_conv': 'delta_w', 'delta_b_ffn_conv': 'delta_w', 'delta_w_ffn_down': 'delta_w', 'delta_ln_ffn_post': 'delta_w', 'new_m_meta_tokens': 'new_m', 'new_m_ln_mix_pre': 'new_m', 'new_m_w_in': 'new_m', 'new_m_b_forget': 'new_m', 'new_m_g_q_latent': 'new_m', 'new_m_g_kv_latent': 'new_m', 'new_m_w_q_up': 'new_m', 'new_m_w_kv_up': 'new_m', 'new_m_g_fox_q': 'new_m', 'new_m_g_fox_k': 'new_m', 'new_m_w_out': 'new_m', 'new_m_ln_mix_post': 'new_m', 'new_m_ln_ffn_pre': 'new_m', 'new_m_w_ffn_up': 'new_m', 'new_m_w_ffn_conv': 'new_m', 'new_m_b_ffn_conv': 'new_m', 'new_m_w_ffn_down': 'new_m', 'new_m_ln_ffn_post': 'new_m', 'new_v_meta_tokens': 'new_v', 'new_v_ln_mix_pre': 'new_v', 'new_v_w_in': 'new_v', 'new_v_b_forget': 'new_v', 'new_v_g_q_latent': 'new_v', 'new_v_g_kv_latent': 'new_v', 'new_v_w_q_up': 'new_v', 'new_v_w_kv_up': 'new_v', 'new_v_g_fox_q': 'new_v', 'new_v_g_fox_k': 'new_v', 'new_v_w_out': 'new_v', 'new_v_ln_mix_post': 'new_v', 'new_v_ln_ffn_pre': 'new_v', 'new_v_w_ffn_up': 'new_v', 'new_v_w_ffn_conv': 'new_v', 'new_v_b_ffn_conv': 'new_v', 'new_v_w_ffn_down': 'new_v', 'new_v_ln_ffn_post': 'new_v'}


def _forward(args):
    return _fwd_reference(*[args[k] for k in FWD_PARAMS])


def _output_shape():
    def fwd():
        inp = _fwd_setup_inputs(0)
        return _fwd_reference(*[inp[k] for k in FWD_PARAMS])
    out = _jax.eval_shape(fwd)
    return out.shape, out.dtype

N_MICROBATCH = 1
ADAM_LR = 0.001
ADAM_B1 = 0.9
ADAM_B2 = 0.999
ADAM_EPS = 1e-08
ADAM_WD = 0.01
ADAM_STEP = 10
PER_EXAMPLE_BATCH_AXIS = {'x': 0, 'loss_target': 0}
SHARED_INPUTS = []
_WEIGHT_DTYPES = {'meta_tokens': _jnp.float32, 'ln_mix_pre': _jnp.float32, 'w_in': _jnp.float32, 'b_forget': _jnp.float32, 'g_q_latent': _jnp.float32, 'g_kv_latent': _jnp.float32, 'w_q_up': _jnp.float32, 'w_kv_up': _jnp.float32, 'g_fox_q': _jnp.float32, 'g_fox_k': _jnp.float32, 'w_out': _jnp.float32, 'ln_mix_post': _jnp.float32, 'ln_ffn_pre': _jnp.float32, 'w_ffn_up': _jnp.float32, 'w_ffn_conv': _jnp.float32, 'b_ffn_conv': _jnp.float32, 'w_ffn_down': _jnp.float32, 'ln_ffn_post': _jnp.float32}
MOMENT_SCALE = {'meta_tokens': 1.607035e+00, 'ln_mix_pre': 1.232987e+01, 'w_in': 7.808117e+00, 'b_forget': 1.044976e+01, 'g_q_latent': 1.783707e+00, 'g_kv_latent': 2.360321e+01, 'w_q_up': 9.102842e-01, 'w_kv_up': 1.170874e+01, 'g_fox_q': 3.035864e+00, 'g_fox_k': 2.994885e+00, 'w_out': 1.265635e+01, 'ln_mix_post': 3.165695e+01, 'ln_ffn_pre': 4.334755e+00, 'w_ffn_up': 1.857592e+00, 'w_ffn_conv': 2.151634e+00, 'b_ffn_conv': 1.000505e+01, 'w_ffn_down': 3.943374e+00, 'ln_ffn_post': 3.145655e+01}


def _to_microbatches(a, axis):
    t = _jnp.moveaxis(a, axis, 0)
    t = t.reshape((N_MICROBATCH, t.shape[0] // N_MICROBATCH) + t.shape[1:])
    return _jnp.moveaxis(t, 1, axis + 1)


def setup_inputs(seed: int = 0) -> dict:
    inp = _fwd_setup_inputs(seed)
    key = _jax.random.fold_in(_jax.random.key(seed), 7919)
    shape, _ = _output_shape()
    out = dict(inp)
    out["loss_target"] = _jax.random.normal(_jax.random.fold_in(key, 0), shape, _jnp.float32)
    for i, name in enumerate(TWIN_WEIGHTS):
        w = inp[name].astype(_jnp.float32)
        if MOMENT_SCALE is None:
            s = _jnp.sqrt(_jnp.mean(_jnp.square(w)) + 1e-30)
        else:
            s = MOMENT_SCALE[name]
        km, kv = _jax.random.split(_jax.random.fold_in(key, i + 1))
        out[name] = w
        out["m_" + name] = s * _jax.random.normal(km, w.shape, _jnp.float32)
        out["v_" + name] = (s * s) * _jax.random.uniform(kv, w.shape, _jnp.float32, 0.5, 1.5)
    if N_MICROBATCH > 1:
        for name, axis in PER_EXAMPLE_BATCH_AXIS.items():
            out[name] = _to_microbatches(out[name], axis)
    return {'x': out['x'], 'meta_tokens': out['meta_tokens'], 'ln_mix_pre': out['ln_mix_pre'], 'w_in': out['w_in'], 'b_forget': out['b_forget'], 'g_q_latent': out['g_q_latent'], 'g_kv_latent': out['g_kv_latent'], 'w_q_up': out['w_q_up'], 'w_kv_up': out['w_kv_up'], 'g_fox_q': out['g_fox_q'], 'g_fox_k': out['g_fox_k'], 'w_out': out['w_out'], 'ln_mix_post': out['ln_mix_post'], 'ln_ffn_pre': out['ln_ffn_pre'], 'w_ffn_up': out['w_ffn_up'], 'w_ffn_conv': out['w_ffn_conv'], 'b_ffn_conv': out['b_ffn_conv'], 'w_ffn_down': out['w_ffn_down'], 'ln_ffn_post': out['ln_ffn_post'], 'loss_target': out['loss_target'], 'm_meta_tokens': out['m_meta_tokens'], 'm_ln_mix_pre': out['m_ln_mix_pre'], 'm_w_in': out['m_w_in'], 'm_b_forget': out['m_b_forget'], 'm_g_q_latent': out['m_g_q_latent'], 'm_g_kv_latent': out['m_g_kv_latent'], 'm_w_q_up': out['m_w_q_up'], 'm_w_kv_up': out['m_w_kv_up'], 'm_g_fox_q': out['m_g_fox_q'], 'm_g_fox_k': out['m_g_fox_k'], 'm_w_out': out['m_w_out'], 'm_ln_mix_post': out['m_ln_mix_post'], 'm_ln_ffn_pre': out['m_ln_ffn_pre'], 'm_w_ffn_up': out['m_w_ffn_up'], 'm_w_ffn_conv': out['m_w_ffn_conv'], 'm_b_ffn_conv': out['m_b_ffn_conv'], 'm_w_ffn_down': out['m_w_ffn_down'], 'm_ln_ffn_post': out['m_ln_ffn_post'], 'v_meta_tokens': out['v_meta_tokens'], 'v_ln_mix_pre': out['v_ln_mix_pre'], 'v_w_in': out['v_w_in'], 'v_b_forget': out['v_b_forget'], 'v_g_q_latent': out['v_g_q_latent'], 'v_g_kv_latent': out['v_g_kv_latent'], 'v_w_q_up': out['v_w_q_up'], 'v_w_kv_up': out['v_w_kv_up'], 'v_g_fox_q': out['v_g_fox_q'], 'v_g_fox_k': out['v_g_fox_k'], 'v_w_out': out['v_w_out'], 'v_ln_mix_post': out['v_ln_mix_post'], 'v_ln_ffn_pre': out['v_ln_ffn_pre'], 'v_w_ffn_up': out['v_w_ffn_up'], 'v_w_ffn_conv': out['v_w_ffn_conv'], 'v_b_ffn_conv': out['v_b_ffn_conv'], 'v_w_ffn_down': out['v_w_ffn_down'], 'v_ln_ffn_post': out['v_ln_ffn_post']}


def _loss(weights, diff, rest, loss_target):
    with _jax.named_scope("forward"):
        args = {**rest, TWIN_DIFF_INPUT: diff, **{k: w.astype(_WEIGHT_DTYPES[k]) for k, w in weights.items()}}
        y = _forward(args)
    with _jax.named_scope("loss_head"):
        err = _jnp.square(y.astype(_jnp.float32) - loss_target)
        return 0.5 * _jnp.sum(_jnp.mean(err, axis=-1)) if err.ndim else 0.5 * err


def _adamw(w, g, m, v):
    m = ADAM_B1 * m + (1.0 - ADAM_B1) * g
    v = ADAM_B2 * v + (1.0 - ADAM_B2) * _jnp.square(g)
    m_hat = m / (1.0 - ADAM_B1 ** ADAM_STEP)
    v_hat = v / (1.0 - ADAM_B2 ** ADAM_STEP)
    delta = -ADAM_LR * (m_hat / (_jnp.sqrt(v_hat) + ADAM_EPS) + ADAM_WD * w)
    return delta, m, v


def reference(x, meta_tokens, ln_mix_pre, w_in, b_forget, g_q_latent, g_kv_latent, w_q_up, w_kv_up, g_fox_q, g_fox_k, w_out, ln_mix_post, ln_ffn_pre, w_ffn_up, w_ffn_conv, b_ffn_conv, w_ffn_down, ln_ffn_post, loss_target, m_meta_tokens, m_ln_mix_pre, m_w_in, m_b_forget, m_g_q_latent, m_g_kv_latent, m_w_q_up, m_w_kv_up, m_g_fox_q, m_g_fox_k, m_w_out, m_ln_mix_post, m_ln_ffn_pre, m_w_ffn_up, m_w_ffn_conv, m_b_ffn_conv, m_w_ffn_down, m_ln_ffn_post, v_meta_tokens, v_ln_mix_pre, v_w_in, v_b_forget, v_g_q_latent, v_g_kv_latent, v_w_q_up, v_w_kv_up, v_g_fox_q, v_g_fox_k, v_w_out, v_ln_mix_post, v_ln_ffn_pre, v_w_ffn_up, v_w_ffn_conv, v_b_ffn_conv, v_w_ffn_down, v_ln_ffn_post):
    given = dict(x=x, meta_tokens=meta_tokens, ln_mix_pre=ln_mix_pre, w_in=w_in, b_forget=b_forget, g_q_latent=g_q_latent, g_kv_latent=g_kv_latent, w_q_up=w_q_up, w_kv_up=w_kv_up, g_fox_q=g_fox_q, g_fox_k=g_fox_k, w_out=w_out, ln_mix_post=ln_mix_post, ln_ffn_pre=ln_ffn_pre, w_ffn_up=w_ffn_up, w_ffn_conv=w_ffn_conv, b_ffn_conv=b_ffn_conv, w_ffn_down=w_ffn_down, ln_ffn_post=ln_ffn_post, loss_target=loss_target, m_meta_tokens=m_meta_tokens, m_ln_mix_pre=m_ln_mix_pre, m_w_in=m_w_in, m_b_forget=m_b_forget, m_g_q_latent=m_g_q_latent, m_g_kv_latent=m_g_kv_latent, m_w_q_up=m_w_q_up, m_w_kv_up=m_w_kv_up, m_g_fox_q=m_g_fox_q, m_g_fox_k=m_g_fox_k, m_w_out=m_w_out, m_ln_mix_post=m_ln_mix_post, m_ln_ffn_pre=m_ln_ffn_pre, m_w_ffn_up=m_w_ffn_up, m_w_ffn_conv=m_w_ffn_conv, m_b_ffn_conv=m_b_ffn_conv, m_w_ffn_down=m_w_ffn_down, m_ln_ffn_post=m_ln_ffn_post, v_meta_tokens=v_meta_tokens, v_ln_mix_pre=v_ln_mix_pre, v_w_in=v_w_in, v_b_forget=v_b_forget, v_g_q_latent=v_g_q_latent, v_g_kv_latent=v_g_kv_latent, v_w_q_up=v_w_q_up, v_w_kv_up=v_w_kv_up, v_g_fox_q=v_g_fox_q, v_g_fox_k=v_g_fox_k, v_w_out=v_w_out, v_ln_mix_post=v_ln_mix_post, v_ln_ffn_pre=v_ln_ffn_pre, v_w_ffn_up=v_w_ffn_up, v_w_ffn_conv=v_w_ffn_conv, v_b_ffn_conv=v_b_ffn_conv, v_w_ffn_down=v_w_ffn_down, v_ln_ffn_post=v_ln_ffn_post)
    weights = {n: given[n] for n in TWIN_WEIGHTS}
    shared = {n: given[n] for n in SHARED_INPUTS}
    per_example = {n: given[n] for n in ['x']}
    grad_fn = _jax.value_and_grad(_loss, argnums=(0, 1))

    def one_microbatch(ex, loss_target):
        ex = dict(ex)
        diff = ex.pop(TWIN_DIFF_INPUT)
        return grad_fn(weights, diff, {**shared, **ex}, loss_target)

    if N_MICROBATCH == 1:
        loss, (grad_w, grad_x) = one_microbatch(per_example, given["loss_target"])
    else:
        def body(carry, xs):
            loss_sum, grad_sum = carry
            l_k, (gw_k, gx_k) = one_microbatch(xs[0], xs[1])
            with _jax.named_scope("update"):
                return (loss_sum + l_k, _jax.tree.map(_jnp.add, grad_sum, gw_k)), gx_k

        init = (_jnp.zeros((), _jnp.float32), _jax.tree.map(_jnp.zeros_like, weights))
        (loss, grad_w), grad_x = _jax.lax.scan(body, init, (per_example, given["loss_target"]))
    with _jax.named_scope("update"):
        delta_w, new_m, new_v = {}, {}, {}
        for n in TWIN_WEIGHTS:
            delta_w[n], new_m[n], new_v[n] = _adamw(weights[n], grad_w[n], given["m_" + n], given["v_" + n])
    return (loss, grad_x, *[grad_w[n] for n in TWIN_WEIGHTS], *[delta_w[n] for n in TWIN_WEIGHTS],
            *[new_m[n] for n in TWIN_WEIGHTS], *[new_v[n] for n in TWIN_WEIGHTS])
```

```python
import functools

import jax
import jax.numpy as jnp
from jax import lax
from jax.experimental import pallas as pl
from jax.experimental.pallas import tpu as pltpu

F32 = jnp.float32
BF16 = jnp.bfloat16
MESH_ID = pl.DeviceIdType.MESH

N_DEV = 8
DEPTH = 4
N_META = 16
CHUNK = 64
Q_BLOCK = 128
HEADS = 8
HEAD_DIM = 128
ROPE_DIM = 64
MLA_Q_LORA = 512
MLA_KV_LORA = 512
FOX_W = HEADS * HEAD_DIM
ROPE_THETA = 10000.0
EPS = 1e-6
NEG = -1e30
IN_COLS = 5192
IN_COLS_PADDED = 5376

ADAM_LR = 0.001
ADAM_B1 = 0.9
ADAM_B2 = 0.999
ADAM_EPS = 1e-08
ADAM_WD = 0.01
ADAM_STEP = 10

LANES = 128
SUBLANES = 8
FLAT_COLS = 1024
VMEM_LIMIT_V7X = 52 * 1024 * 1024

NT_DIMS = (((1,), (1,)), ((), ()))
NN_DIMS = (((1,), (0,)), ((), ()))
TN_DIMS = (((0,), (0,)), ((), ()))

_TILE_M = (1024, 768, 640, 512, 384, 256, 128, 64, 32, 16, 8)
_TILE_N = (1024, 768, 512, 384, 256, 128)
_TILE_C = (512, 768, 640, 384, 256, 128)
_TILE_ATT = (640, 512, 384, 256, 128)
_TILE_FF = (512, 256, 128)


def _pick(n, candidates):
    for c in candidates:
        if n % c == 0:
            return c
    return n


def _row_tile(rows, cols, budget_bytes=2 << 20):
    best = None
    for t in range(SUBLANES, rows + 1, SUBLANES):
        if rows % t == 0 and t * cols * 4 <= budget_bytes:
            best = t
    return best if best is not None else rows


def _params(*semantics):
    return pltpu.CompilerParams(dimension_semantics=semantics, vmem_limit_bytes=VMEM_LIMIT_V7X)


def _matmul(a, b, mode, out_dtype, name):
    if mode == "nn":
        (m, c), (c2, n) = a.shape, b.shape
    elif mode == "nt":
        (m, c), (n, c2) = a.shape, b.shape
    else:
        (c, m), (c2, n) = a.shape, b.shape
    assert c == c2, (a.shape, b.shape, mode)
    tm, tn, tc = _pick(m, _TILE_M), _pick(n, _TILE_N), _pick(c, _TILE_C)
    steps = c // tc
    if mode == "nn":
        a_spec = pl.BlockSpec((tm, tc), lambda i, j, k: (i, k))
        b_spec = pl.BlockSpec((tc, tn), lambda i, j, k: (k, j))
        dims = NN_DIMS
    elif mode == "nt":
        a_spec = pl.BlockSpec((tm, tc), lambda i, j, k: (i, k))
        b_spec = pl.BlockSpec((tn, tc), lambda i, j, k: (j, k))
        dims = NT_DIMS
    else:
        a_spec = pl.BlockSpec((tc, tm), lambda i, j, k: (k, i))
        b_spec = pl.BlockSpec((tc, tn), lambda i, j, k: (k, j))
        dims = TN_DIMS

    def body(a_ref, b_ref, o_ref, acc_ref):
        k = pl.program_id(2)

        @pl.when(k == 0)
        def _():
            acc_ref[...] = jnp.zeros_like(acc_ref)

        acc_ref[...] += lax.dot_general(a_ref[...].astype(BF16), b_ref[...].astype(BF16), dims,
                                        preferred_element_type=F32)

        @pl.when(k == steps - 1)
        def _():
            o_ref[...] = acc_ref[...].astype(o_ref.dtype)

    return pl.pallas_call(
        body, grid=(m // tm, n // tn, steps), in_specs=[a_spec, b_spec],
        out_specs=pl.BlockSpec((tm, tn), lambda i, j, k: (i, j)),
        out_shape=jax.ShapeDtypeStruct((m, n), out_dtype),
        scratch_shapes=[pltpu.VMEM((tm, tn), F32)],
        compiler_params=_params("parallel", "parallel", "arbitrary"), name=name,
    )(a, b)


@jax.custom_vjp
def linear(x, w):
    return _matmul(x, w, "nn", F32, "linear_fwd")


def _linear_fwd(x, w):
    return _matmul(x, w, "nn", F32, "linear_fwd"), (x, w)


def _linear_bwd(res, dy):
    x, w = res
    dx = _matmul(dy, w, "nt", F32, "linear_dx")
    dw = _matmul(x, dy, "tn", w.dtype, "linear_dw")
    return dx, dw


linear.defvjp(_linear_fwd, _linear_bwd)


def _rms_forward(x, g):
    rows, d = x.shape
    tr = _row_tile(rows, d)

    def body(x_ref, g_ref, y_ref):
        xv = x_ref[...]
        r = lax.rsqrt(jnp.mean(xv * xv, axis=-1, keepdims=True) + EPS)
        y_ref[...] = (xv * r) * g_ref[...]

    return pl.pallas_call(
        body, grid=(rows // tr,),
        in_specs=[pl.BlockSpec((tr, d), lambda i: (i, 0)), pl.BlockSpec((1, d), lambda i: (0, 0))],
        out_specs=pl.BlockSpec((tr, d), lambda i: (i, 0)),
        out_shape=jax.ShapeDtypeStruct((rows, d), F32),
        compiler_params=_params("parallel"), name="rmsnorm_fwd",
    )(x, g.reshape(1, d))


def _rms_backward(x, g, dy):
    rows, d = x.shape
    tr = _row_tile(rows, d)

    def body(x_ref, g_ref, dy_ref, dx_ref, dg_ref):
        i = pl.program_id(0)
        xv = x_ref[...]
        dyv = dy_ref[...]
        r = lax.rsqrt(jnp.mean(xv * xv, axis=-1, keepdims=True) + EPS)
        xh = xv * r
        t = dyv * g_ref[...]
        dx_ref[...] = r * (t - xh * jnp.mean(t * xh, axis=-1, keepdims=True))

        @pl.when(i == 0)
        def _():
            dg_ref[...] = jnp.zeros_like(dg_ref)

        dg_ref[...] += jnp.sum(dyv * xh, axis=0, keepdims=True)

    dx, dg = pl.pallas_call(
        body, grid=(rows // tr,),
        in_specs=[pl.BlockSpec((tr, d), lambda i: (i, 0)), pl.BlockSpec((1, d), lambda i: (0, 0)),
                  pl.BlockSpec((tr, d), lambda i: (i, 0))],
        out_specs=[pl.BlockSpec((tr, d), lambda i: (i, 0)), pl.BlockSpec((1, d), lambda i: (0, 0))],
        out_shape=[jax.ShapeDtypeStruct((rows, d), F32), jax.ShapeDtypeStruct((1, d), F32)],
        compiler_params=_params("arbitrary"), name="rmsnorm_bwd",
    )(x, g.reshape(1, d), dy)
    return dx, dg.reshape(g.shape)


@jax.custom_vjp
def rms_norm(x, g):
    return _rms_forward(x, g)


def _rms_norm_fwd(x, g):
    return _rms_forward(x, g), (x, g)


def _rms_norm_bwd(res, dy):
    x, g = res
    return _rms_backward(x, g, dy)


rms_norm.defvjp(_rms_norm_fwd, _rms_norm_bwd)


@functools.partial(jax.custom_vjp, nondiff_argnums=(1,))
def split_cols(x, bounds):
    return tuple(x[:, lo:hi] for lo, hi in bounds)


def _split_cols_fwd(x, bounds):
    return split_cols(x, bounds), x.shape[1]


def _split_cols_bwd(bounds, width, cts):
    parts = list(cts)
    tail = width - bounds[-1][1]
    if tail:
        parts.append(jnp.zeros((parts[0].shape[0], tail), parts[0].dtype))
    return (jnp.concatenate(parts, axis=1),)


split_cols.defvjp(_split_cols_fwd, _split_cols_bwd)


def _visibility_id(pos, kind, l_real):
    if kind == "fox":
        return pos
    pad_chunk = 2 + (l_real - N_META) // CHUNK
    frame_chunk = 1 + jnp.right_shift(pos - N_META, 6)
    return jnp.where(pos < N_META, 0, jnp.where(pos < l_real, frame_chunk, pad_chunk))


def _scores(kind, scale, q1, k1, q2, k2, bias_q, bias_k):
    s = lax.dot_general(q1.astype(BF16), k1.astype(BF16), NT_DIMS, preferred_element_type=F32)
    if kind == "mla":
        s = s + lax.dot_general(q2.astype(BF16), k2.astype(BF16), NT_DIMS, preferred_element_type=F32)
    s = s * scale
    if kind == "fox":
        s = s + bias_q - bias_k
    return s


def _attention_forward(kind, l_real, q1, k1, v, extra_q, extra_k):
    L = q1.shape[0]
    T = _pick(L, _TILE_ATT)
    nb = L // T
    look = 1 if kind == "mla" else 0
    scale = (HEAD_DIM + ROPE_DIM) ** -0.5 if kind == "mla" else HEAD_DIM ** -0.5

    def last_block(i):
        return jnp.minimum(i + look, nb - 1)

    def kblock(i, j):
        return jnp.minimum(j, last_block(i))

    in_specs = [pl.BlockSpec((T, HEAD_DIM), lambda h, i, j: (i, h)),
                pl.BlockSpec((T, HEAD_DIM), lambda h, i, j: (kblock(i, j), h)),
                pl.BlockSpec((T, HEAD_DIM), lambda h, i, j: (kblock(i, j), h))]
    if kind == "mla":
        in_specs += [pl.BlockSpec((None, T, ROPE_DIM), lambda h, i, j: (h, i, 0)),
                     pl.BlockSpec((T, ROPE_DIM), lambda h, i, j: (kblock(i, j), 0))]
    else:
        in_specs += [pl.BlockSpec((None, T, 1), lambda h, i, j: (h, i, 0)),
                     pl.BlockSpec((None, 1, T), lambda h, i, j: (h, 0, kblock(i, j)))]

    def body(q1_ref, k1_ref, v_ref, eq_ref, ek_ref, o_ref, lse_ref, m_ref, l_ref, acc_ref):
        i, j = pl.program_id(1), pl.program_id(2)

        @pl.when(j == 0)
        def _():
            m_ref[...] = jnp.full_like(m_ref, NEG)
            l_ref[...] = jnp.zeros_like(l_ref)
            acc_ref[...] = jnp.zeros_like(acc_ref)

        @pl.when(j <= last_block(i))
        def _():
            if kind == "mla":
                s = _scores(kind, scale, q1_ref[...], k1_ref[...], eq_ref[...], ek_ref[...], None, None)
            else:
                s = _scores(kind, scale, q1_ref[...], k1_ref[...], None, None, eq_ref[...], ek_ref[...])
            pos_q = i * T + lax.broadcasted_iota(jnp.int32, (T, 1), 0)
            pos_k = j * T + lax.broadcasted_iota(jnp.int32, (1, T), 1)
            visible = _visibility_id(pos_k, kind, l_real) <= _visibility_id(pos_q, kind, l_real)
            s = jnp.where(visible, s, NEG)
            m_prev = m_ref[...]
            m_new = jnp.maximum(m_prev, jnp.max(s, axis=1, keepdims=True))
            alpha = jnp.exp(m_prev - m_new)
            p = jnp.exp(s - m_new)
            l_ref[...] = alpha * l_ref[...] + jnp.sum(p, axis=1, keepdims=True)
            vb = v_ref[...].astype(BF16)
            p_hi = p.astype(BF16)
            pv = lax.dot_general(p_hi, vb, NN_DIMS, preferred_element_type=F32)
            if kind == "fox":
                p_lo = (p - p_hi.astype(F32)).astype(BF16)
                pv = pv + lax.dot_general(p_lo, vb, NN_DIMS, preferred_element_type=F32)
            acc_ref[...] = alpha * acc_ref[...] + pv
            m_ref[...] = m_new

        @pl.when(j == last_block(i))
        def _():
            o_ref[...] = acc_ref[...] / l_ref[...]
            lse_ref[...] = m_ref[...] + jnp.log(l_ref[...])

    return pl.pallas_call(
        body, grid=(HEADS, nb, nb), in_specs=in_specs,
        out_specs=[pl.BlockSpec((T, HEAD_DIM), lambda h, i, j: (i, h)),
                   pl.BlockSpec((None, T, 1), lambda h, i, j: (h, i, 0))],
        out_shape=[jax.ShapeDtypeStruct((L, HEADS * HEAD_DIM), F32), jax.ShapeDtypeStruct((HEADS, L, 1), F32)],
        scratch_shapes=[pltpu.VMEM((T, 1), F32), pltpu.VMEM((T, 1), F32), pltpu.VMEM((T, HEAD_DIM), F32)],
        compiler_params=_params("parallel", "parallel", "arbitrary"), name=kind + "_attn_fwd",
    )(q1, k1, v, extra_q, extra_k)


def _attention_dq(kind, l_real, q1, k1, v, extra_q, extra_k, o, do, lse):
    L = q1.shape[0]
    T = _pick(L, _TILE_ATT)
    nb = L // T
    look = 1 if kind == "mla" else 0
    scale = (HEAD_DIM + ROPE_DIM) ** -0.5 if kind == "mla" else HEAD_DIM ** -0.5

    def last_block(i):
        return jnp.minimum(i + look, nb - 1)

    def kblock(i, j):
        return jnp.minimum(j, last_block(i))

    q_tile = pl.BlockSpec((T, HEAD_DIM), lambda h, i, j: (i, h))
    k_tile = pl.BlockSpec((T, HEAD_DIM), lambda h, i, j: (kblock(i, j), h))
    row_stat = pl.BlockSpec((None, T, 1), lambda h, i, j: (h, i, 0))
    if kind == "mla":
        extra_specs = [pl.BlockSpec((None, T, ROPE_DIM), lambda h, i, j: (h, i, 0)),
                       pl.BlockSpec((T, ROPE_DIM), lambda h, i, j: (kblock(i, j), 0))]
    else:
        extra_specs = [row_stat, pl.BlockSpec((None, 1, T), lambda h, i, j: (h, 0, kblock(i, j)))]
    in_specs = [q_tile, k_tile, k_tile] + extra_specs + [q_tile, q_tile, row_stat]
    out_specs = [q_tile, row_stat]
    out_shape = [jax.ShapeDtypeStruct((L, HEADS * HEAD_DIM), F32), jax.ShapeDtypeStruct((HEADS, L, 1), F32)]
    scratch = [pltpu.VMEM((T, HEAD_DIM), F32), pltpu.VMEM((T, 1), F32)]
    if kind == "mla":
        out_specs.append(pl.BlockSpec((None, T, ROPE_DIM), lambda h, i, j: (h, i, 0)))
        out_shape.append(jax.ShapeDtypeStruct((HEADS, L, ROPE_DIM), F32))
        scratch.append(pltpu.VMEM((T, ROPE_DIM), F32))

    def body(q1_ref, k1_ref, v_ref, eq_ref, ek_ref, o_ref, do_ref, lse_ref, dq1_ref, delta_ref, *rest):
        if kind == "mla":
            dq2_ref, acc1_ref, dl_ref, acc2_ref = rest
        else:
            acc1_ref, dl_ref = rest
        i, j = pl.program_id(1), pl.program_id(2)

        @pl.when(j == 0)
        def _():
            acc1_ref[...] = jnp.zeros_like(acc1_ref)
            if kind == "mla":
                acc2_ref[...] = jnp.zeros_like(acc2_ref)
            dl_ref[...] = jnp.sum(do_ref[...].astype(BF16).astype(F32) * o_ref[...], axis=1, keepdims=True)

        @pl.when(j <= last_block(i))
        def _():
            if kind == "mla":
                s = _scores(kind, scale, q1_ref[...], k1_ref[...], eq_ref[...], ek_ref[...], None, None)
            else:
                s = _scores(kind, scale, q1_ref[...], k1_ref[...], None, None, eq_ref[...], ek_ref[...])
            pos_q = i * T + lax.broadcasted_iota(jnp.int32, (T, 1), 0)
            pos_k = j * T + lax.broadcasted_iota(jnp.int32, (1, T), 1)
            visible = _visibility_id(pos_k, kind, l_real) <= _visibility_id(pos_q, kind, l_real)
            p = jnp.exp(jnp.where(visible, s, NEG) - lse_ref[...])
            dp = lax.dot_general(do_ref[...].astype(BF16), v_ref[...].astype(BF16), NT_DIMS,
                                 preferred_element_type=F32)
            ds = (p * (dp - dl_ref[...])).astype(BF16)
            acc1_ref[...] += lax.dot_general(ds, k1_ref[...].astype(BF16), NN_DIMS, preferred_element_type=F32)
            if kind == "mla":
                acc2_ref[...] += lax.dot_general(ds, ek_ref[...].astype(BF16), NN_DIMS,
                                                 preferred_element_type=F32)

        @pl.when(j == last_block(i))
        def _():
            dq1_ref[...] = acc1_ref[...] * scale
            delta_ref[...] = dl_ref[...]
            if kind == "mla":
                dq2_ref[...] = acc2_ref[...] * scale

    outs = pl.pallas_call(
        body, grid=(HEADS, nb, nb), in_specs=in_specs, out_specs=out_specs, out_shape=out_shape,
        scratch_shapes=scratch, compiler_params=_params("parallel", "parallel", "arbitrary"),
        name=kind + "_attn_dq",
    )(q1, k1, v, extra_q, extra_k, o, do, lse)
    if kind == "mla":
        return outs[0], outs[2], outs[1]
    return outs[0], None, outs[1]


def _attention_dkv(kind, l_real, q1, k1, v, extra_q, extra_k, do, lse_row, delta_row):
    L = q1.shape[0]
    T = _pick(L, _TILE_ATT)
    nb = L // T
    look = 1 if kind == "mla" else 0
    scale = (HEAD_DIM + ROPE_DIM) ** -0.5 if kind == "mla" else HEAD_DIM ** -0.5

    def first_block(j):
        return jnp.maximum(j - look, 0)

    def qblock(j, i):
        return jnp.maximum(i, first_block(j))

    k_tile = pl.BlockSpec((T, HEAD_DIM), lambda h, j, i: (j, h))
    q_tile = pl.BlockSpec((T, HEAD_DIM), lambda h, j, i: (qblock(j, i), h))
    q_row = pl.BlockSpec((None, 1, T), lambda h, j, i: (h, 0, qblock(j, i)))
    if kind == "mla":
        extra_specs = [pl.BlockSpec((None, T, ROPE_DIM), lambda h, j, i: (h, qblock(j, i), 0)),
                       pl.BlockSpec((T, ROPE_DIM), lambda h, j, i: (j, 0))]
        third_spec = pl.BlockSpec((None, T, ROPE_DIM), lambda h, j, i: (h, j, 0))
        third_shape = jax.ShapeDtypeStruct((HEADS, L, ROPE_DIM), F32)
        third_scratch = pltpu.VMEM((T, ROPE_DIM), F32)
    else:
        extra_specs = [q_row, pl.BlockSpec((None, T, 1), lambda h, j, i: (h, j, 0))]
        third_spec = pl.BlockSpec((None, T, 1), lambda h, j, i: (h, j, 0))
        third_shape = jax.ShapeDtypeStruct((HEADS, L, 1), F32)
        third_scratch = pltpu.VMEM((T, 1), F32)
    in_specs = [q_tile, k_tile, k_tile] + extra_specs + [q_tile, q_row, q_row]

    def body(q1_ref, k1_ref, v_ref, eq_ref, ek_ref, do_ref, lse_ref, delta_ref,
             dk1_ref, dv_ref, third_ref, acck_ref, accv_ref, acc3_ref):
        j, i = pl.program_id(1), pl.program_id(2)

        @pl.when(i == 0)
        def _():
            acck_ref[...] = jnp.zeros_like(acck_ref)
            accv_ref[...] = jnp.zeros_like(accv_ref)
            acc3_ref[...] = jnp.zeros_like(acc3_ref)

        @pl.when(i >= first_block(j))
        def _():
            if kind == "mla":
                st = _scores(kind, scale, k1_ref[...], q1_ref[...], ek_ref[...], eq_ref[...], None, None)
            else:
                st = _scores(kind, scale, k1_ref[...], q1_ref[...], None, None, eq_ref[...], ek_ref[...])
            pos_k = j * T + lax.broadcasted_iota(jnp.int32, (T, 1), 0)
            pos_q = i * T + lax.broadcasted_iota(jnp.int32, (1, T), 1)
            visible = _visibility_id(pos_k, kind, l_real) <= _visibility_id(pos_q, kind, l_real)
            pt = jnp.exp(jnp.where(visible, st, NEG) - lse_ref[...])
            dob = do_ref[...].astype(BF16)
            accv_ref[...] += lax.dot_general(pt.astype(BF16), dob, NN_DIMS, preferred_element_type=F32)
            dpt = lax.dot_general(v_ref[...].astype(BF16), dob, NT_DIMS, preferred_element_type=F32)
            dst = pt * (dpt - delta_ref[...])
            dsb = dst.astype(BF16)
            acck_ref[...] += lax.dot_general(dsb, q1_ref[...].astype(BF16), NN_DIMS, preferred_element_type=F32)
            if kind == "mla":
                acc3_ref[...] += lax.dot_general(dsb, eq_ref[...].astype(BF16), NN_DIMS,
                                                 preferred_element_type=F32)
            else:
                acc3_ref[...] -= jnp.sum(dst, axis=1, keepdims=True)

        @pl.when(i == nb - 1)
        def _():
            dk1_ref[...] = acck_ref[...] * scale
            dv_ref[...] = accv_ref[...]
            third_ref[...] = acc3_ref[...] * scale if kind == "mla" else acc3_ref[...]

    return pl.pallas_call(
        body, grid=(HEADS, nb, nb), in_specs=in_specs, out_specs=[k_tile, k_tile, third_spec],
        out_shape=[jax.ShapeDtypeStruct((L, HEADS * HEAD_DIM), F32),
                   jax.ShapeDtypeStruct((L, HEADS * HEAD_DIM), F32), third_shape],
        scratch_shapes=[pltpu.VMEM((T, HEAD_DIM), F32), pltpu.VMEM((T, HEAD_DIM), F32), third_scratch],
        compiler_params=_params("parallel", "parallel", "arbitrary"), name=kind + "_attn_dkv",
    )(q1, k1, v, extra_q, extra_k, do, lse_row, delta_row)


def _as_row(col):
    return col.reshape(col.shape[0], 1, col.shape[1])


@functools.partial(jax.custom_vjp, nondiff_argnums=(0,))
def mla_attention(l_real, qn, qr, kn, kr, v):
    return _attention_forward("mla", l_real, qn, kn, v, qr, kr)[0]


def _mla_attention_fwd(l_real, qn, qr, kn, kr, v):
    o, lse = _attention_forward("mla", l_real, qn, kn, v, qr, kr)
    return o, (qn, qr, kn, kr, v, o, lse)


def _mla_attention_bwd(l_real, res, do):
    qn, qr, kn, kr, v, o, lse = res
    dqn, dqr, delta = _attention_dq("mla", l_real, qn, kn, v, qr, kr, o, do, lse)
    dkn, dv, dkr_heads = _attention_dkv("mla", l_real, qn, kn, v, qr, kr, do, _as_row(lse), _as_row(delta))
    return dqn, dqr, dkn, jnp.sum(dkr_heads, axis=0), dv


mla_attention.defvjp(_mla_attention_fwd, _mla_attention_bwd)


@functools.partial(jax.custom_vjp, nondiff_argnums=(0,))
def fox_attention(l_real, q, k, v, c):
    return _attention_forward("fox", l_real, q, k, v, c, _as_row(c))[0]


def _fox_attention_fwd(l_real, q, k, v, c):
    o, lse = _attention_forward("fox", l_real, q, k, v, c, _as_row(c))
    return o, (q, k, v, c, o, lse)


def _fox_attention_bwd(l_real, res, do):
    q, k, v, c, o, lse = res
    dq, _, delta = _attention_dq("fox", l_real, q, k, v, c, _as_row(c), o, do, lse)
    dk, dv, dc = _attention_dkv("fox", l_real, q, k, v, _as_row(c), c, do, _as_row(lse), _as_row(delta))
    return dq, dk, dv, dc


fox_attention.defvjp(_fox_attention_fwd, _fox_attention_bwd)


GELU_C0 = 0.7978845608028654
GELU_C1 = 0.044715


def _shift_rows(x, prev, s):
    r = pltpu.roll(x, s, 0)
    pr = pltpu.roll(prev, s, 0)
    row = lax.broadcasted_iota(jnp.int32, prev.shape, 0)
    top = jnp.where(row < s, pr, r[0:SUBLANES])
    return jnp.concatenate([top, r[SUBLANES:]], axis=0)


def _conv_tiles(L, f):
    return _pick(L, _TILE_ATT), _pick(f, _TILE_FF)


def _conv_gate_forward(u, w, b):
    L, f2 = u.shape
    f = f2 // 2
    tm, tn = _conv_tiles(L, f)
    rb = tm // SUBLANES

    def body(u_ref, up_ref, w_ref, b_ref, o_ref):
        i = pl.program_id(1)
        x = u_ref[...]
        prev = jnp.where(i > 0, up_ref[...], 0.0)
        wv = w_ref[...]
        hc = b_ref[...] + ((wv[0:1] * _shift_rows(x, prev, 2) + wv[1:2] * _shift_rows(x, prev, 1)) + wv[2:3] * x)
        g = hc[:, :tn]
        gelu = 0.5 * g * (1.0 + jnp.tanh(GELU_C0 * (g + GELU_C1 * g * g * g)))
        o_ref[...] = gelu * hc[:, tn:]

    return pl.pallas_call(
        body, grid=(f // tn, L // tm),
        in_specs=[pl.BlockSpec((tm, 2 * tn), lambda j, i: (i, j)),
                  pl.BlockSpec((SUBLANES, 2 * tn), lambda j, i: (jnp.maximum(i * rb - 1, 0), j)),
                  pl.BlockSpec((3, 2 * tn), lambda j, i: (0, j)),
                  pl.BlockSpec((1, 2 * tn), lambda j, i: (0, j))],
        out_specs=pl.BlockSpec((tm, tn), lambda j, i: (i, j)),
        out_shape=jax.ShapeDtypeStruct((L, f), F32),
        compiler_params=_params("parallel", "parallel"), name="conv_gate_fwd",
    )(u, u, w, b)


def _conv_gate_backward(u, w, b, dact):
    L, f2 = u.shape
    f = f2 // 2
    tm, tn = _conv_tiles(L, f)
    rb = tm // SUBLANES
    n_row_blocks = L // SUBLANES
    n_i = L // tm
    ext = tm + SUBLANES

    def next_rows(i):
        return jnp.minimum((i + 1) * rb, n_row_blocks - 1)

    def body(u_ref, up_ref, un_ref, da_ref, dan_ref, w_ref, b_ref, du_ref, dwb_ref):
        i = pl.program_id(1)
        is_last = i == n_i - 1
        prev = jnp.where(i > 0, up_ref[...], 0.0)
        xe = jnp.concatenate([u_ref[...], jnp.where(is_last, 0.0, un_ref[...])], axis=0)
        x1 = _shift_rows(xe, prev, 1)
        x2 = _shift_rows(xe, prev, 2)
        wv = w_ref[...]
        hc = b_ref[...] + ((wv[0:1] * x2 + wv[1:2] * x1) + wv[2:3] * xe)
        g, up = hc[:, :tn], hc[:, tn:]
        da = jnp.concatenate([da_ref[...], jnp.where(is_last, 0.0, dan_ref[...])], axis=0)
        t = jnp.tanh(GELU_C0 * (g + GELU_C1 * g * g * g))
        gelu = 0.5 * g * (1.0 + t)
        dgelu = 0.5 * (1.0 + t) + 0.5 * g * (1.0 - t * t) * (GELU_C0 * (1.0 + 3.0 * GELU_C1 * g * g))
        dh = jnp.concatenate([da * up * dgelu, da * gelu], axis=1)
        dh1 = pltpu.roll(dh, ext - 1, 0)
        dh2 = pltpu.roll(dh, ext - 2, 0)
        du_ref[...] = ((wv[2:3] * dh + wv[1:2] * dh1) + wv[0:1] * dh2)[:tm]
        dw0 = jnp.sum((dh * x2)[:tm], axis=0, keepdims=True)
        dw1 = jnp.sum((dh * x1)[:tm], axis=0, keepdims=True)
        dw2 = jnp.sum((dh * xe)[:tm], axis=0, keepdims=True)
        db = jnp.sum(dh[:tm], axis=0, keepdims=True)
        row = lax.broadcasted_iota(jnp.int32, (SUBLANES, 2 * tn), 0)
        upd = jnp.where(row == 0, dw0, jnp.where(row == 1, dw1, jnp.where(row == 2, dw2,
                        jnp.where(row == 3, db, 0.0))))

        @pl.when(i == 0)
        def _():
            dwb_ref[...] = jnp.zeros_like(dwb_ref)

        dwb_ref[...] += upd

    return pl.pallas_call(
        body, grid=(f // tn, n_i),
        in_specs=[pl.BlockSpec((tm, 2 * tn), lambda j, i: (i, j)),
                  pl.BlockSpec((SUBLANES, 2 * tn), lambda j, i: (jnp.maximum(i * rb - 1, 0), j)),
                  pl.BlockSpec((SUBLANES, 2 * tn), lambda j, i: (next_rows(i), j)),
                  pl.BlockSpec((tm, tn), lambda j, i: (i, j)),
                  pl.BlockSpec((SUBLANES, tn), lambda j, i: (next_rows(i), j)),
                  pl.BlockSpec((3, 2 * tn), lambda j, i: (0, j)),
                  pl.BlockSpec((1, 2 * tn), lambda j, i: (0, j))],
        out_specs=[pl.BlockSpec((tm, 2 * tn), lambda j, i: (i, j)),
                   pl.BlockSpec((SUBLANES, 2 * tn), lambda j, i: (0, j))],
        out_shape=[jax.ShapeDtypeStruct((L, f2), F32), jax.ShapeDtypeStruct((SUBLANES, f2), F32)],
        compiler_params=_params("parallel", "arbitrary"), name="conv_gate_bwd",
    )(u, u, u, dact, dact, w, b)


@jax.custom_vjp
def conv_gate(u, w, b):
    return _conv_gate_forward(u, w, b.reshape(1, -1))


def _conv_gate_fwd(u, w, b):
    return _conv_gate_forward(u, w, b.reshape(1, -1)), (u, w, b)


def _conv_gate_bwd(res, dact):
    u, w, b = res
    du, dwb = _conv_gate_backward(u, w, b.reshape(1, -1), dact)
    return du, dwb[0:3], dwb[3]


conv_gate.defvjp(_conv_gate_fwd, _conv_gate_bwd)


def _loss_rows(y, target):
    rows, d = y.shape
    tr = _row_tile(rows, d)

    def body(y_ref, t_ref, loss_ref, dy_ref):
        err = y_ref[...] - t_ref[...]
        loss_ref[...] = 0.5 * jnp.mean(err * err, axis=-1, keepdims=True)
        dy_ref[...] = err * (1.0 / d)

    return pl.pallas_call(
        body, grid=(rows // tr,),
        in_specs=[pl.BlockSpec((tr, d), lambda i: (i, 0)), pl.BlockSpec((tr, d), lambda i: (i, 0))],
        out_specs=[pl.BlockSpec((tr, 1), lambda i: (i, 0)), pl.BlockSpec((tr, d), lambda i: (i, 0))],
        out_shape=[jax.ShapeDtypeStruct((rows, 1), F32), jax.ShapeDtypeStruct((rows, d), F32)],
        compiler_params=_params("parallel"), name="loss_head",
    )(y, target)


@jax.custom_vjp
def token_loss(y, target):
    return jnp.sum(_loss_rows(y, target)[0])


def _token_loss_fwd(y, target):
    rows, dy = _loss_rows(y, target)
    return jnp.sum(rows), dy


def _token_loss_bwd(dy, ct):
    return ct * dy, -ct * dy


token_loss.defvjp(_token_loss_fwd, _token_loss_bwd)


def _cols_from_devices(g):
    k = g.shape[1]
    return jnp.transpose(g, (1, 0, 2)).reshape(k, -1)


def _interleave_gate_up(a, f):
    tn = _pick(f, _TILE_FF)
    lead = a.shape[:-1]
    a = a.reshape(lead + (2, f // tn, tn))
    return jnp.swapaxes(a, -3, -2).reshape(lead + (2 * f,))


def _rope(x, cos, sin):
    half = x.shape[-1] // 2
    x1, x2 = x[..., :half], x[..., half:]
    return jnp.concatenate([x1 * cos - x2 * sin, x2 * cos + x1 * sin], axis=-1)


PROJ_BOUNDS = ((0, 512), (512, 1024), (1024, 2048), (2048, 3072), (3072, 4096), (4096, 5120), (5120, 5184),
               (5184, 5192))


def _layer(h, big, small, conv_w, l, l_real, cos, sin):
    L, d = h.shape
    w_in = _cols_from_devices(big["w_in"])
    w_in = jnp.concatenate([w_in[:, :1024], w_in[:, 1088:5184], w_in[:, 1024:1088], w_in[:, 5184:],
                            jnp.zeros((d, IN_COLS_PADDED - IN_COLS), w_in.dtype)], axis=1)
    w_q_up = _cols_from_devices(big["w_q_up"]).reshape(MLA_Q_LORA, HEADS, HEAD_DIM + ROPE_DIM)
    w_q_up = jnp.concatenate([w_q_up[:, :, :HEAD_DIM].reshape(MLA_Q_LORA, -1),
                              w_q_up[:, :, HEAD_DIM:].reshape(MLA_Q_LORA, -1)], axis=1)
    w_kv_up = _cols_from_devices(big["w_kv_up"]).reshape(MLA_KV_LORA, HEADS, 2 * HEAD_DIM)
    w_kv_up = jnp.concatenate([w_kv_up[:, :, :HEAD_DIM].reshape(MLA_KV_LORA, -1),
                               w_kv_up[:, :, HEAD_DIM:].reshape(MLA_KV_LORA, -1)], axis=1)
    w_out = big["w_out"].reshape(-1, d)
    f = big["w_ffn_down"].shape[0] * big["w_ffn_down"].shape[1]
    w_ffn_up = _interleave_gate_up(_cols_from_devices(big["w_ffn_up"]), f)
    w_ffn_down = big["w_ffn_down"].reshape(f, d)
    w_conv = _interleave_gate_up(conv_w, f)
    b_conv = _interleave_gate_up(small["b_ffn_conv"][l], f)

    hn = rms_norm(h, small["ln_mix_pre"][l])
    c_q, c_kv, fq, fk, fv, fg, k_rope, ff = split_cols(linear(hn, w_in), PROJ_BOUNDS)

    q = linear(rms_norm(c_q, small["g_q_latent"][l]), w_q_up)
    qn, qr = split_cols(q, ((0, HEADS * HEAD_DIM), (HEADS * HEAD_DIM, HEADS * (HEAD_DIM + ROPE_DIM))))
    kv = linear(rms_norm(c_kv, small["g_kv_latent"][l]), w_kv_up)
    kn, v = split_cols(kv, ((0, HEADS * HEAD_DIM), (HEADS * HEAD_DIM, 2 * HEADS * HEAD_DIM)))
    qr = jnp.transpose(_rope(qr.reshape(L, HEADS, ROPE_DIM), cos[:, None, :], sin[:, None, :]), (1, 0, 2))
    kr = _rope(k_rope, cos, sin)
    a = mla_attention(l_real, qn, qr, kn, kr, v)

    fqn = rms_norm(fq.reshape(L * HEADS, HEAD_DIM), small["g_fox_q"][l]).reshape(L, FOX_W)
    fkn = rms_norm(fk.reshape(L * HEADS, HEAD_DIM), small["g_fox_k"][l]).reshape(L, FOX_W)
    log_f = jax.nn.log_sigmoid(ff + small["b_forget"][l])
    c = jnp.cumsum(log_f, axis=0).T[:, :, None]
    bmix = fox_attention(l_real, fqn, fkn, fv, c) * jax.nn.sigmoid(fg)

    mix = linear(jnp.concatenate([a, bmix], axis=1), w_out)
    h = h + rms_norm(mix, small["ln_mix_post"][l])

    u = linear(rms_norm(h, small["ln_ffn_pre"][l]), w_ffn_up)
    act = conv_gate(u, w_conv, b_conv)
    h = h + rms_norm(linear(act, w_ffn_down), small["ln_ffn_post"][l])
    return h


def _local_loss(big, small, meta, conv_w, x, target):
    s, d = x.shape
    l_real = N_META + s
    l_pad = -(-l_real // Q_BLOCK) * Q_BLOCK
    h = jnp.concatenate([meta, x, jnp.zeros((l_pad - l_real, d), F32)], axis=0)
    half = ROPE_DIM // 2
    inv_freq = ROPE_THETA ** (-jnp.arange(half, dtype=F32) / half)
    ang = jnp.arange(l_pad, dtype=jnp.int32).astype(F32)[:, None] * inv_freq[None, :]
    cos, sin = jnp.cos(ang), jnp.sin(ang)
    for l in range(DEPTH):
        h = _layer(h, big[l], small, conv_w[l], l, l_real, cos, sin)
    return token_loss(h[N_META:l_real], target)


def _peers():
    ix, iy, ic = lax.axis_index("x"), lax.axis_index("y"), lax.axis_index("c")
    me = 4 * ix + 2 * iy + ic
    peers = []
    for r in range(1, N_DEV):
        px = 1 - ix if r & 4 else ix
        py = 1 - iy if r & 2 else iy
        pc = 1 - ic if r & 1 else ic
        peers.append((r, (px, py, pc), 4 * px + 2 * py + pc))
    return me, peers


def _exchange(x, mode, name):
    rows, cols = x.shape[-2:]

    def body(x_ref, o_ref, send_sems, recv_sems, local_sem):
        me, peers = _peers()
        mine = x_ref if mode == "gather" else x_ref.at[me]
        local = pltpu.make_async_copy(mine, o_ref.at[me], local_sem)
        local.start()
        sends = []
        for r, coords, idx in peers:
            src = x_ref if mode == "gather" else x_ref.at[idx]
            cp = pltpu.make_async_remote_copy(src_ref=src, dst_ref=o_ref.at[me], send_sem=send_sems.at[r - 1],
                                              recv_sem=recv_sems.at[r - 1], device_id=coords,
                                              device_id_type=MESH_ID)
            cp.start()
            sends.append(cp)
        for r, coords, idx in peers:
            src = x_ref if mode == "gather" else x_ref.at[idx]
            pltpu.make_async_remote_copy(src_ref=src, dst_ref=o_ref.at[idx], send_sem=send_sems.at[r - 1],
                                         recv_sem=recv_sems.at[r - 1], device_id=coords,
                                         device_id_type=MESH_ID).wait_recv()
        for cp in sends:
            cp.wait_send()
        local.wait()

    return pl.pallas_call(
        body, out_shape=jax.ShapeDtypeStruct((N_DEV, rows, cols), x.dtype),
        in_specs=[pl.BlockSpec(memory_space=pl.ANY)], out_specs=pl.BlockSpec(memory_space=pl.ANY),
        scratch_shapes=[pltpu.SemaphoreType.DMA((N_DEV - 1,)), pltpu.SemaphoreType.DMA((N_DEV - 1,)),
                        pltpu.SemaphoreType.DMA],
        name=name,
    )(x)


def _sum_devices(x, name):
    _, rows, cols = x.shape
    tr = _row_tile(rows, cols, 256 << 10)

    def body(x_ref, o_ref):
        acc = x_ref[0].astype(F32)
        for d in range(1, N_DEV):
            acc = acc + x_ref[d].astype(F32)
        o_ref[...] = acc

    return pl.pallas_call(
        body, grid=(rows // tr,), in_specs=[pl.BlockSpec((N_DEV, tr, cols), lambda i: (0, i, 0))],
        out_specs=pl.BlockSpec((tr, cols), lambda i: (i, 0)), out_shape=jax.ShapeDtypeStruct((rows, cols), F32),
        compiler_params=_params("parallel"), name=name,
    )(x)


def _pack(arrays, dtype, row_multiple, lead=()):
    n_lead = len(lead)
    flat = jnp.concatenate([a.astype(dtype).reshape(lead + (-1,)) for a in arrays], axis=n_lead)
    n = flat.shape[-1]
    quantum = row_multiple * FLAT_COLS
    padded = -(-n // quantum) * quantum
    flat = jnp.pad(flat, [(0, 0)] * n_lead + [(0, padded - n)])
    return flat.reshape(lead + (padded // FLAT_COLS, FLAT_COLS))


def _unpack(buf, shapes, lead=()):
    flat = buf.reshape(lead + (-1,))
    out, off = [], 0
    for shp in shapes:
        n = 1
        for s in shp:
            n *= s
        out.append(flat[..., off:off + n].reshape(lead + tuple(shp)))
        off += n
    return out


def _adamw(w, g, m, v, name):
    shape = w.shape
    cols = shape[-1]
    w2, g2, m2, v2 = (a.reshape(-1, cols) for a in (w, g, m, v))
    rows = w2.shape[0]
    tr = _row_tile(rows, cols, 1 << 20)

    def body(w_ref, g_ref, m_ref, v_ref, d_ref, nm_ref, nv_ref):
        gv = g_ref[...]
        nm = ADAM_B1 * m_ref[...] + (1.0 - ADAM_B1) * gv
        nv = ADAM_B2 * v_ref[...] + (1.0 - ADAM_B2) * (gv * gv)
        m_hat = nm / (1.0 - ADAM_B1 ** ADAM_STEP)
        v_hat = nv / (1.0 - ADAM_B2 ** ADAM_STEP)
        d_ref[...] = -ADAM_LR * (m_hat / (jnp.sqrt(v_hat) + ADAM_EPS) + ADAM_WD * w_ref[...])
        nm_ref[...] = nm
        nv_ref[...] = nv

    spec = pl.BlockSpec((tr, cols), lambda i: (i, 0))
    outs = pl.pallas_call(
        body, grid=(rows // tr,), in_specs=[spec] * 4, out_specs=[spec] * 3,
        out_shape=[jax.ShapeDtypeStruct((rows, cols), F32)] * 3,
        compiler_params=_params("parallel"), name=name,
    )(w2, g2, m2, v2)
    return tuple(o.reshape(shape) for o in outs)


BIG = ("w_in", "w_q_up", "w_kv_up", "w_out", "w_ffn_up", "w_ffn_down")
REPLICATED = ("ln_mix_pre", "b_forget", "g_q_latent", "g_kv_latent", "g_fox_q", "g_fox_k", "ln_mix_post",
              "ln_ffn_pre", "b_ffn_conv", "ln_ffn_post")
WEIGHTS = ("meta_tokens", "ln_mix_pre", "w_in", "b_forget", "g_q_latent", "g_kv_latent", "w_q_up", "w_kv_up",
           "g_fox_q", "g_fox_k", "w_out", "ln_mix_post", "ln_ffn_pre", "w_ffn_up", "w_ffn_conv", "b_ffn_conv",
           "w_ffn_down", "ln_ffn_post")


def kernel(x, meta_tokens, ln_mix_pre, w_in, b_forget, g_q_latent, g_kv_latent, w_q_up, w_kv_up, g_fox_q, g_fox_k, w_out, ln_mix_post, ln_ffn_pre, w_ffn_up, w_ffn_conv, b_ffn_conv, w_ffn_down, ln_ffn_post, loss_target, m_meta_tokens, m_ln_mix_pre, m_w_in, m_b_forget, m_g_q_latent, m_g_kv_latent, m_w_q_up, m_w_kv_up, m_g_fox_q, m_g_fox_k, m_w_out, m_ln_mix_post, m_ln_ffn_pre, m_w_ffn_up, m_w_ffn_conv, m_b_ffn_conv, m_w_ffn_down, m_ln_ffn_post, v_meta_tokens, v_ln_mix_pre, v_w_in, v_b_forget, v_g_q_latent, v_g_kv_latent, v_w_q_up, v_w_kv_up, v_g_fox_q, v_g_fox_k, v_w_out, v_ln_mix_post, v_ln_ffn_pre, v_w_ffn_up, v_w_ffn_conv, v_b_ffn_conv, v_w_ffn_down, v_ln_ffn_post):
    w = dict(meta_tokens=meta_tokens, ln_mix_pre=ln_mix_pre, w_in=w_in, b_forget=b_forget, g_q_latent=g_q_latent,
             g_kv_latent=g_kv_latent, w_q_up=w_q_up, w_kv_up=w_kv_up, g_fox_q=g_fox_q, g_fox_k=g_fox_k, w_out=w_out,
             ln_mix_post=ln_mix_post, ln_ffn_pre=ln_ffn_pre, w_ffn_up=w_ffn_up, w_ffn_conv=w_ffn_conv,
             b_ffn_conv=b_ffn_conv, w_ffn_down=w_ffn_down, ln_ffn_post=ln_ffn_post)
    mom = dict(meta_tokens=m_meta_tokens, ln_mix_pre=m_ln_mix_pre, w_in=m_w_in, b_forget=m_b_forget,
               g_q_latent=m_g_q_latent, g_kv_latent=m_g_kv_latent, w_q_up=m_w_q_up, w_kv_up=m_w_kv_up,
               g_fox_q=m_g_fox_q, g_fox_k=m_g_fox_k, w_out=m_w_out, ln_mix_post=m_ln_mix_post,
               ln_ffn_pre=m_ln_ffn_pre, w_ffn_up=m_w_ffn_up, w_ffn_conv=m_w_ffn_conv, b_ffn_conv=m_b_ffn_conv,
               w_ffn_down=m_w_ffn_down, ln_ffn_post=m_ln_ffn_post)
    var = dict(meta_tokens=v_meta_tokens, ln_mix_pre=v_ln_mix_pre, w_in=v_w_in, b_forget=v_b_forget,
               g_q_latent=v_g_q_latent, g_kv_latent=v_g_kv_latent, w_q_up=v_w_q_up, w_kv_up=v_w_kv_up,
               g_fox_q=v_g_fox_q, g_fox_k=v_g_fox_k, w_out=v_w_out, ln_mix_post=v_ln_mix_post,
               ln_ffn_pre=v_ln_ffn_pre, w_ffn_up=v_w_ffn_up, w_ffn_conv=v_w_ffn_conv, b_ffn_conv=v_b_ffn_conv,
               w_ffn_down=v_w_ffn_down, ln_ffn_post=v_ln_ffn_post)
    me = 4 * lax.axis_index("x") + 2 * lax.axis_index("y") + lax.axis_index("c")

    big_shapes = [w[n].shape[1:] for n in BIG]
    big = []
    for l in range(DEPTH):
        gathered = _exchange(_pack([w[n][l] for n in BIG], BF16, 16), "gather", "gather_weights")
        big.append(dict(zip(BIG, _unpack(gathered, big_shapes, (N_DEV,)))))
    meta_shape, conv_shape = meta_tokens.shape, w_ffn_conv.shape
    gathered = _exchange(_pack([meta_tokens, w_ffn_conv], F32, SUBLANES), "gather", "gather_f32")
    meta_g, conv_g = _unpack(gathered, [meta_shape, conv_shape], (N_DEV,))
    meta_full = _cols_from_devices(meta_g)
    conv_full = jnp.transpose(conv_g, (1, 2, 0, 3)).reshape(DEPTH, conv_shape[1], -1)
    small = {n: w[n] for n in REPLICATED}

    loss, grads = jax.value_and_grad(_local_loss, argnums=(0, 1, 2, 3, 4))(
        big, small, meta_full, [conv_full[l] for l in range(DEPTH)], x[0], loss_target[0])
    g_big, g_small, g_meta, g_conv, g_x = grads
    loss = lax.psum(loss, ("x", "y", "c"))

    grad = {}
    per_layer = []
    for l in range(DEPTH):
        received = _exchange(_pack([g_big[l][n] for n in BIG], BF16, 16, (N_DEV,)), "scatter", "scatter_grads")
        per_layer.append(_unpack(_sum_devices(received, "sum_grads"), big_shapes))
    for k, n in enumerate(BIG):
        grad[n] = jnp.stack([per_layer[l][k] for l in range(DEPTH)])

    small_arrays = [g_small[n] for n in REPLICATED] + [g_meta, jnp.stack(g_conv)]
    small_shapes = [a.shape for a in small_arrays]
    summed = _sum_devices(_exchange(_pack(small_arrays, F32, SUBLANES), "gather", "gather_small_grads"),
                          "sum_small_grads")
    summed = _unpack(summed, small_shapes)
    for n, g in zip(REPLICATED, summed):
        grad[n] = g
    grad["meta_tokens"] = lax.dynamic_slice_in_dim(summed[-2], me * meta_shape[1], meta_shape[1], axis=1)
    grad["w_ffn_conv"] = lax.dynamic_slice_in_dim(summed[-1], me * conv_shape[2], conv_shape[2], axis=2)

    delta, new_m, new_v = {}, {}, {}
    for n in BIG:
        delta[n], new_m[n], new_v[n] = _adamw(w[n], grad[n], mom[n], var[n], "adamw_" + n)
    rest = [n for n in WEIGHTS if n not in BIG]
    rest_shapes = [w[n].shape for n in rest]
    flat = [_pack([src[n] for n in rest], F32, SUBLANES) for src in (w, grad, mom, var)]
    outs = _adamw(*flat, "adamw_small")
    for dst, buf in zip((delta, new_m, new_v), outs):
        for n, a in zip(rest, _unpack(buf, rest_shapes)):
            dst[n] = a

    return (loss, g_x[None], *[grad[n] for n in WEIGHTS], *[delta[n] for n in WEIGHTS],
            *[new_m[n] for n in WEIGHTS], *[new_v[n] for n in WEIGHTS])
```

```python
import functools

import jax
import jax.numpy as jnp
from jax import lax
from jax.experimental import pallas as pl
from jax.experimental.pallas import tpu as pltpu

F32 = jnp.float32
BF16 = jnp.bfloat16
MESH_ID = pl.DeviceIdType.MESH

N_DEV = 8
DEPTH = 4
N_META = 16
CHUNK = 64
Q_BLOCK = 128
HEADS = 8
HEAD_DIM = 128
ROPE_DIM = 64
MLA_Q_LORA = 512
MLA_KV_LORA = 512
FOX_W = HEADS * HEAD_DIM
ROPE_THETA = 10000.0
EPS = 1e-6
NEG = -1e30
IN_COLS = 5192
IN_COLS_PADDED = 5376

ADAM_LR = 0.001
ADAM_B1 = 0.9
ADAM_B2 = 0.999
ADAM_EPS = 1e-08
ADAM_WD = 0.01
ADAM_STEP = 10

LANES = 128
SUBLANES = 8
FLAT_COLS = 1024
VMEM_LIMIT_V7X = 52 * 1024 * 1024

NT_DIMS = (((1,), (1,)), ((), ()))
NN_DIMS = (((1,), (0,)), ((), ()))
TN_DIMS = (((0,), (0,)), ((), ()))

_TILE_ATT = (640, 512, 384, 256, 128)
_TILE_FF = (512, 256, 128)


def _pick(n, candidates):
    for c in candidates:
        if n % c == 0:
            return c
    return n


def _row_tile(rows, cols, budget_bytes=2 << 20, align=SUBLANES):
    best = None
    for t in range(align, rows + 1, align):
        if rows % t == 0 and t * cols * 4 <= budget_bytes:
            best = t
    return best if best is not None else rows


def _params(*semantics):
    return pltpu.CompilerParams(dimension_semantics=semantics, vmem_limit_bytes=VMEM_LIMIT_V7X)


MATMUL_VMEM_BUDGET = 36 << 20
MXU_FLOPS_V7X = 9.0e14
HBM_BYTES_PER_S_V7X = 2.5e12
GRID_STEP_S = 0.35e-6
MXU_DIM = 256


def _tile_candidates(n, cap):
    c = [t for t in range(LANES, min(n, cap) + 1, LANES) if n % t == 0]
    return c if c else [n]


def _matmul_tiles(m, n, c, a_bytes, b_bytes, o_bytes):
    best, best_cost = None, None
    for tc in _tile_candidates(c, 4096):
        steps = c // tc
        for tm in _tile_candidates(m, 2048):
            for tn in _tile_candidates(n, 2048):
                vmem = 2 * (tm * tc * a_bytes + tc * tn * b_bytes + tm * tn * o_bytes)
                vmem += tm * tn * 4 if steps > 1 else 0
                if vmem > MATMUL_VMEM_BUDGET:
                    continue
                traffic = m * c * a_bytes * (1 if steps == 1 else n // tn) + c * n * b_bytes * (m // tm)
                traffic += m * n * o_bytes
                grid = (m // tm) * (n // tn) * steps
                accumulate = 0 if steps == 1 else grid * tm * tn * 8 / 4.0e12
                fill = (-(-tn // MXU_DIM) * MXU_DIM / tn) * (-(-tc // MXU_DIM) * MXU_DIM / tc)
                cost = max(2.0 * m * n * c * fill / MXU_FLOPS_V7X, traffic / HBM_BYTES_PER_S_V7X)
                cost += grid * GRID_STEP_S + accumulate
                if best_cost is None or cost < best_cost:
                    best, best_cost = (tm, tn, tc), cost
    return best


def _matmul(a, b, mode, out_dtype, name):
    if mode == "nn":
        (m, c), (c2, n) = a.shape, b.shape
    elif mode == "nt":
        (m, c), (n, c2) = a.shape, b.shape
    else:
        (c, m), (c2, n) = a.shape, b.shape
    assert c == c2, (a.shape, b.shape, mode)
    tm, tn, tc = _matmul_tiles(m, n, c, a.dtype.itemsize, b.dtype.itemsize, jnp.dtype(out_dtype).itemsize)
    steps = c // tc
    if mode == "nn":
        a_spec = pl.BlockSpec((tm, tc), lambda i, j, k: (i, k))
        b_spec = pl.BlockSpec((tc, tn), lambda i, j, k: (k, j))
        dims = NN_DIMS
    elif mode == "nt":
        a_spec = pl.BlockSpec((tm, tc), lambda i, j, k: (i, k))
        b_spec = pl.BlockSpec((tn, tc), lambda i, j, k: (j, k))
        dims = NT_DIMS
    else:
        a_spec = pl.BlockSpec((tc, tm), lambda i, j, k: (k, i))
        b_spec = pl.BlockSpec((tc, tn), lambda i, j, k: (k, j))
        dims = TN_DIMS

    def body(a_ref, b_ref, o_ref, acc_ref):
        k = pl.program_id(2)

        @pl.when(k == 0)
        def _():
            acc_ref[...] = jnp.zeros_like(acc_ref)

        acc_ref[...] += lax.dot_general(a_ref[...].astype(BF16), b_ref[...].astype(BF16), dims,
                                        preferred_element_type=F32)

        @pl.when(k == steps - 1)
        def _():
            o_ref[...] = acc_ref[...].astype(o_ref.dtype)

    def body_whole(a_ref, b_ref, o_ref):
        o_ref[...] = lax.dot_general(a_ref[...].astype(BF16), b_ref[...].astype(BF16), dims,
                                     preferred_element_type=F32).astype(o_ref.dtype)

    return pl.pallas_call(
        body if steps > 1 else body_whole, grid=(m // tm, n // tn, steps), in_specs=[a_spec, b_spec],
        out_specs=pl.BlockSpec((tm, tn), lambda i, j, k: (i, j)),
        out_shape=jax.ShapeDtypeStruct((m, n), out_dtype),
        scratch_shapes=[pltpu.VMEM((tm, tn), F32)] if steps > 1 else [],
        compiler_params=_params("parallel", "parallel", "arbitrary"), name=name,
    )(a, b)


@jax.custom_vjp
def linear(x, w):
    return _matmul(x, w, "nn", F32, "linear_fwd")


def _linear_fwd(x, w):
    return _matmul(x, w, "nn", F32, "linear_fwd"), (x, w)


def _linear_bwd(res, dy):
    x, w = res
    dx = _matmul(dy, w, "nt", F32, "linear_dx")
    dw = _matmul(x, dy, "tn", w.dtype, "linear_dw")
    return dx, dw


linear.defvjp(_linear_fwd, _linear_bwd)


def _rms_forward(x, g):
    rows, d = x.shape
    gd = g.shape[0]
    tr = _row_tile(rows, d)

    def body(x_ref, g_ref, y_ref):
        for c0 in range(0, d, gd):
            xv = x_ref[:, c0:c0 + gd]
            r = lax.rsqrt(jnp.mean(xv * xv, axis=-1, keepdims=True) + EPS)
            y_ref[:, c0:c0 + gd] = (xv * r) * g_ref[...]

    return pl.pallas_call(
        body, grid=(rows // tr,),
        in_specs=[pl.BlockSpec((tr, d), lambda i: (i, 0)), pl.BlockSpec((1, gd), lambda i: (0, 0))],
        out_specs=pl.BlockSpec((tr, d), lambda i: (i, 0)),
        out_shape=jax.ShapeDtypeStruct((rows, d), F32),
        compiler_params=_params("parallel"), name="rmsnorm_fwd",
    )(x, g.reshape(1, gd))


def _rms_backward(x, g, dy):
    rows, d = x.shape
    gd = g.shape[0]
    tr = _row_tile(rows, d)

    def body(x_ref, g_ref, dy_ref, dx_ref, dg_ref):
        i = pl.program_id(0)

        @pl.when(i == 0)
        def _():
            dg_ref[...] = jnp.zeros_like(dg_ref)

        for c0 in range(0, d, gd):
            xv = x_ref[:, c0:c0 + gd]
            dyv = dy_ref[:, c0:c0 + gd]
            r = lax.rsqrt(jnp.mean(xv * xv, axis=-1, keepdims=True) + EPS)
            xh = xv * r
            t = dyv * g_ref[...]
            dx_ref[:, c0:c0 + gd] = r * (t - xh * jnp.mean(t * xh, axis=-1, keepdims=True))
            dg_ref[...] += jnp.sum(dyv * xh, axis=0, keepdims=True)

    dx, dg = pl.pallas_call(
        body, grid=(rows // tr,),
        in_specs=[pl.BlockSpec((tr, d), lambda i: (i, 0)), pl.BlockSpec((1, gd), lambda i: (0, 0)),
                  pl.BlockSpec((tr, d), lambda i: (i, 0))],
        out_specs=[pl.BlockSpec((tr, d), lambda i: (i, 0)), pl.BlockSpec((1, gd), lambda i: (0, 0))],
        out_shape=[jax.ShapeDtypeStruct((rows, d), F32), jax.ShapeDtypeStruct((1, gd), F32)],
        compiler_params=_params("arbitrary"), name="rmsnorm_bwd",
    )(x, g.reshape(1, gd), dy)
    return dx, dg.reshape(g.shape)


@jax.custom_vjp
def rms_norm(x, g):
    return _rms_forward(x, g)


def _rms_norm_fwd(x, g):
    return _rms_forward(x, g), (x, g)


def _rms_norm_bwd(res, dy):
    x, g = res
    return _rms_backward(x, g, dy)


rms_norm.defvjp(_rms_norm_fwd, _rms_norm_bwd)


@functools.partial(jax.custom_vjp, nondiff_argnums=(1,))
def split_cols(x, bounds):
    return tuple(x[:, lo:hi] for lo, hi in bounds)


def _split_cols_fwd(x, bounds):
    return split_cols(x, bounds), x.shape[1]


def _split_cols_bwd(bounds, width, cts):
    parts = list(cts)
    tail = width - bounds[-1][1]
    if tail:
        parts.append(jnp.zeros((parts[0].shape[0], tail), parts[0].dtype))
    return (jnp.concatenate(parts, axis=1),)


split_cols.defvjp(_split_cols_fwd, _split_cols_bwd)


def _visibility_id(pos, kind, l_real):
    if kind == "fox":
        return pos
    pad_chunk = 2 + (l_real - N_META) // CHUNK
    frame_chunk = 1 + jnp.right_shift(pos - N_META, 6)
    return jnp.where(pos < N_META, 0, jnp.where(pos < l_real, frame_chunk, pad_chunk))


ATT_ROWS = 32


def _raw_scores(kind, a1, b1, a2, b2):
    s = lax.dot_general(a1.astype(BF16), b1.astype(BF16), NT_DIMS, preferred_element_type=F32)
    if kind == "mla":
        s = s + lax.dot_general(a2.astype(BF16), b2.astype(BF16), NT_DIMS, preferred_element_type=F32)
    return s


def _row_chunks(n_rows, fn):
    def step(r, carry):
        fn(pl.ds(pl.multiple_of(r * ATT_ROWS, ATT_ROWS), ATT_ROWS))
        return carry

    lax.fori_loop(0, n_rows // ATT_ROWS, step, 0)


def _attention_forward(kind, l_real, q1, k1, v, extra_q, extra_k):
    L = q1.shape[0]
    T = _pick(L, _TILE_ATT)
    nb = L // T
    look = 1 if kind == "mla" else 0
    scale = (HEAD_DIM + ROPE_DIM) ** -0.5 if kind == "mla" else HEAD_DIM ** -0.5

    def last_block(i):
        return jnp.minimum(i + look, nb - 1)

    def kblock(i, j):
        return jnp.minimum(j, last_block(i))

    in_specs = [pl.BlockSpec((T, HEAD_DIM), lambda h, i, j: (i, h)),
                pl.BlockSpec((T, HEAD_DIM), lambda h, i, j: (kblock(i, j), h)),
                pl.BlockSpec((T, HEAD_DIM), lambda h, i, j: (kblock(i, j), h))]
    if kind == "mla":
        in_specs += [pl.BlockSpec((None, T, ROPE_DIM), lambda h, i, j: (h, i, 0)),
                     pl.BlockSpec((T, ROPE_DIM), lambda h, i, j: (kblock(i, j), 0))]
    else:
        in_specs += [pl.BlockSpec((None, T, 1), lambda h, i, j: (h, i, 0)),
                     pl.BlockSpec((None, 1, T), lambda h, i, j: (h, 0, kblock(i, j)))]
    scratch = [pltpu.VMEM((T, 1), F32), pltpu.VMEM((T, 1), F32), pltpu.VMEM((T, HEAD_DIM), F32),
               pltpu.VMEM((T, T), F32), pltpu.VMEM((T, T), BF16)]
    if kind == "fox":
        scratch.append(pltpu.VMEM((T, T), BF16))

    def body(q1_ref, k1_ref, v_ref, eq_ref, ek_ref, o_ref, lse_ref, m_ref, l_ref, acc_ref, s_ref, p_ref, *rest):
        i, j = pl.program_id(1), pl.program_id(2)

        @pl.when(j == 0)
        def _():
            m_ref[...] = jnp.full_like(m_ref, NEG)
            l_ref[...] = jnp.zeros_like(l_ref)
            acc_ref[...] = jnp.zeros_like(acc_ref)

        def block(masked):
            s_ref[...] = _raw_scores(kind, q1_ref[...], k1_ref[...], eq_ref[...] if kind == "mla" else None,
                                     ek_ref[...] if kind == "mla" else None)
            id_k = _visibility_id(j * T + lax.broadcasted_iota(jnp.int32, (1, T), 1), kind, l_real)

            def chunk(rows):
                s = s_ref[rows, :] * scale
                if kind == "fox":
                    s = s + eq_ref[rows, :] - ek_ref[...]
                if masked:
                    pos_q = i * T + rows.start + lax.broadcasted_iota(jnp.int32, (ATT_ROWS, 1), 0)
                    s = jnp.where(id_k <= _visibility_id(pos_q, kind, l_real), s, NEG)
                m_prev = m_ref[rows, :]
                m_new = jnp.maximum(m_prev, jnp.max(s, axis=1, keepdims=True))
                alpha = jnp.exp(m_prev - m_new)
                p = jnp.exp(s - m_new)
                l_ref[rows, :] = alpha * l_ref[rows, :] + jnp.sum(p, axis=1, keepdims=True)
                m_ref[rows, :] = m_new
                acc_ref[rows, :] = alpha * acc_ref[rows, :]
                p_hi = p.astype(BF16)
                p_ref[rows, :] = p_hi
                if kind == "fox":
                    rest[0][rows, :] = (p - p_hi.astype(F32)).astype(BF16)

            _row_chunks(T, chunk)
            vb = v_ref[...].astype(BF16)
            pv = lax.dot_general(p_ref[...], vb, NN_DIMS, preferred_element_type=F32)
            if kind == "fox":
                pv = pv + lax.dot_general(rest[0][...], vb, NN_DIMS, preferred_element_type=F32)
            acc_ref[...] += pv

        @pl.when(j < i)
        def _():
            block(False)

        @pl.when((j >= i) & (j <= last_block(i)))
        def _():
            block(True)

        @pl.when(j == last_block(i))
        def _():
            o_ref[...] = acc_ref[...] / l_ref[...]
            lse_ref[...] = m_ref[...] + jnp.log(l_ref[...])

    return pl.pallas_call(
        body, grid=(HEADS, nb, nb), in_specs=in_specs,
        out_specs=[pl.BlockSpec((T, HEAD_DIM), lambda h, i, j: (i, h)),
                   pl.BlockSpec((None, T, 1), lambda h, i, j: (h, i, 0))],
        out_shape=[jax.ShapeDtypeStruct((L, HEADS * HEAD_DIM), F32), jax.ShapeDtypeStruct((HEADS, L, 1), F32)],
        scratch_shapes=scratch,
        compiler_params=_params("parallel", "parallel", "arbitrary"), name=kind + "_attn_fwd",
    )(q1, k1, v, extra_q, extra_k)


def _attention_dq(kind, l_real, q1, k1, v, extra_q, extra_k, o, do, lse):
    L = q1.shape[0]
    T = _pick(L, _TILE_ATT)
    nb = L // T
    look = 1 if kind == "mla" else 0
    scale = (HEAD_DIM + ROPE_DIM) ** -0.5 if kind == "mla" else HEAD_DIM ** -0.5

    def last_block(i):
        return jnp.minimum(i + look, nb - 1)

    def kblock(i, j):
        return jnp.minimum(j, last_block(i))

    q_tile = pl.BlockSpec((T, HEAD_DIM), lambda h, i, j: (i, h))
    k_tile = pl.BlockSpec((T, HEAD_DIM), lambda h, i, j: (kblock(i, j), h))
    row_stat = pl.BlockSpec((None, T, 1), lambda h, i, j: (h, i, 0))
    if kind == "mla":
        extra_specs = [pl.BlockSpec((None, T, ROPE_DIM), lambda h, i, j: (h, i, 0)),
                       pl.BlockSpec((T, ROPE_DIM), lambda h, i, j: (kblock(i, j), 0))]
    else:
        extra_specs = [row_stat, pl.BlockSpec((None, 1, T), lambda h, i, j: (h, 0, kblock(i, j)))]
    in_specs = [q_tile, k_tile, k_tile] + extra_specs + [q_tile, q_tile, row_stat]
    out_specs = [q_tile, row_stat]
    out_shape = [jax.ShapeDtypeStruct((L, HEADS * HEAD_DIM), F32), jax.ShapeDtypeStruct((HEADS, L, 1), F32)]
    scratch = [pltpu.VMEM((T, HEAD_DIM), F32), pltpu.VMEM((T, 1), F32), pltpu.VMEM((T, T), F32),
               pltpu.VMEM((T, T), F32), pltpu.VMEM((T, T), BF16)]
    if kind == "mla":
        out_specs.append(pl.BlockSpec((None, T, ROPE_DIM), lambda h, i, j: (h, i, 0)))
        out_shape.append(jax.ShapeDtypeStruct((HEADS, L, ROPE_DIM), F32))
        scratch.append(pltpu.VMEM((T, ROPE_DIM), F32))

    def body(q1_ref, k1_ref, v_ref, eq_ref, ek_ref, o_ref, do_ref, lse_ref, dq1_ref, delta_ref, *rest):
        if kind == "mla":
            dq2_ref, acc1_ref, dl_ref, s_ref, dp_ref, ds_ref, acc2_ref = rest
        else:
            acc1_ref, dl_ref, s_ref, dp_ref, ds_ref = rest
        i, j = pl.program_id(1), pl.program_id(2)

        @pl.when(j == 0)
        def _():
            acc1_ref[...] = jnp.zeros_like(acc1_ref)
            if kind == "mla":
                acc2_ref[...] = jnp.zeros_like(acc2_ref)
            dl_ref[...] = jnp.sum(do_ref[...].astype(BF16).astype(F32) * o_ref[...], axis=1, keepdims=True)

        def block(masked):
            s_ref[...] = _raw_scores(kind, q1_ref[...], k1_ref[...], eq_ref[...] if kind == "mla" else None,
                                     ek_ref[...] if kind == "mla" else None)
            dp_ref[...] = lax.dot_general(do_ref[...].astype(BF16), v_ref[...].astype(BF16), NT_DIMS,
                                          preferred_element_type=F32)
            id_k = _visibility_id(j * T + lax.broadcasted_iota(jnp.int32, (1, T), 1), kind, l_real)

            def chunk(rows):
                s = s_ref[rows, :] * scale
                if kind == "fox":
                    s = s + eq_ref[rows, :] - ek_ref[...]
                if masked:
                    pos_q = i * T + rows.start + lax.broadcasted_iota(jnp.int32, (ATT_ROWS, 1), 0)
                    s = jnp.where(id_k <= _visibility_id(pos_q, kind, l_real), s, NEG)
                p = jnp.exp(s - lse_ref[rows, :])
                ds_ref[rows, :] = (p * (dp_ref[rows, :] - dl_ref[rows, :])).astype(BF16)

            _row_chunks(T, chunk)
            ds = ds_ref[...]
            acc1_ref[...] += lax.dot_general(ds, k1_ref[...].astype(BF16), NN_DIMS, preferred_element_type=F32)
            if kind == "mla":
                acc2_ref[...] += lax.dot_general(ds, ek_ref[...].astype(BF16), NN_DIMS,
                                                 preferred_element_type=F32)

        @pl.when(j < i)
        def _():
            block(False)

        @pl.when((j >= i) & (j <= last_block(i)))
        def _():
            block(True)

        @pl.when(j == last_block(i))
        def _():
            dq1_ref[...] = acc1_ref[...] * scale
            delta_ref[...] = dl_ref[...]
            if kind == "mla":
                dq2_ref[...] = acc2_ref[...] * scale

    outs = pl.pallas_call(
        body, grid=(HEADS, nb, nb), in_specs=in_specs, out_specs=out_specs, out_shape=out_shape,
        scratch_shapes=scratch, compiler_params=_params("parallel", "parallel", "arbitrary"),
        name=kind + "_attn_dq",
    )(q1, k1, v, extra_q, extra_k, o, do, lse)
    if kind == "mla":
        return outs[0], outs[2], outs[1]
    return outs[0], None, outs[1]


def _attention_dkv(kind, l_real, q1, k1, v, extra_q, extra_k, do, lse_row, delta_row):
    L = q1.shape[0]
    T = _pick(L, _TILE_ATT)
    nb = L // T
    look = 1 if kind == "mla" else 0
    scale = (HEAD_DIM + ROPE_DIM) ** -0.5 if kind == "mla" else HEAD_DIM ** -0.5

    def first_block(j):
        return jnp.maximum(j - look, 0)

    def qblock(j, i):
        return jnp.maximum(i, first_block(j))

    k_tile = pl.BlockSpec((T, HEAD_DIM), lambda h, j, i: (j, h))
    q_tile = pl.BlockSpec((T, HEAD_DIM), lambda h, j, i: (qblock(j, i), h))
    q_row = pl.BlockSpec((None, 1, T), lambda h, j, i: (h, 0, qblock(j, i)))
    if kind == "mla":
        extra_specs = [pl.BlockSpec((None, T, ROPE_DIM), lambda h, j, i: (h, qblock(j, i), 0)),
                       pl.BlockSpec((T, ROPE_DIM), lambda h, j, i: (j, 0))]
        third_spec = pl.BlockSpec((None, T, ROPE_DIM), lambda h, j, i: (h, j, 0))
        third_shape = jax.ShapeDtypeStruct((HEADS, L, ROPE_DIM), F32)
        third_scratch = pltpu.VMEM((T, ROPE_DIM), F32)
    else:
        extra_specs = [q_row, pl.BlockSpec((None, T, 1), lambda h, j, i: (h, j, 0))]
        third_spec = pl.BlockSpec((None, T, 1), lambda h, j, i: (h, j, 0))
        third_shape = jax.ShapeDtypeStruct((HEADS, L, 1), F32)
        third_scratch = pltpu.VMEM((T, 1), F32)
    in_specs = [q_tile, k_tile, k_tile] + extra_specs + [q_tile, q_row, q_row]

    def body(q1_ref, k1_ref, v_ref, eq_ref, ek_ref, do_ref, lse_ref, delta_ref,
             dk1_ref, dv_ref, third_ref, acck_ref, accv_ref, acc3_ref, st_ref, dpt_ref, pt_ref, dst_ref):
        j, i = pl.program_id(1), pl.program_id(2)

        @pl.when(i == 0)
        def _():
            acck_ref[...] = jnp.zeros_like(acck_ref)
            accv_ref[...] = jnp.zeros_like(accv_ref)
            acc3_ref[...] = jnp.zeros_like(acc3_ref)

        def block(masked):
            dob = do_ref[...].astype(BF16)
            st_ref[...] = _raw_scores(kind, k1_ref[...], q1_ref[...], ek_ref[...] if kind == "mla" else None,
                                      eq_ref[...] if kind == "mla" else None)
            dpt_ref[...] = lax.dot_general(v_ref[...].astype(BF16), dob, NT_DIMS, preferred_element_type=F32)
            id_q = _visibility_id(i * T + lax.broadcasted_iota(jnp.int32, (1, T), 1), kind, l_real)

            def chunk(rows):
                st = st_ref[rows, :] * scale
                if kind == "fox":
                    st = st + eq_ref[...] - ek_ref[rows, :]
                if masked:
                    pos_k = j * T + rows.start + lax.broadcasted_iota(jnp.int32, (ATT_ROWS, 1), 0)
                    st = jnp.where(_visibility_id(pos_k, kind, l_real) <= id_q, st, NEG)
                pt = jnp.exp(st - lse_ref[...])
                pt_ref[rows, :] = pt.astype(BF16)
                dst = pt * (dpt_ref[rows, :] - delta_ref[...])
                dst_ref[rows, :] = dst.astype(BF16)
                if kind == "fox":
                    acc3_ref[rows, :] -= jnp.sum(dst, axis=1, keepdims=True)

            _row_chunks(T, chunk)
            accv_ref[...] += lax.dot_general(pt_ref[...], dob, NN_DIMS, preferred_element_type=F32)
            dsb = dst_ref[...]
            acck_ref[...] += lax.dot_general(dsb, q1_ref[...].astype(BF16), NN_DIMS, preferred_element_type=F32)
            if kind == "mla":
                acc3_ref[...] += lax.dot_general(dsb, eq_ref[...].astype(BF16), NN_DIMS,
                                                 preferred_element_type=F32)

        @pl.when(i > j)
        def _():
            block(False)

        @pl.when((i <= j) & (i >= first_block(j)))
        def _():
            block(True)

        @pl.when(i == nb - 1)
        def _():
            dk1_ref[...] = acck_ref[...] * scale
            dv_ref[...] = accv_ref[...]
            third_ref[...] = acc3_ref[...] * scale if kind == "mla" else acc3_ref[...]

    return pl.pallas_call(
        body, grid=(HEADS, nb, nb), in_specs=in_specs, out_specs=[k_tile, k_tile, third_spec],
        out_shape=[jax.ShapeDtypeStruct((L, HEADS * HEAD_DIM), F32),
                   jax.ShapeDtypeStruct((L, HEADS * HEAD_DIM), F32), third_shape],
        scratch_shapes=[pltpu.VMEM((T, HEAD_DIM), F32), pltpu.VMEM((T, HEAD_DIM), F32), third_scratch,
                        pltpu.VMEM((T, T), F32), pltpu.VMEM((T, T), F32), pltpu.VMEM((T, T), BF16),
                        pltpu.VMEM((T, T), BF16)],
        compiler_params=_params("parallel", "parallel", "arbitrary"), name=kind + "_attn_dkv",
    )(q1, k1, v, extra_q, extra_k, do, lse_row, delta_row)


def _as_row(col):
    return col.reshape(col.shape[0], 1, col.shape[1])


@functools.partial(jax.custom_vjp, nondiff_argnums=(0,))
def mla_attention(l_real, qn, qr, kn, kr, v):
    return _attention_forward("mla", l_real, qn, kn, v, qr, kr)[0]


def _mla_attention_fwd(l_real, qn, qr, kn, kr, v):
    o, lse = _attention_forward("mla", l_real, qn, kn, v, qr, kr)
    return o, (qn, qr, kn, kr, v, o, lse)


def _mla_attention_bwd(l_real, res, do):
    qn, qr, kn, kr, v, o, lse = res
    dqn, dqr, delta = _attention_dq("mla", l_real, qn, kn, v, qr, kr, o, do, lse)
    dkn, dv, dkr_heads = _attention_dkv("mla", l_real, qn, kn, v, qr, kr, do, _as_row(lse), _as_row(delta))
    return dqn, dqr, dkn, jnp.sum(dkr_heads, axis=0), dv


mla_attention.defvjp(_mla_attention_fwd, _mla_attention_bwd)


@functools.partial(jax.custom_vjp, nondiff_argnums=(0,))
def fox_attention(l_real, q, k, v, c):
    return _attention_forward("fox", l_real, q, k, v, c, _as_row(c))[0]


def _fox_attention_fwd(l_real, q, k, v, c):
    o, lse = _attention_forward("fox", l_real, q, k, v, c, _as_row(c))
    return o, (q, k, v, c, o, lse)


def _fox_attention_bwd(l_real, res, do):
    q, k, v, c, o, lse = res
    dq, _, delta = _attention_dq("fox", l_real, q, k, v, c, _as_row(c), o, do, lse)
    dk, dv, dc = _attention_dkv("fox", l_real, q, k, v, _as_row(c), c, do, _as_row(lse), _as_row(delta))
    return dq, dk, dv, dc


fox_attention.defvjp(_fox_attention_fwd, _fox_attention_bwd)


GELU_C0 = 0.7978845608028654
GELU_C1 = 0.044715


def _shift_rows(x, prev, s):
    r = pltpu.roll(x, s, 0)
    pr = pltpu.roll(prev, s, 0)
    row = lax.broadcasted_iota(jnp.int32, prev.shape, 0)
    top = jnp.where(row < s, pr, r[0:SUBLANES])
    return jnp.concatenate([top, r[SUBLANES:]], axis=0)


def _conv_tiles(L, f):
    return _pick(L, _TILE_ATT), _pick(f, _TILE_FF)


def _conv_gate_forward(u, w, b):
    L, f2 = u.shape
    f = f2 // 2
    tm, tn = _conv_tiles(L, f)
    rb = tm // SUBLANES

    def body(u_ref, up_ref, w_ref, b_ref, o_ref):
        i = pl.program_id(1)
        x = u_ref[...]
        prev = jnp.where(i > 0, up_ref[...], 0.0)
        wv = w_ref[...]
        hc = b_ref[...] + ((wv[0:1] * _shift_rows(x, prev, 2) + wv[1:2] * _shift_rows(x, prev, 1)) + wv[2:3] * x)
        g = hc[:, :tn]
        gelu = 0.5 * g * (1.0 + jnp.tanh(GELU_C0 * (g + GELU_C1 * g * g * g)))
        o_ref[...] = gelu * hc[:, tn:]

    return pl.pallas_call(
        body, grid=(f // tn, L // tm),
        in_specs=[pl.BlockSpec((tm, 2 * tn), lambda j, i: (i, j)),
                  pl.BlockSpec((SUBLANES, 2 * tn), lambda j, i: (jnp.maximum(i * rb - 1, 0), j)),
                  pl.BlockSpec((3, 2 * tn), lambda j, i: (0, j)),
                  pl.BlockSpec((1, 2 * tn), lambda j, i: (0, j))],
        out_specs=pl.BlockSpec((tm, tn), lambda j, i: (i, j)),
        out_shape=jax.ShapeDtypeStruct((L, f), F32),
        compiler_params=_params("parallel", "parallel"), name="conv_gate_fwd",
    )(u, u, w, b)


def _conv_gate_backward(u, w, b, dact):
    L, f2 = u.shape
    f = f2 // 2
    tm, tn = _conv_tiles(L, f)
    rb = tm // SUBLANES
    n_row_blocks = L // SUBLANES
    n_i = L // tm
    ext = tm + SUBLANES

    def next_rows(i):
        return jnp.minimum((i + 1) * rb, n_row_blocks - 1)

    def body(u_ref, up_ref, un_ref, da_ref, dan_ref, w_ref, b_ref, du_ref, dwb_ref):
        i = pl.program_id(1)
        is_last = i == n_i - 1
        prev = jnp.where(i > 0, up_ref[...], 0.0)
        xe = jnp.concatenate([u_ref[...], jnp.where(is_last, 0.0, un_ref[...])], axis=0)
        x1 = _shift_rows(xe, prev, 1)
        x2 = _shift_rows(xe, prev, 2)
        wv = w_ref[...]
        hc = b_ref[...] + ((wv[0:1] * x2 + wv[1:2] * x1) + wv[2:3] * xe)
        g, up = hc[:, :tn], hc[:, tn:]
        da = jnp.concatenate([da_ref[...], jnp.where(is_last, 0.0, dan_ref[...])], axis=0)
        t = jnp.tanh(GELU_C0 * (g + GELU_C1 * g * g * g))
        gelu = 0.5 * g * (1.0 + t)
        dgelu = 0.5 * (1.0 + t) + 0.5 * g * (1.0 - t * t) * (GELU_C0 * (1.0 + 3.0 * GELU_C1 * g * g))
        dh = jnp.concatenate([da * up * dgelu, da * gelu], axis=1)
        dh1 = pltpu.roll(dh, ext - 1, 0)
        dh2 = pltpu.roll(dh, ext - 2, 0)
        du_ref[...] = ((wv[2:3] * dh + wv[1:2] * dh1) + wv[0:1] * dh2)[:tm]
        dw0 = jnp.sum((dh * x2)[:tm], axis=0, keepdims=True)
        dw1 = jnp.sum((dh * x1)[:tm], axis=0, keepdims=True)
        dw2 = jnp.sum((dh * xe)[:tm], axis=0, keepdims=True)
        db = jnp.sum(dh[:tm], axis=0, keepdims=True)
        row = lax.broadcasted_iota(jnp.int32, (SUBLANES, 2 * tn), 0)
        upd = jnp.where(row == 0, dw0, jnp.where(row == 1, dw1, jnp.where(row == 2, dw2,
                        jnp.where(row == 3, db, 0.0))))

        @pl.when(i == 0)
        def _():
            dwb_ref[...] = jnp.zeros_like(dwb_ref)

        dwb_ref[...] += upd

    return pl.pallas_call(
        body, grid=(f // tn, n_i),
        in_specs=[pl.BlockSpec((tm, 2 * tn), lambda j, i: (i, j)),
                  pl.BlockSpec((SUBLANES, 2 * tn), lambda j, i: (jnp.maximum(i * rb - 1, 0), j)),
                  pl.BlockSpec((SUBLANES, 2 * tn), lambda j, i: (next_rows(i), j)),
                  pl.BlockSpec((tm, tn), lambda j, i: (i, j)),
                  pl.BlockSpec((SUBLANES, tn), lambda j, i: (next_rows(i), j)),
                  pl.BlockSpec((3, 2 * tn), lambda j, i: (0, j)),
                  pl.BlockSpec((1, 2 * tn), lambda j, i: (0, j))],
        out_specs=[pl.BlockSpec((tm, 2 * tn), lambda j, i: (i, j)),
                   pl.BlockSpec((SUBLANES, 2 * tn), lambda j, i: (0, j))],
        out_shape=[jax.ShapeDtypeStruct((L, f2), F32), jax.ShapeDtypeStruct((SUBLANES, f2), F32)],
        compiler_params=_params("parallel", "arbitrary"), name="conv_gate_bwd",
    )(u, u, u, dact, dact, w, b)


@jax.custom_vjp
def conv_gate(u, w, b):
    return _conv_gate_forward(u, w, b.reshape(1, -1))


def _conv_gate_fwd(u, w, b):
    return _conv_gate_forward(u, w, b.reshape(1, -1)), (u, w, b)


def _conv_gate_bwd(res, dact):
    u, w, b = res
    du, dwb = _conv_gate_backward(u, w, b.reshape(1, -1), dact)
    return du, dwb[0:3], dwb[3]


conv_gate.defvjp(_conv_gate_fwd, _conv_gate_bwd)


def _loss_rows(y, target):
    rows, d = y.shape
    tr = _row_tile(rows, d)

    def body(y_ref, t_ref, loss_ref, dy_ref):
        err = y_ref[...] - t_ref[...]
        loss_ref[...] = 0.5 * jnp.mean(err * err, axis=-1, keepdims=True)
        dy_ref[...] = err * (1.0 / d)

    return pl.pallas_call(
        body, grid=(rows // tr,),
        in_specs=[pl.BlockSpec((tr, d), lambda i: (i, 0)), pl.BlockSpec((tr, d), lambda i: (i, 0))],
        out_specs=[pl.BlockSpec((tr, 1), lambda i: (i, 0)), pl.BlockSpec((tr, d), lambda i: (i, 0))],
        out_shape=[jax.ShapeDtypeStruct((rows, 1), F32), jax.ShapeDtypeStruct((rows, d), F32)],
        compiler_params=_params("parallel"), name="loss_head",
    )(y, target)


@jax.custom_vjp
def token_loss(y, target):
    return jnp.sum(_loss_rows(y, target)[0])


def _token_loss_fwd(y, target):
    rows, dy = _loss_rows(y, target)
    return jnp.sum(rows), dy


def _token_loss_bwd(dy, ct):
    return ct * dy, -ct * dy


token_loss.defvjp(_token_loss_fwd, _token_loss_bwd)


def _cols_from_devices(g):
    k = g.shape[1]
    return jnp.transpose(g, (1, 0, 2)).reshape(k, -1)


def _interleave_gate_up(a, f):
    tn = _pick(f, _TILE_FF)
    lead = a.shape[:-1]
    a = a.reshape(lead + (2, f // tn, tn))
    return jnp.swapaxes(a, -3, -2).reshape(lead + (2 * f,))


def _rope(x, cos, sin):
    half = x.shape[-1] // 2
    x1, x2 = x[..., :half], x[..., half:]
    return jnp.concatenate([x1 * cos - x2 * sin, x2 * cos + x1 * sin], axis=-1)


PROJ_BOUNDS = ((0, 512), (512, 1024), (1024, 2048), (2048, 3072), (3072, 4096), (4096, 5120), (5120, 5184),
               (5184, 5192))


def _layer(h, big, small, conv_w, l, l_real, cos, sin):
    L, d = h.shape
    w_in = _cols_from_devices(big["w_in"])
    w_in = jnp.concatenate([w_in[:, :1024], w_in[:, 1088:5184], w_in[:, 1024:1088], w_in[:, 5184:],
                            jnp.zeros((d, IN_COLS_PADDED - IN_COLS), w_in.dtype)], axis=1)
    w_q_up = _cols_from_devices(big["w_q_up"]).reshape(MLA_Q_LORA, HEADS, HEAD_DIM + ROPE_DIM)
    w_q_up = jnp.concatenate([w_q_up[:, :, :HEAD_DIM].reshape(MLA_Q_LORA, -1),
                              w_q_up[:, :, HEAD_DIM:].reshape(MLA_Q_LORA, -1)], axis=1)
    w_kv_up = _cols_from_devices(big["w_kv_up"]).reshape(MLA_KV_LORA, HEADS, 2 * HEAD_DIM)
    w_kv_up = jnp.concatenate([w_kv_up[:, :, :HEAD_DIM].reshape(MLA_KV_LORA, -1),
                               w_kv_up[:, :, HEAD_DIM:].reshape(MLA_KV_LORA, -1)], axis=1)
    w_out = big["w_out"].reshape(-1, d)
    f = big["w_ffn_down"].shape[0] * big["w_ffn_down"].shape[1]
    w_ffn_up = _interleave_gate_up(_cols_from_devices(big["w_ffn_up"]), f)
    w_ffn_down = big["w_ffn_down"].reshape(f, d)
    w_conv = _interleave_gate_up(conv_w, f)
    b_conv = _interleave_gate_up(small["b_ffn_conv"][l], f)

    hn = rms_norm(h, small["ln_mix_pre"][l])
    c_q, c_kv, fq, fk, fv, fg, k_rope, ff = split_cols(linear(hn, w_in), PROJ_BOUNDS)

    q = linear(rms_norm(c_q, small["g_q_latent"][l]), w_q_up)
    qn, qr = split_cols(q, ((0, HEADS * HEAD_DIM), (HEADS * HEAD_DIM, HEADS * (HEAD_DIM + ROPE_DIM))))
    kv = linear(rms_norm(c_kv, small["g_kv_latent"][l]), w_kv_up)
    kn, v = split_cols(kv, ((0, HEADS * HEAD_DIM), (HEADS * HEAD_DIM, 2 * HEADS * HEAD_DIM)))
    qr = jnp.transpose(_rope(qr.reshape(L, HEADS, ROPE_DIM), cos[:, None, :], sin[:, None, :]), (1, 0, 2))
    kr = _rope(k_rope, cos, sin)
    a = mla_attention(l_real, qn, qr, kn, kr, v)

    fqn = rms_norm(fq, small["g_fox_q"][l])
    fkn = rms_norm(fk, small["g_fox_k"][l])
    log_f = jax.nn.log_sigmoid(ff + small["b_forget"][l])
    c = jnp.cumsum(log_f, axis=0).T[:, :, None]
    bmix = fox_attention(l_real, fqn, fkn, fv, c) * jax.nn.sigmoid(fg)

    mix = linear(jnp.concatenate([a, bmix], axis=1), w_out)
    h = h + rms_norm(mix, small["ln_mix_post"][l])

    u = linear(rms_norm(h, small["ln_ffn_pre"][l]), w_ffn_up)
    act = conv_gate(u, w_conv, b_conv)
    h = h + rms_norm(linear(act, w_ffn_down), small["ln_ffn_post"][l])
    return h


def _local_loss(big, small, meta, conv_w, x, target):
    s, d = x.shape
    l_real = N_META + s
    l_pad = -(-l_real // Q_BLOCK) * Q_BLOCK
    h = jnp.concatenate([meta, x, jnp.zeros((l_pad - l_real, d), F32)], axis=0)
    half = ROPE_DIM // 2
    inv_freq = ROPE_THETA ** (-jnp.arange(half, dtype=F32) / half)
    ang = jnp.arange(l_pad, dtype=jnp.int32).astype(F32)[:, None] * inv_freq[None, :]
    cos, sin = jnp.cos(ang), jnp.sin(ang)
    for l in range(DEPTH):
        h = _layer(h, big[l], small, conv_w[l], l, l_real, cos, sin)
    return token_loss(h[N_META:l_real], target)


ANY_SPACE = pl.BlockSpec(memory_space=pl.ANY)


def _place():
    ix, iy, ic = lax.axis_index("x"), lax.axis_index("y"), lax.axis_index("c")
    return ix, iy, ic, [(1 - ix, iy), (ix, 1 - iy), (1 - ix, 1 - iy)]


def _comm_call(body, arrays, out_shapes, n_remote, n_local, name):
    return pl.pallas_call(
        body, out_shape=out_shapes, in_specs=[ANY_SPACE] * len(arrays), out_specs=[ANY_SPACE] * len(out_shapes),
        scratch_shapes=[pltpu.SemaphoreType.DMA((n_remote,)), pltpu.SemaphoreType.DMA((n_remote,)),
                        pltpu.SemaphoreType.DMA((n_local,))],
        name=name,
    )(*arrays)


def _gather(arrays, name):
    n = len(arrays)

    def body(*refs):
        xs, outs = refs[:n], refs[n:2 * n]
        send_sems, recv_sems, local_sems = refs[2 * n:]
        ix, iy, ic, chips = _place()
        me, sibling = (ix, iy, ic), (ix, iy, 1 - ic)

        def copy(a, k, block, to, src=None):
            dst = outs[a].at[4 * block[0] + 2 * block[1] + block[2]]
            return pltpu.make_async_remote_copy(
                src_ref=dst if src is None else src, dst_ref=dst, send_sem=send_sems.at[7 * a + k],
                recv_sem=recv_sems.at[7 * a + k], device_id=to, device_id_type=MESH_ID)

        local, sent = [], []
        for a in range(n):
            mine = pltpu.make_async_copy(xs[a], outs[a].at[4 * ix + 2 * iy + ic], local_sems.at[a])
            mine.start()
            local.append(mine)
            first = [copy(a, 0, me, sibling, src=xs[a])]
            first += [copy(a, 1 + j, me, (*chip, ic), src=xs[a]) for j, chip in enumerate(chips)]
            for cp in first:
                cp.start()
            sent += first
        for a in range(n):
            for j, chip in enumerate(chips):
                copy(a, 1 + j, (*chip, ic), me).wait_recv()
                passed = copy(a, 4 + j, (*chip, ic), sibling)
                passed.start()
                sent.append(passed)
        for a in range(n):
            copy(a, 0, sibling, me).wait_recv()
            for j, chip in enumerate(chips):
                copy(a, 4 + j, (*chip, 1 - ic), me).wait_recv()
        for cp in sent:
            cp.wait_send()
        for cp in local:
            cp.wait()

    out_shapes = [jax.ShapeDtypeStruct((N_DEV,) + a.shape, a.dtype) for a in arrays]
    return _comm_call(body, arrays, out_shapes, 7 * n, n, name)


def _swap_with_sibling(arrays, name):
    n = len(arrays)

    def body(*refs):
        xs, outs = refs[:n], refs[n:2 * n]
        send_sems, recv_sems, _ = refs[2 * n:]
        ix, iy, ic, _ = _place()
        copies = [pltpu.make_async_remote_copy(
            src_ref=xs[a], dst_ref=outs[a], send_sem=send_sems.at[a], recv_sem=recv_sems.at[a],
            device_id=(ix, iy, 1 - ic), device_id_type=MESH_ID) for a in range(n)]
        for cp in copies:
            cp.start()
        for cp in copies:
            cp.wait()

    out_shapes = [jax.ShapeDtypeStruct(a.shape, a.dtype) for a in arrays]
    return _comm_call(body, arrays, out_shapes, n, 1, name)


def _exchange_chips(arrays, name):
    n = len(arrays)

    def body(*refs):
        xs, outs = refs[:n], refs[n:2 * n]
        send_sems, recv_sems, local_sems = refs[2 * n:]
        ix, iy, ic, chips = _place()
        my_chip = 2 * ix + iy
        local, sent = [], []
        for a in range(n):
            mine = pltpu.make_async_copy(xs[a].at[my_chip], outs[a].at[my_chip], local_sems.at[a])
            mine.start()
            local.append(mine)
            for j, chip in enumerate(chips):
                cp = pltpu.make_async_remote_copy(
                    src_ref=xs[a].at[2 * chip[0] + chip[1]], dst_ref=outs[a].at[my_chip],
                    send_sem=send_sems.at[3 * a + j], recv_sem=recv_sems.at[3 * a + j],
                    device_id=(*chip, ic), device_id_type=MESH_ID)
                cp.start()
                sent.append(cp)
        for a in range(n):
            for j, chip in enumerate(chips):
                pltpu.make_async_remote_copy(
                    src_ref=xs[a].at[my_chip], dst_ref=outs[a].at[2 * chip[0] + chip[1]],
                    send_sem=send_sems.at[3 * a + j], recv_sem=recv_sems.at[3 * a + j],
                    device_id=(*chip, ic), device_id_type=MESH_ID).wait_recv()
        for cp in sent:
            cp.wait_send()
        for cp in local:
            cp.wait()

    out_shapes = [jax.ShapeDtypeStruct(a.shape, a.dtype) for a in arrays]
    return _comm_call(body, arrays, out_shapes, 3 * n, n, name)


def _sum_slots(x, out_dtype, name):
    slots, rows, cols = x.shape
    tr = _row_tile(rows, cols, (2 << 20) // slots, 16)

    def body(x_ref, o_ref):
        acc = x_ref[0].astype(F32)
        for s in range(1, slots):
            acc = acc + x_ref[s].astype(F32)
        o_ref[...] = acc.astype(o_ref.dtype)

    return pl.pallas_call(
        body, grid=(rows // tr,), in_specs=[pl.BlockSpec((slots, tr, cols), lambda i: (0, i, 0))],
        out_specs=pl.BlockSpec((tr, cols), lambda i: (i, 0)), out_shape=jax.ShapeDtypeStruct((rows, cols), out_dtype),
        compiler_params=_params("parallel"), name=name,
    )(x)


def _add_pairs(a, b, name):
    slots, rows, cols = a.shape
    tr = _row_tile(rows, cols, 1 << 20, 16)

    def body(a_ref, b_ref, o_ref):
        o_ref[...] = (a_ref[...].astype(F32) + b_ref[...].astype(F32)).astype(o_ref.dtype)

    spec = pl.BlockSpec((None, tr, cols), lambda s, i: (s, i, 0))
    return pl.pallas_call(
        body, grid=(slots, rows // tr), in_specs=[spec, spec], out_specs=spec,
        out_shape=jax.ShapeDtypeStruct(a.shape, BF16), compiler_params=_params("parallel", "parallel"), name=name,
    )(a, b)


def _reduce_scatter(grads, ic):
    by_chip = [g.reshape((4, 2) + g.shape[1:]) for g in grads]
    keep = [lax.dynamic_index_in_dim(g, ic, axis=1, keepdims=False) for g in by_chip]
    give = [lax.dynamic_index_in_dim(g, 1 - ic, axis=1, keepdims=False) for g in by_chip]
    got = _swap_with_sibling(give, "scatter_sibling")
    pairs = [_add_pairs(k, g, "add_pairs") for k, g in zip(keep, got)]
    received = _exchange_chips(pairs, "scatter_chips")
    return [_sum_slots(r, F32, "sum_grads") for r in received]


def _pack(arrays, dtype, row_multiple):
    flat = jnp.concatenate([a.astype(dtype).reshape(-1) for a in arrays])
    n = flat.shape[0]
    quantum = row_multiple * FLAT_COLS
    padded = -(-n // quantum) * quantum
    return jnp.pad(flat, (0, padded - n)).reshape(padded // FLAT_COLS, FLAT_COLS)


def _unpack(buf, shapes):
    flat = buf.reshape(-1)
    out, off = [], 0
    for shp in shapes:
        n = 1
        for s in shp:
            n *= s
        out.append(flat[off:off + n].reshape(tuple(shp)))
        off += n
    return out


def _adamw(w, g, m, v, name):
    shape = w.shape
    cols = shape[-1]
    w2, g2, m2, v2 = (a.reshape(-1, cols) for a in (w, g, m, v))
    rows = w2.shape[0]
    tr = _row_tile(rows, cols, 1 << 20)

    def body(w_ref, g_ref, m_ref, v_ref, d_ref, nm_ref, nv_ref):
        gv = g_ref[...]
        nm = ADAM_B1 * m_ref[...] + (1.0 - ADAM_B1) * gv
        nv = ADAM_B2 * v_ref[...] + (1.0 - ADAM_B2) * (gv * gv)
        m_hat = nm / (1.0 - ADAM_B1 ** ADAM_STEP)
        v_hat = nv / (1.0 - ADAM_B2 ** ADAM_STEP)
        d_ref[...] = -ADAM_LR * (m_hat / (jnp.sqrt(v_hat) + ADAM_EPS) + ADAM_WD * w_ref[...])
        nm_ref[...] = nm
        nv_ref[...] = nv

    spec = pl.BlockSpec((tr, cols), lambda i: (i, 0))
    outs = pl.pallas_call(
        body, grid=(rows // tr,), in_specs=[spec] * 4, out_specs=[spec] * 3,
        out_shape=[jax.ShapeDtypeStruct((rows, cols), F32)] * 3,
        compiler_params=_params("parallel"), name=name,
    )(w2, g2, m2, v2)
    return tuple(o.reshape(shape) for o in outs)


BIG = ("w_in", "w_q_up", "w_kv_up", "w_out", "w_ffn_up", "w_ffn_down")
REPLICATED = ("ln_mix_pre", "b_forget", "g_q_latent", "g_kv_latent", "g_fox_q", "g_fox_k", "ln_mix_post",
              "ln_ffn_pre", "b_ffn_conv", "ln_ffn_post")
WEIGHTS = ("meta_tokens", "ln_mix_pre", "w_in", "b_forget", "g_q_latent", "g_kv_latent", "w_q_up", "w_kv_up",
           "g_fox_q", "g_fox_k", "w_out", "ln_mix_post", "ln_ffn_pre", "w_ffn_up", "w_ffn_conv", "b_ffn_conv",
           "w_ffn_down", "ln_ffn_post")


def kernel(x, meta_tokens, ln_mix_pre, w_in, b_forget, g_q_latent, g_kv_latent, w_q_up, w_kv_up, g_fox_q, g_fox_k, w_out, ln_mix_post, ln_ffn_pre, w_ffn_up, w_ffn_conv, b_ffn_conv, w_ffn_down, ln_ffn_post, loss_target, m_meta_tokens, m_ln_mix_pre, m_w_in, m_b_forget, m_g_q_latent, m_g_kv_latent, m_w_q_up, m_w_kv_up, m_g_fox_q, m_g_fox_k, m_w_out, m_ln_mix_post, m_ln_ffn_pre, m_w_ffn_up, m_w_ffn_conv, m_b_ffn_conv, m_w_ffn_down, m_ln_ffn_post, v_meta_tokens, v_ln_mix_pre, v_w_in, v_b_forget, v_g_q_latent, v_g_kv_latent, v_w_q_up, v_w_kv_up, v_g_fox_q, v_g_fox_k, v_w_out, v_ln_mix_post, v_ln_ffn_pre, v_w_ffn_up, v_w_ffn_conv, v_b_ffn_conv, v_w_ffn_down, v_ln_ffn_post):
    w = dict(meta_tokens=meta_tokens, ln_mix_pre=ln_mix_pre, w_in=w_in, b_forget=b_forget, g_q_latent=g_q_latent,
             g_kv_latent=g_kv_latent, w_q_up=w_q_up, w_kv_up=w_kv_up, g_fox_q=g_fox_q, g_fox_k=g_fox_k, w_out=w_out,
             ln_mix_post=ln_mix_post, ln_ffn_pre=ln_ffn_pre, w_ffn_up=w_ffn_up, w_ffn_conv=w_ffn_conv,
             b_ffn_conv=b_ffn_conv, w_ffn_down=w_ffn_down, ln_ffn_post=ln_ffn_post)
    mom = dict(meta_tokens=m_meta_tokens, ln_mix_pre=m_ln_mix_pre, w_in=m_w_in, b_forget=m_b_forget,
               g_q_latent=m_g_q_latent, g_kv_latent=m_g_kv_latent, w_q_up=m_w_q_up, w_kv_up=m_w_kv_up,
               g_fox_q=m_g_fox_q, g_fox_k=m_g_fox_k, w_out=m_w_out, ln_mix_post=m_ln_mix_post,
               ln_ffn_pre=m_ln_ffn_pre, w_ffn_up=m_w_ffn_up, w_ffn_conv=m_w_ffn_conv, b_ffn_conv=m_b_ffn_conv,
               w_ffn_down=m_w_ffn_down, ln_ffn_post=m_ln_ffn_post)
    var = dict(meta_tokens=v_meta_tokens, ln_mix_pre=v_ln_mix_pre, w_in=v_w_in, b_forget=v_b_forget,
               g_q_latent=v_g_q_latent, g_kv_latent=v_g_kv_latent, w_q_up=v_w_q_up, w_kv_up=v_w_kv_up,
               g_fox_q=v_g_fox_q, g_fox_k=v_g_fox_k, w_out=v_w_out, ln_mix_post=v_ln_mix_post,
               ln_ffn_pre=v_ln_ffn_pre, w_ffn_up=v_w_ffn_up, w_ffn_conv=v_w_ffn_conv, b_ffn_conv=v_b_ffn_conv,
               w_ffn_down=v_w_ffn_down, ln_ffn_post=v_ln_ffn_post)
    ic = lax.axis_index("c")
    me = 4 * lax.axis_index("x") + 2 * lax.axis_index("y") + ic

    gathered = _gather([w[n].astype(BF16) for n in BIG] + [meta_tokens, w_ffn_conv], "gather_weights")
    big = [{n: gathered[k][:, l] for k, n in enumerate(BIG)} for l in range(DEPTH)]
    meta_shape, conv_shape = meta_tokens.shape, w_ffn_conv.shape
    meta_full = _cols_from_devices(gathered[len(BIG)])
    conv_full = jnp.transpose(gathered[len(BIG) + 1], (1, 2, 0, 3)).reshape(DEPTH, conv_shape[1], -1)
    small = {n: w[n] for n in REPLICATED}

    loss, grads = jax.value_and_grad(_local_loss, argnums=(0, 1, 2, 3, 4))(
        big, small, meta_full, [conv_full[l] for l in range(DEPTH)], x[0], loss_target[0])
    g_big, g_small, g_meta, g_conv, g_x = grads
    loss = lax.psum(loss, ("x", "y", "c"))

    grad = {}
    per_layer = [_reduce_scatter([g_big[l][n] for n in BIG], ic) for l in range(DEPTH)]
    for k, n in enumerate(BIG):
        grad[n] = jnp.stack([per_layer[l][k] for l in range(DEPTH)])

    small_arrays = [g_small[n] for n in REPLICATED] + [g_meta, jnp.stack(g_conv)]
    small_shapes = [a.shape for a in small_arrays]
    partials = _gather([_pack(small_arrays, F32, 16)], "gather_small_grads")[0]
    summed = _unpack(_sum_slots(partials, F32, "sum_small_grads"), small_shapes)
    for n, g in zip(REPLICATED, summed):
        grad[n] = g
    grad["meta_tokens"] = lax.dynamic_slice_in_dim(summed[-2], me * meta_shape[1], meta_shape[1], axis=1)
    grad["w_ffn_conv"] = lax.dynamic_slice_in_dim(summed[-1], me * conv_shape[2], conv_shape[2], axis=2)

    delta, new_m, new_v = {}, {}, {}
    for n in BIG:
        delta[n], new_m[n], new_v[n] = _adamw(w[n], grad[n], mom[n], var[n], "adamw_" + n)
    rest = [n for n in WEIGHTS if n not in BIG]
    rest_shapes = [w[n].shape for n in rest]
    flat = [_pack([src[n] for n in rest], F32, SUBLANES) for src in (w, grad, mom, var)]
    outs = _adamw(*flat, "adamw_small")
    for dst, buf in zip((delta, new_m, new_v), outs):
        for n, a in zip(rest, _unpack(buf, rest_shapes)):
            dst[n] = a

    return (loss, g_x[None], *[grad[n] for n in WEIGHTS], *[delta[n] for n in WEIGHTS],
            *[new_m[n] for n in WEIGHTS], *[new_v[n] for n in WEIGHTS])
```

```python
import functools

import jax
import jax.numpy as jnp
from jax import lax
from jax.experimental import pallas as pl
from jax.experimental.pallas import tpu as pltpu

F32 = jnp.float32
BF16 = jnp.bfloat16
MESH_ID = pl.DeviceIdType.MESH

N_DEV = 8
DEPTH = 4
N_META = 16
CHUNK = 64
Q_BLOCK = 128
HEADS = 8
HEAD_DIM = 128
ROPE_DIM = 64
MLA_Q_LORA = 512
MLA_KV_LORA = 512
FOX_W = HEADS * HEAD_DIM
ROPE_THETA = 10000.0
EPS = 1e-6
NEG = -1e30
IN_COLS = 5192
IN_COLS_PADDED = 5376

ADAM_LR = 0.001
ADAM_B1 = 0.9
ADAM_B2 = 0.999
ADAM_EPS = 1e-08
ADAM_WD = 0.01
ADAM_STEP = 10

LANES = 128
SUBLANES = 8
FLAT_COLS = 1024
VMEM_LIMIT_V7X = 52 * 1024 * 1024

NT_DIMS = (((1,), (1,)), ((), ()))
NN_DIMS = (((1,), (0,)), ((), ()))
TN_DIMS = (((0,), (0,)), ((), ()))

_TILE_ATT = (640, 512, 384, 256, 128)
_TILE_FF = (512, 256, 128)


def _pick(n, candidates):
    for c in candidates:
        if n % c == 0:
            return c
    return n


def _row_tile(rows, cols, budget_bytes=2 << 20, align=SUBLANES):
    best = None
    for t in range(align, rows + 1, align):
        if rows % t == 0 and t * cols * 4 <= budget_bytes:
            best = t
    return best if best is not None else rows


def _params(*semantics):
    return pltpu.CompilerParams(dimension_semantics=semantics, vmem_limit_bytes=VMEM_LIMIT_V7X)


MATMUL_VMEM_BUDGET = 36 << 20
MXU_FLOPS_V7X = 9.0e14
HBM_BYTES_PER_S_V7X = 2.5e12
GRID_STEP_S = 0.35e-6
MXU_DIM = 256


def _tile_candidates(n, cap):
    c = [t for t in range(LANES, min(n, cap) + 1, LANES) if n % t == 0]
    return c if c else [n]


def _matmul_tiles(m, n, c, a_bytes, b_bytes, o_bytes):
    best, best_cost = None, None
    for tc in _tile_candidates(c, 4096):
        steps = c // tc
        for tm in _tile_candidates(m, 2048):
            for tn in _tile_candidates(n, 2048):
                vmem = 2 * (tm * tc * a_bytes + tc * tn * b_bytes + tm * tn * o_bytes)
                vmem += tm * tn * 4 if steps > 1 else 0
                if vmem > MATMUL_VMEM_BUDGET:
                    continue
                traffic = m * c * a_bytes * (1 if steps == 1 else n // tn) + c * n * b_bytes * (m // tm)
                traffic += m * n * o_bytes
                grid = (m // tm) * (n // tn) * steps
                accumulate = 0 if steps == 1 else grid * tm * tn * 8 / 4.0e12
                fill = (-(-tn // MXU_DIM) * MXU_DIM / tn) * (-(-tc // MXU_DIM) * MXU_DIM / tc)
                cost = max(2.0 * m * n * c * fill / MXU_FLOPS_V7X, traffic / HBM_BYTES_PER_S_V7X)
                cost += grid * GRID_STEP_S + accumulate
                if best_cost is None or cost < best_cost:
                    best, best_cost = (tm, tn, tc), cost
    return best


def _matmul(a, b, mode, out_dtype, name):
    if mode == "nn":
        (m, c), (c2, n) = a.shape, b.shape
    elif mode == "nt":
        (m, c), (n, c2) = a.shape, b.shape
    else:
        (c, m), (c2, n) = a.shape, b.shape
    assert c == c2, (a.shape, b.shape, mode)
    tm, tn, tc = _matmul_tiles(m, n, c, a.dtype.itemsize, b.dtype.itemsize, jnp.dtype(out_dtype).itemsize)
    steps = c // tc
    if mode == "nn":
        a_spec = pl.BlockSpec((tm, tc), lambda i, j, k: (i, k))
        b_spec = pl.BlockSpec((tc, tn), lambda i, j, k: (k, j))
        dims = NN_DIMS
    elif mode == "nt":
        a_spec = pl.BlockSpec((tm, tc), lambda i, j, k: (i, k))
        b_spec = pl.BlockSpec((tn, tc), lambda i, j, k: (j, k))
        dims = NT_DIMS
    else:
        a_spec = pl.BlockSpec((tc, tm), lambda i, j, k: (k, i))
        b_spec = pl.BlockSpec((tc, tn), lambda i, j, k: (k, j))
        dims = TN_DIMS

    def body(a_ref, b_ref, o_ref, acc_ref):
        k = pl.program_id(2)

        @pl.when(k == 0)
        def _():
            acc_ref[...] = jnp.zeros_like(acc_ref)

        acc_ref[...] += lax.dot_general(a_ref[...].astype(BF16), b_ref[...].astype(BF16), dims,
                                        preferred_element_type=F32)

        @pl.when(k == steps - 1)
        def _():
            o_ref[...] = acc_ref[...].astype(o_ref.dtype)

    def body_whole(a_ref, b_ref, o_ref):
        o_ref[...] = lax.dot_general(a_ref[...].astype(BF16), b_ref[...].astype(BF16), dims,
                                     preferred_element_type=F32).astype(o_ref.dtype)

    return pl.pallas_call(
        body if steps > 1 else body_whole, grid=(m // tm, n // tn, steps), in_specs=[a_spec, b_spec],
        out_specs=pl.BlockSpec((tm, tn), lambda i, j, k: (i, j)),
        out_shape=jax.ShapeDtypeStruct((m, n), out_dtype),
        scratch_shapes=[pltpu.VMEM((tm, tn), F32)] if steps > 1 else [],
        compiler_params=_params("parallel", "parallel", "arbitrary"), name=name,
    )(a, b)


@jax.custom_vjp
def linear(x, w):
    return _matmul(x, w, "nn", F32, "linear_fwd")


def _linear_fwd(x, w):
    return _matmul(x, w, "nn", F32, "linear_fwd"), (x, w)


def _linear_bwd(res, dy):
    x, w = res
    dx = _matmul(dy, w, "nt", F32, "linear_dx")
    dw = _matmul(x, dy, "tn", w.dtype, "linear_dw")
    return dx, dw


linear.defvjp(_linear_fwd, _linear_bwd)


def _rms_forward(x, g):
    rows, d = x.shape
    gd = g.shape[0]
    tr = _row_tile(rows, d)

    def body(x_ref, g_ref, y_ref):
        for c0 in range(0, d, gd):
            xv = x_ref[:, c0:c0 + gd]
            r = lax.rsqrt(jnp.mean(xv * xv, axis=-1, keepdims=True) + EPS)
            y_ref[:, c0:c0 + gd] = (xv * r) * g_ref[...]

    return pl.pallas_call(
        body, grid=(rows // tr,),
        in_specs=[pl.BlockSpec((tr, d), lambda i: (i, 0)), pl.BlockSpec((1, gd), lambda i: (0, 0))],
        out_specs=pl.BlockSpec((tr, d), lambda i: (i, 0)),
        out_shape=jax.ShapeDtypeStruct((rows, d), F32),
        compiler_params=_params("parallel"), name="rmsnorm_fwd",
    )(x, g.reshape(1, gd))


def _rms_backward(x, g, dy):
    rows, d = x.shape
    gd = g.shape[0]
    tr = _row_tile(rows, d)

    def body(x_ref, g_ref, dy_ref, dx_ref, dg_ref):
        i = pl.program_id(0)

        @pl.when(i == 0)
        def _():
            dg_ref[...] = jnp.zeros_like(dg_ref)

        for c0 in range(0, d, gd):
            xv = x_ref[:, c0:c0 + gd]
            dyv = dy_ref[:, c0:c0 + gd]
            r = lax.rsqrt(jnp.mean(xv * xv, axis=-1, keepdims=True) + EPS)
            xh = xv * r
            t = dyv * g_ref[...]
            dx_ref[:, c0:c0 + gd] = r * (t - xh * jnp.mean(t * xh, axis=-1, keepdims=True))
            dg_ref[...] += jnp.sum(dyv * xh, axis=0, keepdims=True)

    dx, dg = pl.pallas_call(
        body, grid=(rows // tr,),
        in_specs=[pl.BlockSpec((tr, d), lambda i: (i, 0)), pl.BlockSpec((1, gd), lambda i: (0, 0)),
                  pl.BlockSpec((tr, d), lambda i: (i, 0))],
        out_specs=[pl.BlockSpec((tr, d), lambda i: (i, 0)), pl.BlockSpec((1, gd), lambda i: (0, 0))],
        out_shape=[jax.ShapeDtypeStruct((rows, d), F32), jax.ShapeDtypeStruct((1, gd), F32)],
        compiler_params=_params("arbitrary"), name="rmsnorm_bwd",
    )(x, g.reshape(1, gd), dy)
    return dx, dg.reshape(g.shape)


@jax.custom_vjp
def rms_norm(x, g):
    return _rms_forward(x, g)


def _rms_norm_fwd(x, g):
    return _rms_forward(x, g), (x, g)


def _rms_norm_bwd(res, dy):
    x, g = res
    return _rms_backward(x, g, dy)


rms_norm.defvjp(_rms_norm_fwd, _rms_norm_bwd)


@functools.partial(jax.custom_vjp, nondiff_argnums=(1,))
def split_cols(x, bounds):
    return tuple(x[:, lo:hi] for lo, hi in bounds)


def _split_cols_fwd(x, bounds):
    return split_cols(x, bounds), x.shape[1]


def _split_cols_bwd(bounds, width, cts):
    parts = list(cts)
    tail = width - bounds[-1][1]
    if tail:
        parts.append(jnp.zeros((parts[0].shape[0], tail), parts[0].dtype))
    return (jnp.concatenate(parts, axis=1),)


split_cols.defvjp(_split_cols_fwd, _split_cols_bwd)


def _visibility_id(pos, kind, l_real):
    if kind == "fox":
        return pos
    pad_chunk = 2 + (l_real - N_META) // CHUNK
    frame_chunk = 1 + jnp.right_shift(pos - N_META, 6)
    return jnp.where(pos < N_META, 0, jnp.where(pos < l_real, frame_chunk, pad_chunk))


def _raw_scores(kind, a1, b1, a2, b2):
    s = lax.dot_general(a1.astype(BF16), b1.astype(BF16), NT_DIMS, preferred_element_type=F32)
    if kind == "mla":
        s = s + lax.dot_general(a2.astype(BF16), b2.astype(BF16), NT_DIMS, preferred_element_type=F32)
    return s


def _block_pairs(nb, look, by_key):
    outer, inner = [], []
    for a in range(nb):
        rng = range(max(a - look, 0), nb) if by_key else range(0, min(a + look, nb - 1) + 1)
        for b in rng:
            outer.append(a)
            inner.append(b)
    return jnp.asarray(outer, jnp.int32), jnp.asarray(inner, jnp.int32)


def _attention_call(body, n_pairs, in_specs, out_specs, out_shape, scratch, name, tables, operands):
    return pl.pallas_call(
        body,
        grid_spec=pltpu.PrefetchScalarGridSpec(num_scalar_prefetch=2, grid=(HEADS, n_pairs), in_specs=in_specs,
                                               out_specs=out_specs, scratch_shapes=scratch),
        out_shape=out_shape, compiler_params=_params("parallel", "arbitrary"), name=name,
    )(*tables, *operands)


def _attention_forward(kind, l_real, q1, k1, v, extra_q, extra_k):
    L = q1.shape[0]
    T = _pick(L, _TILE_ATT)
    nb = L // T
    look = 1 if kind == "mla" else 0
    scale = (HEAD_DIM + ROPE_DIM) ** -0.5 if kind == "mla" else HEAD_DIM ** -0.5
    tables = _block_pairs(nb, look, False)

    q_tile = pl.BlockSpec((T, HEAD_DIM), lambda h, t, it, jt: (it[t], h))
    k_tile = pl.BlockSpec((T, HEAD_DIM), lambda h, t, it, jt: (jt[t], h))
    row_stat = pl.BlockSpec((None, T, 1), lambda h, t, it, jt: (h, it[t], 0))
    if kind == "mla":
        extra_specs = [pl.BlockSpec((None, T, ROPE_DIM), lambda h, t, it, jt: (h, it[t], 0)),
                       pl.BlockSpec((T, ROPE_DIM), lambda h, t, it, jt: (jt[t], 0))]
    else:
        extra_specs = [row_stat, pl.BlockSpec((None, 1, T), lambda h, t, it, jt: (h, 0, jt[t]))]

    def body(it_ref, jt_ref, q1_ref, k1_ref, v_ref, eq_ref, ek_ref, o_ref, lse_ref, m_ref, l_ref, acc_ref):
        t = pl.program_id(1)
        i, j = it_ref[t], jt_ref[t]

        @pl.when(j == 0)
        def _():
            m_ref[...] = jnp.full_like(m_ref, NEG)
            l_ref[...] = jnp.zeros_like(l_ref)
            acc_ref[...] = jnp.zeros_like(acc_ref)

        def block(masked):
            s = _raw_scores(kind, q1_ref[...], k1_ref[...], eq_ref[...] if kind == "mla" else None,
                            ek_ref[...] if kind == "mla" else None) * scale
            if kind == "fox":
                s = s + eq_ref[...] - ek_ref[...]
            if masked:
                pos_q = i * T + lax.broadcasted_iota(jnp.int32, (T, 1), 0)
                pos_k = j * T + lax.broadcasted_iota(jnp.int32, (1, T), 1)
                s = jnp.where(_visibility_id(pos_k, kind, l_real) <= _visibility_id(pos_q, kind, l_real), s, NEG)
            m_prev = m_ref[...]
            m_new = jnp.maximum(m_prev, jnp.max(s, axis=1, keepdims=True))
            alpha = jnp.exp(m_prev - m_new)
            p = jnp.exp(s - m_new)
            l_ref[...] = alpha * l_ref[...] + jnp.sum(p, axis=1, keepdims=True)
            m_ref[...] = m_new
            vb = v_ref[...].astype(BF16)
            p_hi = p.astype(BF16)
            pv = lax.dot_general(p_hi, vb, NN_DIMS, preferred_element_type=F32)
            if kind == "fox":
                p_lo = (p - p_hi.astype(F32)).astype(BF16)
                pv = pv + lax.dot_general(p_lo, vb, NN_DIMS, preferred_element_type=F32)
            acc_ref[...] = alpha * acc_ref[...] + pv

        @pl.when(j < i)
        def _():
            block(False)

        @pl.when(j >= i)
        def _():
            block(True)

        @pl.when(j == jnp.minimum(i + look, nb - 1))
        def _():
            o_ref[...] = acc_ref[...] / l_ref[...]
            lse_ref[...] = m_ref[...] + jnp.log(l_ref[...])

    return _attention_call(
        body, tables[0].shape[0], [q_tile, k_tile, k_tile] + extra_specs, [q_tile, row_stat],
        [jax.ShapeDtypeStruct((L, HEADS * HEAD_DIM), F32), jax.ShapeDtypeStruct((HEADS, L, 1), F32)],
        [pltpu.VMEM((T, 1), F32), pltpu.VMEM((T, 1), F32), pltpu.VMEM((T, HEAD_DIM), F32)],
        kind + "_attn_fwd", tables, (q1, k1, v, extra_q, extra_k))


def _attention_dq(kind, l_real, q1, k1, v, extra_q, extra_k, o, do, lse):
    L = q1.shape[0]
    T = _pick(L, _TILE_ATT)
    nb = L // T
    look = 1 if kind == "mla" else 0
    scale = (HEAD_DIM + ROPE_DIM) ** -0.5 if kind == "mla" else HEAD_DIM ** -0.5
    tables = _block_pairs(nb, look, False)

    q_tile = pl.BlockSpec((T, HEAD_DIM), lambda h, t, it, jt: (it[t], h))
    k_tile = pl.BlockSpec((T, HEAD_DIM), lambda h, t, it, jt: (jt[t], h))
    row_stat = pl.BlockSpec((None, T, 1), lambda h, t, it, jt: (h, it[t], 0))
    if kind == "mla":
        extra_specs = [pl.BlockSpec((None, T, ROPE_DIM), lambda h, t, it, jt: (h, it[t], 0)),
                       pl.BlockSpec((T, ROPE_DIM), lambda h, t, it, jt: (jt[t], 0))]
    else:
        extra_specs = [row_stat, pl.BlockSpec((None, 1, T), lambda h, t, it, jt: (h, 0, jt[t]))]
    in_specs = [q_tile, k_tile, k_tile] + extra_specs + [q_tile, q_tile, row_stat]
    out_specs = [q_tile, row_stat]
    out_shape = [jax.ShapeDtypeStruct((L, HEADS * HEAD_DIM), F32), jax.ShapeDtypeStruct((HEADS, L, 1), F32)]
    scratch = [pltpu.VMEM((T, HEAD_DIM), F32), pltpu.VMEM((T, 1), F32)]
    if kind == "mla":
        out_specs.append(pl.BlockSpec((None, T, ROPE_DIM), lambda h, t, it, jt: (h, it[t], 0)))
        out_shape.append(jax.ShapeDtypeStruct((HEADS, L, ROPE_DIM), F32))
        scratch.append(pltpu.VMEM((T, ROPE_DIM), F32))

    def body(it_ref, jt_ref, q1_ref, k1_ref, v_ref, eq_ref, ek_ref, o_ref, do_ref, lse_ref, dq1_ref, delta_ref,
             *rest):
        if kind == "mla":
            dq2_ref, acc1_ref, dl_ref, acc2_ref = rest
        else:
            acc1_ref, dl_ref = rest
        t = pl.program_id(1)
        i, j = it_ref[t], jt_ref[t]

        @pl.when(j == 0)
        def _():
            acc1_ref[...] = jnp.zeros_like(acc1_ref)
            if kind == "mla":
                acc2_ref[...] = jnp.zeros_like(acc2_ref)
            dl_ref[...] = jnp.sum(do_ref[...].astype(BF16).astype(F32) * o_ref[...], axis=1, keepdims=True)

        def block(masked):
            s = _raw_scores(kind, q1_ref[...], k1_ref[...], eq_ref[...] if kind == "mla" else None,
                            ek_ref[...] if kind == "mla" else None) * scale
            if kind == "fox":
                s = s + eq_ref[...] - ek_ref[...]
            if masked:
                pos_q = i * T + lax.broadcasted_iota(jnp.int32, (T, 1), 0)
                pos_k = j * T + lax.broadcasted_iota(jnp.int32, (1, T), 1)
                s = jnp.where(_visibility_id(pos_k, kind, l_real) <= _visibility_id(pos_q, kind, l_real), s, NEG)
            p = jnp.exp(s - lse_ref[...])
            dp = lax.dot_general(do_ref[...].astype(BF16), v_ref[...].astype(BF16), NT_DIMS,
                                 preferred_element_type=F32)
            ds = (p * (dp - dl_ref[...])).astype(BF16)
            acc1_ref[...] += lax.dot_general(ds, k1_ref[...].astype(BF16), NN_DIMS, preferred_element_type=F32)
            if kind == "mla":
                acc2_ref[...] += lax.dot_general(ds, ek_ref[...].astype(BF16), NN_DIMS,
                                                 preferred_element_type=F32)

        @pl.when(j < i)
        def _():
            block(False)

        @pl.when(j >= i)
        def _():
            block(True)

        @pl.when(j == jnp.minimum(i + look, nb - 1))
        def _():
            dq1_ref[...] = acc1_ref[...] * scale
            delta_ref[...] = dl_ref[...]
            if kind == "mla":
                dq2_ref[...] = acc2_ref[...] * scale

    outs = _attention_call(body, tables[0].shape[0], in_specs, out_specs, out_shape, scratch, kind + "_attn_dq",
                           tables, (q1, k1, v, extra_q, extra_k, o, do, lse))
    if kind == "mla":
        return outs[0], outs[2], outs[1]
    return outs[0], None, outs[1]


def _attention_dkv(kind, l_real, q1, k1, v, extra_q, extra_k, do, lse_row, delta_row):
    L = q1.shape[0]
    T = _pick(L, _TILE_ATT)
    nb = L // T
    look = 1 if kind == "mla" else 0
    scale = (HEAD_DIM + ROPE_DIM) ** -0.5 if kind == "mla" else HEAD_DIM ** -0.5
    tables = _block_pairs(nb, look, True)

    k_tile = pl.BlockSpec((T, HEAD_DIM), lambda h, t, jt, it: (jt[t], h))
    q_tile = pl.BlockSpec((T, HEAD_DIM), lambda h, t, jt, it: (it[t], h))
    q_row = pl.BlockSpec((None, 1, T), lambda h, t, jt, it: (h, 0, it[t]))
    if kind == "mla":
        extra_specs = [pl.BlockSpec((None, T, ROPE_DIM), lambda h, t, jt, it: (h, it[t], 0)),
                       pl.BlockSpec((T, ROPE_DIM), lambda h, t, jt, it: (jt[t], 0))]
        third_spec = pl.BlockSpec((None, T, ROPE_DIM), lambda h, t, jt, it: (h, jt[t], 0))
        third_shape = jax.ShapeDtypeStruct((HEADS, L, ROPE_DIM), F32)
        third_scratch = pltpu.VMEM((T, ROPE_DIM), F32)
    else:
        extra_specs = [q_row, pl.BlockSpec((None, T, 1), lambda h, t, jt, it: (h, jt[t], 0))]
        third_spec = pl.BlockSpec((None, T, 1), lambda h, t, jt, it: (h, jt[t], 0))
        third_shape = jax.ShapeDtypeStruct((HEADS, L, 1), F32)
        third_scratch = pltpu.VMEM((T, 1), F32)
    in_specs = [q_tile, k_tile, k_tile] + extra_specs + [q_tile, q_row, q_row]

    def body(jt_ref, it_ref, q1_ref, k1_ref, v_ref, eq_ref, ek_ref, do_ref, lse_ref, delta_ref,
             dk1_ref, dv_ref, third_ref, acck_ref, accv_ref, acc3_ref):
        t = pl.program_id(1)
        j, i = jt_ref[t], it_ref[t]

        @pl.when(i == jnp.maximum(j - look, 0))
        def _():
            acck_ref[...] = jnp.zeros_like(acck_ref)
            accv_ref[...] = jnp.zeros_like(accv_ref)
            acc3_ref[...] = jnp.zeros_like(acc3_ref)

        def block(masked):
            st = _raw_scores(kind, k1_ref[...], q1_ref[...], ek_ref[...] if kind == "mla" else None,
                             eq_ref[...] if kind == "mla" else None) * scale
            if kind == "fox":
                st = st + eq_ref[...] - ek_ref[...]
            if masked:
                pos_k = j * T + lax.broadcasted_iota(jnp.int32, (T, 1), 0)
                pos_q = i * T + lax.broadcasted_iota(jnp.int32, (1, T), 1)
                st = jnp.where(_visibility_id(pos_k, kind, l_real) <= _visibility_id(pos_q, kind, l_real), st, NEG)
            pt = jnp.exp(st - lse_ref[...])
            dob = do_ref[...].astype(BF16)
            accv_ref[...] += lax.dot_general(pt.astype(BF16), dob, NN_DIMS, preferred_element_type=F32)
            dpt = lax.dot_general(v_ref[...].astype(BF16), dob, NT_DIMS, preferred_element_type=F32)
            dst = pt * (dpt - delta_ref[...])
            dsb = dst.astype(BF16)
            acck_ref[...] += lax.dot_general(dsb, q1_ref[...].astype(BF16), NN_DIMS, preferred_element_type=F32)
            if kind == "mla":
                acc3_ref[...] += lax.dot_general(dsb, eq_ref[...].astype(BF16), NN_DIMS,
                                                 preferred_element_type=F32)
            else:
                acc3_ref[...] -= jnp.sum(dst, axis=1, keepdims=True)

        @pl.when(i > j)
        def _():
            block(False)

        @pl.when(i <= j)
        def _():
            block(True)

        @pl.when(i == nb - 1)
        def _():
            dk1_ref[...] = acck_ref[...] * scale
            dv_ref[...] = accv_ref[...]
            third_ref[...] = acc3_ref[...] * scale if kind == "mla" else acc3_ref[...]

    return _attention_call(
        body, tables[0].shape[0], in_specs, [k_tile, k_tile, third_spec],
        [jax.ShapeDtypeStruct((L, HEADS * HEAD_DIM), F32), jax.ShapeDtypeStruct((L, HEADS * HEAD_DIM), F32),
         third_shape],
        [pltpu.VMEM((T, HEAD_DIM), F32), pltpu.VMEM((T, HEAD_DIM), F32), third_scratch],
        kind + "_attn_dkv", tables, (q1, k1, v, extra_q, extra_k, do, lse_row, delta_row))


def _as_row(col):
    return col.reshape(col.shape[0], 1, col.shape[1])


@functools.partial(jax.custom_vjp, nondiff_argnums=(0,))
def mla_attention(l_real, qn, qr, kn, kr, v):
    return _attention_forward("mla", l_real, qn, kn, v, qr, kr)[0]


def _mla_attention_fwd(l_real, qn, qr, kn, kr, v):
    o, lse = _attention_forward("mla", l_real, qn, kn, v, qr, kr)
    return o, (qn, qr, kn, kr, v, o, lse)


def _mla_attention_bwd(l_real, res, do):
    qn, qr, kn, kr, v, o, lse = res
    dqn, dqr, delta = _attention_dq("mla", l_real, qn, kn, v, qr, kr, o, do, lse)
    dkn, dv, dkr_heads = _attention_dkv("mla", l_real, qn, kn, v, qr, kr, do, _as_row(lse), _as_row(delta))
    return dqn, dqr, dkn, jnp.sum(dkr_heads, axis=0), dv


mla_attention.defvjp(_mla_attention_fwd, _mla_attention_bwd)


@functools.partial(jax.custom_vjp, nondiff_argnums=(0,))
def fox_attention(l_real, q, k, v, c):
    return _attention_forward("fox", l_real, q, k, v, c, _as_row(c))[0]


def _fox_attention_fwd(l_real, q, k, v, c):
    o, lse = _attention_forward("fox", l_real, q, k, v, c, _as_row(c))
    return o, (q, k, v, c, o, lse)


def _fox_attention_bwd(l_real, res, do):
    q, k, v, c, o, lse = res
    dq, _, delta = _attention_dq("fox", l_real, q, k, v, c, _as_row(c), o, do, lse)
    dk, dv, dc = _attention_dkv("fox", l_real, q, k, v, _as_row(c), c, do, _as_row(lse), _as_row(delta))
    return dq, dk, dv, dc


fox_attention.defvjp(_fox_attention_fwd, _fox_attention_bwd)


GELU_C0 = 0.7978845608028654
GELU_C1 = 0.044715


def _shift_rows(x, prev, s):
    r = pltpu.roll(x, s, 0)
    pr = pltpu.roll(prev, s, 0)
    row = lax.broadcasted_iota(jnp.int32, prev.shape, 0)
    top = jnp.where(row < s, pr, r[0:SUBLANES])
    return jnp.concatenate([top, r[SUBLANES:]], axis=0)


def _conv_tiles(L, f):
    return _pick(L, _TILE_ATT), _pick(f, _TILE_FF)


def _conv_gate_forward(u, w, b):
    L, f2 = u.shape
    f = f2 // 2
    tm, tn = _conv_tiles(L, f)
    rb = tm // SUBLANES

    def body(u_ref, up_ref, w_ref, b_ref, o_ref):
        i = pl.program_id(1)
        x = u_ref[...]
        prev = jnp.where(i > 0, up_ref[...], 0.0)
        wv = w_ref[...]
        hc = b_ref[...] + ((wv[0:1] * _shift_rows(x, prev, 2) + wv[1:2] * _shift_rows(x, prev, 1)) + wv[2:3] * x)
        g = hc[:, :tn]
        gelu = 0.5 * g * (1.0 + jnp.tanh(GELU_C0 * (g + GELU_C1 * g * g * g)))
        o_ref[...] = gelu * hc[:, tn:]

    return pl.pallas_call(
        body, grid=(f // tn, L // tm),
        in_specs=[pl.BlockSpec((tm, 2 * tn), lambda j, i: (i, j)),
                  pl.BlockSpec((SUBLANES, 2 * tn), lambda j, i: (jnp.maximum(i * rb - 1, 0), j)),
                  pl.BlockSpec((3, 2 * tn), lambda j, i: (0, j)),
                  pl.BlockSpec((1, 2 * tn), lambda j, i: (0, j))],
        out_specs=pl.BlockSpec((tm, tn), lambda j, i: (i, j)),
        out_shape=jax.ShapeDtypeStruct((L, f), F32),
        compiler_params=_params("parallel", "parallel"), name="conv_gate_fwd",
    )(u, u, w, b)


def _conv_gate_backward(u, w, b, dact):
    L, f2 = u.shape
    f = f2 // 2
    tm, tn = _conv_tiles(L, f)
    rb = tm // SUBLANES
    n_row_blocks = L // SUBLANES
    n_i = L // tm
    ext = tm + SUBLANES

    def next_rows(i):
        return jnp.minimum((i + 1) * rb, n_row_blocks - 1)

    def body(u_ref, up_ref, un_ref, da_ref, dan_ref, w_ref, b_ref, du_ref, dwb_ref):
        i = pl.program_id(1)
        is_last = i == n_i - 1
        prev = jnp.where(i > 0, up_ref[...], 0.0)
        xe = jnp.concatenate([u_ref[...], jnp.where(is_last, 0.0, un_ref[...])], axis=0)
        x1 = _shift_rows(xe, prev, 1)
        x2 = _shift_rows(xe, prev, 2)
        wv = w_ref[...]
        hc = b_ref[...] + ((wv[0:1] * x2 + wv[1:2] * x1) + wv[2:3] * xe)
        g, up = hc[:, :tn], hc[:, tn:]
        da = jnp.concatenate([da_ref[...], jnp.where(is_last, 0.0, dan_ref[...])], axis=0)
        t = jnp.tanh(GELU_C0 * (g + GELU_C1 * g * g * g))
        gelu = 0.5 * g * (1.0 + t)
        dgelu = 0.5 * (1.0 + t) + 0.5 * g * (1.0 - t * t) * (GELU_C0 * (1.0 + 3.0 * GELU_C1 * g * g))
        dh = jnp.concatenate([da * up * dgelu, da * gelu], axis=1)
        dh1 = pltpu.roll(dh, ext - 1, 0)
        dh2 = pltpu.roll(dh, ext - 2, 0)
        du_ref[...] = ((wv[2:3] * dh + wv[1:2] * dh1) + wv[0:1] * dh2)[:tm]
        dw0 = jnp.sum((dh * x2)[:tm], axis=0, keepdims=True)
        dw1 = jnp.sum((dh * x1)[:tm], axis=0, keepdims=True)
        dw2 = jnp.sum((dh * xe)[:tm], axis=0, keepdims=True)
        db = jnp.sum(dh[:tm], axis=0, keepdims=True)
        row = lax.broadcasted_iota(jnp.int32, (SUBLANES, 2 * tn), 0)
        upd = jnp.where(row == 0, dw0, jnp.where(row == 1, dw1, jnp.where(row == 2, dw2,
                        jnp.where(row == 3, db, 0.0))))

        @pl.when(i == 0)
        def _():
            dwb_ref[...] = jnp.zeros_like(dwb_ref)

        dwb_ref[...] += upd

    return pl.pallas_call(
        body, grid=(f // tn, n_i),
        in_specs=[pl.BlockSpec((tm, 2 * tn), lambda j, i: (i, j)),
                  pl.BlockSpec((SUBLANES, 2 * tn), lambda j, i: (jnp.maximum(i * rb - 1, 0), j)),
                  pl.BlockSpec((SUBLANES, 2 * tn), lambda j, i: (next_rows(i), j)),
                  pl.BlockSpec((tm, tn), lambda j, i: (i, j)),
                  pl.BlockSpec((SUBLANES, tn), lambda j, i: (next_rows(i), j)),
                  pl.BlockSpec((3, 2 * tn), lambda j, i: (0, j)),
                  pl.BlockSpec((1, 2 * tn), lambda j, i: (0, j))],
        out_specs=[pl.BlockSpec((tm, 2 * tn), lambda j, i: (i, j)),
                   pl.BlockSpec((SUBLANES, 2 * tn), lambda j, i: (0, j))],
        out_shape=[jax.ShapeDtypeStruct((L, f2), F32), jax.ShapeDtypeStruct((SUBLANES, f2), F32)],
        compiler_params=_params("parallel", "arbitrary"), name="conv_gate_bwd",
    )(u, u, u, dact, dact, w, b)


@jax.custom_vjp
def conv_gate(u, w, b):
    return _conv_gate_forward(u, w, b.reshape(1, -1))


def _conv_gate_fwd(u, w, b):
    return _conv_gate_forward(u, w, b.reshape(1, -1)), (u, w, b)


def _conv_gate_bwd(res, dact):
    u, w, b = res
    du, dwb = _conv_gate_backward(u, w, b.reshape(1, -1), dact)
    return du, dwb[0:3], dwb[3]


conv_gate.defvjp(_conv_gate_fwd, _conv_gate_bwd)


def _loss_rows(y, target):
    rows, d = y.shape
    tr = _row_tile(rows, d)

    def body(y_ref, t_ref, loss_ref, dy_ref):
        err = y_ref[...] - t_ref[...]
        loss_ref[...] = 0.5 * jnp.mean(err * err, axis=-1, keepdims=True)
        dy_ref[...] = err * (1.0 / d)

    return pl.pallas_call(
        body, grid=(rows // tr,),
        in_specs=[pl.BlockSpec((tr, d), lambda i: (i, 0)), pl.BlockSpec((tr, d), lambda i: (i, 0))],
        out_specs=[pl.BlockSpec((tr, 1), lambda i: (i, 0)), pl.BlockSpec((tr, d), lambda i: (i, 0))],
        out_shape=[jax.ShapeDtypeStruct((rows, 1), F32), jax.ShapeDtypeStruct((rows, d), F32)],
        compiler_params=_params("parallel"), name="loss_head",
    )(y, target)


@jax.custom_vjp
def token_loss(y, target):
    return jnp.sum(_loss_rows(y, target)[0])


def _token_loss_fwd(y, target):
    rows, dy = _loss_rows(y, target)
    return jnp.sum(rows), dy


def _token_loss_bwd(dy, ct):
    return ct * dy, -ct * dy


token_loss.defvjp(_token_loss_fwd, _token_loss_bwd)


def _cols_from_devices(g):
    k = g.shape[1]
    return jnp.transpose(g, (1, 0, 2)).reshape(k, -1)


def _interleave_gate_up(a, f):
    tn = _pick(f, _TILE_FF)
    lead = a.shape[:-1]
    a = a.reshape(lead + (2, f // tn, tn))
    return jnp.swapaxes(a, -3, -2).reshape(lead + (2 * f,))


def _rope(x, cos, sin):
    half = x.shape[-1] // 2
    x1, x2 = x[..., :half], x[..., half:]
    return jnp.concatenate([x1 * cos - x2 * sin, x2 * cos + x1 * sin], axis=-1)


PROJ_BOUNDS = ((0, 512), (512, 1024), (1024, 2048), (2048, 3072), (3072, 4096), (4096, 5120), (5120, 5184),
               (5184, 5192))


def _layer(h, big, small, conv_w, l, l_real, cos, sin):
    L, d = h.shape
    w_in = _cols_from_devices(big["w_in"])
    w_in = jnp.concatenate([w_in[:, :1024], w_in[:, 1088:5184], w_in[:, 1024:1088], w_in[:, 5184:],
                            jnp.zeros((d, IN_COLS_PADDED - IN_COLS), w_in.dtype)], axis=1)
    w_q_up = _cols_from_devices(big["w_q_up"]).reshape(MLA_Q_LORA, HEADS, HEAD_DIM + ROPE_DIM)
    w_q_up = jnp.concatenate([w_q_up[:, :, :HEAD_DIM].reshape(MLA_Q_LORA, -1),
                              w_q_up[:, :, HEAD_DIM:].reshape(MLA_Q_LORA, -1)], axis=1)
    w_kv_up = _cols_from_devices(big["w_kv_up"]).reshape(MLA_KV_LORA, HEADS, 2 * HEAD_DIM)
    w_kv_up = jnp.concatenate([w_kv_up[:, :, :HEAD_DIM].reshape(MLA_KV_LORA, -1),
                               w_kv_up[:, :, HEAD_DIM:].reshape(MLA_KV_LORA, -1)], axis=1)
    w_out = big["w_out"].reshape(-1, d)
    f = big["w_ffn_down"].shape[0] * big["w_ffn_down"].shape[1]
    w_ffn_up = _interleave_gate_up(_cols_from_devices(big["w_ffn_up"]), f)
    w_ffn_down = big["w_ffn_down"].reshape(f, d)
    w_conv = _interleave_gate_up(conv_w, f)
    b_conv = _interleave_gate_up(small["b_ffn_conv"][l], f)

    hn = rms_norm(h, small["ln_mix_pre"][l])
    c_q, c_kv, fq, fk, fv, fg, k_rope, ff = split_cols(linear(hn, w_in), PROJ_BOUNDS)

    q = linear(rms_norm(c_q, small["g_q_latent"][l]), w_q_up)
    qn, qr = split_cols(q, ((0, HEADS * HEAD_DIM), (HEADS * HEAD_DIM, HEADS * (HEAD_DIM + ROPE_DIM))))
    kv = linear(rms_norm(c_kv, small["g_kv_latent"][l]), w_kv_up)
    kn, v = split_cols(kv, ((0, HEADS * HEAD_DIM), (HEADS * HEAD_DIM, 2 * HEADS * HEAD_DIM)))
    qr = jnp.transpose(_rope(qr.reshape(L, HEADS, ROPE_DIM), cos[:, None, :], sin[:, None, :]), (1, 0, 2))
    kr = _rope(k_rope, cos, sin)
    a = mla_attention(l_real, qn, qr, kn, kr, v)

    fqn = rms_norm(fq, small["g_fox_q"][l])
    fkn = rms_norm(fk, small["g_fox_k"][l])
    log_f = jax.nn.log_sigmoid(ff + small["b_forget"][l])
    c = jnp.cumsum(log_f, axis=0).T[:, :, None]
    bmix = fox_attention(l_real, fqn, fkn, fv, c) * jax.nn.sigmoid(fg)

    mix = linear(jnp.concatenate([a, bmix], axis=1), w_out)
    h = h + rms_norm(mix, small["ln_mix_post"][l])

    u = linear(rms_norm(h, small["ln_ffn_pre"][l]), w_ffn_up)
    act = conv_gate(u, w_conv, b_conv)
    h = h + rms_norm(linear(act, w_ffn_down), small["ln_ffn_post"][l])
    return h


def _local_loss(big, small, meta, conv_w, x, target):
    s, d = x.shape
    l_real = N_META + s
    l_pad = -(-l_real // Q_BLOCK) * Q_BLOCK
    h = jnp.concatenate([meta, x, jnp.zeros((l_pad - l_real, d), F32)], axis=0)
    half = ROPE_DIM // 2
    inv_freq = ROPE_THETA ** (-jnp.arange(half, dtype=F32) / half)
    ang = jnp.arange(l_pad, dtype=jnp.int32).astype(F32)[:, None] * inv_freq[None, :]
    cos, sin = jnp.cos(ang), jnp.sin(ang)
    for l in range(DEPTH):
        h = _layer(h, big[l], small, conv_w[l], l, l_real, cos, sin)
    return token_loss(h[N_META:l_real], target)


ANY_SPACE = pl.BlockSpec(memory_space=pl.ANY)


def _place():
    ix, iy, ic = lax.axis_index("x"), lax.axis_index("y"), lax.axis_index("c")
    return ix, iy, ic, [(1 - ix, iy), (ix, 1 - iy), (1 - ix, 1 - iy)]


def _comm_call(body, arrays, out_shapes, n_remote, n_local, name):
    return pl.pallas_call(
        body, out_shape=out_shapes, in_specs=[ANY_SPACE] * len(arrays), out_specs=[ANY_SPACE] * len(out_shapes),
        scratch_shapes=[pltpu.SemaphoreType.DMA((n_remote,)), pltpu.SemaphoreType.DMA((n_remote,)),
                        pltpu.SemaphoreType.DMA((n_local,))],
        name=name,
    )(*arrays)


def _gather(arrays, name):
    n = len(arrays)

    def body(*refs):
        xs, outs = refs[:n], refs[n:2 * n]
        send_sems, recv_sems, local_sems = refs[2 * n:]
        ix, iy, ic, chips = _place()
        me, sibling = (ix, iy, ic), (ix, iy, 1 - ic)

        def copy(a, k, block, to, src=None):
            dst = outs[a].at[4 * block[0] + 2 * block[1] + block[2]]
            return pltpu.make_async_remote_copy(
                src_ref=dst if src is None else src, dst_ref=dst, send_sem=send_sems.at[7 * a + k],
                recv_sem=recv_sems.at[7 * a + k], device_id=to, device_id_type=MESH_ID)

        local, sent = [], []
        for a in range(n):
            mine = pltpu.make_async_copy(xs[a], outs[a].at[4 * ix + 2 * iy + ic], local_sems.at[a])
            mine.start()
            local.append(mine)
            first = [copy(a, 0, me, sibling, src=xs[a])]
            first += [copy(a, 1 + j, me, (*chip, ic), src=xs[a]) for j, chip in enumerate(chips)]
            for cp in first:
                cp.start()
            sent += first
        for a in range(n):
            for j, chip in enumerate(chips):
                copy(a, 1 + j, (*chip, ic), me).wait_recv()
                passed = copy(a, 4 + j, (*chip, ic), sibling)
                passed.start()
                sent.append(passed)
        for a in range(n):
            copy(a, 0, sibling, me).wait_recv()
            for j, chip in enumerate(chips):
                copy(a, 4 + j, (*chip, 1 - ic), me).wait_recv()
        for cp in sent:
            cp.wait_send()
        for cp in local:
            cp.wait()

    out_shapes = [jax.ShapeDtypeStruct((N_DEV,) + a.shape, a.dtype) for a in arrays]
    return _comm_call(body, arrays, out_shapes, 7 * n, n, name)


def _swap_with_sibling(arrays, name):
    n = len(arrays)

    def body(*refs):
        xs, outs = refs[:n], refs[n:2 * n]
        send_sems, recv_sems, _ = refs[2 * n:]
        ix, iy, ic, _ = _place()
        copies = [pltpu.make_async_remote_copy(
            src_ref=xs[a], dst_ref=outs[a], send_sem=send_sems.at[a], recv_sem=recv_sems.at[a],
            device_id=(ix, iy, 1 - ic), device_id_type=MESH_ID) for a in range(n)]
        for cp in copies:
            cp.start()
        for cp in copies:
            cp.wait()

    out_shapes = [jax.ShapeDtypeStruct(a.shape, a.dtype) for a in arrays]
    return _comm_call(body, arrays, out_shapes, n, 1, name)


def _exchange_chips(arrays, name):
    n = len(arrays)

    def body(*refs):
        xs, outs = refs[:n], refs[n:2 * n]
        send_sems, recv_sems, local_sems = refs[2 * n:]
        ix, iy, ic, chips = _place()
        my_chip = 2 * ix + iy
        local, sent = [], []
        for a in range(n):
            mine = pltpu.make_async_copy(xs[a].at[my_chip], outs[a].at[my_chip], local_sems.at[a])
            mine.start()
            local.append(mine)
            for j, chip in enumerate(chips):
                cp = pltpu.make_async_remote_copy(
                    src_ref=xs[a].at[2 * chip[0] + chip[1]], dst_ref=outs[a].at[my_chip],
                    send_sem=send_sems.at[3 * a + j], recv_sem=recv_sems.at[3 * a + j],
                    device_id=(*chip, ic), device_id_type=MESH_ID)
                cp.start()
                sent.append(cp)
        for a in range(n):
            for j, chip in enumerate(chips):
                pltpu.make_async_remote_copy(
                    src_ref=xs[a].at[my_chip], dst_ref=outs[a].at[2 * chip[0] + chip[1]],
                    send_sem=send_sems.at[3 * a + j], recv_sem=recv_sems.at[3 * a + j],
                    device_id=(*chip, ic), device_id_type=MESH_ID).wait_recv()
        for cp in sent:
            cp.wait_send()
        for cp in local:
            cp.wait()

    out_shapes = [jax.ShapeDtypeStruct(a.shape, a.dtype) for a in arrays]
    return _comm_call(body, arrays, out_shapes, 3 * n, n, name)


def _sum_slots(x, out_dtype, name):
    slots, rows, cols = x.shape
    tr = _row_tile(rows, cols, (2 << 20) // slots, 16)

    def body(x_ref, o_ref):
        acc = x_ref[0].astype(F32)
        for s in range(1, slots):
            acc = acc + x_ref[s].astype(F32)
        o_ref[...] = acc.astype(o_ref.dtype)

    return pl.pallas_call(
        body, grid=(rows // tr,), in_specs=[pl.BlockSpec((slots, tr, cols), lambda i: (0, i, 0))],
        out_specs=pl.BlockSpec((tr, cols), lambda i: (i, 0)), out_shape=jax.ShapeDtypeStruct((rows, cols), out_dtype),
        compiler_params=_params("parallel"), name=name,
    )(x)


def _add_pairs(a, b, name):
    slots, rows, cols = a.shape
    tr = _row_tile(rows, cols, 1 << 20, 16)

    def body(a_ref, b_ref, o_ref):
        o_ref[...] = (a_ref[...].astype(F32) + b_ref[...].astype(F32)).astype(o_ref.dtype)

    spec = pl.BlockSpec((None, tr, cols), lambda s, i: (s, i, 0))
    return pl.pallas_call(
        body, grid=(slots, rows // tr), in_specs=[spec, spec], out_specs=spec,
        out_shape=jax.ShapeDtypeStruct(a.shape, BF16), compiler_params=_params("parallel", "parallel"), name=name,
    )(a, b)


def _reduce_scatter(grads, ic):
    by_chip = [g.reshape((4, 2) + g.shape[1:]) for g in grads]
    keep = [lax.dynamic_index_in_dim(g, ic, axis=1, keepdims=False) for g in by_chip]
    give = [lax.dynamic_index_in_dim(g, 1 - ic, axis=1, keepdims=False) for g in by_chip]
    got = _swap_with_sibling(give, "scatter_sibling")
    pairs = [_add_pairs(k, g, "add_pairs") for k, g in zip(keep, got)]
    received = _exchange_chips(pairs, "scatter_chips")
    return [_sum_slots(r, F32, "sum_grads") for r in received]


def _pack(arrays, dtype, row_multiple):
    flat = jnp.concatenate([a.astype(dtype).reshape(-1) for a in arrays])
    n = flat.shape[0]
    quantum = row_multiple * FLAT_COLS
    padded = -(-n // quantum) * quantum
    return jnp.pad(flat, (0, padded - n)).reshape(padded // FLAT_COLS, FLAT_COLS)


def _unpack(buf, shapes):
    flat = buf.reshape(-1)
    out, off = [], 0
    for shp in shapes:
        n = 1
        for s in shp:
            n *= s
        out.append(flat[off:off + n].reshape(tuple(shp)))
        off += n
    return out


def _adamw(w, g, m, v, name):
    shape = w.shape
    cols = shape[-1]
    w2, g2, m2, v2 = (a.reshape(-1, cols) for a in (w, g, m, v))
    rows = w2.shape[0]
    tr = _row_tile(rows, cols, 1 << 20)

    def body(w_ref, g_ref, m_ref, v_ref, d_ref, nm_ref, nv_ref):
        gv = g_ref[...]
        nm = ADAM_B1 * m_ref[...] + (1.0 - ADAM_B1) * gv
        nv = ADAM_B2 * v_ref[...] + (1.0 - ADAM_B2) * (gv * gv)
        m_hat = nm / (1.0 - ADAM_B1 ** ADAM_STEP)
        v_hat = nv / (1.0 - ADAM_B2 ** ADAM_STEP)
        d_ref[...] = -ADAM_LR * (m_hat / (jnp.sqrt(v_hat) + ADAM_EPS) + ADAM_WD * w_ref[...])
        nm_ref[...] = nm
        nv_ref[...] = nv

    spec = pl.BlockSpec((tr, cols), lambda i: (i, 0))
    outs = pl.pallas_call(
        body, grid=(rows // tr,), in_specs=[spec] * 4, out_specs=[spec] * 3,
        out_shape=[jax.ShapeDtypeStruct((rows, cols), F32)] * 3,
        compiler_params=_params("parallel"), name=name,
    )(w2, g2, m2, v2)
    return tuple(o.reshape(shape) for o in outs)


BIG = ("w_in", "w_q_up", "w_kv_up", "w_out", "w_ffn_up", "w_ffn_down")
REPLICATED = ("ln_mix_pre", "b_forget", "g_q_latent", "g_kv_latent", "g_fox_q", "g_fox_k", "ln_mix_post",
              "ln_ffn_pre", "b_ffn_conv", "ln_ffn_post")
WEIGHTS = ("meta_tokens", "ln_mix_pre", "w_in", "b_forget", "g_q_latent", "g_kv_latent", "w_q_up", "w_kv_up",
           "g_fox_q", "g_fox_k", "w_out", "ln_mix_post", "ln_ffn_pre", "w_ffn_up", "w_ffn_conv", "b_ffn_conv",
           "w_ffn_down", "ln_ffn_post")


def kernel(x, meta_tokens, ln_mix_pre, w_in, b_forget, g_q_latent, g_kv_latent, w_q_up, w_kv_up, g_fox_q, g_fox_k, w_out, ln_mix_post, ln_ffn_pre, w_ffn_up, w_ffn_conv, b_ffn_conv, w_ffn_down, ln_ffn_post, loss_target, m_meta_tokens, m_ln_mix_pre, m_w_in, m_b_forget, m_g_q_latent, m_g_kv_latent, m_w_q_up, m_w_kv_up, m_g_fox_q, m_g_fox_k, m_w_out, m_ln_mix_post, m_ln_ffn_pre, m_w_ffn_up, m_w_ffn_conv, m_b_ffn_conv, m_w_ffn_down, m_ln_ffn_post, v_meta_tokens, v_ln_mix_pre, v_w_in, v_b_forget, v_g_q_latent, v_g_kv_latent, v_w_q_up, v_w_kv_up, v_g_fox_q, v_g_fox_k, v_w_out, v_ln_mix_post, v_ln_ffn_pre, v_w_ffn_up, v_w_ffn_conv, v_b_ffn_conv, v_w_ffn_down, v_ln_ffn_post):
    w = dict(meta_tokens=meta_tokens, ln_mix_pre=ln_mix_pre, w_in=w_in, b_forget=b_forget, g_q_latent=g_q_latent,
             g_kv_latent=g_kv_latent, w_q_up=w_q_up, w_kv_up=w_kv_up, g_fox_q=g_fox_q, g_fox_k=g_fox_k, w_out=w_out,
             ln_mix_post=ln_mix_post, ln_ffn_pre=ln_ffn_pre, w_ffn_up=w_ffn_up, w_ffn_conv=w_ffn_conv,
             b_ffn_conv=b_ffn_conv, w_ffn_down=w_ffn_down, ln_ffn_post=ln_ffn_post)
    mom = dict(meta_tokens=m_meta_tokens, ln_mix_pre=m_ln_mix_pre, w_in=m_w_in, b_forget=m_b_forget,
               g_q_latent=m_g_q_latent, g_kv_latent=m_g_kv_latent, w_q_up=m_w_q_up, w_kv_up=m_w_kv_up,
               g_fox_q=m_g_fox_q, g_fox_k=m_g_fox_k, w_out=m_w_out, ln_mix_post=m_ln_mix_post,
               ln_ffn_pre=m_ln_ffn_pre, w_ffn_up=m_w_ffn_up, w_ffn_conv=m_w_ffn_conv, b_ffn_conv=m_b_ffn_conv,
               w_ffn_down=m_w_ffn_down, ln_ffn_post=m_ln_ffn_post)
    var = dict(meta_tokens=v_meta_tokens, ln_mix_pre=v_ln_mix_pre, w_in=v_w_in, b_forget=v_b_forget,
               g_q_latent=v_g_q_latent, g_kv_latent=v_g_kv_latent, w_q_up=v_w_q_up, w_kv_up=v_w_kv_up,
               g_fox_q=v_g_fox_q, g_fox_k=v_g_fox_k, w_out=v_w_out, ln_mix_post=v_ln_mix_post,
               ln_ffn_pre=v_ln_ffn_pre, w_ffn_up=v_w_ffn_up, w_ffn_conv=v_w_ffn_conv, b_ffn_conv=v_b_ffn_conv,
               w_ffn_down=v_w_ffn_down, ln_ffn_post=v_ln_ffn_post)
    ic = lax.axis_index("c")
    me = 4 * lax.axis_index("x") + 2 * lax.axis_index("y") + ic

    gathered = _gather([w[n].astype(BF16) for n in BIG] + [meta_tokens, w_ffn_conv], "gather_weights")
    big = [{n: gathered[k][:, l] for k, n in enumerate(BIG)} for l in range(DEPTH)]
    meta_shape, conv_shape = meta_tokens.shape, w_ffn_conv.shape
    meta_full = _cols_from_devices(gathered[len(BIG)])
    conv_full = jnp.transpose(gathered[len(BIG) + 1], (1, 2, 0, 3)).reshape(DEPTH, conv_shape[1], -1)
    small = {n: w[n] for n in REPLICATED}

    loss, grads = jax.value_and_grad(_local_loss, argnums=(0, 1, 2, 3, 4))(
        big, small, meta_full, [conv_full[l] for l in range(DEPTH)], x[0], loss_target[0])
    g_big, g_small, g_meta, g_conv, g_x = grads
    loss = lax.psum(loss, ("x", "y", "c"))

    grad = {}
    per_layer = [_reduce_scatter([g_big[l][n] for n in BIG], ic) for l in range(DEPTH)]
    for k, n in enumerate(BIG):
        grad[n] = jnp.stack([per_layer[l][k] for l in range(DEPTH)])

    small_arrays = [g_small[n] for n in REPLICATED] + [g_meta, jnp.stack(g_conv)]
    small_shapes = [a.shape for a in small_arrays]
    partials = _gather([_pack(small_arrays, F32, 16)], "gather_small_grads")[0]
    summed = _unpack(_sum_slots(partials, F32, "sum_small_grads"), small_shapes)
    for n, g in zip(REPLICATED, summed):
        grad[n] = g
    grad["meta_tokens"] = lax.dynamic_slice_in_dim(summed[-2], me * meta_shape[1], meta_shape[1], axis=1)
    grad["w_ffn_conv"] = lax.dynamic_slice_in_dim(summed[-1], me * conv_shape[2], conv_shape[2], axis=2)

    delta, new_m, new_v = {}, {}, {}
    for n in BIG:
        delta[n], new_m[n], new_v[n] = _adamw(w[n], grad[n], mom[n], var[n], "adamw_" + n)
    rest = [n for n in WEIGHTS if n not in BIG]
    rest_shapes = [w[n].shape for n in rest]
    flat = [_pack([src[n] for n in rest], F32, SUBLANES) for src in (w, grad, mom, var)]
    outs = _adamw(*flat, "adamw_small")
    for dst, buf in zip((delta, new_m, new_v), outs):
        for n, a in zip(rest, _unpack(buf, rest_shapes)):
            dst[n] = a

    return (loss, g_x[None], *[grad[n] for n in WEIGHTS], *[delta[n] for n in WEIGHTS],
            *[new_m[n] for n in WEIGHTS], *[new_v[n] for n in WEIGHTS])
```

```python
import functools

import jax
import jax.numpy as jnp
from jax import lax
from jax.experimental import pallas as pl
from jax.experimental.pallas import tpu as pltpu

F32 = jnp.float32
BF16 = jnp.bfloat16
MESH_ID = pl.DeviceIdType.MESH

N_DEV = 8
DEPTH = 4
N_META = 16
CHUNK = 64
Q_BLOCK = 128
HEADS = 8
HEAD_DIM = 128
ROPE_DIM = 64
MLA_Q_LORA = 512
MLA_KV_LORA = 512
FOX_W = HEADS * HEAD_DIM
ROPE_THETA = 10000.0
EPS = 1e-6
NEG = -1e30
IN_COLS = 5192
IN_COLS_PADDED = 5376

ADAM_LR = 0.001
ADAM_B1 = 0.9
ADAM_B2 = 0.999
ADAM_EPS = 1e-08
ADAM_WD = 0.01
ADAM_STEP = 10

LANES = 128
SUBLANES = 8
FLAT_COLS = 1024
VMEM_LIMIT_V7X = 52 * 1024 * 1024

NT_DIMS = (((1,), (1,)), ((), ()))
NN_DIMS = (((1,), (0,)), ((), ()))
TN_DIMS = (((0,), (0,)), ((), ()))

_TILE_ATT = (640, 512, 384, 256, 128)
_TILE_FF = (512, 256, 128)


def _pick(n, candidates):
    for c in candidates:
        if n % c == 0:
            return c
    return n


def _row_tile(rows, cols, budget_bytes=2 << 20, align=SUBLANES):
    best = None
    for t in range(align, rows + 1, align):
        if rows % t == 0 and t * cols * 4 <= budget_bytes:
            best = t
    return best if best is not None else rows


def _params(*semantics):
    return pltpu.CompilerParams(dimension_semantics=semantics, vmem_limit_bytes=VMEM_LIMIT_V7X)


MATMUL_VMEM_BUDGET = 36 << 20
MXU_FLOPS_V7X = 9.0e14
HBM_BYTES_PER_S_V7X = 2.5e12
GRID_STEP_S = 0.35e-6
MXU_DIM = 256


def _tile_candidates(n, cap):
    c = [t for t in range(LANES, min(n, cap) + 1, LANES) if n % t == 0]
    return c if c else [n]


def _matmul_tiles(m, n, c, a_bytes, b_bytes, o_bytes):
    best, best_cost = None, None
    for tc in _tile_candidates(c, 4096):
        steps = c // tc
        for tm in _tile_candidates(m, 2048):
            for tn in _tile_candidates(n, 2048):
                vmem = 2 * (tm * tc * a_bytes + tc * tn * b_bytes + tm * tn * o_bytes)
                vmem += tm * tn * 4 if steps > 1 else 0
                if vmem > MATMUL_VMEM_BUDGET:
                    continue
                traffic = m * c * a_bytes * (1 if steps == 1 else n // tn) + c * n * b_bytes * (m // tm)
                traffic += m * n * o_bytes
                grid = (m // tm) * (n // tn) * steps
                accumulate = 0 if steps == 1 else grid * tm * tn * 8 / 4.0e12
                fill = (-(-tn // MXU_DIM) * MXU_DIM / tn) * (-(-tc // MXU_DIM) * MXU_DIM / tc)
                cost = max(2.0 * m * n * c * fill / MXU_FLOPS_V7X, traffic / HBM_BYTES_PER_S_V7X)
                cost += grid * GRID_STEP_S + accumulate
                if best_cost is None or cost < best_cost:
                    best, best_cost = (tm, tn, tc), cost
    return best


def _matmul(a, b, mode, out_dtype, name):
    if mode == "nn":
        (m, c), (c2, n) = a.shape, b.shape
    elif mode == "nt":
        (m, c), (n, c2) = a.shape, b.shape
    else:
        (c, m), (c2, n) = a.shape, b.shape
    assert c == c2, (a.shape, b.shape, mode)
    tm, tn, tc = _matmul_tiles(m, n, c, a.dtype.itemsize, b.dtype.itemsize, jnp.dtype(out_dtype).itemsize)
    steps = c // tc
    if mode == "nn":
        a_spec = pl.BlockSpec((tm, tc), lambda i, j, k: (i, k))
        b_spec = pl.BlockSpec((tc, tn), lambda i, j, k: (k, j))
        dims = NN_DIMS
    elif mode == "nt":
        a_spec = pl.BlockSpec((tm, tc), lambda i, j, k: (i, k))
        b_spec = pl.BlockSpec((tn, tc), lambda i, j, k: (j, k))
        dims = NT_DIMS
    else:
        a_spec = pl.BlockSpec((tc, tm), lambda i, j, k: (k, i))
        b_spec = pl.BlockSpec((tc, tn), lambda i, j, k: (k, j))
        dims = TN_DIMS

    def body(a_ref, b_ref, o_ref, acc_ref):
        k = pl.program_id(2)

        @pl.when(k == 0)
        def _():
            acc_ref[...] = jnp.zeros_like(acc_ref)

        acc_ref[...] += lax.dot_general(a_ref[...].astype(BF16), b_ref[...].astype(BF16), dims,
                                        preferred_element_type=F32)

        @pl.when(k == steps - 1)
        def _():
            o_ref[...] = acc_ref[...].astype(o_ref.dtype)

    def body_whole(a_ref, b_ref, o_ref):
        o_ref[...] = lax.dot_general(a_ref[...].astype(BF16), b_ref[...].astype(BF16), dims,
                                     preferred_element_type=F32).astype(o_ref.dtype)

    return pl.pallas_call(
        body if steps > 1 else body_whole, grid=(m // tm, n // tn, steps), in_specs=[a_spec, b_spec],
        out_specs=pl.BlockSpec((tm, tn), lambda i, j, k: (i, j)),
        out_shape=jax.ShapeDtypeStruct((m, n), out_dtype),
        scratch_shapes=[pltpu.VMEM((tm, tn), F32)] if steps > 1 else [],
        compiler_params=_params("parallel", "parallel", "arbitrary"), name=name,
    )(a, b)


@jax.custom_vjp
def linear(x, w):
    return _matmul(x, w, "nn", F32, "linear_fwd")


def _linear_fwd(x, w):
    return _matmul(x, w, "nn", F32, "linear_fwd"), (x, w)


def _linear_bwd(res, dy):
    x, w = res
    dx = _matmul(dy, w, "nt", F32, "linear_dx")
    dw = _matmul(x, dy, "tn", w.dtype, "linear_dw")
    return dx, dw


linear.defvjp(_linear_fwd, _linear_bwd)


def _rms_forward(x, g):
    rows, d = x.shape
    gd = g.shape[0]
    tr = _row_tile(rows, d)

    def body(x_ref, g_ref, y_ref):
        for c0 in range(0, d, gd):
            xv = x_ref[:, c0:c0 + gd]
            r = lax.rsqrt(jnp.mean(xv * xv, axis=-1, keepdims=True) + EPS)
            y_ref[:, c0:c0 + gd] = (xv * r) * g_ref[...]

    return pl.pallas_call(
        body, grid=(rows // tr,),
        in_specs=[pl.BlockSpec((tr, d), lambda i: (i, 0)), pl.BlockSpec((1, gd), lambda i: (0, 0))],
        out_specs=pl.BlockSpec((tr, d), lambda i: (i, 0)),
        out_shape=jax.ShapeDtypeStruct((rows, d), F32),
        compiler_params=_params("parallel"), name="rmsnorm_fwd",
    )(x, g.reshape(1, gd))


def _rms_backward(x, g, dy):
    rows, d = x.shape
    gd = g.shape[0]
    tr = _row_tile(rows, d)

    def body(x_ref, g_ref, dy_ref, dx_ref, dg_ref):
        i = pl.program_id(0)

        @pl.when(i == 0)
        def _():
            dg_ref[...] = jnp.zeros_like(dg_ref)

        for c0 in range(0, d, gd):
            xv = x_ref[:, c0:c0 + gd]
            dyv = dy_ref[:, c0:c0 + gd]
            r = lax.rsqrt(jnp.mean(xv * xv, axis=-1, keepdims=True) + EPS)
            xh = xv * r
            t = dyv * g_ref[...]
            dx_ref[:, c0:c0 + gd] = r * (t - xh * jnp.mean(t * xh, axis=-1, keepdims=True))
            dg_ref[...] += jnp.sum(dyv * xh, axis=0, keepdims=True)

    dx, dg = pl.pallas_call(
        body, grid=(rows // tr,),
        in_specs=[pl.BlockSpec((tr, d), lambda i: (i, 0)), pl.BlockSpec((1, gd), lambda i: (0, 0)),
                  pl.BlockSpec((tr, d), lambda i: (i, 0))],
        out_specs=[pl.BlockSpec((tr, d), lambda i: (i, 0)), pl.BlockSpec((1, gd), lambda i: (0, 0))],
        out_shape=[jax.ShapeDtypeStruct((rows, d), F32), jax.ShapeDtypeStruct((1, gd), F32)],
        compiler_params=_params("arbitrary"), name="rmsnorm_bwd",
    )(x, g.reshape(1, gd), dy)
    return dx, dg.reshape(g.shape)


@jax.custom_vjp
def rms_norm(x, g):
    return _rms_forward(x, g)


def _rms_norm_fwd(x, g):
    return _rms_forward(x, g), (x, g)


def _rms_norm_bwd(res, dy):
    x, g = res
    return _rms_backward(x, g, dy)


rms_norm.defvjp(_rms_norm_fwd, _rms_norm_bwd)


@functools.partial(jax.custom_vjp, nondiff_argnums=(1,))
def split_cols(x, bounds):
    return tuple(x[:, lo:hi] for lo, hi in bounds)


def _split_cols_fwd(x, bounds):
    return split_cols(x, bounds), x.shape[1]


def _split_cols_bwd(bounds, width, cts):
    parts = list(cts)
    tail = width - bounds[-1][1]
    if tail:
        parts.append(jnp.zeros((parts[0].shape[0], tail), parts[0].dtype))
    return (jnp.concatenate(parts, axis=1),)


split_cols.defvjp(_split_cols_fwd, _split_cols_bwd)


def _visibility_id(pos, kind, l_real):
    if kind == "fox":
        return pos
    pad_chunk = 2 + (l_real - N_META) // CHUNK
    frame_chunk = 1 + jnp.right_shift(pos - N_META, 6)
    return jnp.where(pos < N_META, 0, jnp.where(pos < l_real, frame_chunk, pad_chunk))


def _raw_scores(kind, a1, b1, a2, b2):
    s = lax.dot_general(a1.astype(BF16), b1.astype(BF16), NT_DIMS, preferred_element_type=F32)
    if kind == "mla":
        s = s + lax.dot_general(a2.astype(BF16), b2.astype(BF16), NT_DIMS, preferred_element_type=F32)
    return s


def _block_pairs(nb, look, by_key):
    outer, inner = [], []
    for a in range(nb):
        rng = range(max(a - look, 0), nb) if by_key else range(0, min(a + look, nb - 1) + 1)
        for b in rng:
            outer.append(a)
            inner.append(b)
    return jnp.asarray(outer, jnp.int32), jnp.asarray(inner, jnp.int32)


def _attention_call(body, n_pairs, in_specs, out_specs, out_shape, scratch, name, tables, operands):
    return pl.pallas_call(
        body,
        grid_spec=pltpu.PrefetchScalarGridSpec(num_scalar_prefetch=2, grid=(HEADS, n_pairs), in_specs=in_specs,
                                               out_specs=out_specs, scratch_shapes=scratch),
        out_shape=out_shape, compiler_params=_params("parallel", "arbitrary"), name=name,
    )(*tables, *operands)


def _attention_forward(kind, l_real, q1, k1, v, extra_q, extra_k):
    L = q1.shape[0]
    T = _pick(L, _TILE_ATT)
    nb = L // T
    look = 1 if kind == "mla" else 0
    scale = (HEAD_DIM + ROPE_DIM) ** -0.5 if kind == "mla" else HEAD_DIM ** -0.5
    tables = _block_pairs(nb, look, False)

    q_tile = pl.BlockSpec((T, HEAD_DIM), lambda h, t, it, jt: (it[t], h))
    k_tile = pl.BlockSpec((T, HEAD_DIM), lambda h, t, it, jt: (jt[t], h))
    row_stat = pl.BlockSpec((None, T, 1), lambda h, t, it, jt: (h, it[t], 0))
    if kind == "mla":
        extra_specs = [pl.BlockSpec((None, T, ROPE_DIM), lambda h, t, it, jt: (h, it[t], 0)),
                       pl.BlockSpec((T, ROPE_DIM), lambda h, t, it, jt: (jt[t], 0))]
    else:
        extra_specs = [row_stat, pl.BlockSpec((None, 1, T), lambda h, t, it, jt: (h, 0, jt[t]))]

    def body(it_ref, jt_ref, q1_ref, k1_ref, v_ref, eq_ref, ek_ref, o_ref, lse_ref, m_ref, l_ref, acc_ref):
        t = pl.program_id(1)
        i, j = it_ref[t], jt_ref[t]

        @pl.when(j == 0)
        def _():
            m_ref[...] = jnp.full_like(m_ref, NEG)
            l_ref[...] = jnp.zeros_like(l_ref)
            acc_ref[...] = jnp.zeros_like(acc_ref)

        def block(masked):
            s = _raw_scores(kind, q1_ref[...], k1_ref[...], eq_ref[...] if kind == "mla" else None,
                            ek_ref[...] if kind == "mla" else None) * scale
            if kind == "fox":
                s = s + eq_ref[...] - ek_ref[...]
            if masked:
                pos_q = i * T + lax.broadcasted_iota(jnp.int32, (T, 1), 0)
                pos_k = j * T + lax.broadcasted_iota(jnp.int32, (1, T), 1)
                s = jnp.where(_visibility_id(pos_k, kind, l_real) <= _visibility_id(pos_q, kind, l_real), s, NEG)
            m_prev = m_ref[...]
            m_new = jnp.maximum(m_prev, jnp.max(s, axis=1, keepdims=True))
            alpha = jnp.exp(m_prev - m_new)
            p = jnp.exp(s - m_new)
            l_ref[...] = alpha * l_ref[...] + jnp.sum(p, axis=1, keepdims=True)
            m_ref[...] = m_new
            vb = v_ref[...].astype(BF16)
            p_hi = p.astype(BF16)
            pv = lax.dot_general(p_hi, vb, NN_DIMS, preferred_element_type=F32)
            if kind == "fox":
                p_lo = (p - p_hi.astype(F32)).astype(BF16)
                pv = pv + lax.dot_general(p_lo, vb, NN_DIMS, preferred_element_type=F32)
            acc_ref[...] = alpha * acc_ref[...] + pv

        @pl.when(j < i)
        def _():
            block(False)

        @pl.when(j >= i)
        def _():
            block(True)

        @pl.when(j == jnp.minimum(i + look, nb - 1))
        def _():
            o_ref[...] = acc_ref[...] / l_ref[...]
            lse_ref[...] = m_ref[...] + jnp.log(l_ref[...])

    return _attention_call(
        body, tables[0].shape[0], [q_tile, k_tile, k_tile] + extra_specs, [q_tile, row_stat],
        [jax.ShapeDtypeStruct((L, HEADS * HEAD_DIM), F32), jax.ShapeDtypeStruct((HEADS, L, 1), F32)],
        [pltpu.VMEM((T, 1), F32), pltpu.VMEM((T, 1), F32), pltpu.VMEM((T, HEAD_DIM), F32)],
        kind + "_attn_fwd", tables, (q1, k1, v, extra_q, extra_k))


def _attention_delta(o, do):
    L = o.shape[0]
    T = _pick(L, _TILE_ATT)
    tile = pl.BlockSpec((T, HEAD_DIM), lambda i, h: (i, h))

    def body(o_ref, do_ref, delta_ref):
        delta_ref[...] = jnp.sum(do_ref[...].astype(BF16).astype(F32) * o_ref[...], axis=1, keepdims=True)

    return pl.pallas_call(
        body, grid=(L // T, HEADS), in_specs=[tile, tile],
        out_specs=pl.BlockSpec((None, T, 1), lambda i, h: (h, i, 0)),
        out_shape=jax.ShapeDtypeStruct((HEADS, L, 1), F32),
        compiler_params=_params("parallel", "parallel"), name="attn_delta",
    )(o, do)


def _attention_backward(kind, l_real, q1, k1, v, extra_q, extra_k, do, lse_row, delta_row):
    L = q1.shape[0]
    T = _pick(L, _TILE_ATT)
    nb = L // T
    look = 1 if kind == "mla" else 0
    scale = (HEAD_DIM + ROPE_DIM) ** -0.5 if kind == "mla" else HEAD_DIM ** -0.5
    tables = _block_pairs(nb, look, True)

    k_tile = pl.BlockSpec((T, HEAD_DIM), lambda h, t, jt, it: (jt[t], h))
    q_tile = pl.BlockSpec((T, HEAD_DIM), lambda h, t, jt, it: (it[t], h))
    q_row = pl.BlockSpec((None, 1, T), lambda h, t, jt, it: (h, 0, it[t]))
    if kind == "mla":
        extra_specs = [pl.BlockSpec((None, T, ROPE_DIM), lambda h, t, jt, it: (h, it[t], 0)),
                       pl.BlockSpec((T, ROPE_DIM), lambda h, t, jt, it: (jt[t], 0))]
        third_spec = pl.BlockSpec((None, T, ROPE_DIM), lambda h, t, jt, it: (h, jt[t], 0))
        third_shape = jax.ShapeDtypeStruct((HEADS, L, ROPE_DIM), F32)
        third_scratch = pltpu.VMEM((T, ROPE_DIM), F32)
    else:
        extra_specs = [q_row, pl.BlockSpec((None, T, 1), lambda h, t, jt, it: (h, jt[t], 0))]
        third_spec = pl.BlockSpec((None, T, 1), lambda h, t, jt, it: (h, jt[t], 0))
        third_shape = jax.ShapeDtypeStruct((HEADS, L, 1), F32)
        third_scratch = pltpu.VMEM((T, 1), F32)
    in_specs = [q_tile, k_tile, k_tile] + extra_specs + [q_tile, q_row, q_row]
    n_pairs = tables[0].shape[0]
    out_specs = [k_tile, k_tile, third_spec, pl.BlockSpec((L, HEAD_DIM), lambda h, t, jt, it: (0, h))]
    out_shape = [jax.ShapeDtypeStruct((L, HEADS * HEAD_DIM), F32), jax.ShapeDtypeStruct((L, HEADS * HEAD_DIM), F32),
                 third_shape, jax.ShapeDtypeStruct((L, HEADS * HEAD_DIM), F32)]
    if kind == "mla":
        out_specs.append(pl.BlockSpec((None, L, ROPE_DIM), lambda h, t, jt, it: (h, 0, 0)))
        out_shape.append(jax.ShapeDtypeStruct((HEADS, L, ROPE_DIM), F32))

    def body(jt_ref, it_ref, q1_ref, k1_ref, v_ref, eq_ref, ek_ref, do_ref, lse_ref, delta_ref,
             dk1_ref, dv_ref, third_ref, dq1_ref, *rest):
        if kind == "mla":
            dq2_ref, acck_ref, accv_ref, acc3_ref = rest
        else:
            acck_ref, accv_ref, acc3_ref = rest
        t = pl.program_id(1)
        j, i = jt_ref[t], it_ref[t]
        q_rows = pl.ds(pl.multiple_of(i * T, T), T)

        @pl.when(t == 0)
        def _():
            dq1_ref[...] = jnp.zeros_like(dq1_ref)
            if kind == "mla":
                dq2_ref[...] = jnp.zeros_like(dq2_ref)

        @pl.when(i == jnp.maximum(j - look, 0))
        def _():
            acck_ref[...] = jnp.zeros_like(acck_ref)
            accv_ref[...] = jnp.zeros_like(accv_ref)
            acc3_ref[...] = jnp.zeros_like(acc3_ref)

        def block(masked):
            if kind == "mla":
                kb = jnp.concatenate([k1_ref[...].astype(BF16), ek_ref[...].astype(BF16)], axis=1)
                qb = jnp.concatenate([q1_ref[...].astype(BF16), eq_ref[...].astype(BF16)], axis=1)
            else:
                kb, qb = k1_ref[...].astype(BF16), q1_ref[...].astype(BF16)
            st = lax.dot_general(kb, qb, NT_DIMS, preferred_element_type=F32) * scale
            if kind == "fox":
                st = st + eq_ref[...] - ek_ref[...]
            if masked:
                pos_k = j * T + lax.broadcasted_iota(jnp.int32, (T, 1), 0)
                pos_q = i * T + lax.broadcasted_iota(jnp.int32, (1, T), 1)
                st = jnp.where(_visibility_id(pos_k, kind, l_real) <= _visibility_id(pos_q, kind, l_real), st, NEG)
            pt = jnp.exp(st - lse_ref[...])
            dob = do_ref[...].astype(BF16)
            accv_ref[...] += lax.dot_general(pt.astype(BF16), dob, NN_DIMS, preferred_element_type=F32)
            dpt = lax.dot_general(v_ref[...].astype(BF16), dob, NT_DIMS, preferred_element_type=F32)
            dst = pt * (dpt - delta_ref[...])
            dsb = dst.astype(BF16)
            dk = lax.dot_general(dsb, qb, NN_DIMS, preferred_element_type=F32)
            dq = lax.dot_general(dsb, kb, TN_DIMS, preferred_element_type=F32)
            if kind == "mla":
                acck_ref[...] += dk[:, :HEAD_DIM]
                acc3_ref[...] += dk[:, HEAD_DIM:]
                dq1_ref[q_rows, :] += dq[:, :HEAD_DIM]
                dq2_ref[q_rows, :] += dq[:, HEAD_DIM:]
            else:
                acck_ref[...] += dk
                dq1_ref[q_rows, :] += dq
                acc3_ref[...] -= jnp.sum(dst, axis=1, keepdims=True)

        @pl.when(i > j)
        def _():
            block(False)

        @pl.when(i <= j)
        def _():
            block(True)

        @pl.when(i == nb - 1)
        def _():
            dk1_ref[...] = acck_ref[...] * scale
            dv_ref[...] = accv_ref[...]
            third_ref[...] = acc3_ref[...] * scale if kind == "mla" else acc3_ref[...]

        @pl.when(t == n_pairs - 1)
        def _():
            dq1_ref[...] = dq1_ref[...] * scale
            if kind == "mla":
                dq2_ref[...] = dq2_ref[...] * scale

    return _attention_call(
        body, n_pairs, in_specs, out_specs, out_shape,
        [pltpu.VMEM((T, HEAD_DIM), F32), pltpu.VMEM((T, HEAD_DIM), F32), third_scratch],
        kind + "_attn_bwd", tables, (q1, k1, v, extra_q, extra_k, do, lse_row, delta_row))


def _as_row(col):
    return col.reshape(col.shape[0], 1, col.shape[1])


@functools.partial(jax.custom_vjp, nondiff_argnums=(0,))
def mla_attention(l_real, qn, qr, kn, kr, v):
    return _attention_forward("mla", l_real, qn, kn, v, qr, kr)[0]


def _mla_attention_fwd(l_real, qn, qr, kn, kr, v):
    o, lse = _attention_forward("mla", l_real, qn, kn, v, qr, kr)
    return o, (qn, qr, kn, kr, v, o, lse)


def _mla_attention_bwd(l_real, res, do):
    qn, qr, kn, kr, v, o, lse = res
    delta = _attention_delta(o, do)
    dkn, dv, dkr_heads, dqn, dqr = _attention_backward("mla", l_real, qn, kn, v, qr, kr, do, _as_row(lse),
                                                       _as_row(delta))
    return dqn, dqr, dkn, jnp.sum(dkr_heads, axis=0), dv


mla_attention.defvjp(_mla_attention_fwd, _mla_attention_bwd)


@functools.partial(jax.custom_vjp, nondiff_argnums=(0,))
def fox_attention(l_real, q, k, v, c):
    return _attention_forward("fox", l_real, q, k, v, c, _as_row(c))[0]


def _fox_attention_fwd(l_real, q, k, v, c):
    o, lse = _attention_forward("fox", l_real, q, k, v, c, _as_row(c))
    return o, (q, k, v, c, o, lse)


def _fox_attention_bwd(l_real, res, do):
    q, k, v, c, o, lse = res
    delta = _attention_delta(o, do)
    dk, dv, dc, dq = _attention_backward("fox", l_real, q, k, v, _as_row(c), c, do, _as_row(lse), _as_row(delta))
    return dq, dk, dv, dc


fox_attention.defvjp(_fox_attention_fwd, _fox_attention_bwd)


GELU_C0 = 0.7978845608028654
GELU_C1 = 0.044715


def _shift_rows(x, prev, s):
    r = pltpu.roll(x, s, 0)
    pr = pltpu.roll(prev, s, 0)
    row = lax.broadcasted_iota(jnp.int32, prev.shape, 0)
    top = jnp.where(row < s, pr, r[0:SUBLANES])
    return jnp.concatenate([top, r[SUBLANES:]], axis=0)


def _conv_tiles(L, f):
    return _pick(L, _TILE_ATT), _pick(f, _TILE_FF)


def _conv_gate_forward(u, w, b):
    L, f2 = u.shape
    f = f2 // 2
    tm, tn = _conv_tiles(L, f)
    rb = tm // SUBLANES

    def body(u_ref, up_ref, w_ref, b_ref, o_ref):
        i = pl.program_id(1)
        x = u_ref[...]
        prev = jnp.where(i > 0, up_ref[...], 0.0)
        wv = w_ref[...]
        hc = b_ref[...] + ((wv[0:1] * _shift_rows(x, prev, 2) + wv[1:2] * _shift_rows(x, prev, 1)) + wv[2:3] * x)
        g = hc[:, :tn]
        gelu = 0.5 * g * (1.0 + jnp.tanh(GELU_C0 * (g + GELU_C1 * g * g * g)))
        o_ref[...] = gelu * hc[:, tn:]

    return pl.pallas_call(
        body, grid=(f // tn, L // tm),
        in_specs=[pl.BlockSpec((tm, 2 * tn), lambda j, i: (i, j)),
                  pl.BlockSpec((SUBLANES, 2 * tn), lambda j, i: (jnp.maximum(i * rb - 1, 0), j)),
                  pl.BlockSpec((3, 2 * tn), lambda j, i: (0, j)),
                  pl.BlockSpec((1, 2 * tn), lambda j, i: (0, j))],
        out_specs=pl.BlockSpec((tm, tn), lambda j, i: (i, j)),
        out_shape=jax.ShapeDtypeStruct((L, f), F32),
        compiler_params=_params("parallel", "parallel"), name="conv_gate_fwd",
    )(u, u, w, b)


def _conv_gate_backward(u, w, b, dact):
    L, f2 = u.shape
    f = f2 // 2
    tm, tn = _conv_tiles(L, f)
    rb = tm // SUBLANES
    n_row_blocks = L // SUBLANES
    n_i = L // tm
    ext = tm + SUBLANES

    def next_rows(i):
        return jnp.minimum((i + 1) * rb, n_row_blocks - 1)

    def body(u_ref, up_ref, un_ref, da_ref, dan_ref, w_ref, b_ref, du_ref, dwb_ref):
        i = pl.program_id(1)
        is_last = i == n_i - 1
        prev = jnp.where(i > 0, up_ref[...], 0.0)
        xe = jnp.concatenate([u_ref[...], jnp.where(is_last, 0.0, un_ref[...])], axis=0)
        x1 = _shift_rows(xe, prev, 1)
        x2 = _shift_rows(xe, prev, 2)
        wv = w_ref[...]
        hc = b_ref[...] + ((wv[0:1] * x2 + wv[1:2] * x1) + wv[2:3] * xe)
        g, up = hc[:, :tn], hc[:, tn:]
        da = jnp.concatenate([da_ref[...], jnp.where(is_last, 0.0, dan_ref[...])], axis=0)
        t = jnp.tanh(GELU_C0 * (g + GELU_C1 * g * g * g))
        gelu = 0.5 * g * (1.0 + t)
        dgelu = 0.5 * (1.0 + t) + 0.5 * g * (1.0 - t * t) * (GELU_C0 * (1.0 + 3.0 * GELU_C1 * g * g))
        dh = jnp.concatenate([da * up * dgelu, da * gelu], axis=1)
        dh1 = pltpu.roll(dh, ext - 1, 0)
        dh2 = pltpu.roll(dh, ext - 2, 0)
        du_ref[...] = ((wv[2:3] * dh + wv[1:2] * dh1) + wv[0:1] * dh2)[:tm]
        dw0 = jnp.sum((dh * x2)[:tm], axis=0, keepdims=True)
        dw1 = jnp.sum((dh * x1)[:tm], axis=0, keepdims=True)
        dw2 = jnp.sum((dh * xe)[:tm], axis=0, keepdims=True)
        db = jnp.sum(dh[:tm], axis=0, keepdims=True)
        row = lax.broadcasted_iota(jnp.int32, (SUBLANES, 2 * tn), 0)
        upd = jnp.where(row == 0, dw0, jnp.where(row == 1, dw1, jnp.where(row == 2, dw2,
                        jnp.where(row == 3, db, 0.0))))

        @pl.when(i == 0)
        def _():
            dwb_ref[...] = jnp.zeros_like(dwb_ref)

        dwb_ref[...] += upd

    return pl.pallas_call(
        body, grid=(f // tn, n_i),
        in_specs=[pl.BlockSpec((tm, 2 * tn), lambda j, i: (i, j)),
                  pl.BlockSpec((SUBLANES, 2 * tn), lambda j, i: (jnp.maximum(i * rb - 1, 0), j)),
                  pl.BlockSpec((SUBLANES, 2 * tn), lambda j, i: (next_rows(i), j)),
                  pl.BlockSpec((tm, tn), lambda j, i: (i, j)),
                  pl.BlockSpec((SUBLANES, tn), lambda j, i: (next_rows(i), j)),
                  pl.BlockSpec((3, 2 * tn), lambda j, i: (0, j)),
                  pl.BlockSpec((1, 2 * tn), lambda j, i: (0, j))],
        out_specs=[pl.BlockSpec((tm, 2 * tn), lambda j, i: (i, j)),
                   pl.BlockSpec((SUBLANES, 2 * tn), lambda j, i: (0, j))],
        out_shape=[jax.ShapeDtypeStruct((L, f2), F32), jax.ShapeDtypeStruct((SUBLANES, f2), F32)],
        compiler_params=_params("parallel", "arbitrary"), name="conv_gate_bwd",
    )(u, u, u, dact, dact, w, b)


@jax.custom_vjp
def conv_gate(u, w, b):
    return _conv_gate_forward(u, w, b.reshape(1, -1))


def _conv_gate_fwd(u, w, b):
    return _conv_gate_forward(u, w, b.reshape(1, -1)), (u, w, b)


def _conv_gate_bwd(res, dact):
    u, w, b = res
    du, dwb = _conv_gate_backward(u, w, b.reshape(1, -1), dact)
    return du, dwb[0:3], dwb[3]


conv_gate.defvjp(_conv_gate_fwd, _conv_gate_bwd)


def _loss_rows(y, target):
    rows, d = y.shape
    tr = _row_tile(rows, d)

    def body(y_ref, t_ref, loss_ref, dy_ref):
        err = y_ref[...] - t_ref[...]
        loss_ref[...] = 0.5 * jnp.mean(err * err, axis=-1, keepdims=True)
        dy_ref[...] = err * (1.0 / d)

    return pl.pallas_call(
        body, grid=(rows // tr,),
        in_specs=[pl.BlockSpec((tr, d), lambda i: (i, 0)), pl.BlockSpec((tr, d), lambda i: (i, 0))],
        out_specs=[pl.BlockSpec((tr, 1), lambda i: (i, 0)), pl.BlockSpec((tr, d), lambda i: (i, 0))],
        out_shape=[jax.ShapeDtypeStruct((rows, 1), F32), jax.ShapeDtypeStruct((rows, d), F32)],
        compiler_params=_params("parallel"), name="loss_head",
    )(y, target)


@jax.custom_vjp
def token_loss(y, target):
    return jnp.sum(_loss_rows(y, target)[0])


def _token_loss_fwd(y, target):
    rows, dy = _loss_rows(y, target)
    return jnp.sum(rows), dy


def _token_loss_bwd(dy, ct):
    return ct * dy, -ct * dy


token_loss.defvjp(_token_loss_fwd, _token_loss_bwd)


def _cols_from_devices(g):
    k = g.shape[1]
    return jnp.transpose(g, (1, 0, 2)).reshape(k, -1)


def _interleave_gate_up(a, f):
    tn = _pick(f, _TILE_FF)
    lead = a.shape[:-1]
    a = a.reshape(lead + (2, f // tn, tn))
    return jnp.swapaxes(a, -3, -2).reshape(lead + (2 * f,))


def _rope(x, cos, sin):
    half = x.shape[-1] // 2
    x1, x2 = x[..., :half], x[..., half:]
    return jnp.concatenate([x1 * cos - x2 * sin, x2 * cos + x1 * sin], axis=-1)


PROJ_BOUNDS = ((0, 512), (512, 1024), (1024, 2048), (2048, 3072), (3072, 4096), (4096, 5120), (5120, 5184),
               (5184, 5192))


def _layer(h, big, small, conv_w, l, l_real, cos, sin):
    L, d = h.shape
    w_in = _cols_from_devices(big["w_in"])
    w_in = jnp.concatenate([w_in[:, :1024], w_in[:, 1088:5184], w_in[:, 1024:1088], w_in[:, 5184:],
                            jnp.zeros((d, IN_COLS_PADDED - IN_COLS), w_in.dtype)], axis=1)
    w_q_up = _cols_from_devices(big["w_q_up"]).reshape(MLA_Q_LORA, HEADS, HEAD_DIM + ROPE_DIM)
    w_q_up = jnp.concatenate([w_q_up[:, :, :HEAD_DIM].reshape(MLA_Q_LORA, -1),
                              w_q_up[:, :, HEAD_DIM:].reshape(MLA_Q_LORA, -1)], axis=1)
    w_kv_up = _cols_from_devices(big["w_kv_up"]).reshape(MLA_KV_LORA, HEADS, 2 * HEAD_DIM)
    w_kv_up = jnp.concatenate([w_kv_up[:, :, :HEAD_DIM].reshape(MLA_KV_LORA, -1),
                               w_kv_up[:, :, HEAD_DIM:].reshape(MLA_KV_LORA, -1)], axis=1)
    w_out = big["w_out"].reshape(-1, d)
    f = big["w_ffn_down"].shape[0] * big["w_ffn_down"].shape[1]
    w_ffn_up = _interleave_gate_up(_cols_from_devices(big["w_ffn_up"]), f)
    w_ffn_down = big["w_ffn_down"].reshape(f, d)
    w_conv = _interleave_gate_up(conv_w, f)
    b_conv = _interleave_gate_up(small["b_ffn_conv"][l], f)

    hn = rms_norm(h, small["ln_mix_pre"][l])
    c_q, c_kv, fq, fk, fv, fg, k_rope, ff = split_cols(linear(hn, w_in), PROJ_BOUNDS)

    q = linear(rms_norm(c_q, small["g_q_latent"][l]), w_q_up)
    qn, qr = split_cols(q, ((0, HEADS * HEAD_DIM), (HEADS * HEAD_DIM, HEADS * (HEAD_DIM + ROPE_DIM))))
    kv = linear(rms_norm(c_kv, small["g_kv_latent"][l]), w_kv_up)
    kn, v = split_cols(kv, ((0, HEADS * HEAD_DIM), (HEADS * HEAD_DIM, 2 * HEADS * HEAD_DIM)))
    qr = jnp.transpose(_rope(qr.reshape(L, HEADS, ROPE_DIM), cos[:, None, :], sin[:, None, :]), (1, 0, 2))
    kr = _rope(k_rope, cos, sin)
    a = mla_attention(l_real, qn, qr, kn, kr, v)

    fqn = rms_norm(fq, small["g_fox_q"][l])
    fkn = rms_norm(fk, small["g_fox_k"][l])
    log_f = jax.nn.log_sigmoid(ff + small["b_forget"][l])
    c = jnp.cumsum(log_f, axis=0).T[:, :, None]
    bmix = fox_attention(l_real, fqn, fkn, fv, c) * jax.nn.sigmoid(fg)

    mix = linear(jnp.concatenate([a, bmix], axis=1), w_out)
    h = h + rms_norm(mix, small["ln_mix_post"][l])

    u = linear(rms_norm(h, small["ln_ffn_pre"][l]), w_ffn_up)
    act = conv_gate(u, w_conv, b_conv)
    h = h + rms_norm(linear(act, w_ffn_down), small["ln_ffn_post"][l])
    return h


def _local_loss(big, small, meta, conv_w, x, target):
    s, d = x.shape
    l_real = N_META + s
    l_pad = -(-l_real // Q_BLOCK) * Q_BLOCK
    h = jnp.concatenate([meta, x, jnp.zeros((l_pad - l_real, d), F32)], axis=0)
    half = ROPE_DIM // 2
    inv_freq = ROPE_THETA ** (-jnp.arange(half, dtype=F32) / half)
    ang = jnp.arange(l_pad, dtype=jnp.int32).astype(F32)[:, None] * inv_freq[None, :]
    cos, sin = jnp.cos(ang), jnp.sin(ang)
    for l in range(DEPTH):
        h = _layer(h, big[l], small, conv_w[l], l, l_real, cos, sin)
    return token_loss(h[N_META:l_real], target)


ANY_SPACE = pl.BlockSpec(memory_space=pl.ANY)


def _place():
    ix, iy, ic = lax.axis_index("x"), lax.axis_index("y"), lax.axis_index("c")
    return ix, iy, ic, [(1 - ix, iy), (ix, 1 - iy), (1 - ix, 1 - iy)]


def _comm_call(body, arrays, out_shapes, n_remote, n_local, name):
    return pl.pallas_call(
        body, out_shape=out_shapes, in_specs=[ANY_SPACE] * len(arrays), out_specs=[ANY_SPACE] * len(out_shapes),
        scratch_shapes=[pltpu.SemaphoreType.DMA((n_remote,)), pltpu.SemaphoreType.DMA((n_remote,)),
                        pltpu.SemaphoreType.DMA((n_local,))],
        name=name,
    )(*arrays)


def _gather(arrays, name):
    n = len(arrays)

    def body(*refs):
        xs, outs = refs[:n], refs[n:2 * n]
        send_sems, recv_sems, local_sems = refs[2 * n:]
        ix, iy, ic, chips = _place()
        me, sibling = (ix, iy, ic), (ix, iy, 1 - ic)

        def copy(a, k, block, to, src=None):
            dst = outs[a].at[4 * block[0] + 2 * block[1] + block[2]]
            return pltpu.make_async_remote_copy(
                src_ref=dst if src is None else src, dst_ref=dst, send_sem=send_sems.at[7 * a + k],
                recv_sem=recv_sems.at[7 * a + k], device_id=to, device_id_type=MESH_ID)

        local, sent = [], []
        for a in range(n):
            mine = pltpu.make_async_copy(xs[a], outs[a].at[4 * ix + 2 * iy + ic], local_sems.at[a])
            mine.start()
            local.append(mine)
            first = [copy(a, 0, me, sibling, src=xs[a])]
            first += [copy(a, 1 + j, me, (*chip, ic), src=xs[a]) for j, chip in enumerate(chips)]
            for cp in first:
                cp.start()
            sent += first
        for a in range(n):
            for j, chip in enumerate(chips):
                copy(a, 1 + j, (*chip, ic), me).wait_recv()
                passed = copy(a, 4 + j, (*chip, ic), sibling)
                passed.start()
                sent.append(passed)
        for a in range(n):
            copy(a, 0, sibling, me).wait_recv()
            for j, chip in enumerate(chips):
                copy(a, 4 + j, (*chip, 1 - ic), me).wait_recv()
        for cp in sent:
            cp.wait_send()
        for cp in local:
            cp.wait()

    out_shapes = [jax.ShapeDtypeStruct((N_DEV,) + a.shape, a.dtype) for a in arrays]
    return _comm_call(body, arrays, out_shapes, 7 * n, n, name)


def _swap_with_sibling(arrays, name):
    n = len(arrays)

    def body(*refs):
        xs, outs = refs[:n], refs[n:2 * n]
        send_sems, recv_sems, _ = refs[2 * n:]
        ix, iy, ic, _ = _place()
        copies = [pltpu.make_async_remote_copy(
            src_ref=xs[a], dst_ref=outs[a], send_sem=send_sems.at[a], recv_sem=recv_sems.at[a],
            device_id=(ix, iy, 1 - ic), device_id_type=MESH_ID) for a in range(n)]
        for cp in copies:
            cp.start()
        for cp in copies:
            cp.wait()

    out_shapes = [jax.ShapeDtypeStruct(a.shape, a.dtype) for a in arrays]
    return _comm_call(body, arrays, out_shapes, n, 1, name)


def _exchange_chips(arrays, name):
    n = len(arrays)

    def body(*refs):
        xs, outs = refs[:n], refs[n:2 * n]
        send_sems, recv_sems, local_sems = refs[2 * n:]
        ix, iy, ic, chips = _place()
        my_chip = 2 * ix + iy
        local, sent = [], []
        for a in range(n):
            mine = pltpu.make_async_copy(xs[a].at[my_chip], outs[a].at[my_chip], local_sems.at[a])
            mine.start()
            local.append(mine)
            for j, chip in enumerate(chips):
                cp = pltpu.make_async_remote_copy(
                    src_ref=xs[a].at[2 * chip[0] + chip[1]], dst_ref=outs[a].at[my_chip],
                    send_sem=send_sems.at[3 * a + j], recv_sem=recv_sems.at[3 * a + j],
                    device_id=(*chip, ic), device_id_type=MESH_ID)
                cp.start()
                sent.append(cp)
        for a in range(n):
            for j, chip in enumerate(chips):
                pltpu.make_async_remote_copy(
                    src_ref=xs[a].at[my_chip], dst_ref=outs[a].at[2 * chip[0] + chip[1]],
                    send_sem=send_sems.at[3 * a + j], recv_sem=recv_sems.at[3 * a + j],
                    device_id=(*chip, ic), device_id_type=MESH_ID).wait_recv()
        for cp in sent:
            cp.wait_send()
        for cp in local:
            cp.wait()

    out_shapes = [jax.ShapeDtypeStruct(a.shape, a.dtype) for a in arrays]
    return _comm_call(body, arrays, out_shapes, 3 * n, n, name)


def _sum_slots(x, out_dtype, name):
    slots, rows, cols = x.shape
    tr = _row_tile(rows, cols, (2 << 20) // slots, 16)

    def body(x_ref, o_ref):
        acc = x_ref[0].astype(F32)
        for s in range(1, slots):
            acc = acc + x_ref[s].astype(F32)
        o_ref[...] = acc.astype(o_ref.dtype)

    return pl.pallas_call(
        body, grid=(rows // tr,), in_specs=[pl.BlockSpec((slots, tr, cols), lambda i: (0, i, 0))],
        out_specs=pl.BlockSpec((tr, cols), lambda i: (i, 0)), out_shape=jax.ShapeDtypeStruct((rows, cols), out_dtype),
        compiler_params=_params("parallel"), name=name,
    )(x)


def _add_pairs(a, b, name):
    slots, rows, cols = a.shape
    tr = _row_tile(rows, cols, 1 << 20, 16)

    def body(a_ref, b_ref, o_ref):
        o_ref[...] = (a_ref[...].astype(F32) + b_ref[...].astype(F32)).astype(o_ref.dtype)

    spec = pl.BlockSpec((None, tr, cols), lambda s, i: (s, i, 0))
    return pl.pallas_call(
        body, grid=(slots, rows // tr), in_specs=[spec, spec], out_specs=spec,
        out_shape=jax.ShapeDtypeStruct(a.shape, BF16), compiler_params=_params("parallel", "parallel"), name=name,
    )(a, b)


def _reduce_scatter(grads, ic):
    by_chip = [g.reshape((4, 2) + g.shape[1:]) for g in grads]
    keep = [lax.dynamic_index_in_dim(g, ic, axis=1, keepdims=False) for g in by_chip]
    give = [lax.dynamic_index_in_dim(g, 1 - ic, axis=1, keepdims=False) for g in by_chip]
    got = _swap_with_sibling(give, "scatter_sibling")
    pairs = [_add_pairs(k, g, "add_pairs") for k, g in zip(keep, got)]
    received = _exchange_chips(pairs, "scatter_chips")
    return [_sum_slots(r, F32, "sum_grads") for r in received]


def _pack(arrays, dtype, row_multiple):
    flat = jnp.concatenate([a.astype(dtype).reshape(-1) for a in arrays])
    n = flat.shape[0]
    quantum = row_multiple * FLAT_COLS
    padded = -(-n // quantum) * quantum
    return jnp.pad(flat, (0, padded - n)).reshape(padded // FLAT_COLS, FLAT_COLS)


def _unpack(buf, shapes):
    flat = buf.reshape(-1)
    out, off = [], 0
    for shp in shapes:
        n = 1
        for s in shp:
            n *= s
        out.append(flat[off:off + n].reshape(tuple(shp)))
        off += n
    return out


def _adamw(w, g, m, v, name):
    shape = w.shape
    cols = shape[-1]
    w2, g2, m2, v2 = (a.reshape(-1, cols) for a in (w, g, m, v))
    rows = w2.shape[0]
    tr = _row_tile(rows, cols, 1 << 20)

    def body(w_ref, g_ref, m_ref, v_ref, d_ref, nm_ref, nv_ref):
        gv = g_ref[...]
        nm = ADAM_B1 * m_ref[...] + (1.0 - ADAM_B1) * gv
        nv = ADAM_B2 * v_ref[...] + (1.0 - ADAM_B2) * (gv * gv)
        m_hat = nm / (1.0 - ADAM_B1 ** ADAM_STEP)
        v_hat = nv / (1.0 - ADAM_B2 ** ADAM_STEP)
        d_ref[...] = -ADAM_LR * (m_hat / (jnp.sqrt(v_hat) + ADAM_EPS) + ADAM_WD * w_ref[...])
        nm_ref[...] = nm
        nv_ref[...] = nv

    spec = pl.BlockSpec((tr, cols), lambda i: (i, 0))
    outs = pl.pallas_call(
        body, grid=(rows // tr,), in_specs=[spec] * 4, out_specs=[spec] * 3,
        out_shape=[jax.ShapeDtypeStruct((rows, cols), F32)] * 3,
        compiler_params=_params("parallel"), name=name,
    )(w2, g2, m2, v2)
    return tuple(o.reshape(shape) for o in outs)


BIG = ("w_in", "w_q_up", "w_kv_up", "w_out", "w_ffn_up", "w_ffn_down")
REPLICATED = ("ln_mix_pre", "b_forget", "g_q_latent", "g_kv_latent", "g_fox_q", "g_fox_k", "ln_mix_post",
              "ln_ffn_pre", "b_ffn_conv", "ln_ffn_post")
WEIGHTS = ("meta_tokens", "ln_mix_pre", "w_in", "b_forget", "g_q_latent", "g_kv_latent", "w_q_up", "w_kv_up",
           "g_fox_q", "g_fox_k", "w_out", "ln_mix_post", "ln_ffn_pre", "w_ffn_up", "w_ffn_conv", "b_ffn_conv",
           "w_ffn_down", "ln_ffn_post")


def kernel(x, meta_tokens, ln_mix_pre, w_in, b_forget, g_q_latent, g_kv_latent, w_q_up, w_kv_up, g_fox_q, g_fox_k, w_out, ln_mix_post, ln_ffn_pre, w_ffn_up, w_ffn_conv, b_ffn_conv, w_ffn_down, ln_ffn_post, loss_target, m_meta_tokens, m_ln_mix_pre, m_w_in, m_b_forget, m_g_q_latent, m_g_kv_latent, m_w_q_up, m_w_kv_up, m_g_fox_q, m_g_fox_k, m_w_out, m_ln_mix_post, m_ln_ffn_pre, m_w_ffn_up, m_w_ffn_conv, m_b_ffn_conv, m_w_ffn_down, m_ln_ffn_post, v_meta_tokens, v_ln_mix_pre, v_w_in, v_b_forget, v_g_q_latent, v_g_kv_latent, v_w_q_up, v_w_kv_up, v_g_fox_q, v_g_fox_k, v_w_out, v_ln_mix_post, v_ln_ffn_pre, v_w_ffn_up, v_w_ffn_conv, v_b_ffn_conv, v_w_ffn_down, v_ln_ffn_post):
    w = dict(meta_tokens=meta_tokens, ln_mix_pre=ln_mix_pre, w_in=w_in, b_forget=b_forget, g_q_latent=g_q_latent,
             g_kv_latent=g_kv_latent, w_q_up=w_q_up, w_kv_up=w_kv_up, g_fox_q=g_fox_q, g_fox_k=g_fox_k, w_out=w_out,
             ln_mix_post=ln_mix_post, ln_ffn_pre=ln_ffn_pre, w_ffn_up=w_ffn_up, w_ffn_conv=w_ffn_conv,
             b_ffn_conv=b_ffn_conv, w_ffn_down=w_ffn_down, ln_ffn_post=ln_ffn_post)
    mom = dict(meta_tokens=m_meta_tokens, ln_mix_pre=m_ln_mix_pre, w_in=m_w_in, b_forget=m_b_forget,
               g_q_latent=m_g_q_latent, g_kv_latent=m_g_kv_latent, w_q_up=m_w_q_up, w_kv_up=m_w_kv_up,
               g_fox_q=m_g_fox_q, g_fox_k=m_g_fox_k, w_out=m_w_out, ln_mix_post=m_ln_mix_post,
               ln_ffn_pre=m_ln_ffn_pre, w_ffn_up=m_w_ffn_up, w_ffn_conv=m_w_ffn_conv, b_ffn_conv=m_b_ffn_conv,
               w_ffn_down=m_w_ffn_down, ln_ffn_post=m_ln_ffn_post)
    var = dict(meta_tokens=v_meta_tokens, ln_mix_pre=v_ln_mix_pre, w_in=v_w_in, b_forget=v_b_forget,
               g_q_latent=v_g_q_latent, g_kv_latent=v_g_kv_latent, w_q_up=v_w_q_up, w_kv_up=v_w_kv_up,
               g_fox_q=v_g_fox_q, g_fox_k=v_g_fox_k, w_out=v_w_out, ln_mix_post=v_ln_mix_post,
               ln_ffn_pre=v_ln_ffn_pre, w_ffn_up=v_w_ffn_up, w_ffn_conv=v_w_ffn_conv, b_ffn_conv=v_b_ffn_conv,
               w_ffn_down=v_w_ffn_down, ln_ffn_post=v_ln_ffn_post)
    ic = lax.axis_index("c")
    me = 4 * lax.axis_index("x") + 2 * lax.axis_index("y") + ic

    gathered = _gather([w[n].astype(BF16) for n in BIG] + [meta_tokens, w_ffn_conv], "gather_weights")
    big = [{n: gathered[k][:, l] for k, n in enumerate(BIG)} for l in range(DEPTH)]
    meta_shape, conv_shape = meta_tokens.shape, w_ffn_conv.shape
    meta_full = _cols_from_devices(gathered[len(BIG)])
    conv_full = jnp.transpose(gathered[len(BIG) + 1], (1, 2, 0, 3)).reshape(DEPTH, conv_shape[1], -1)
    small = {n: w[n] for n in REPLICATED}

    loss, grads = jax.value_and_grad(_local_loss, argnums=(0, 1, 2, 3, 4))(
        big, small, meta_full, [conv_full[l] for l in range(DEPTH)], x[0], loss_target[0])
    g_big, g_small, g_meta, g_conv, g_x = grads
    loss = lax.psum(loss, ("x", "y", "c"))

    grad = {}
    per_layer = [_reduce_scatter([g_big[l][n] for n in BIG], ic) for l in range(DEPTH)]
    for k, n in enumerate(BIG):
        grad[n] = jnp.stack([per_layer[l][k] for l in range(DEPTH)])

    small_arrays = [g_small[n] for n in REPLICATED] + [g_meta, jnp.stack(g_conv)]
    small_shapes = [a.shape for a in small_arrays]
    partials = _gather([_pack(small_arrays, F32, 16)], "gather_small_grads")[0]
    summed = _unpack(_sum_slots(partials, F32, "sum_small_grads"), small_shapes)
    for n, g in zip(REPLICATED, summed):
        grad[n] = g
    grad["meta_tokens"] = lax.dynamic_slice_in_dim(summed[-2], me * meta_shape[1], meta_shape[1], axis=1)
    grad["w_ffn_conv"] = lax.dynamic_slice_in_dim(summed[-1], me * conv_shape[2], conv_shape[2], axis=2)

    delta, new_m, new_v = {}, {}, {}
    for n in BIG:
        delta[n], new_m[n], new_v[n] = _adamw(w[n], grad[n], mom[n], var[n], "adamw_" + n)
    rest = [n for n in WEIGHTS if n not in BIG]
    rest_shapes = [w[n].shape for n in rest]
    flat = [_pack([src[n] for n in rest], F32, SUBLANES) for src in (w, grad, mom, var)]
    outs = _adamw(*flat, "adamw_small")
    for dst, buf in zip((delta, new_m, new_v), outs):
        for n, a in zip(rest, _unpack(buf, rest_shapes)):
            dst[n] = a

    return (loss, g_x[None], *[grad[n] for n in WEIGHTS], *[delta[n] for n in WEIGHTS],
            *[new_m[n] for n in WEIGHTS], *[new_v[n] for n in WEIGHTS])
```

```python
import functools

import jax
import jax.numpy as jnp
from jax import lax
from jax.experimental import pallas as pl
from jax.experimental.pallas import tpu as pltpu

F32 = jnp.float32
BF16 = jnp.bfloat16
MESH_ID = pl.DeviceIdType.MESH

N_DEV = 8
DEPTH = 4
N_META = 16
CHUNK = 64
Q_BLOCK = 128
HEADS = 8
HEAD_DIM = 128
ROPE_DIM = 64
MLA_Q_LORA = 512
MLA_KV_LORA = 512
FOX_W = HEADS * HEAD_DIM
ROPE_THETA = 10000.0
EPS = 1e-6
NEG = -1e30
IN_COLS = 5192
IN_COLS_PADDED = 5376

ADAM_LR = 0.001
ADAM_B1 = 0.9
ADAM_B2 = 0.999
ADAM_EPS = 1e-08
ADAM_WD = 0.01
ADAM_STEP = 10

LANES = 128
SUBLANES = 8
FLAT_COLS = 1024
VMEM_LIMIT_V7X = 52 * 1024 * 1024

NT_DIMS = (((1,), (1,)), ((), ()))
NN_DIMS = (((1,), (0,)), ((), ()))
TN_DIMS = (((0,), (0,)), ((), ()))

_TILE_ATT = (640, 512, 384, 256, 128)
_TILE_FF = (512, 256, 128)


def _pick(n, candidates):
    for c in candidates:
        if n % c == 0:
            return c
    return n


def _row_tile(rows, cols, budget_bytes=2 << 20, align=SUBLANES):
    best = None
    for t in range(align, rows + 1, align):
        if rows % t == 0 and t * cols * 4 <= budget_bytes:
            best = t
    return best if best is not None else rows


def _params(*semantics):
    return pltpu.CompilerParams(dimension_semantics=semantics, vmem_limit_bytes=VMEM_LIMIT_V7X)


MATMUL_VMEM_BUDGET = 36 << 20
MXU_FLOPS_V7X = 9.0e14
HBM_BYTES_PER_S_V7X = 2.5e12
GRID_STEP_S = 0.35e-6
MXU_DIM = 256


def _tile_candidates(n, cap):
    c = [t for t in range(LANES, min(n, cap) + 1, LANES) if n % t == 0]
    return c if c else [n]


def _matmul_tiles(m, n, c, a_bytes, b_bytes, o_bytes):
    best, best_cost = None, None
    for tc in _tile_candidates(c, 4096):
        steps = c // tc
        for tm in _tile_candidates(m, 2048):
            for tn in _tile_candidates(n, 2048):
                vmem = 2 * (tm * tc * a_bytes + tc * tn * b_bytes + tm * tn * o_bytes)
                vmem += tm * tn * 4 if steps > 1 else 0
                if vmem > MATMUL_VMEM_BUDGET:
                    continue
                traffic = m * c * a_bytes * (1 if steps == 1 else n // tn) + c * n * b_bytes * (m // tm)
                traffic += m * n * o_bytes
                grid = (m // tm) * (n // tn) * steps
                accumulate = 0 if steps == 1 else grid * tm * tn * 8 / 4.0e12
                fill = (-(-tn // MXU_DIM) * MXU_DIM / tn) * (-(-tc // MXU_DIM) * MXU_DIM / tc)
                cost = max(2.0 * m * n * c * fill / MXU_FLOPS_V7X, traffic / HBM_BYTES_PER_S_V7X)
                cost += grid * GRID_STEP_S + accumulate
                if best_cost is None or cost < best_cost:
                    best, best_cost = (tm, tn, tc), cost
    return best


def _matmul(a, b, mode, out_dtype, name):
    if mode == "nn":
        (m, c), (c2, n) = a.shape, b.shape
    elif mode == "nt":
        (m, c), (n, c2) = a.shape, b.shape
    else:
        (c, m), (c2, n) = a.shape, b.shape
    assert c == c2, (a.shape, b.shape, mode)
    tm, tn, tc = _matmul_tiles(m, n, c, a.dtype.itemsize, b.dtype.itemsize, jnp.dtype(out_dtype).itemsize)
    steps = c // tc
    if mode == "nn":
        a_spec = pl.BlockSpec((tm, tc), lambda i, j, k: (i, k))
        b_spec = pl.BlockSpec((tc, tn), lambda i, j, k: (k, j))
        dims = NN_DIMS
    elif mode == "nt":
        a_spec = pl.BlockSpec((tm, tc), lambda i, j, k: (i, k))
        b_spec = pl.BlockSpec((tn, tc), lambda i, j, k: (j, k))
        dims = NT_DIMS
    else:
        a_spec = pl.BlockSpec((tc, tm), lambda i, j, k: (k, i))
        b_spec = pl.BlockSpec((tc, tn), lambda i, j, k: (k, j))
        dims = TN_DIMS

    def body(a_ref, b_ref, o_ref, acc_ref):
        k = pl.program_id(2)

        @pl.when(k == 0)
        def _():
            acc_ref[...] = jnp.zeros_like(acc_ref)

        acc_ref[...] += lax.dot_general(a_ref[...].astype(BF16), b_ref[...].astype(BF16), dims,
                                        preferred_element_type=F32)

        @pl.when(k == steps - 1)
        def _():
            o_ref[...] = acc_ref[...].astype(o_ref.dtype)

    def body_whole(a_ref, b_ref, o_ref):
        o_ref[...] = lax.dot_general(a_ref[...].astype(BF16), b_ref[...].astype(BF16), dims,
                                     preferred_element_type=F32).astype(o_ref.dtype)

    return pl.pallas_call(
        body if steps > 1 else body_whole, grid=(m // tm, n // tn, steps), in_specs=[a_spec, b_spec],
        out_specs=pl.BlockSpec((tm, tn), lambda i, j, k: (i, j)),
        out_shape=jax.ShapeDtypeStruct((m, n), out_dtype),
        scratch_shapes=[pltpu.VMEM((tm, tn), F32)] if steps > 1 else [],
        compiler_params=_params("parallel", "parallel", "arbitrary"), name=name,
    )(a, b)


@jax.custom_vjp
def linear(x, w):
    return _matmul(x, w, "nn", F32, "linear_fwd")


def _linear_fwd(x, w):
    return _matmul(x, w, "nn", F32, "linear_fwd"), (x, w)


def _linear_bwd(res, dy):
    x, w = res
    dx = _matmul(dy, w, "nt", F32, "linear_dx")
    dw = _matmul(x, dy, "tn", w.dtype, "linear_dw")
    return dx, dw


linear.defvjp(_linear_fwd, _linear_bwd)


def _rms_forward(x, g, out_dtype=F32):
    rows, d = x.shape
    gd = g.shape[0]
    tr = _row_tile(rows, d, align=16)

    def body(x_ref, g_ref, y_ref):
        for c0 in range(0, d, gd):
            xv = x_ref[:, c0:c0 + gd]
            r = lax.rsqrt(jnp.mean(xv * xv, axis=-1, keepdims=True) + EPS)
            y_ref[:, c0:c0 + gd] = ((xv * r) * g_ref[...]).astype(y_ref.dtype)

    return pl.pallas_call(
        body, grid=(rows // tr,),
        in_specs=[pl.BlockSpec((tr, d), lambda i: (i, 0)), pl.BlockSpec((1, gd), lambda i: (0, 0))],
        out_specs=pl.BlockSpec((tr, d), lambda i: (i, 0)),
        out_shape=jax.ShapeDtypeStruct((rows, d), out_dtype),
        compiler_params=_params("parallel"), name="rmsnorm_fwd",
    )(x, g.reshape(1, gd))


def _rms_backward(x, g, dy):
    rows, d = x.shape
    gd = g.shape[0]
    tr = _row_tile(rows, d)

    def body(x_ref, g_ref, dy_ref, dx_ref, dg_ref):
        i = pl.program_id(0)

        @pl.when(i == 0)
        def _():
            dg_ref[...] = jnp.zeros_like(dg_ref)

        for c0 in range(0, d, gd):
            xv = x_ref[:, c0:c0 + gd]
            dyv = dy_ref[:, c0:c0 + gd]
            r = lax.rsqrt(jnp.mean(xv * xv, axis=-1, keepdims=True) + EPS)
            xh = xv * r
            t = dyv * g_ref[...]
            dx_ref[:, c0:c0 + gd] = r * (t - xh * jnp.mean(t * xh, axis=-1, keepdims=True))
            dg_ref[...] += jnp.sum(dyv * xh, axis=0, keepdims=True)

    dx, dg = pl.pallas_call(
        body, grid=(rows // tr,),
        in_specs=[pl.BlockSpec((tr, d), lambda i: (i, 0)), pl.BlockSpec((1, gd), lambda i: (0, 0)),
                  pl.BlockSpec((tr, d), lambda i: (i, 0))],
        out_specs=[pl.BlockSpec((tr, d), lambda i: (i, 0)), pl.BlockSpec((1, gd), lambda i: (0, 0))],
        out_shape=[jax.ShapeDtypeStruct((rows, d), F32), jax.ShapeDtypeStruct((1, gd), F32)],
        compiler_params=_params("arbitrary"), name="rmsnorm_bwd",
    )(x, g.reshape(1, gd), dy)
    return dx, dg.reshape(g.shape)


@jax.custom_vjp
def rms_norm(x, g):
    return _rms_forward(x, g)


def _rms_norm_fwd(x, g):
    return _rms_forward(x, g), (x, g)


def _rms_norm_bwd(res, dy):
    x, g = res
    return _rms_backward(x, g, dy)


rms_norm.defvjp(_rms_norm_fwd, _rms_norm_bwd)


@jax.custom_vjp
def norm_linear(x, g, w):
    return _matmul(_rms_forward(x, g, BF16), w, "nn", F32, "linear_fwd")


def _norm_linear_fwd(x, g, w):
    y = _rms_forward(x, g, BF16)
    return _matmul(y, w, "nn", F32, "linear_fwd"), (x, g, w, y)


def _norm_linear_bwd(res, dz):
    x, g, w, y = res
    dy = _matmul(dz, w, "nt", F32, "linear_dx")
    dw = _matmul(y, dz, "tn", w.dtype, "linear_dw")
    dx, dg = _rms_backward(x, g, dy)
    return dx, dg, dw


norm_linear.defvjp(_norm_linear_fwd, _norm_linear_bwd)


@functools.partial(jax.custom_vjp, nondiff_argnums=(1,))
def split_cols(x, bounds):
    return tuple(x[:, lo:hi] for lo, hi in bounds)


def _split_cols_fwd(x, bounds):
    return split_cols(x, bounds), x.shape[1]


def _split_cols_bwd(bounds, width, cts):
    parts = list(cts)
    tail = width - bounds[-1][1]
    if tail:
        parts.append(jnp.zeros((parts[0].shape[0], tail), parts[0].dtype))
    return (jnp.concatenate(parts, axis=1),)


split_cols.defvjp(_split_cols_fwd, _split_cols_bwd)


def _visibility_id(pos, kind, l_real):
    if kind == "fox":
        return pos
    pad_chunk = 2 + (l_real - N_META) // CHUNK
    frame_chunk = 1 + jnp.right_shift(pos - N_META, 6)
    return jnp.where(pos < N_META, 0, jnp.where(pos < l_real, frame_chunk, pad_chunk))


def _raw_scores(kind, a1, b1, a2, b2):
    s = lax.dot_general(a1.astype(BF16), b1.astype(BF16), NT_DIMS, preferred_element_type=F32)
    if kind == "mla":
        s = s + lax.dot_general(a2.astype(BF16), b2.astype(BF16), NT_DIMS, preferred_element_type=F32)
    return s


def _block_pairs(nb, look, by_key):
    outer, inner = [], []
    for a in range(nb):
        rng = range(max(a - look, 0), nb) if by_key else range(0, min(a + look, nb - 1) + 1)
        for b in rng:
            outer.append(a)
            inner.append(b)
    return jnp.asarray(outer, jnp.int32), jnp.asarray(inner, jnp.int32)


def _attention_call(body, n_pairs, in_specs, out_specs, out_shape, scratch, name, tables, operands):
    return pl.pallas_call(
        body,
        grid_spec=pltpu.PrefetchScalarGridSpec(num_scalar_prefetch=2, grid=(HEADS, n_pairs), in_specs=in_specs,
                                               out_specs=out_specs, scratch_shapes=scratch),
        out_shape=out_shape, compiler_params=_params("parallel", "arbitrary"), name=name,
    )(*tables, *operands)


def _attention_forward(kind, l_real, q1, k1, v, extras):
    L = q1.shape[0]
    T = _pick(L, _TILE_ATT)
    nb = L // T
    look = 1 if kind == "mla" else 0
    scale = (HEAD_DIM + ROPE_DIM) ** -0.5 if kind == "mla" else HEAD_DIM ** -0.5
    tables = _block_pairs(nb, look, False)

    q_tile = pl.BlockSpec((T, HEAD_DIM), lambda h, t, it, jt: (it[t], h))
    k_tile = pl.BlockSpec((T, HEAD_DIM), lambda h, t, it, jt: (jt[t], h))
    row_stat = pl.BlockSpec((None, T, 1), lambda h, t, it, jt: (h, it[t], 0))
    if kind == "mla":
        extra_specs = [pl.BlockSpec((None, T, ROPE_DIM), lambda h, t, it, jt: (h, it[t], 0)),
                       pl.BlockSpec((T, ROPE_DIM), lambda h, t, it, jt: (jt[t], 0))]
    else:
        extra_specs = [pl.BlockSpec((None, 1, T), lambda h, t, it, jt: (h, 0, jt[t]))]

    def body(it_ref, jt_ref, q1_ref, k1_ref, v_ref, *rest):
        if kind == "mla":
            eq_ref, ek_ref, o_ref, lse_ref, m_ref, l_ref, acc_ref = rest
        else:
            ek_ref, o_ref, lse_ref, m_ref, l_ref, acc_ref = rest
        t = pl.program_id(1)
        i, j = it_ref[t], jt_ref[t]

        @pl.when(j == 0)
        def _():
            m_ref[...] = jnp.full_like(m_ref, NEG)
            l_ref[...] = jnp.zeros_like(l_ref)
            acc_ref[...] = jnp.zeros_like(acc_ref)

        def block(masked):
            if kind == "mla":
                s = _raw_scores(kind, q1_ref[...] * scale, k1_ref[...], eq_ref[...] * scale, ek_ref[...])
            else:
                s = _raw_scores(kind, q1_ref[...] * scale, k1_ref[...], None, None) - ek_ref[...]
            if masked:
                pos_q = i * T + lax.broadcasted_iota(jnp.int32, (T, 1), 0)
                pos_k = j * T + lax.broadcasted_iota(jnp.int32, (1, T), 1)
                s = jnp.where(_visibility_id(pos_k, kind, l_real) <= _visibility_id(pos_q, kind, l_real), s, NEG)
            m_prev = m_ref[...]
            m_new = jnp.maximum(m_prev, jnp.max(s, axis=1, keepdims=True))
            alpha = jnp.exp(m_prev - m_new)
            p = jnp.exp(s - m_new)
            l_ref[...] = alpha * l_ref[...] + jnp.sum(p, axis=1, keepdims=True)
            m_ref[...] = m_new
            vb = v_ref[...].astype(BF16)
            p_hi = p.astype(BF16)
            pv = lax.dot_general(p_hi, vb, NN_DIMS, preferred_element_type=F32)
            if kind == "fox":
                p_lo = (p - p_hi.astype(F32)).astype(BF16)
                pv = pv + lax.dot_general(p_lo, vb, NN_DIMS, preferred_element_type=F32)
            acc_ref[...] = alpha * acc_ref[...] + pv

        @pl.when(j < i)
        def _():
            block(False)

        @pl.when(j >= i)
        def _():
            block(True)

        @pl.when(j == jnp.minimum(i + look, nb - 1))
        def _():
            o_ref[...] = acc_ref[...] / l_ref[...]
            lse_ref[...] = m_ref[...] + jnp.log(l_ref[...])

    return _attention_call(
        body, tables[0].shape[0], [q_tile, k_tile, k_tile] + extra_specs, [q_tile, row_stat],
        [jax.ShapeDtypeStruct((L, HEADS * HEAD_DIM), F32), jax.ShapeDtypeStruct((HEADS, L, 1), F32)],
        [pltpu.VMEM((T, 1), F32), pltpu.VMEM((T, 1), F32), pltpu.VMEM((T, HEAD_DIM), F32)],
        kind + "_attn_fwd", tables, (q1, k1, v, *extras))


def _attention_delta(o, do):
    L = o.shape[0]
    T = _pick(L, _TILE_ATT)
    tile = pl.BlockSpec((T, HEAD_DIM), lambda i, h: (i, h))

    def body(o_ref, do_ref, delta_ref):
        delta_ref[...] = jnp.sum(do_ref[...].astype(BF16).astype(F32) * o_ref[...], axis=1, keepdims=True)

    return pl.pallas_call(
        body, grid=(L // T, HEADS), in_specs=[tile, tile],
        out_specs=pl.BlockSpec((None, T, 1), lambda i, h: (h, i, 0)),
        out_shape=jax.ShapeDtypeStruct((HEADS, L, 1), F32),
        compiler_params=_params("parallel", "parallel"), name="attn_delta",
    )(o, do)


def _attention_backward(kind, l_real, q1, k1, v, extras, do, lse_row, delta_row):
    L = q1.shape[0]
    T = _pick(L, _TILE_ATT)
    nb = L // T
    look = 1 if kind == "mla" else 0
    scale = (HEAD_DIM + ROPE_DIM) ** -0.5 if kind == "mla" else HEAD_DIM ** -0.5
    tables = _block_pairs(nb, look, True)

    k_tile = pl.BlockSpec((T, HEAD_DIM), lambda h, t, jt, it: (jt[t], h))
    q_tile = pl.BlockSpec((T, HEAD_DIM), lambda h, t, jt, it: (it[t], h))
    q_row = pl.BlockSpec((None, 1, T), lambda h, t, jt, it: (h, 0, it[t]))
    if kind == "mla":
        extra_specs = [pl.BlockSpec((None, T, ROPE_DIM), lambda h, t, jt, it: (h, it[t], 0)),
                       pl.BlockSpec((T, ROPE_DIM), lambda h, t, jt, it: (jt[t], 0))]
        third_spec = pl.BlockSpec((None, T, ROPE_DIM), lambda h, t, jt, it: (h, jt[t], 0))
        third_shape = jax.ShapeDtypeStruct((HEADS, L, ROPE_DIM), F32)
        third_scratch = pltpu.VMEM((T, ROPE_DIM), F32)
    else:
        extra_specs = [pl.BlockSpec((None, T, 1), lambda h, t, jt, it: (h, jt[t], 0))]
        third_spec = pl.BlockSpec((None, T, 1), lambda h, t, jt, it: (h, jt[t], 0))
        third_shape = jax.ShapeDtypeStruct((HEADS, L, 1), F32)
        third_scratch = pltpu.VMEM((T, 1), F32)
    in_specs = [q_tile, k_tile, k_tile] + extra_specs + [q_tile, q_row, q_row]
    n_pairs = tables[0].shape[0]
    out_specs = [k_tile, k_tile, third_spec, pl.BlockSpec((L, HEAD_DIM), lambda h, t, jt, it: (0, h))]
    out_shape = [jax.ShapeDtypeStruct((L, HEADS * HEAD_DIM), F32), jax.ShapeDtypeStruct((L, HEADS * HEAD_DIM), F32),
                 third_shape, jax.ShapeDtypeStruct((L, HEADS * HEAD_DIM), F32)]
    if kind == "mla":
        out_specs.append(pl.BlockSpec((None, L, ROPE_DIM), lambda h, t, jt, it: (h, 0, 0)))
        out_shape.append(jax.ShapeDtypeStruct((HEADS, L, ROPE_DIM), F32))

    def body(jt_ref, it_ref, q1_ref, k1_ref, v_ref, *rest):
        if kind == "mla":
            (eq_ref, ek_ref, do_ref, lse_ref, delta_ref, dk1_ref, dv_ref, third_ref, dq1_ref, dq2_ref,
             acck_ref, accv_ref, acc3_ref) = rest
        else:
            (ek_ref, do_ref, lse_ref, delta_ref, dk1_ref, dv_ref, third_ref, dq1_ref,
             acck_ref, accv_ref, acc3_ref) = rest
        t = pl.program_id(1)
        j, i = jt_ref[t], it_ref[t]
        q_rows = pl.ds(pl.multiple_of(i * T, T), T)

        @pl.when(t == 0)
        def _():
            dq1_ref[...] = jnp.zeros_like(dq1_ref)
            if kind == "mla":
                dq2_ref[...] = jnp.zeros_like(dq2_ref)

        @pl.when(i == jnp.maximum(j - look, 0))
        def _():
            acck_ref[...] = jnp.zeros_like(acck_ref)
            accv_ref[...] = jnp.zeros_like(accv_ref)
            acc3_ref[...] = jnp.zeros_like(acc3_ref)

        def block(masked):
            if kind == "mla":
                kb = jnp.concatenate([k1_ref[...].astype(BF16), ek_ref[...].astype(BF16)], axis=1)
                qb = jnp.concatenate([(q1_ref[...] * scale).astype(BF16), (eq_ref[...] * scale).astype(BF16)],
                                     axis=1)
            else:
                kb, qb = k1_ref[...].astype(BF16), (q1_ref[...] * scale).astype(BF16)
            st = lax.dot_general(kb, qb, NT_DIMS, preferred_element_type=F32)
            if kind == "fox":
                st = st - ek_ref[...]
            if masked:
                pos_k = j * T + lax.broadcasted_iota(jnp.int32, (T, 1), 0)
                pos_q = i * T + lax.broadcasted_iota(jnp.int32, (1, T), 1)
                st = jnp.where(_visibility_id(pos_k, kind, l_real) <= _visibility_id(pos_q, kind, l_real), st, NEG)
            pt = jnp.exp(st - lse_ref[...])
            dob = do_ref[...].astype(BF16)
            accv_ref[...] += lax.dot_general(pt.astype(BF16), dob, NN_DIMS, preferred_element_type=F32)
            dpt = lax.dot_general(v_ref[...].astype(BF16), dob, NT_DIMS, preferred_element_type=F32)
            dst = pt * (dpt - delta_ref[...])
            dsb = dst.astype(BF16)
            dk = lax.dot_general(dsb, qb, NN_DIMS, preferred_element_type=F32)
            dq = lax.dot_general(dsb, kb, TN_DIMS, preferred_element_type=F32)
            if kind == "mla":
                acck_ref[...] += dk[:, :HEAD_DIM]
                acc3_ref[...] += dk[:, HEAD_DIM:]
                dq1_ref[q_rows, :] += dq[:, :HEAD_DIM]
                dq2_ref[q_rows, :] += dq[:, HEAD_DIM:]
            else:
                acck_ref[...] += dk
                dq1_ref[q_rows, :] += dq
                acc3_ref[...] -= jnp.sum(dst, axis=1, keepdims=True)

        @pl.when(i > j)
        def _():
            block(False)

        @pl.when(i <= j)
        def _():
            block(True)

        @pl.when(i == nb - 1)
        def _():
            dk1_ref[...] = acck_ref[...]
            dv_ref[...] = accv_ref[...]
            third_ref[...] = acc3_ref[...]

        @pl.when(t == n_pairs - 1)
        def _():
            dq1_ref[...] = dq1_ref[...] * scale
            if kind == "mla":
                dq2_ref[...] = dq2_ref[...] * scale

    return _attention_call(
        body, n_pairs, in_specs, out_specs, out_shape,
        [pltpu.VMEM((T, HEAD_DIM), F32), pltpu.VMEM((T, HEAD_DIM), F32), third_scratch],
        kind + "_attn_bwd", tables, (q1, k1, v, *extras, do, lse_row, delta_row))


def _as_row(col):
    return col.reshape(col.shape[0], 1, col.shape[1])


@functools.partial(jax.custom_vjp, nondiff_argnums=(0,))
def mla_attention(l_real, qn, qr, kn, kr, v):
    return _attention_forward("mla", l_real, qn, kn, v, (qr, kr))[0]


def _mla_attention_fwd(l_real, qn, qr, kn, kr, v):
    o, lse = _attention_forward("mla", l_real, qn, kn, v, (qr, kr))
    return o, (qn, qr, kn, kr, v, o, lse)


def _mla_attention_bwd(l_real, res, do):
    qn, qr, kn, kr, v, o, lse = res
    delta = _attention_delta(o, do)
    dkn, dv, dkr_heads, dqn, dqr = _attention_backward("mla", l_real, qn, kn, v, (qr, kr), do, _as_row(lse),
                                                       _as_row(delta))
    return dqn, dqr, dkn, jnp.sum(dkr_heads, axis=0), dv


mla_attention.defvjp(_mla_attention_fwd, _mla_attention_bwd)


@functools.partial(jax.custom_vjp, nondiff_argnums=(0,))
def fox_attention(l_real, q, k, v, c):
    return _attention_forward("fox", l_real, q, k, v, (_as_row(c),))[0]


def _fox_attention_fwd(l_real, q, k, v, c):
    o, lse = _attention_forward("fox", l_real, q, k, v, (_as_row(c),))
    return o, (q, k, v, c, o, lse)


def _fox_attention_bwd(l_real, res, do):
    q, k, v, c, o, lse = res
    delta = _attention_delta(o, do)
    dk, dv, dc, dq = _attention_backward("fox", l_real, q, k, v, (c,), do, _as_row(lse), _as_row(delta))
    return dq, dk, dv, dc


fox_attention.defvjp(_fox_attention_fwd, _fox_attention_bwd)


GELU_C0 = 0.7978845608028654
GELU_C1 = 0.044715


def _shift_rows(x, prev, s):
    r = pltpu.roll(x, s, 0)
    pr = pltpu.roll(prev, s, 0)
    row = lax.broadcasted_iota(jnp.int32, prev.shape, 0)
    top = jnp.where(row < s, pr, r[0:SUBLANES])
    return jnp.concatenate([top, r[SUBLANES:]], axis=0)


def _conv_tiles(L, f):
    return _pick(L, _TILE_ATT), _pick(f, _TILE_FF)


def _conv_gate_forward(u, w, b, out_dtype):
    L, f2 = u.shape
    f = f2 // 2
    tm, tn = _conv_tiles(L, f)
    rb = tm // SUBLANES

    def body(u_ref, up_ref, w_ref, b_ref, o_ref):
        i = pl.program_id(1)
        x = u_ref[...]
        prev = jnp.where(i > 0, up_ref[...], 0.0)
        wv = w_ref[...]
        hc = b_ref[...] + ((wv[0:1] * _shift_rows(x, prev, 2) + wv[1:2] * _shift_rows(x, prev, 1)) + wv[2:3] * x)
        g = hc[:, :tn]
        gelu = 0.5 * g * (1.0 + jnp.tanh(GELU_C0 * (g + GELU_C1 * g * g * g)))
        o_ref[...] = (gelu * hc[:, tn:]).astype(o_ref.dtype)

    return pl.pallas_call(
        body, grid=(f // tn, L // tm),
        in_specs=[pl.BlockSpec((tm, 2 * tn), lambda j, i: (i, j)),
                  pl.BlockSpec((SUBLANES, 2 * tn), lambda j, i: (jnp.maximum(i * rb - 1, 0), j)),
                  pl.BlockSpec((3, 2 * tn), lambda j, i: (0, j)),
                  pl.BlockSpec((1, 2 * tn), lambda j, i: (0, j))],
        out_specs=pl.BlockSpec((tm, tn), lambda j, i: (i, j)),
        out_shape=jax.ShapeDtypeStruct((L, f), out_dtype),
        compiler_params=_params("parallel", "parallel"), name="conv_gate_fwd",
    )(u, u, w, b)


def _conv_gate_backward(u, w, b, dact, du_dtype):
    L, f2 = u.shape
    f = f2 // 2
    tm, tn = _conv_tiles(L, f)
    rb = tm // SUBLANES
    n_row_blocks = L // SUBLANES
    n_i = L // tm
    ext = tm + SUBLANES

    def next_rows(i):
        return jnp.minimum((i + 1) * rb, n_row_blocks - 1)

    def body(u_ref, up_ref, un_ref, da_ref, dan_ref, w_ref, b_ref, du_ref, dwb_ref):
        i = pl.program_id(1)
        is_last = i == n_i - 1
        prev = jnp.where(i > 0, up_ref[...], 0.0)
        xe = jnp.concatenate([u_ref[...], jnp.where(is_last, 0.0, un_ref[...])], axis=0)
        x1 = _shift_rows(xe, prev, 1)
        x2 = _shift_rows(xe, prev, 2)
        wv = w_ref[...]
        hc = b_ref[...] + ((wv[0:1] * x2 + wv[1:2] * x1) + wv[2:3] * xe)
        g, up = hc[:, :tn], hc[:, tn:]
        da = jnp.concatenate([da_ref[...], jnp.where(is_last, 0.0, dan_ref[...])], axis=0)
        t = jnp.tanh(GELU_C0 * (g + GELU_C1 * g * g * g))
        gelu = 0.5 * g * (1.0 + t)
        dgelu = 0.5 * (1.0 + t) + 0.5 * g * (1.0 - t * t) * (GELU_C0 * (1.0 + 3.0 * GELU_C1 * g * g))
        dh = jnp.concatenate([da * up * dgelu, da * gelu], axis=1)
        dh1 = pltpu.roll(dh, ext - 1, 0)
        dh2 = pltpu.roll(dh, ext - 2, 0)
        du_ref[...] = ((wv[2:3] * dh + wv[1:2] * dh1) + wv[0:1] * dh2)[:tm].astype(du_ref.dtype)
        dw0 = jnp.sum((dh * x2)[:tm], axis=0, keepdims=True)
        dw1 = jnp.sum((dh * x1)[:tm], axis=0, keepdims=True)
        dw2 = jnp.sum((dh * xe)[:tm], axis=0, keepdims=True)
        db = jnp.sum(dh[:tm], axis=0, keepdims=True)
        row = lax.broadcasted_iota(jnp.int32, (SUBLANES, 2 * tn), 0)
        upd = jnp.where(row == 0, dw0, jnp.where(row == 1, dw1, jnp.where(row == 2, dw2,
                        jnp.where(row == 3, db, 0.0))))

        @pl.when(i == 0)
        def _():
            dwb_ref[...] = jnp.zeros_like(dwb_ref)

        dwb_ref[...] += upd

    return pl.pallas_call(
        body, grid=(f // tn, n_i),
        in_specs=[pl.BlockSpec((tm, 2 * tn), lambda j, i: (i, j)),
                  pl.BlockSpec((SUBLANES, 2 * tn), lambda j, i: (jnp.maximum(i * rb - 1, 0), j)),
                  pl.BlockSpec((SUBLANES, 2 * tn), lambda j, i: (next_rows(i), j)),
                  pl.BlockSpec((tm, tn), lambda j, i: (i, j)),
                  pl.BlockSpec((SUBLANES, tn), lambda j, i: (next_rows(i), j)),
                  pl.BlockSpec((3, 2 * tn), lambda j, i: (0, j)),
                  pl.BlockSpec((1, 2 * tn), lambda j, i: (0, j))],
        out_specs=[pl.BlockSpec((tm, 2 * tn), lambda j, i: (i, j)),
                   pl.BlockSpec((SUBLANES, 2 * tn), lambda j, i: (0, j))],
        out_shape=[jax.ShapeDtypeStruct((L, f2), du_dtype), jax.ShapeDtypeStruct((SUBLANES, f2), F32)],
        compiler_params=_params("parallel", "arbitrary"), name="conv_gate_bwd",
    )(u, u, u, dact, dact, w, b)


@jax.custom_vjp
def conv_ffn(h, g, w_up, w_conv, b_conv, w_down):
    return _conv_ffn_fwd(h, g, w_up, w_conv, b_conv, w_down)[0]


def _conv_ffn_fwd(h, g, w_up, w_conv, b_conv, w_down):
    y = _rms_forward(h, g, BF16)
    u = _matmul(y, w_up, "nn", F32, "linear_fwd")
    act = _conv_gate_forward(u, w_conv, b_conv.reshape(1, -1), BF16)
    return _matmul(act, w_down, "nn", F32, "linear_fwd"), (h, g, w_up, w_conv, b_conv, w_down, y, u, act)


def _conv_ffn_bwd(res, df):
    h, g, w_up, w_conv, b_conv, w_down, y, u, act = res
    dact = _matmul(df, w_down, "nt", F32, "linear_dx")
    dw_down = _matmul(act, df, "tn", w_down.dtype, "linear_dw")
    du, dwb = _conv_gate_backward(u, w_conv, b_conv.reshape(1, -1), dact, BF16)
    dy = _matmul(du, w_up, "nt", F32, "linear_dx")
    dw_up = _matmul(y, du, "tn", w_up.dtype, "linear_dw")
    dh, dg = _rms_backward(h, g, dy)
    return dh, dg, dw_up, dwb[0:3], dwb[3], dw_down


conv_ffn.defvjp(_conv_ffn_fwd, _conv_ffn_bwd)


def _loss_rows(y, target):
    rows, d = y.shape
    tr = _row_tile(rows, d)

    def body(y_ref, t_ref, loss_ref, dy_ref):
        err = y_ref[...] - t_ref[...]
        loss_ref[...] = 0.5 * jnp.mean(err * err, axis=-1, keepdims=True)
        dy_ref[...] = err * (1.0 / d)

    return pl.pallas_call(
        body, grid=(rows // tr,),
        in_specs=[pl.BlockSpec((tr, d), lambda i: (i, 0)), pl.BlockSpec((tr, d), lambda i: (i, 0))],
        out_specs=[pl.BlockSpec((tr, 1), lambda i: (i, 0)), pl.BlockSpec((tr, d), lambda i: (i, 0))],
        out_shape=[jax.ShapeDtypeStruct((rows, 1), F32), jax.ShapeDtypeStruct((rows, d), F32)],
        compiler_params=_params("parallel"), name="loss_head",
    )(y, target)


@jax.custom_vjp
def token_loss(y, target):
    return jnp.sum(_loss_rows(y, target)[0])


def _token_loss_fwd(y, target):
    rows, dy = _loss_rows(y, target)
    return jnp.sum(rows), dy


def _token_loss_bwd(dy, ct):
    return ct * dy, -ct * dy


token_loss.defvjp(_token_loss_fwd, _token_loss_bwd)


def _cols_from_devices(g):
    k = g.shape[1]
    return jnp.transpose(g, (1, 0, 2)).reshape(k, -1)


def _interleave_gate_up(a, f):
    tn = _pick(f, _TILE_FF)
    lead = a.shape[:-1]
    a = a.reshape(lead + (2, f // tn, tn))
    return jnp.swapaxes(a, -3, -2).reshape(lead + (2 * f,))


def _rope(x, cos, sin):
    half = x.shape[-1] // 2
    x1, x2 = x[..., :half], x[..., half:]
    return jnp.concatenate([x1 * cos - x2 * sin, x2 * cos + x1 * sin], axis=-1)


PROJ_BOUNDS = ((0, 512), (512, 1024), (1024, 2048), (2048, 3072), (3072, 4096), (4096, 5120), (5120, 5184),
               (5184, 5192))


def _layer(h, big, small, conv_w, l, l_real, cos, sin):
    L, d = h.shape
    w_in = _cols_from_devices(big["w_in"])
    w_in = jnp.concatenate([w_in[:, :1024], w_in[:, 1088:5184], w_in[:, 1024:1088], w_in[:, 5184:],
                            jnp.zeros((d, IN_COLS_PADDED - IN_COLS), w_in.dtype)], axis=1)
    w_q_up = _cols_from_devices(big["w_q_up"]).reshape(MLA_Q_LORA, HEADS, HEAD_DIM + ROPE_DIM)
    w_q_up = jnp.concatenate([w_q_up[:, :, :HEAD_DIM].reshape(MLA_Q_LORA, -1),
                              w_q_up[:, :, HEAD_DIM:].reshape(MLA_Q_LORA, -1)], axis=1)
    w_kv_up = _cols_from_devices(big["w_kv_up"]).reshape(MLA_KV_LORA, HEADS, 2 * HEAD_DIM)
    w_kv_up = jnp.concatenate([w_kv_up[:, :, :HEAD_DIM].reshape(MLA_KV_LORA, -1),
                               w_kv_up[:, :, HEAD_DIM:].reshape(MLA_KV_LORA, -1)], axis=1)
    w_out = big["w_out"].reshape(-1, d)
    f = big["w_ffn_down"].shape[0] * big["w_ffn_down"].shape[1]
    w_ffn_up = _interleave_gate_up(_cols_from_devices(big["w_ffn_up"]), f)
    w_ffn_down = big["w_ffn_down"].reshape(f, d)
    w_conv = _interleave_gate_up(conv_w, f)
    b_conv = _interleave_gate_up(small["b_ffn_conv"][l], f)

    c_q, c_kv, fq, fk, fv, fg, k_rope, ff = split_cols(norm_linear(h, small["ln_mix_pre"][l], w_in), PROJ_BOUNDS)

    q = norm_linear(c_q, small["g_q_latent"][l], w_q_up)
    qn, qr = split_cols(q, ((0, HEADS * HEAD_DIM), (HEADS * HEAD_DIM, HEADS * (HEAD_DIM + ROPE_DIM))))
    kv = norm_linear(c_kv, small["g_kv_latent"][l], w_kv_up)
    kn, v = split_cols(kv, ((0, HEADS * HEAD_DIM), (HEADS * HEAD_DIM, 2 * HEADS * HEAD_DIM)))
    qr = jnp.transpose(_rope(qr.reshape(L, HEADS, ROPE_DIM), cos[:, None, :], sin[:, None, :]), (1, 0, 2))
    kr = _rope(k_rope, cos, sin)
    a = mla_attention(l_real, qn, qr, kn, kr, v)

    fqn = rms_norm(fq, small["g_fox_q"][l])
    fkn = rms_norm(fk, small["g_fox_k"][l])
    log_f = jax.nn.log_sigmoid(ff + small["b_forget"][l])
    c = jnp.cumsum(log_f, axis=0).T[:, :, None]
    bmix = fox_attention(l_real, fqn, fkn, fv, c) * jax.nn.sigmoid(fg)

    mix = linear(jnp.concatenate([a, bmix], axis=1), w_out)
    h = h + rms_norm(mix, small["ln_mix_post"][l])

    f_out = conv_ffn(h, small["ln_ffn_pre"][l], w_ffn_up, w_conv, b_conv, w_ffn_down)
    h = h + rms_norm(f_out, small["ln_ffn_post"][l])
    return h


def _local_loss(big, small, meta, conv_w, x, target):
    s, d = x.shape
    l_real = N_META + s
    l_pad = -(-l_real // Q_BLOCK) * Q_BLOCK
    h = jnp.concatenate([meta, x, jnp.zeros((l_pad - l_real, d), F32)], axis=0)
    half = ROPE_DIM // 2
    inv_freq = ROPE_THETA ** (-jnp.arange(half, dtype=F32) / half)
    ang = jnp.arange(l_pad, dtype=jnp.int32).astype(F32)[:, None] * inv_freq[None, :]
    cos, sin = jnp.cos(ang), jnp.sin(ang)
    for l in range(DEPTH):
        h = _layer(h, big[l], small, conv_w[l], l, l_real, cos, sin)
    return token_loss(h[N_META:l_real], target)


ANY_SPACE = pl.BlockSpec(memory_space=pl.ANY)


def _place():
    ix, iy, ic = lax.axis_index("x"), lax.axis_index("y"), lax.axis_index("c")
    return ix, iy, ic, [(1 - ix, iy), (ix, 1 - iy), (1 - ix, 1 - iy)]


def _comm_call(body, arrays, out_shapes, n_remote, n_local, name):
    return pl.pallas_call(
        body, out_shape=out_shapes, in_specs=[ANY_SPACE] * len(arrays), out_specs=[ANY_SPACE] * len(out_shapes),
        scratch_shapes=[pltpu.SemaphoreType.DMA((n_remote,)), pltpu.SemaphoreType.DMA((n_remote,)),
                        pltpu.SemaphoreType.DMA((n_local,))],
        name=name,
    )(*arrays)


def _gather(arrays, name):
    n = len(arrays)

    def body(*refs):
        xs, outs = refs[:n], refs[n:2 * n]
        send_sems, recv_sems, local_sems = refs[2 * n:]
        ix, iy, ic, chips = _place()
        me, sibling = (ix, iy, ic), (ix, iy, 1 - ic)

        def copy(a, k, block, to, src=None):
            dst = outs[a].at[4 * block[0] + 2 * block[1] + block[2]]
            return pltpu.make_async_remote_copy(
                src_ref=dst if src is None else src, dst_ref=dst, send_sem=send_sems.at[7 * a + k],
                recv_sem=recv_sems.at[7 * a + k], device_id=to, device_id_type=MESH_ID)

        local, sent = [], []
        for a in range(n):
            mine = pltpu.make_async_copy(xs[a], outs[a].at[4 * ix + 2 * iy + ic], local_sems.at[a])
            mine.start()
            local.append(mine)
            first = [copy(a, 0, me, sibling, src=xs[a])]
            first += [copy(a, 1 + j, me, (*chip, ic), src=xs[a]) for j, chip in enumerate(chips)]
            for cp in first:
                cp.start()
            sent += first
        for a in range(n):
            for j, chip in enumerate(chips):
                copy(a, 1 + j, (*chip, ic), me).wait_recv()
                passed = copy(a, 4 + j, (*chip, ic), sibling)
                passed.start()
                sent.append(passed)
        for a in range(n):
            copy(a, 0, sibling, me).wait_recv()
            for j, chip in enumerate(chips):
                copy(a, 4 + j, (*chip, 1 - ic), me).wait_recv()
        for cp in sent:
            cp.wait_send()
        for cp in local:
            cp.wait()

    out_shapes = [jax.ShapeDtypeStruct((N_DEV,) + a.shape, a.dtype) for a in arrays]
    return _comm_call(body, arrays, out_shapes, 7 * n, n, name)


def _swap_with_sibling(arrays, name):
    n = len(arrays)

    def body(*refs):
        xs, outs = refs[:n], refs[n:2 * n]
        send_sems, recv_sems, _ = refs[2 * n:]
        ix, iy, ic, _ = _place()
        copies = [pltpu.make_async_remote_copy(
            src_ref=xs[a], dst_ref=outs[a], send_sem=send_sems.at[a], recv_sem=recv_sems.at[a],
            device_id=(ix, iy, 1 - ic), device_id_type=MESH_ID) for a in range(n)]
        for cp in copies:
            cp.start()
        for cp in copies:
            cp.wait()

    out_shapes = [jax.ShapeDtypeStruct(a.shape, a.dtype) for a in arrays]
    return _comm_call(body, arrays, out_shapes, n, 1, name)


def _exchange_chips(arrays, name):
    n = len(arrays)

    def body(*refs):
        xs, outs = refs[:n], refs[n:2 * n]
        send_sems, recv_sems, local_sems = refs[2 * n:]
        ix, iy, ic, chips = _place()
        my_chip = 2 * ix + iy
        local, sent = [], []
        for a in range(n):
            mine = pltpu.make_async_copy(xs[a].at[my_chip], outs[a].at[my_chip], local_sems.at[a])
            mine.start()
            local.append(mine)
            for j, chip in enumerate(chips):
                cp = pltpu.make_async_remote_copy(
                    src_ref=xs[a].at[2 * chip[0] + chip[1]], dst_ref=outs[a].at[my_chip],
                    send_sem=send_sems.at[3 * a + j], recv_sem=recv_sems.at[3 * a + j],
                    device_id=(*chip, ic), device_id_type=MESH_ID)
                cp.start()
                sent.append(cp)
        for a in range(n):
            for j, chip in enumerate(chips):
                pltpu.make_async_remote_copy(
                    src_ref=xs[a].at[my_chip], dst_ref=outs[a].at[2 * chip[0] + chip[1]],
                    send_sem=send_sems.at[3 * a + j], recv_sem=recv_sems.at[3 * a + j],
                    device_id=(*chip, ic), device_id_type=MESH_ID).wait_recv()
        for cp in sent:
            cp.wait_send()
        for cp in local:
            cp.wait()

    out_shapes = [jax.ShapeDtypeStruct(a.shape, a.dtype) for a in arrays]
    return _comm_call(body, arrays, out_shapes, 3 * n, n, name)


def _sum_slots(x, out_dtype, name):
    slots, rows, cols = x.shape
    tr = _row_tile(rows, cols, (2 << 20) // slots, 16)

    def body(x_ref, o_ref):
        acc = x_ref[0].astype(F32)
        for s in range(1, slots):
            acc = acc + x_ref[s].astype(F32)
        o_ref[...] = acc.astype(o_ref.dtype)

    return pl.pallas_call(
        body, grid=(rows // tr,), in_specs=[pl.BlockSpec((slots, tr, cols), lambda i: (0, i, 0))],
        out_specs=pl.BlockSpec((tr, cols), lambda i: (i, 0)), out_shape=jax.ShapeDtypeStruct((rows, cols), out_dtype),
        compiler_params=_params("parallel"), name=name,
    )(x)


def _add_pairs(a, b, name):
    slots, rows, cols = a.shape
    tr = _row_tile(rows, cols, 1 << 20, 16)

    def body(a_ref, b_ref, o_ref):
        o_ref[...] = (a_ref[...].astype(F32) + b_ref[...].astype(F32)).astype(o_ref.dtype)

    spec = pl.BlockSpec((None, tr, cols), lambda s, i: (s, i, 0))
    return pl.pallas_call(
        body, grid=(slots, rows // tr), in_specs=[spec, spec], out_specs=spec,
        out_shape=jax.ShapeDtypeStruct(a.shape, BF16), compiler_params=_params("parallel", "parallel"), name=name,
    )(a, b)


def _reduce_scatter(grads, ic):
    by_chip = [g.reshape((4, 2) + g.shape[1:]) for g in grads]
    keep = [lax.dynamic_index_in_dim(g, ic, axis=1, keepdims=False) for g in by_chip]
    give = [lax.dynamic_index_in_dim(g, 1 - ic, axis=1, keepdims=False) for g in by_chip]
    got = _swap_with_sibling(give, "scatter_sibling")
    pairs = [_add_pairs(k, g, "add_pairs") for k, g in zip(keep, got)]
    received = _exchange_chips(pairs, "scatter_chips")
    return [_sum_slots(r, F32, "sum_grads") for r in received]


def _pack(arrays, dtype, row_multiple):
    flat = jnp.concatenate([a.astype(dtype).reshape(-1) for a in arrays])
    n = flat.shape[0]
    quantum = row_multiple * FLAT_COLS
    padded = -(-n // quantum) * quantum
    return jnp.pad(flat, (0, padded - n)).reshape(padded // FLAT_COLS, FLAT_COLS)


def _unpack(buf, shapes):
    flat = buf.reshape(-1)
    out, off = [], 0
    for shp in shapes:
        n = 1
        for s in shp:
            n *= s
        out.append(flat[off:off + n].reshape(tuple(shp)))
        off += n
    return out


def _adamw(w, g, m, v, name):
    shape = w.shape
    cols = shape[-1]
    w2, g2, m2, v2 = (a.reshape(-1, cols) for a in (w, g, m, v))
    rows = w2.shape[0]
    tr = _row_tile(rows, cols, 1 << 20)

    def body(w_ref, g_ref, m_ref, v_ref, d_ref, nm_ref, nv_ref):
        gv = g_ref[...]
        nm = ADAM_B1 * m_ref[...] + (1.0 - ADAM_B1) * gv
        nv = ADAM_B2 * v_ref[...] + (1.0 - ADAM_B2) * (gv * gv)
        m_hat = nm / (1.0 - ADAM_B1 ** ADAM_STEP)
        v_hat = nv / (1.0 - ADAM_B2 ** ADAM_STEP)
        d_ref[...] = -ADAM_LR * (m_hat / (jnp.sqrt(v_hat) + ADAM_EPS) + ADAM_WD * w_ref[...])
        nm_ref[...] = nm
        nv_ref[...] = nv

    spec = pl.BlockSpec((tr, cols), lambda i: (i, 0))
    outs = pl.pallas_call(
        body, grid=(rows // tr,), in_specs=[spec] * 4, out_specs=[spec] * 3,
        out_shape=[jax.ShapeDtypeStruct((rows, cols), F32)] * 3,
        compiler_params=_params("parallel"), name=name,
    )(w2, g2, m2, v2)
    return tuple(o.reshape(shape) for o in outs)


BIG = ("w_in", "w_q_up", "w_kv_up", "w_out", "w_ffn_up", "w_ffn_down")
REPLICATED = ("ln_mix_pre", "b_forget", "g_q_latent", "g_kv_latent", "g_fox_q", "g_fox_k", "ln_mix_post",
              "ln_ffn_pre", "b_ffn_conv", "ln_ffn_post")
WEIGHTS = ("meta_tokens", "ln_mix_pre", "w_in", "b_forget", "g_q_latent", "g_kv_latent", "w_q_up", "w_kv_up",
           "g_fox_q", "g_fox_k", "w_out", "ln_mix_post", "ln_ffn_pre", "w_ffn_up", "w_ffn_conv", "b_ffn_conv",
           "w_ffn_down", "ln_ffn_post")


def kernel(x, meta_tokens, ln_mix_pre, w_in, b_forget, g_q_latent, g_kv_latent, w_q_up, w_kv_up, g_fox_q, g_fox_k, w_out, ln_mix_post, ln_ffn_pre, w_ffn_up, w_ffn_conv, b_ffn_conv, w_ffn_down, ln_ffn_post, loss_target, m_meta_tokens, m_ln_mix_pre, m_w_in, m_b_forget, m_g_q_latent, m_g_kv_latent, m_w_q_up, m_w_kv_up, m_g_fox_q, m_g_fox_k, m_w_out, m_ln_mix_post, m_ln_ffn_pre, m_w_ffn_up, m_w_ffn_conv, m_b_ffn_conv, m_w_ffn_down, m_ln_ffn_post, v_meta_tokens, v_ln_mix_pre, v_w_in, v_b_forget, v_g_q_latent, v_g_kv_latent, v_w_q_up, v_w_kv_up, v_g_fox_q, v_g_fox_k, v_w_out, v_ln_mix_post, v_ln_ffn_pre, v_w_ffn_up, v_w_ffn_conv, v_b_ffn_conv, v_w_ffn_down, v_ln_ffn_post):
    w = dict(meta_tokens=meta_tokens, ln_mix_pre=ln_mix_pre, w_in=w_in, b_forget=b_forget, g_q_latent=g_q_latent,
             g_kv_latent=g_kv_latent, w_q_up=w_q_up, w_kv_up=w_kv_up, g_fox_q=g_fox_q, g_fox_k=g_fox_k, w_out=w_out,
             ln_mix_post=ln_mix_post, ln_ffn_pre=ln_ffn_pre, w_ffn_up=w_ffn_up, w_ffn_conv=w_ffn_conv,
             b_ffn_conv=b_ffn_conv, w_ffn_down=w_ffn_down, ln_ffn_post=ln_ffn_post)
    mom = dict(meta_tokens=m_meta_tokens, ln_mix_pre=m_ln_mix_pre, w_in=m_w_in, b_forget=m_b_forget,
               g_q_latent=m_g_q_latent, g_kv_latent=m_g_kv_latent, w_q_up=m_w_q_up, w_kv_up=m_w_kv_up,
               g_fox_q=m_g_fox_q, g_fox_k=m_g_fox_k, w_out=m_w_out, ln_mix_post=m_ln_mix_post,
               ln_ffn_pre=m_ln_ffn_pre, w_ffn_up=m_w_ffn_up, w_ffn_conv=m_w_ffn_conv, b_ffn_conv=m_b_ffn_conv,
               w_ffn_down=m_w_ffn_down, ln_ffn_post=m_ln_ffn_post)
    var = dict(meta_tokens=v_meta_tokens, ln_mix_pre=v_ln_mix_pre, w_in=v_w_in, b_forget=v_b_forget,
               g_q_latent=v_g_q_latent, g_kv_latent=v_g_kv_latent, w_q_up=v_w_q_up, w_kv_up=v_w_kv_up,
               g_fox_q=v_g_fox_q, g_fox_k=v_g_fox_k, w_out=v_w_out, ln_mix_post=v_ln_mix_post,
               ln_ffn_pre=v_ln_ffn_pre, w_ffn_up=v_w_ffn_up, w_ffn_conv=v_w_ffn_conv, b_ffn_conv=v_b_ffn_conv,
               w_ffn_down=v_w_ffn_down, ln_ffn_post=v_ln_ffn_post)
    ic = lax.axis_index("c")
    me = 4 * lax.axis_index("x") + 2 * lax.axis_index("y") + ic

    gathered = _gather([w[n].astype(BF16) for n in BIG] + [meta_tokens, w_ffn_conv], "gather_weights")
    big = [{n: gathered[k][:, l] for k, n in enumerate(BIG)} for l in range(DEPTH)]
    meta_shape, conv_shape = meta_tokens.shape, w_ffn_conv.shape
    meta_full = _cols_from_devices(gathered[len(BIG)])
    conv_full = jnp.transpose(gathered[len(BIG) + 1], (1, 2, 0, 3)).reshape(DEPTH, conv_shape[1], -1)
    small = {n: w[n] for n in REPLICATED}

    loss, grads = jax.value_and_grad(_local_loss, argnums=(0, 1, 2, 3, 4))(
        big, small, meta_full, [conv_full[l] for l in range(DEPTH)], x[0], loss_target[0])
    g_big, g_small, g_meta, g_conv, g_x = grads
    loss = lax.psum(loss, ("x", "y", "c"))

    grad = {}
    per_layer = [_reduce_scatter([g_big[l][n] for n in BIG], ic) for l in range(DEPTH)]
    for k, n in enumerate(BIG):
        grad[n] = jnp.stack([per_layer[l][k] for l in range(DEPTH)])

    small_arrays = [g_small[n] for n in REPLICATED] + [g_meta, jnp.stack(g_conv)]
    small_shapes = [a.shape for a in small_arrays]
    partials = _gather([_pack(small_arrays, F32, 16)], "gather_small_grads")[0]
    summed = _unpack(_sum_slots(partials, F32, "sum_small_grads"), small_shapes)
    for n, g in zip(REPLICATED, summed):
        grad[n] = g
    grad["meta_tokens"] = lax.dynamic_slice_in_dim(summed[-2], me * meta_shape[1], meta_shape[1], axis=1)
    grad["w_ffn_conv"] = lax.dynamic_slice_in_dim(summed[-1], me * conv_shape[2], conv_shape[2], axis=2)

    delta, new_m, new_v = {}, {}, {}
    for n in BIG:
        delta[n], new_m[n], new_v[n] = _adamw(w[n], grad[n], mom[n], var[n], "adamw_" + n)
    rest = [n for n in WEIGHTS if n not in BIG]
    rest_shapes = [w[n].shape for n in rest]
    flat = [_pack([src[n] for n in rest], F32, SUBLANES) for src in (w, grad, mom, var)]
    outs = _adamw(*flat, "adamw_small")
    for dst, buf in zip((delta, new_m, new_v), outs):
        for n, a in zip(rest, _unpack(buf, rest_shapes)):
            dst[n] = a

    return (loss, g_x[None], *[grad[n] for n in WEIGHTS], *[delta[n] for n in WEIGHTS],
            *[new_m[n] for n in WEIGHTS], *[new_v[n] for n in WEIGHTS])
```

```python
import functools

import jax
import jax.numpy as jnp
from jax import lax
from jax.experimental import pallas as pl
from jax.experimental.pallas import tpu as pltpu

F32 = jnp.float32
BF16 = jnp.bfloat16
MESH_ID = pl.DeviceIdType.MESH

N_DEV = 8
DEPTH = 4
N_META = 16
CHUNK = 64
Q_BLOCK = 128
HEADS = 8
HEAD_DIM = 128
ROPE_DIM = 64
MLA_Q_LORA = 512
MLA_KV_LORA = 512
FOX_W = HEADS * HEAD_DIM
ROPE_THETA = 10000.0
EPS = 1e-6
NEG = -1e30
IN_COLS = 5192
IN_COLS_PADDED = 5376

ADAM_LR = 0.001
ADAM_B1 = 0.9
ADAM_B2 = 0.999
ADAM_EPS = 1e-08
ADAM_WD = 0.01
ADAM_STEP = 10

LANES = 128
SUBLANES = 8
FLAT_COLS = 1024
VMEM_LIMIT_V7X = 52 * 1024 * 1024

NT_DIMS = (((1,), (1,)), ((), ()))
NN_DIMS = (((1,), (0,)), ((), ()))
TN_DIMS = (((0,), (0,)), ((), ()))

_TILE_ATT = (640, 512, 384, 256, 128)
_TILE_FF = (512, 256, 128)


def _pick(n, candidates):
    for c in candidates:
        if n % c == 0:
            return c
    return n


def _row_tile(rows, cols, budget_bytes=2 << 20, align=SUBLANES):
    best = None
    for t in range(align, rows + 1, align):
        if rows % t == 0 and t * cols * 4 <= budget_bytes:
            best = t
    return best if best is not None else rows


def _params(*semantics):
    return pltpu.CompilerParams(dimension_semantics=semantics, vmem_limit_bytes=VMEM_LIMIT_V7X)


MATMUL_VMEM_BUDGET = 36 << 20
MXU_FLOPS_V7X = 9.0e14
HBM_BYTES_PER_S_V7X = 2.5e12
GRID_STEP_S = 0.35e-6
MXU_DIM = 256


def _tile_candidates(n, cap):
    c = [t for t in range(LANES, min(n, cap) + 1, LANES) if n % t == 0]
    return c if c else [n]


def _matmul_tiles(m, n, c, a_bytes, b_bytes, o_bytes):
    best, best_cost = None, None
    for tc in _tile_candidates(c, 4096):
        steps = c // tc
        for tm in _tile_candidates(m, 2048):
            for tn in _tile_candidates(n, 2048):
                vmem = 2 * (tm * tc * a_bytes + tc * tn * b_bytes + tm * tn * o_bytes)
                vmem += tm * tn * 4 if steps > 1 else 0
                if vmem > MATMUL_VMEM_BUDGET:
                    continue
                traffic = m * c * a_bytes * (1 if steps == 1 else n // tn) + c * n * b_bytes * (m // tm)
                traffic += m * n * o_bytes
                grid = (m // tm) * (n // tn) * steps
                accumulate = 0 if steps == 1 else grid * tm * tn * 8 / 4.0e12
                fill = (-(-tn // MXU_DIM) * MXU_DIM / tn) * (-(-tc // MXU_DIM) * MXU_DIM / tc)
                cost = max(2.0 * m * n * c * fill / MXU_FLOPS_V7X, traffic / HBM_BYTES_PER_S_V7X)
                cost += grid * GRID_STEP_S + accumulate
                if best_cost is None or cost < best_cost:
                    best, best_cost = (tm, tn, tc), cost
    return best


def _matmul(a, b, mode, out_dtype, name):
    if mode == "nn":
        (m, c), (c2, n) = a.shape, b.shape
    elif mode == "nt":
        (m, c), (n, c2) = a.shape, b.shape
    else:
        (c, m), (c2, n) = a.shape, b.shape
    assert c == c2, (a.shape, b.shape, mode)
    tm, tn, tc = _matmul_tiles(m, n, c, a.dtype.itemsize, b.dtype.itemsize, jnp.dtype(out_dtype).itemsize)
    steps = c // tc
    if mode == "nn":
        a_spec = pl.BlockSpec((tm, tc), lambda i, j, k: (i, k))
        b_spec = pl.BlockSpec((tc, tn), lambda i, j, k: (k, j))
        dims = NN_DIMS
    elif mode == "nt":
        a_spec = pl.BlockSpec((tm, tc), lambda i, j, k: (i, k))
        b_spec = pl.BlockSpec((tn, tc), lambda i, j, k: (j, k))
        dims = NT_DIMS
    else:
        a_spec = pl.BlockSpec((tc, tm), lambda i, j, k: (k, i))
        b_spec = pl.BlockSpec((tc, tn), lambda i, j, k: (k, j))
        dims = TN_DIMS

    def body(a_ref, b_ref, o_ref, acc_ref):
        k = pl.program_id(2)

        @pl.when(k == 0)
        def _():
            acc_ref[...] = jnp.zeros_like(acc_ref)

        acc_ref[...] += lax.dot_general(a_ref[...].astype(BF16), b_ref[...].astype(BF16), dims,
                                        preferred_element_type=F32)

        @pl.when(k == steps - 1)
        def _():
            o_ref[...] = acc_ref[...].astype(o_ref.dtype)

    def body_whole(a_ref, b_ref, o_ref):
        o_ref[...] = lax.dot_general(a_ref[...].astype(BF16), b_ref[...].astype(BF16), dims,
                                     preferred_element_type=F32).astype(o_ref.dtype)

    return pl.pallas_call(
        body if steps > 1 else body_whole, grid=(m // tm, n // tn, steps), in_specs=[a_spec, b_spec],
        out_specs=pl.BlockSpec((tm, tn), lambda i, j, k: (i, j)),
        out_shape=jax.ShapeDtypeStruct((m, n), out_dtype),
        scratch_shapes=[pltpu.VMEM((tm, tn), F32)] if steps > 1 else [],
        compiler_params=_params("parallel", "parallel", "arbitrary"), name=name,
    )(a, b)


@jax.custom_vjp
def linear(x, w):
    return _matmul(x, w, "nn", F32, "linear_fwd")


def _linear_fwd(x, w):
    return _matmul(x, w, "nn", F32, "linear_fwd"), (x, w)


def _linear_bwd(res, dy):
    x, w = res
    dx = _matmul(dy, w, "nt", F32, "linear_dx")
    dw = _matmul(x, dy, "tn", w.dtype, "linear_dw")
    return dx, dw


linear.defvjp(_linear_fwd, _linear_bwd)


def _rms_forward(x, g, out_dtype=F32, residual=None):
    rows, d = x.shape
    gd = g.shape[0]
    tr = _row_tile(rows, d, align=16)
    tile = pl.BlockSpec((tr, d), lambda i: (i, 0))

    def body(x_ref, g_ref, *rest):
        y_ref = rest[-1]
        for c0 in range(0, d, gd):
            xv = x_ref[:, c0:c0 + gd]
            r = lax.rsqrt(jnp.mean(xv * xv, axis=-1, keepdims=True) + EPS)
            y = (xv * r) * g_ref[...]
            if residual is not None:
                y = rest[0][:, c0:c0 + gd] + y
            y_ref[:, c0:c0 + gd] = y.astype(y_ref.dtype)

    extra = [] if residual is None else [residual]
    return pl.pallas_call(
        body, grid=(rows // tr,),
        in_specs=[tile, pl.BlockSpec((1, gd), lambda i: (0, 0))] + [tile] * len(extra),
        out_specs=tile, out_shape=jax.ShapeDtypeStruct((rows, d), out_dtype),
        compiler_params=_params("parallel"), name="rmsnorm_fwd",
    )(x, g.reshape(1, gd), *extra)


def _rms_backward(x, g, dy, residual=None):
    rows, d = x.shape
    gd = g.shape[0]
    tr = _row_tile(rows, d)
    tile = pl.BlockSpec((tr, d), lambda i: (i, 0))

    def body(x_ref, g_ref, dy_ref, *rest):
        dx_ref, dg_ref = rest[-2:]
        i = pl.program_id(0)

        @pl.when(i == 0)
        def _():
            dg_ref[...] = jnp.zeros_like(dg_ref)

        for c0 in range(0, d, gd):
            xv = x_ref[:, c0:c0 + gd]
            dyv = dy_ref[:, c0:c0 + gd]
            r = lax.rsqrt(jnp.mean(xv * xv, axis=-1, keepdims=True) + EPS)
            xh = xv * r
            t = dyv * g_ref[...]
            dx = r * (t - xh * jnp.mean(t * xh, axis=-1, keepdims=True))
            if residual is not None:
                dx = rest[0][:, c0:c0 + gd] + dx
            dx_ref[:, c0:c0 + gd] = dx
            dg_ref[...] += jnp.sum(dyv * xh, axis=0, keepdims=True)

    extra = [] if residual is None else [residual]
    dx, dg = pl.pallas_call(
        body, grid=(rows // tr,),
        in_specs=[tile, pl.BlockSpec((1, gd), lambda i: (0, 0)), tile] + [tile] * len(extra),
        out_specs=[tile, pl.BlockSpec((1, gd), lambda i: (0, 0))],
        out_shape=[jax.ShapeDtypeStruct((rows, d), F32), jax.ShapeDtypeStruct((1, gd), F32)],
        compiler_params=_params("arbitrary"), name="rmsnorm_bwd",
    )(x, g.reshape(1, gd), dy, *extra)
    return dx, dg.reshape(g.shape)


@jax.custom_vjp
def rms_norm(x, g):
    return _rms_forward(x, g)


def _rms_norm_fwd(x, g):
    return _rms_forward(x, g), (x, g)


def _rms_norm_bwd(res, dy):
    x, g = res
    return _rms_backward(x, g, dy)


rms_norm.defvjp(_rms_norm_fwd, _rms_norm_bwd)


@jax.custom_vjp
def add_norm(h, x, g):
    return _rms_forward(x, g, F32, h)


def _add_norm_fwd(h, x, g):
    return _rms_forward(x, g, F32, h), (x, g)


def _add_norm_bwd(res, dy):
    x, g = res
    dx, dg = _rms_backward(x, g, dy)
    return dy, dx, dg


add_norm.defvjp(_add_norm_fwd, _add_norm_bwd)


@jax.custom_vjp
def norm_linear(x, g, w):
    return x, _matmul(_rms_forward(x, g, BF16), w, "nn", F32, "linear_fwd")


def _norm_linear_fwd(x, g, w):
    y = _rms_forward(x, g, BF16)
    return (x, _matmul(y, w, "nn", F32, "linear_fwd")), (x, g, w, y)


def _norm_linear_bwd(res, cts):
    x, g, w, y = res
    dx_other, dz = cts
    dy = _matmul(dz, w, "nt", F32, "linear_dx")
    dw = _matmul(y, dz, "tn", w.dtype, "linear_dw")
    dx, dg = _rms_backward(x, g, dy, dx_other)
    return dx, dg, dw


norm_linear.defvjp(_norm_linear_fwd, _norm_linear_bwd)


@functools.partial(jax.custom_vjp, nondiff_argnums=(1,))
def split_cols(x, bounds):
    return tuple(x[:, lo:hi] for lo, hi in bounds)


def _split_cols_fwd(x, bounds):
    return split_cols(x, bounds), x.shape[1]


def _split_cols_bwd(bounds, width, cts):
    parts = list(cts)
    tail = width - bounds[-1][1]
    if tail:
        parts.append(jnp.zeros((parts[0].shape[0], tail), parts[0].dtype))
    return (jnp.concatenate(parts, axis=1),)


split_cols.defvjp(_split_cols_fwd, _split_cols_bwd)


def _visibility_id(pos, kind, l_real):
    if kind == "fox":
        return pos
    pad_chunk = 2 + (l_real - N_META) // CHUNK
    frame_chunk = 1 + jnp.right_shift(pos - N_META, 6)
    return jnp.where(pos < N_META, 0, jnp.where(pos < l_real, frame_chunk, pad_chunk))


def _raw_scores(kind, a1, b1, a2, b2):
    s = lax.dot_general(a1.astype(BF16), b1.astype(BF16), NT_DIMS, preferred_element_type=F32)
    if kind == "mla":
        s = s + lax.dot_general(a2.astype(BF16), b2.astype(BF16), NT_DIMS, preferred_element_type=F32)
    return s


def _block_pairs(nb, look, by_key):
    outer, inner = [], []
    for a in range(nb):
        rng = range(max(a - look, 0), nb) if by_key else range(0, min(a + look, nb - 1) + 1)
        for b in rng:
            outer.append(a)
            inner.append(b)
    return jnp.asarray(outer, jnp.int32), jnp.asarray(inner, jnp.int32)


def _attention_call(body, n_pairs, in_specs, out_specs, out_shape, scratch, name, tables, operands):
    return pl.pallas_call(
        body,
        grid_spec=pltpu.PrefetchScalarGridSpec(num_scalar_prefetch=2, grid=(HEADS, n_pairs), in_specs=in_specs,
                                               out_specs=out_specs, scratch_shapes=scratch),
        out_shape=out_shape, compiler_params=_params("parallel", "arbitrary"), name=name,
    )(*tables, *operands)


def _attention_forward(kind, l_real, q1, k1, v, extras):
    L = q1.shape[0]
    T = _pick(L, _TILE_ATT)
    nb = L // T
    look = 1 if kind == "mla" else 0
    scale = (HEAD_DIM + ROPE_DIM) ** -0.5 if kind == "mla" else HEAD_DIM ** -0.5
    tables = _block_pairs(nb, look, False)

    q_tile = pl.BlockSpec((T, HEAD_DIM), lambda h, t, it, jt: (it[t], h))
    k_tile = pl.BlockSpec((T, HEAD_DIM), lambda h, t, it, jt: (jt[t], h))
    row_stat = pl.BlockSpec((None, T, 1), lambda h, t, it, jt: (h, it[t], 0))
    if kind == "mla":
        extra_specs = [pl.BlockSpec((None, T, ROPE_DIM), lambda h, t, it, jt: (h, it[t], 0)),
                       pl.BlockSpec((T, ROPE_DIM), lambda h, t, it, jt: (jt[t], 0))]
    else:
        extra_specs = [pl.BlockSpec((None, 1, T), lambda h, t, it, jt: (h, 0, jt[t]))]

    def body(it_ref, jt_ref, q1_ref, k1_ref, v_ref, *rest):
        if kind == "mla":
            eq_ref, ek_ref, o_ref, lse_ref, m_ref, l_ref, acc_ref = rest
        else:
            ek_ref, o_ref, lse_ref, m_ref, l_ref, acc_ref = rest
        t = pl.program_id(1)
        i, j = it_ref[t], jt_ref[t]

        @pl.when(j == 0)
        def _():
            m_ref[...] = jnp.full_like(m_ref, NEG)
            l_ref[...] = jnp.zeros_like(l_ref)
            acc_ref[...] = jnp.zeros_like(acc_ref)

        def block(masked):
            if kind == "mla":
                s = _raw_scores(kind, q1_ref[...] * scale, k1_ref[...], eq_ref[...] * scale, ek_ref[...])
            else:
                s = _raw_scores(kind, q1_ref[...] * scale, k1_ref[...], None, None) - ek_ref[...]
            if masked:
                pos_q = i * T + lax.broadcasted_iota(jnp.int32, (T, 1), 0)
                pos_k = j * T + lax.broadcasted_iota(jnp.int32, (1, T), 1)
                s = jnp.where(_visibility_id(pos_k, kind, l_real) <= _visibility_id(pos_q, kind, l_real), s, NEG)
            m_prev = m_ref[...]
            m_new = jnp.maximum(m_prev, jnp.max(s, axis=1, keepdims=True))
            alpha = jnp.exp(m_prev - m_new)
            p = jnp.exp(s - m_new)
            l_ref[...] = alpha * l_ref[...] + jnp.sum(p, axis=1, keepdims=True)
            m_ref[...] = m_new
            vb = v_ref[...].astype(BF16)
            p_hi = p.astype(BF16)
            pv = lax.dot_general(p_hi, vb, NN_DIMS, preferred_element_type=F32)
            if kind == "fox":
                p_lo = (p - p_hi.astype(F32)).astype(BF16)
                pv = pv + lax.dot_general(p_lo, vb, NN_DIMS, preferred_element_type=F32)
            acc_ref[...] = alpha * acc_ref[...] + pv

        @pl.when(j < i)
        def _():
            block(False)

        @pl.when(j >= i)
        def _():
            block(True)

        @pl.when(j == jnp.minimum(i + look, nb - 1))
        def _():
            o_ref[...] = acc_ref[...] / l_ref[...]
            lse_ref[...] = m_ref[...] + jnp.log(l_ref[...])

    return _attention_call(
        body, tables[0].shape[0], [q_tile, k_tile, k_tile] + extra_specs, [q_tile, row_stat],
        [jax.ShapeDtypeStruct((L, HEADS * HEAD_DIM), F32), jax.ShapeDtypeStruct((HEADS, L, 1), F32)],
        [pltpu.VMEM((T, 1), F32), pltpu.VMEM((T, 1), F32), pltpu.VMEM((T, HEAD_DIM), F32)],
        kind + "_attn_fwd", tables, (q1, k1, v, *extras))


def _attention_delta(o, do):
    L = o.shape[0]
    T = _pick(L, _TILE_ATT)
    tile = pl.BlockSpec((T, HEADS * HEAD_DIM), lambda i: (i, 0))

    def body(o_ref, do_ref, delta_ref):
        for h in range(HEADS):
            cols = slice(h * HEAD_DIM, (h + 1) * HEAD_DIM)
            delta_ref[h] = jnp.sum(do_ref[:, cols].astype(BF16).astype(F32) * o_ref[:, cols], axis=1, keepdims=True)

    return pl.pallas_call(
        body, grid=(L // T,), in_specs=[tile, tile],
        out_specs=pl.BlockSpec((HEADS, T, 1), lambda i: (0, i, 0)),
        out_shape=jax.ShapeDtypeStruct((HEADS, L, 1), F32),
        compiler_params=_params("parallel"), name="attn_delta",
    )(o, do)


def _attention_backward(kind, l_real, q1, k1, v, extras, do, lse_row, delta_row):
    L = q1.shape[0]
    T = _pick(L, _TILE_ATT)
    nb = L // T
    look = 1 if kind == "mla" else 0
    scale = (HEAD_DIM + ROPE_DIM) ** -0.5 if kind == "mla" else HEAD_DIM ** -0.5
    tables = _block_pairs(nb, look, True)

    k_tile = pl.BlockSpec((T, HEAD_DIM), lambda h, t, jt, it: (jt[t], h))
    q_tile = pl.BlockSpec((T, HEAD_DIM), lambda h, t, jt, it: (it[t], h))
    q_row = pl.BlockSpec((None, 1, T), lambda h, t, jt, it: (h, 0, it[t]))
    if kind == "mla":
        extra_specs = [pl.BlockSpec((None, T, ROPE_DIM), lambda h, t, jt, it: (h, it[t], 0)),
                       pl.BlockSpec((T, ROPE_DIM), lambda h, t, jt, it: (jt[t], 0))]
        third_spec = pl.BlockSpec((None, T, ROPE_DIM), lambda h, t, jt, it: (h, jt[t], 0))
        third_shape = jax.ShapeDtypeStruct((HEADS, L, ROPE_DIM), F32)
        third_scratch = pltpu.VMEM((T, ROPE_DIM), F32)
    else:
        extra_specs = [pl.BlockSpec((None, T, 1), lambda h, t, jt, it: (h, jt[t], 0))]
        third_spec = pl.BlockSpec((None, T, 1), lambda h, t, jt, it: (h, jt[t], 0))
        third_shape = jax.ShapeDtypeStruct((HEADS, L, 1), F32)
        third_scratch = pltpu.VMEM((T, 1), F32)
    in_specs = [q_tile, k_tile, k_tile] + extra_specs + [q_tile, q_row, q_row]
    n_pairs = tables[0].shape[0]
    out_specs = [k_tile, k_tile, third_spec, pl.BlockSpec((L, HEAD_DIM), lambda h, t, jt, it: (0, h))]
    out_shape = [jax.ShapeDtypeStruct((L, HEADS * HEAD_DIM), F32), jax.ShapeDtypeStruct((L, HEADS * HEAD_DIM), F32),
                 third_shape, jax.ShapeDtypeStruct((L, HEADS * HEAD_DIM), F32)]
    if kind == "mla":
        out_specs.append(pl.BlockSpec((None, L, ROPE_DIM), lambda h, t, jt, it: (h, 0, 0)))
        out_shape.append(jax.ShapeDtypeStruct((HEADS, L, ROPE_DIM), F32))

    def body(jt_ref, it_ref, q1_ref, k1_ref, v_ref, *rest):
        if kind == "mla":
            (eq_ref, ek_ref, do_ref, lse_ref, delta_ref, dk1_ref, dv_ref, third_ref, dq1_ref, dq2_ref,
             acck_ref, accv_ref, acc3_ref) = rest
        else:
            (ek_ref, do_ref, lse_ref, delta_ref, dk1_ref, dv_ref, third_ref, dq1_ref,
             acck_ref, accv_ref, acc3_ref) = rest
        t = pl.program_id(1)
        j, i = jt_ref[t], it_ref[t]
        q_rows = pl.ds(pl.multiple_of(i * T, T), T)

        @pl.when(t == 0)
        def _():
            dq1_ref[...] = jnp.zeros_like(dq1_ref)
            if kind == "mla":
                dq2_ref[...] = jnp.zeros_like(dq2_ref)

        @pl.when(i == jnp.maximum(j - look, 0))
        def _():
            acck_ref[...] = jnp.zeros_like(acck_ref)
            accv_ref[...] = jnp.zeros_like(accv_ref)
            acc3_ref[...] = jnp.zeros_like(acc3_ref)

        def block(masked):
            if kind == "mla":
                kb = jnp.concatenate([k1_ref[...].astype(BF16), ek_ref[...].astype(BF16)], axis=1)
                qb = jnp.concatenate([(q1_ref[...] * scale).astype(BF16), (eq_ref[...] * scale).astype(BF16)],
                                     axis=1)
            else:
                kb, qb = k1_ref[...].astype(BF16), (q1_ref[...] * scale).astype(BF16)
            st = lax.dot_general(kb, qb, NT_DIMS, preferred_element_type=F32)
            if kind == "fox":
                st = st - ek_ref[...]
            if masked:
                pos_k = j * T + lax.broadcasted_iota(jnp.int32, (T, 1), 0)
                pos_q = i * T + lax.broadcasted_iota(jnp.int32, (1, T), 1)
                st = jnp.where(_visibility_id(pos_k, kind, l_real) <= _visibility_id(pos_q, kind, l_real), st, NEG)
            pt = jnp.exp(st - lse_ref[...])
            dob = do_ref[...].astype(BF16)
            accv_ref[...] += lax.dot_general(pt.astype(BF16), dob, NN_DIMS, preferred_element_type=F32)
            dpt = lax.dot_general(v_ref[...].astype(BF16), dob, NT_DIMS, preferred_element_type=F32)
            dst = pt * (dpt - delta_ref[...])
            dsb = dst.astype(BF16)
            dk = lax.dot_general(dsb, qb, NN_DIMS, preferred_element_type=F32)
            dq = lax.dot_general(dsb, kb, TN_DIMS, preferred_element_type=F32)
            if kind == "mla":
                acck_ref[...] += dk[:, :HEAD_DIM]
                acc3_ref[...] += dk[:, HEAD_DIM:]
                dq1_ref[q_rows, :] += dq[:, :HEAD_DIM]
                dq2_ref[q_rows, :] += dq[:, HEAD_DIM:]
            else:
                acck_ref[...] += dk
                dq1_ref[q_rows, :] += dq
                acc3_ref[...] -= jnp.sum(dst, axis=1, keepdims=True)

        @pl.when(i > j)
        def _():
            block(False)

        @pl.when(i <= j)
        def _():
            block(True)

        @pl.when(i == nb - 1)
        def _():
            dk1_ref[...] = acck_ref[...]
            dv_ref[...] = accv_ref[...]
            third_ref[...] = acc3_ref[...]

        @pl.when(t == n_pairs - 1)
        def _():
            dq1_ref[...] = dq1_ref[...] * scale
            if kind == "mla":
                dq2_ref[...] = dq2_ref[...] * scale

    return _attention_call(
        body, n_pairs, in_specs, out_specs, out_shape,
        [pltpu.VMEM((T, HEAD_DIM), F32), pltpu.VMEM((T, HEAD_DIM), F32), third_scratch],
        kind + "_attn_bwd", tables, (q1, k1, v, *extras, do, lse_row, delta_row))


def _as_row(col):
    return col.reshape(col.shape[0], 1, col.shape[1])


@functools.partial(jax.custom_vjp, nondiff_argnums=(0,))
def mla_attention(l_real, qn, qr, kn, kr, v):
    return _attention_forward("mla", l_real, qn, kn, v, (qr, kr))[0]


def _mla_attention_fwd(l_real, qn, qr, kn, kr, v):
    o, lse = _attention_forward("mla", l_real, qn, kn, v, (qr, kr))
    return o, (qn, qr, kn, kr, v, o, lse)


def _mla_attention_bwd(l_real, res, do):
    qn, qr, kn, kr, v, o, lse = res
    delta = _attention_delta(o, do)
    dkn, dv, dkr_heads, dqn, dqr = _attention_backward("mla", l_real, qn, kn, v, (qr, kr), do, _as_row(lse),
                                                       _as_row(delta))
    return dqn, dqr, dkn, jnp.sum(dkr_heads, axis=0), dv


mla_attention.defvjp(_mla_attention_fwd, _mla_attention_bwd)


@functools.partial(jax.custom_vjp, nondiff_argnums=(0,))
def fox_attention(l_real, q, k, v, c):
    return _attention_forward("fox", l_real, q, k, v, (_as_row(c),))[0]


def _fox_attention_fwd(l_real, q, k, v, c):
    o, lse = _attention_forward("fox", l_real, q, k, v, (_as_row(c),))
    return o, (q, k, v, c, o, lse)


def _fox_attention_bwd(l_real, res, do):
    q, k, v, c, o, lse = res
    delta = _attention_delta(o, do)
    dk, dv, dc, dq = _attention_backward("fox", l_real, q, k, v, (c,), do, _as_row(lse), _as_row(delta))
    return dq, dk, dv, dc


fox_attention.defvjp(_fox_attention_fwd, _fox_attention_bwd)


GELU_C0 = 0.7978845608028654
GELU_C1 = 0.044715


def _shift_rows(x, prev, s):
    r = pltpu.roll(x, s, 0)
    pr = pltpu.roll(prev, s, 0)
    row = lax.broadcasted_iota(jnp.int32, prev.shape, 0)
    top = jnp.where(row < s, pr, r[0:SUBLANES])
    return jnp.concatenate([top, r[SUBLANES:]], axis=0)


def _conv_tiles(L, f):
    return _pick(L, _TILE_ATT), _pick(f, _TILE_FF)


def _conv_gate_forward(u, w, b, out_dtype):
    L, f2 = u.shape
    f = f2 // 2
    tm, tn = _conv_tiles(L, f)
    rb = tm // SUBLANES

    def body(u_ref, up_ref, w_ref, b_ref, o_ref):
        i = pl.program_id(1)
        x = u_ref[...]
        prev = jnp.where(i > 0, up_ref[...], 0.0)
        wv = w_ref[...]
        hc = b_ref[...] + ((wv[0:1] * _shift_rows(x, prev, 2) + wv[1:2] * _shift_rows(x, prev, 1)) + wv[2:3] * x)
        g = hc[:, :tn]
        gelu = 0.5 * g * (1.0 + jnp.tanh(GELU_C0 * (g + GELU_C1 * g * g * g)))
        o_ref[...] = (gelu * hc[:, tn:]).astype(o_ref.dtype)

    return pl.pallas_call(
        body, grid=(f // tn, L // tm),
        in_specs=[pl.BlockSpec((tm, 2 * tn), lambda j, i: (i, j)),
                  pl.BlockSpec((SUBLANES, 2 * tn), lambda j, i: (jnp.maximum(i * rb - 1, 0), j)),
                  pl.BlockSpec((3, 2 * tn), lambda j, i: (0, j)),
                  pl.BlockSpec((1, 2 * tn), lambda j, i: (0, j))],
        out_specs=pl.BlockSpec((tm, tn), lambda j, i: (i, j)),
        out_shape=jax.ShapeDtypeStruct((L, f), out_dtype),
        compiler_params=_params("parallel", "parallel"), name="conv_gate_fwd",
    )(u, u, w, b)


def _conv_gate_backward(u, w, b, dact, du_dtype):
    L, f2 = u.shape
    f = f2 // 2
    tm, tn = _conv_tiles(L, f)
    rb = tm // SUBLANES
    n_row_blocks = L // SUBLANES
    n_i = L // tm
    ext = tm + SUBLANES

    def next_rows(i):
        return jnp.minimum((i + 1) * rb, n_row_blocks - 1)

    def body(u_ref, up_ref, un_ref, da_ref, dan_ref, w_ref, b_ref, du_ref, dwb_ref):
        i = pl.program_id(1)
        is_last = i == n_i - 1
        prev = jnp.where(i > 0, up_ref[...], 0.0)
        xe = jnp.concatenate([u_ref[...], jnp.where(is_last, 0.0, un_ref[...])], axis=0)
        x1 = _shift_rows(xe, prev, 1)
        x2 = _shift_rows(xe, prev, 2)
        wv = w_ref[...]
        hc = b_ref[...] + ((wv[0:1] * x2 + wv[1:2] * x1) + wv[2:3] * xe)
        g, up = hc[:, :tn], hc[:, tn:]
        da = jnp.concatenate([da_ref[...], jnp.where(is_last, 0.0, dan_ref[...])], axis=0)
        t = jnp.tanh(GELU_C0 * (g + GELU_C1 * g * g * g))
        gelu = 0.5 * g * (1.0 + t)
        dgelu = 0.5 * (1.0 + t) + 0.5 * g * (1.0 - t * t) * (GELU_C0 * (1.0 + 3.0 * GELU_C1 * g * g))
        dh = jnp.concatenate([da * up * dgelu, da * gelu], axis=1)
        dh1 = pltpu.roll(dh, ext - 1, 0)
        dh2 = pltpu.roll(dh, ext - 2, 0)
        du_ref[...] = ((wv[2:3] * dh + wv[1:2] * dh1) + wv[0:1] * dh2)[:tm].astype(du_ref.dtype)
        dw0 = jnp.sum((dh * x2)[:tm], axis=0, keepdims=True)
        dw1 = jnp.sum((dh * x1)[:tm], axis=0, keepdims=True)
        dw2 = jnp.sum((dh * xe)[:tm], axis=0, keepdims=True)
        db = jnp.sum(dh[:tm], axis=0, keepdims=True)
        row = lax.broadcasted_iota(jnp.int32, (SUBLANES, 2 * tn), 0)
        upd = jnp.where(row == 0, dw0, jnp.where(row == 1, dw1, jnp.where(row == 2, dw2,
                        jnp.where(row == 3, db, 0.0))))

        @pl.when(i == 0)
        def _():
            dwb_ref[...] = jnp.zeros_like(dwb_ref)

        dwb_ref[...] += upd

    return pl.pallas_call(
        body, grid=(f // tn, n_i),
        in_specs=[pl.BlockSpec((tm, 2 * tn), lambda j, i: (i, j)),
                  pl.BlockSpec((SUBLANES, 2 * tn), lambda j, i: (jnp.maximum(i * rb - 1, 0), j)),
                  pl.BlockSpec((SUBLANES, 2 * tn), lambda j, i: (next_rows(i), j)),
                  pl.BlockSpec((tm, tn), lambda j, i: (i, j)),
                  pl.BlockSpec((SUBLANES, tn), lambda j, i: (next_rows(i), j)),
                  pl.BlockSpec((3, 2 * tn), lambda j, i: (0, j)),
                  pl.BlockSpec((1, 2 * tn), lambda j, i: (0, j))],
        out_specs=[pl.BlockSpec((tm, 2 * tn), lambda j, i: (i, j)),
                   pl.BlockSpec((SUBLANES, 2 * tn), lambda j, i: (0, j))],
        out_shape=[jax.ShapeDtypeStruct((L, f2), du_dtype), jax.ShapeDtypeStruct((SUBLANES, f2), F32)],
        compiler_params=_params("parallel", "arbitrary"), name="conv_gate_bwd",
    )(u, u, u, dact, dact, w, b)


@jax.custom_vjp
def conv_ffn(h, g, w_up, w_conv, b_conv, w_down):
    return _conv_ffn_fwd(h, g, w_up, w_conv, b_conv, w_down)[0]


def _conv_ffn_fwd(h, g, w_up, w_conv, b_conv, w_down):
    y = _rms_forward(h, g, BF16)
    u = _matmul(y, w_up, "nn", F32, "linear_fwd")
    act = _conv_gate_forward(u, w_conv, b_conv.reshape(1, -1), BF16)
    return (h, _matmul(act, w_down, "nn", F32, "linear_fwd")), (h, g, w_up, w_conv, b_conv, w_down, y, u, act)


def _conv_ffn_bwd(res, cts):
    h, g, w_up, w_conv, b_conv, w_down, y, u, act = res
    dh_other, df = cts
    dact = _matmul(df, w_down, "nt", F32, "linear_dx")
    dw_down = _matmul(act, df, "tn", w_down.dtype, "linear_dw")
    du, dwb = _conv_gate_backward(u, w_conv, b_conv.reshape(1, -1), dact, BF16)
    dy = _matmul(du, w_up, "nt", F32, "linear_dx")
    dw_up = _matmul(y, du, "tn", w_up.dtype, "linear_dw")
    dh, dg = _rms_backward(h, g, dy, dh_other)
    return dh, dg, dw_up, dwb[0:3], dwb[3], dw_down


conv_ffn.defvjp(_conv_ffn_fwd, _conv_ffn_bwd)


def _loss_rows(y, target):
    rows, d = y.shape
    tr = _row_tile(rows, d)

    def body(y_ref, t_ref, loss_ref, dy_ref):
        err = y_ref[...] - t_ref[...]
        loss_ref[...] = 0.5 * jnp.mean(err * err, axis=-1, keepdims=True)
        dy_ref[...] = err * (1.0 / d)

    return pl.pallas_call(
        body, grid=(rows // tr,),
        in_specs=[pl.BlockSpec((tr, d), lambda i: (i, 0)), pl.BlockSpec((tr, d), lambda i: (i, 0))],
        out_specs=[pl.BlockSpec((tr, 1), lambda i: (i, 0)), pl.BlockSpec((tr, d), lambda i: (i, 0))],
        out_shape=[jax.ShapeDtypeStruct((rows, 1), F32), jax.ShapeDtypeStruct((rows, d), F32)],
        compiler_params=_params("parallel"), name="loss_head",
    )(y, target)


@jax.custom_vjp
def token_loss(y, target):
    return jnp.sum(_loss_rows(y, target)[0])


def _token_loss_fwd(y, target):
    rows, dy = _loss_rows(y, target)
    return jnp.sum(rows), dy


def _token_loss_bwd(dy, ct):
    return ct * dy, -ct * dy


token_loss.defvjp(_token_loss_fwd, _token_loss_bwd)


def _cols_from_devices(g):
    k = g.shape[1]
    return jnp.transpose(g, (1, 0, 2)).reshape(k, -1)


@functools.partial(jax.custom_vjp, nondiff_argnums=(1,))
def _interleave_gate_up(a, f):
    tn = _pick(f, _TILE_FF)
    parts = []
    for j in range(f // tn):
        parts += [a[..., j * tn:(j + 1) * tn], a[..., f + j * tn:f + (j + 1) * tn]]
    return jnp.concatenate(parts, axis=-1)


def _interleave_fwd(a, f):
    return _interleave_gate_up(a, f), None


def _interleave_bwd(f, _, ct):
    tn = _pick(f, _TILE_FF)
    gate = [ct[..., 2 * j * tn:(2 * j + 1) * tn] for j in range(f // tn)]
    up = [ct[..., (2 * j + 1) * tn:(2 * j + 2) * tn] for j in range(f // tn)]
    return (jnp.concatenate(gate + up, axis=-1),)


_interleave_gate_up.defvjp(_interleave_fwd, _interleave_bwd)


def _rope(x, cos, sin):
    half = x.shape[-1] // 2
    x1, x2 = x[..., :half], x[..., half:]
    return jnp.concatenate([x1 * cos - x2 * sin, x2 * cos + x1 * sin], axis=-1)


PROJ_BOUNDS = ((0, 512), (512, 1024), (1024, 2048), (2048, 3072), (3072, 4096), (4096, 5120), (5120, 5184),
               (5184, 5192))


def _layer(h, big, small, conv_w, l, l_real, cos, sin):
    L, d = h.shape
    w_in = _cols_from_devices(big["w_in"])
    w_in = jnp.concatenate([w_in[:, :1024], w_in[:, 1088:5184], w_in[:, 1024:1088], w_in[:, 5184:],
                            jnp.zeros((d, IN_COLS_PADDED - IN_COLS), w_in.dtype)], axis=1)
    w_q_up = _cols_from_devices(big["w_q_up"]).reshape(MLA_Q_LORA, HEADS, HEAD_DIM + ROPE_DIM)
    w_q_up = jnp.concatenate([w_q_up[:, :, :HEAD_DIM].reshape(MLA_Q_LORA, -1),
                              w_q_up[:, :, HEAD_DIM:].reshape(MLA_Q_LORA, -1)], axis=1)
    w_kv_up = _cols_from_devices(big["w_kv_up"]).reshape(MLA_KV_LORA, HEADS, 2 * HEAD_DIM)
    w_kv_up = jnp.concatenate([w_kv_up[:, :, :HEAD_DIM].reshape(MLA_KV_LORA, -1),
                               w_kv_up[:, :, HEAD_DIM:].reshape(MLA_KV_LORA, -1)], axis=1)
    w_out = big["w_out"].reshape(-1, d)
    f = big["w_ffn_down"].shape[0] * big["w_ffn_down"].shape[1]
    w_ffn_up = _interleave_gate_up(_cols_from_devices(big["w_ffn_up"]), f)
    w_ffn_down = big["w_ffn_down"].reshape(f, d)
    w_conv = _interleave_gate_up(conv_w, f)
    b_conv = _interleave_gate_up(small["b_ffn_conv"][l], f)

    h, proj = norm_linear(h, small["ln_mix_pre"][l], w_in)
    c_q, c_kv, fq, fk, fv, fg, k_rope, ff = split_cols(proj, PROJ_BOUNDS)

    q = norm_linear(c_q, small["g_q_latent"][l], w_q_up)[1]
    qn, qr = split_cols(q, ((0, HEADS * HEAD_DIM), (HEADS * HEAD_DIM, HEADS * (HEAD_DIM + ROPE_DIM))))
    kv = norm_linear(c_kv, small["g_kv_latent"][l], w_kv_up)[1]
    kn, v = split_cols(kv, ((0, HEADS * HEAD_DIM), (HEADS * HEAD_DIM, 2 * HEADS * HEAD_DIM)))
    qr = jnp.transpose(_rope(qr.reshape(L, HEADS, ROPE_DIM), cos[:, None, :], sin[:, None, :]), (1, 0, 2))
    kr = _rope(k_rope, cos, sin)
    a = mla_attention(l_real, qn, qr, kn, kr, v)

    fqn = rms_norm(fq, small["g_fox_q"][l])
    fkn = rms_norm(fk, small["g_fox_k"][l])
    log_f = jax.nn.log_sigmoid(ff + small["b_forget"][l])
    c = jnp.cumsum(log_f, axis=0).T[:, :, None]
    bmix = fox_attention(l_real, fqn, fkn, fv, c) * jax.nn.sigmoid(fg)

    mix = linear(jnp.concatenate([a, bmix], axis=1), w_out)
    h = add_norm(h, mix, small["ln_mix_post"][l])

    h, f_out = conv_ffn(h, small["ln_ffn_pre"][l], w_ffn_up, w_conv, b_conv, w_ffn_down)
    h = add_norm(h, f_out, small["ln_ffn_post"][l])
    return h


def _local_loss(big, small, meta, conv_w, x, target):
    s, d = x.shape
    l_real = N_META + s
    l_pad = -(-l_real // Q_BLOCK) * Q_BLOCK
    h = jnp.concatenate([meta, x, jnp.zeros((l_pad - l_real, d), F32)], axis=0)
    half = ROPE_DIM // 2
    inv_freq = ROPE_THETA ** (-jnp.arange(half, dtype=F32) / half)
    ang = jnp.arange(l_pad, dtype=jnp.int32).astype(F32)[:, None] * inv_freq[None, :]
    cos, sin = jnp.cos(ang), jnp.sin(ang)
    for l in range(DEPTH):
        h = _layer(h, big[l], small, conv_w[l], l, l_real, cos, sin)
    return token_loss(h[N_META:l_real], target)


ANY_SPACE = pl.BlockSpec(memory_space=pl.ANY)


def _place():
    ix, iy, ic = lax.axis_index("x"), lax.axis_index("y"), lax.axis_index("c")
    return ix, iy, ic, [(1 - ix, iy), (ix, 1 - iy), (1 - ix, 1 - iy)]


def _comm_call(body, arrays, out_shapes, n_remote, n_local, name):
    return pl.pallas_call(
        body, out_shape=out_shapes, in_specs=[ANY_SPACE] * len(arrays), out_specs=[ANY_SPACE] * len(out_shapes),
        scratch_shapes=[pltpu.SemaphoreType.DMA((n_remote,)), pltpu.SemaphoreType.DMA((n_remote,)),
                        pltpu.SemaphoreType.DMA((n_local,))],
        name=name,
    )(*arrays)


def _gather(arrays, name):
    n = len(arrays)

    def body(*refs):
        xs, outs = refs[:n], refs[n:2 * n]
        send_sems, recv_sems, local_sems = refs[2 * n:]
        ix, iy, ic, chips = _place()
        me, sibling = (ix, iy, ic), (ix, iy, 1 - ic)

        def copy(a, k, block, to, src=None):
            dst = outs[a].at[4 * block[0] + 2 * block[1] + block[2]]
            return pltpu.make_async_remote_copy(
                src_ref=dst if src is None else src, dst_ref=dst, send_sem=send_sems.at[7 * a + k],
                recv_sem=recv_sems.at[7 * a + k], device_id=to, device_id_type=MESH_ID)

        local, sent = [], []
        for a in range(n):
            mine = pltpu.make_async_copy(xs[a], outs[a].at[4 * ix + 2 * iy + ic], local_sems.at[a])
            mine.start()
            local.append(mine)
            first = [copy(a, 0, me, sibling, src=xs[a])]
            first += [copy(a, 1 + j, me, (*chip, ic), src=xs[a]) for j, chip in enumerate(chips)]
            for cp in first:
                cp.start()
            sent += first
        for a in range(n):
            for j, chip in enumerate(chips):
                copy(a, 1 + j, (*chip, ic), me).wait_recv()
                passed = copy(a, 4 + j, (*chip, ic), sibling)
                passed.start()
                sent.append(passed)
        for a in range(n):
            copy(a, 0, sibling, me).wait_recv()
            for j, chip in enumerate(chips):
                copy(a, 4 + j, (*chip, 1 - ic), me).wait_recv()
        for cp in sent:
            cp.wait_send()
        for cp in local:
            cp.wait()

    out_shapes = [jax.ShapeDtypeStruct((N_DEV,) + a.shape, a.dtype) for a in arrays]
    return _comm_call(body, arrays, out_shapes, 7 * n, n, name)


def _swap_with_sibling(arrays, name):
    n = len(arrays)

    def body(*refs):
        xs, outs = refs[:n], refs[n:2 * n]
        send_sems, recv_sems, _ = refs[2 * n:]
        ix, iy, ic, _ = _place()
        copies = [pltpu.make_async_remote_copy(
            src_ref=xs[a], dst_ref=outs[a], send_sem=send_sems.at[a], recv_sem=recv_sems.at[a],
            device_id=(ix, iy, 1 - ic), device_id_type=MESH_ID) for a in range(n)]
        for cp in copies:
            cp.start()
        for cp in copies:
            cp.wait()

    out_shapes = [jax.ShapeDtypeStruct(a.shape, a.dtype) for a in arrays]
    return _comm_call(body, arrays, out_shapes, n, 1, name)


def _exchange_chips(arrays, name):
    n = len(arrays)

    def body(*refs):
        xs, outs = refs[:n], refs[n:2 * n]
        send_sems, recv_sems, local_sems = refs[2 * n:]
        ix, iy, ic, chips = _place()
        my_chip = 2 * ix + iy
        local, sent = [], []
        for a in range(n):
            mine = pltpu.make_async_copy(xs[a].at[my_chip], outs[a].at[my_chip], local_sems.at[a])
            mine.start()
            local.append(mine)
            for j, chip in enumerate(chips):
                cp = pltpu.make_async_remote_copy(
                    src_ref=xs[a].at[2 * chip[0] + chip[1]], dst_ref=outs[a].at[my_chip],
                    send_sem=send_sems.at[3 * a + j], recv_sem=recv_sems.at[3 * a + j],
                    device_id=(*chip, ic), device_id_type=MESH_ID)
                cp.start()
                sent.append(cp)
        for a in range(n):
            for j, chip in enumerate(chips):
                pltpu.make_async_remote_copy(
                    src_ref=xs[a].at[my_chip], dst_ref=outs[a].at[2 * chip[0] + chip[1]],
                    send_sem=send_sems.at[3 * a + j], recv_sem=recv_sems.at[3 * a + j],
                    device_id=(*chip, ic), device_id_type=MESH_ID).wait_recv()
        for cp in sent:
            cp.wait_send()
        for cp in local:
            cp.wait()

    out_shapes = [jax.ShapeDtypeStruct(a.shape, a.dtype) for a in arrays]
    return _comm_call(body, arrays, out_shapes, 3 * n, n, name)


def _sum_slots(x, out_dtype, name):
    slots, rows, cols = x.shape
    tr = _row_tile(rows, cols, (2 << 20) // slots, 16)

    def body(x_ref, o_ref):
        acc = x_ref[0].astype(F32)
        for s in range(1, slots):
            acc = acc + x_ref[s].astype(F32)
        o_ref[...] = acc.astype(o_ref.dtype)

    return pl.pallas_call(
        body, grid=(rows // tr,), in_specs=[pl.BlockSpec((slots, tr, cols), lambda i: (0, i, 0))],
        out_specs=pl.BlockSpec((tr, cols), lambda i: (i, 0)), out_shape=jax.ShapeDtypeStruct((rows, cols), out_dtype),
        compiler_params=_params("parallel"), name=name,
    )(x)


def _add_pairs(a, b, name):
    slots, rows, cols = a.shape
    tr = _row_tile(rows, cols, 1 << 20, 16)

    def body(a_ref, b_ref, o_ref):
        o_ref[...] = (a_ref[...].astype(F32) + b_ref[...].astype(F32)).astype(o_ref.dtype)

    spec = pl.BlockSpec((None, tr, cols), lambda s, i: (s, i, 0))
    return pl.pallas_call(
        body, grid=(slots, rows // tr), in_specs=[spec, spec], out_specs=spec,
        out_shape=jax.ShapeDtypeStruct(a.shape, BF16), compiler_params=_params("parallel", "parallel"), name=name,
    )(a, b)


def _reduce_scatter(grads, ic):
    by_chip = [g.reshape((4, 2) + g.shape[1:]) for g in grads]
    keep = [lax.dynamic_index_in_dim(g, ic, axis=1, keepdims=False) for g in by_chip]
    give = [lax.dynamic_index_in_dim(g, 1 - ic, axis=1, keepdims=False) for g in by_chip]
    got = _swap_with_sibling(give, "scatter_sibling")
    pairs = [_add_pairs(k, g, "add_pairs") for k, g in zip(keep, got)]
    received = _exchange_chips(pairs, "scatter_chips")
    return [_sum_slots(r, F32, "sum_grads") for r in received]


def _pack(arrays, dtype, row_multiple):
    flat = jnp.concatenate([a.astype(dtype).reshape(-1) for a in arrays])
    n = flat.shape[0]
    quantum = row_multiple * FLAT_COLS
    padded = -(-n // quantum) * quantum
    return jnp.pad(flat, (0, padded - n)).reshape(padded // FLAT_COLS, FLAT_COLS)


def _unpack(buf, shapes):
    flat = buf.reshape(-1)
    out, off = [], 0
    for shp in shapes:
        n = 1
        for s in shp:
            n *= s
        out.append(flat[off:off + n].reshape(tuple(shp)))
        off += n
    return out


def _adamw(w, g, m, v, name):
    shape = w.shape
    cols = shape[-1]
    w2, g2, m2, v2 = (a.reshape(-1, cols) for a in (w, g, m, v))
    rows = w2.shape[0]
    tr = _row_tile(rows, cols, 1 << 20)

    def body(w_ref, g_ref, m_ref, v_ref, d_ref, nm_ref, nv_ref):
        gv = g_ref[...]
        nm = ADAM_B1 * m_ref[...] + (1.0 - ADAM_B1) * gv
        nv = ADAM_B2 * v_ref[...] + (1.0 - ADAM_B2) * (gv * gv)
        m_hat = nm / (1.0 - ADAM_B1 ** ADAM_STEP)
        v_hat = nv / (1.0 - ADAM_B2 ** ADAM_STEP)
        d_ref[...] = -ADAM_LR * (m_hat / (jnp.sqrt(v_hat) + ADAM_EPS) + ADAM_WD * w_ref[...])
        nm_ref[...] = nm
        nv_ref[...] = nv

    spec = pl.BlockSpec((tr, cols), lambda i: (i, 0))
    outs = pl.pallas_call(
        body, grid=(rows // tr,), in_specs=[spec] * 4, out_specs=[spec] * 3,
        out_shape=[jax.ShapeDtypeStruct((rows, cols), F32)] * 3,
        compiler_params=_params("parallel"), name=name,
    )(w2, g2, m2, v2)
    return tuple(o.reshape(shape) for o in outs)


BIG = ("w_in", "w_q_up", "w_kv_up", "w_out", "w_ffn_up", "w_ffn_down")
REPLICATED = ("ln_mix_pre", "b_forget", "g_q_latent", "g_kv_latent", "g_fox_q", "g_fox_k", "ln_mix_post",
              "ln_ffn_pre", "b_ffn_conv", "ln_ffn_post")
WEIGHTS = ("meta_tokens", "ln_mix_pre", "w_in", "b_forget", "g_q_latent", "g_kv_latent", "w_q_up", "w_kv_up",
           "g_fox_q", "g_fox_k", "w_out", "ln_mix_post", "ln_ffn_pre", "w_ffn_up", "w_ffn_conv", "b_ffn_conv",
           "w_ffn_down", "ln_ffn_post")


def kernel(x, meta_tokens, ln_mix_pre, w_in, b_forget, g_q_latent, g_kv_latent, w_q_up, w_kv_up, g_fox_q, g_fox_k, w_out, ln_mix_post, ln_ffn_pre, w_ffn_up, w_ffn_conv, b_ffn_conv, w_ffn_down, ln_ffn_post, loss_target, m_meta_tokens, m_ln_mix_pre, m_w_in, m_b_forget, m_g_q_latent, m_g_kv_latent, m_w_q_up, m_w_kv_up, m_g_fox_q, m_g_fox_k, m_w_out, m_ln_mix_post, m_ln_ffn_pre, m_w_ffn_up, m_w_ffn_conv, m_b_ffn_conv, m_w_ffn_down, m_ln_ffn_post, v_meta_tokens, v_ln_mix_pre, v_w_in, v_b_forget, v_g_q_latent, v_g_kv_latent, v_w_q_up, v_w_kv_up, v_g_fox_q, v_g_fox_k, v_w_out, v_ln_mix_post, v_ln_ffn_pre, v_w_ffn_up, v_w_ffn_conv, v_b_ffn_conv, v_w_ffn_down, v_ln_ffn_post):
    w = dict(meta_tokens=meta_tokens, ln_mix_pre=ln_mix_pre, w_in=w_in, b_forget=b_forget, g_q_latent=g_q_latent,
             g_kv_latent=g_kv_latent, w_q_up=w_q_up, w_kv_up=w_kv_up, g_fox_q=g_fox_q, g_fox_k=g_fox_k, w_out=w_out,
             ln_mix_post=ln_mix_post, ln_ffn_pre=ln_ffn_pre, w_ffn_up=w_ffn_up, w_ffn_conv=w_ffn_conv,
             b_ffn_conv=b_ffn_conv, w_ffn_down=w_ffn_down, ln_ffn_post=ln_ffn_post)
    mom = dict(meta_tokens=m_meta_tokens, ln_mix_pre=m_ln_mix_pre, w_in=m_w_in, b_forget=m_b_forget,
               g_q_latent=m_g_q_latent, g_kv_latent=m_g_kv_latent, w_q_up=m_w_q_up, w_kv_up=m_w_kv_up,
               g_fox_q=m_g_fox_q, g_fox_k=m_g_fox_k, w_out=m_w_out, ln_mix_post=m_ln_mix_post,
               ln_ffn_pre=m_ln_ffn_pre, w_ffn_up=m_w_ffn_up, w_ffn_conv=m_w_ffn_conv, b_ffn_conv=m_b_ffn_conv,
               w_ffn_down=m_w_ffn_down, ln_ffn_post=m_ln_ffn_post)
    var = dict(meta_tokens=v_meta_tokens, ln_mix_pre=v_ln_mix_pre, w_in=v_w_in, b_forget=v_b_forget,
               g_q_latent=v_g_q_latent, g_kv_latent=v_g_kv_latent, w_q_up=v_w_q_up, w_kv_up=v_w_kv_up,
               g_fox_q=v_g_fox_q, g_fox_k=v_g_fox_k, w_out=v_w_out, ln_mix_post=v_ln_mix_post,
               ln_ffn_pre=v_ln_ffn_pre, w_ffn_up=v_w_ffn_up, w_ffn_conv=v_w_ffn_conv, b_ffn_conv=v_b_ffn_conv,
               w_ffn_down=v_w_ffn_down, ln_ffn_post=v_ln_ffn_post)
    ic = lax.axis_index("c")
    me = 4 * lax.axis_index("x") + 2 * lax.axis_index("y") + ic

    gathered = _gather([w[n].astype(BF16) for n in BIG] + [meta_tokens, w_ffn_conv], "gather_weights")
    big = [{n: gathered[k][:, l] for k, n in enumerate(BIG)} for l in range(DEPTH)]
    meta_shape, conv_shape = meta_tokens.shape, w_ffn_conv.shape
    meta_full = _cols_from_devices(gathered[len(BIG)])
    conv_full = jnp.transpose(gathered[len(BIG) + 1], (1, 2, 0, 3)).reshape(DEPTH, conv_shape[1], -1)
    small = {n: w[n] for n in REPLICATED}

    loss, grads = jax.value_and_grad(_local_loss, argnums=(0, 1, 2, 3, 4))(
        big, small, meta_full, [conv_full[l] for l in range(DEPTH)], x[0], loss_target[0])
    g_big, g_small, g_meta, g_conv, g_x = grads
    loss = lax.psum(loss, ("x", "y", "c"))

    grad = {}
    per_layer = [_reduce_scatter([g_big[l][n] for n in BIG], ic) for l in range(DEPTH)]
    for k, n in enumerate(BIG):
        grad[n] = jnp.stack([per_layer[l][k] for l in range(DEPTH)])

    small_arrays = [g_small[n] for n in REPLICATED] + [g_meta, jnp.stack(g_conv)]
    small_shapes = [a.shape for a in small_arrays]
    partials = _gather([_pack(small_arrays, F32, 16)], "gather_small_grads")[0]
    summed = _unpack(_sum_slots(partials, F32, "sum_small_grads"), small_shapes)
    for n, g in zip(REPLICATED, summed):
        grad[n] = g
    grad["meta_tokens"] = lax.dynamic_slice_in_dim(summed[-2], me * meta_shape[1], meta_shape[1], axis=1)
    grad["w_ffn_conv"] = lax.dynamic_slice_in_dim(summed[-1], me * conv_shape[2], conv_shape[2], axis=2)

    delta, new_m, new_v = {}, {}, {}
    for n in BIG:
        delta[n], new_m[n], new_v[n] = _adamw(w[n], grad[n], mom[n], var[n], "adamw_" + n)
    rest = [n for n in WEIGHTS if n not in BIG]
    rest_shapes = [w[n].shape for n in rest]
    flat = [_pack([src[n] for n in rest], F32, SUBLANES) for src in (w, grad, mom, var)]
    outs = _adamw(*flat, "adamw_small")
    for dst, buf in zip((delta, new_m, new_v), outs):
        for n, a in zip(rest, _unpack(buf, rest_shapes)):
            dst[n] = a

    return (loss, g_x[None], *[grad[n] for n in WEIGHTS], *[delta[n] for n in WEIGHTS],
            *[new_m[n] for n in WEIGHTS], *[new_v[n] for n in WEIGHTS])
```

```python
import functools

import jax
import jax.numpy as jnp
from jax import lax
from jax.experimental import pallas as pl
from jax.experimental.pallas import tpu as pltpu

F32 = jnp.float32
BF16 = jnp.bfloat16
MESH_ID = pl.DeviceIdType.MESH

N_DEV = 8
DEPTH = 4
N_META = 16
CHUNK = 64
Q_BLOCK = 128
HEADS = 8
HEAD_DIM = 128
ROPE_DIM = 64
MLA_Q_LORA = 512
MLA_KV_LORA = 512
FOX_W = HEADS * HEAD_DIM
ROPE_THETA = 10000.0
EPS = 1e-6
NEG = -1e30
IN_COLS = 5192
IN_COLS_PADDED = 5376

ADAM_LR = 0.001
ADAM_B1 = 0.9
ADAM_B2 = 0.999
ADAM_EPS = 1e-08
ADAM_WD = 0.01
ADAM_STEP = 10

LANES = 128
SUBLANES = 8
FLAT_COLS = 1024
VMEM_LIMIT_V7X = 52 * 1024 * 1024

NT_DIMS = (((1,), (1,)), ((), ()))
NN_DIMS = (((1,), (0,)), ((), ()))
TN_DIMS = (((0,), (0,)), ((), ()))

_TILE_ATT = (640, 512, 384, 256, 128)
_TILE_FF = (512, 256, 128)


def _pick(n, candidates):
    for c in candidates:
        if n % c == 0:
            return c
    return n


def _row_tile(rows, cols, budget_bytes=2 << 20, align=SUBLANES):
    best = None
    for t in range(align, rows + 1, align):
        if rows % t == 0 and t * cols * 4 <= budget_bytes:
            best = t
    return best if best is not None else rows


def _params(*semantics):
    return pltpu.CompilerParams(dimension_semantics=semantics, vmem_limit_bytes=VMEM_LIMIT_V7X)


MATMUL_VMEM_BUDGET = 36 << 20
MXU_FLOPS_V7X = 9.0e14
HBM_BYTES_PER_S_V7X = 2.5e12
GRID_STEP_S = 0.35e-6
MXU_DIM = 256


def _tile_candidates(n, cap):
    c = [t for t in range(LANES, min(n, cap) + 1, LANES) if n % t == 0]
    return c if c else [n]


def _matmul_tiles(m, n, c, a_bytes, b_bytes, o_bytes):
    best, best_cost = None, None
    for tc in _tile_candidates(c, 4096):
        steps = c // tc
        for tm in _tile_candidates(m, 2048):
            for tn in _tile_candidates(n, 2048):
                vmem = 2 * (tm * tc * a_bytes + tc * tn * b_bytes + tm * tn * o_bytes)
                vmem += tm * tn * 4 if steps > 1 else 0
                if vmem > MATMUL_VMEM_BUDGET:
                    continue
                traffic = m * c * a_bytes * (1 if steps == 1 else n // tn) + c * n * b_bytes * (m // tm)
                traffic += m * n * o_bytes
                grid = (m // tm) * (n // tn) * steps
                accumulate = 0 if steps == 1 else grid * tm * tn * 8 / 4.0e12
                fill = (-(-tn // MXU_DIM) * MXU_DIM / tn) * (-(-tc // MXU_DIM) * MXU_DIM / tc)
                cost = max(2.0 * m * n * c * fill / MXU_FLOPS_V7X, traffic / HBM_BYTES_PER_S_V7X)
                cost += grid * GRID_STEP_S + accumulate
                if best_cost is None or cost < best_cost:
                    best, best_cost = (tm, tn, tc), cost
    return best


def _matmul(a, b, mode, out_dtype, name):
    if mode == "nn":
        (m, c), (c2, n) = a.shape, b.shape
    elif mode == "nt":
        (m, c), (n, c2) = a.shape, b.shape
    else:
        (c, m), (c2, n) = a.shape, b.shape
    assert c == c2, (a.shape, b.shape, mode)
    tm, tn, tc = _matmul_tiles(m, n, c, a.dtype.itemsize, b.dtype.itemsize, jnp.dtype(out_dtype).itemsize)
    steps = c // tc
    if mode == "nn":
        a_spec = pl.BlockSpec((tm, tc), lambda i, j, k: (i, k))
        b_spec = pl.BlockSpec((tc, tn), lambda i, j, k: (k, j))
        dims = NN_DIMS
    elif mode == "nt":
        a_spec = pl.BlockSpec((tm, tc), lambda i, j, k: (i, k))
        b_spec = pl.BlockSpec((tn, tc), lambda i, j, k: (j, k))
        dims = NT_DIMS
    else:
        a_spec = pl.BlockSpec((tc, tm), lambda i, j, k: (k, i))
        b_spec = pl.BlockSpec((tc, tn), lambda i, j, k: (k, j))
        dims = TN_DIMS

    def body(a_ref, b_ref, o_ref, acc_ref):
        k = pl.program_id(2)

        @pl.when(k == 0)
        def _():
            acc_ref[...] = jnp.zeros_like(acc_ref)

        acc_ref[...] += lax.dot_general(a_ref[...].astype(BF16), b_ref[...].astype(BF16), dims,
                                        preferred_element_type=F32)

        @pl.when(k == steps - 1)
        def _():
            o_ref[...] = acc_ref[...].astype(o_ref.dtype)

    def body_whole(a_ref, b_ref, o_ref):
        o_ref[...] = lax.dot_general(a_ref[...].astype(BF16), b_ref[...].astype(BF16), dims,
                                     preferred_element_type=F32).astype(o_ref.dtype)

    return pl.pallas_call(
        body if steps > 1 else body_whole, grid=(m // tm, n // tn, steps), in_specs=[a_spec, b_spec],
        out_specs=pl.BlockSpec((tm, tn), lambda i, j, k: (i, j)),
        out_shape=jax.ShapeDtypeStruct((m, n), out_dtype),
        scratch_shapes=[pltpu.VMEM((tm, tn), F32)] if steps > 1 else [],
        compiler_params=_params("parallel", "parallel", "arbitrary"), name=name,
    )(a, b)


@jax.custom_vjp
def linear(x, w):
    return _matmul(x, w, "nn", F32, "linear_fwd")


def _linear_fwd(x, w):
    return _matmul(x, w, "nn", F32, "linear_fwd"), (x, w)


def _linear_bwd(res, dy):
    x, w = res
    dx = _matmul(dy, w, "nt", F32, "linear_dx")
    dw = _matmul(x, dy, "tn", w.dtype, "linear_dw")
    return dx, dw


linear.defvjp(_linear_fwd, _linear_bwd)


def _matmul_pair(a, b, wa, wb, name):
    m, ka = a.shape
    kb = b.shape[1]
    n = wa.shape[1]
    tm, tn, tc = _matmul_tiles(m, n, ka + kb, a.dtype.itemsize, wa.dtype.itemsize, 4)
    assert tc == ka + kb, (tc, ka, kb)

    def body(a_ref, b_ref, wa_ref, wb_ref, o_ref):
        o_ref[...] = (
            lax.dot_general(a_ref[...].astype(BF16), wa_ref[...].astype(BF16), NN_DIMS, preferred_element_type=F32)
            + lax.dot_general(b_ref[...].astype(BF16), wb_ref[...].astype(BF16), NN_DIMS, preferred_element_type=F32))

    return pl.pallas_call(
        body, grid=(m // tm, n // tn),
        in_specs=[pl.BlockSpec((tm, ka), lambda i, j: (i, 0)), pl.BlockSpec((tm, kb), lambda i, j: (i, 0)),
                  pl.BlockSpec((ka, tn), lambda i, j: (0, j)), pl.BlockSpec((kb, tn), lambda i, j: (0, j))],
        out_specs=pl.BlockSpec((tm, tn), lambda i, j: (i, j)), out_shape=jax.ShapeDtypeStruct((m, n), F32),
        compiler_params=_params("parallel", "parallel"), name=name,
    )(a, b, wa, wb)


@jax.custom_vjp
def linear_pair(a, b, w):
    ka = a.shape[1]
    return _matmul_pair(a, b, w[:ka], w[ka:], "linear_pair_fwd")


def _linear_pair_fwd(a, b, w):
    return linear_pair(a, b, w), (a, b, w)


def _linear_pair_bwd(res, dz):
    a, b, w = res
    ka = a.shape[1]
    da = _matmul(dz, w[:ka], "nt", F32, "linear_dx")
    db = _matmul(dz, w[ka:], "nt", F32, "linear_dx")
    dw = jnp.concatenate([_matmul(a, dz, "tn", w.dtype, "linear_dw"), _matmul(b, dz, "tn", w.dtype, "linear_dw")],
                         axis=0)
    return da, db, dw


linear_pair.defvjp(_linear_pair_fwd, _linear_pair_bwd)


def _rms_forward(x, g, out_dtype=F32, residual=None):
    rows, d = x.shape
    gd = g.shape[0]
    tr = _row_tile(rows, d, align=16)
    tile = pl.BlockSpec((tr, d), lambda i: (i, 0))

    def body(x_ref, g_ref, *rest):
        y_ref = rest[-1]
        for c0 in range(0, d, gd):
            xv = x_ref[:, c0:c0 + gd]
            r = lax.rsqrt(jnp.mean(xv * xv, axis=-1, keepdims=True) + EPS)
            y = (xv * r) * g_ref[...]
            if residual is not None:
                y = rest[0][:, c0:c0 + gd] + y
            y_ref[:, c0:c0 + gd] = y.astype(y_ref.dtype)

    extra = [] if residual is None else [residual]
    return pl.pallas_call(
        body, grid=(rows // tr,),
        in_specs=[tile, pl.BlockSpec((1, gd), lambda i: (0, 0))] + [tile] * len(extra),
        out_specs=tile, out_shape=jax.ShapeDtypeStruct((rows, d), out_dtype),
        compiler_params=_params("parallel"), name="rmsnorm_fwd",
    )(x, g.reshape(1, gd), *extra)


def _rms_backward(x, g, dy, residual=None):
    rows, d = x.shape
    gd = g.shape[0]
    tr = _row_tile(rows, d)
    tile = pl.BlockSpec((tr, d), lambda i: (i, 0))

    def body(x_ref, g_ref, dy_ref, *rest):
        dx_ref, dg_ref = rest[-2:]
        i = pl.program_id(0)

        @pl.when(i == 0)
        def _():
            dg_ref[...] = jnp.zeros_like(dg_ref)

        for c0 in range(0, d, gd):
            xv = x_ref[:, c0:c0 + gd]
            dyv = dy_ref[:, c0:c0 + gd]
            r = lax.rsqrt(jnp.mean(xv * xv, axis=-1, keepdims=True) + EPS)
            xh = xv * r
            t = dyv * g_ref[...]
            dx = r * (t - xh * jnp.mean(t * xh, axis=-1, keepdims=True))
            if residual is not None:
                dx = rest[0][:, c0:c0 + gd] + dx
            dx_ref[:, c0:c0 + gd] = dx
            dg_ref[...] += jnp.sum(dyv * xh, axis=0, keepdims=True)

    extra = [] if residual is None else [residual]
    dx, dg = pl.pallas_call(
        body, grid=(rows // tr,),
        in_specs=[tile, pl.BlockSpec((1, gd), lambda i: (0, 0)), tile] + [tile] * len(extra),
        out_specs=[tile, pl.BlockSpec((1, gd), lambda i: (0, 0))],
        out_shape=[jax.ShapeDtypeStruct((rows, d), F32), jax.ShapeDtypeStruct((1, gd), F32)],
        compiler_params=_params("arbitrary"), name="rmsnorm_bwd",
    )(x, g.reshape(1, gd), dy, *extra)
    return dx, dg.reshape(g.shape)


@jax.custom_vjp
def rms_norm(x, g):
    return _rms_forward(x, g)


def _rms_norm_fwd(x, g):
    return _rms_forward(x, g), (x, g)


def _rms_norm_bwd(res, dy):
    x, g = res
    return _rms_backward(x, g, dy)


rms_norm.defvjp(_rms_norm_fwd, _rms_norm_bwd)


@jax.custom_vjp
def add_norm(h, x, g):
    return _rms_forward(x, g, F32, h)


def _add_norm_fwd(h, x, g):
    return _rms_forward(x, g, F32, h), (x, g)


def _add_norm_bwd(res, dy):
    x, g = res
    dx, dg = _rms_backward(x, g, dy)
    return dy, dx, dg


add_norm.defvjp(_add_norm_fwd, _add_norm_bwd)


@jax.custom_vjp
def norm_linear(x, g, w):
    return x, _matmul(_rms_forward(x, g, BF16), w, "nn", F32, "linear_fwd")


def _norm_linear_fwd(x, g, w):
    y = _rms_forward(x, g, BF16)
    return (x, _matmul(y, w, "nn", F32, "linear_fwd")), (x, g, w, y)


def _norm_linear_bwd(res, cts):
    x, g, w, y = res
    dx_other, dz = cts
    dy = _matmul(dz, w, "nt", F32, "linear_dx")
    dw = _matmul(y, dz, "tn", w.dtype, "linear_dw")
    dx, dg = _rms_backward(x, g, dy, dx_other)
    return dx, dg, dw


norm_linear.defvjp(_norm_linear_fwd, _norm_linear_bwd)


@functools.partial(jax.custom_vjp, nondiff_argnums=(1,))
def split_cols(x, bounds):
    return tuple(x[:, lo:hi] for lo, hi in bounds)


def _split_cols_fwd(x, bounds):
    return split_cols(x, bounds), x.shape[1]


def _split_cols_bwd(bounds, width, cts):
    parts = list(cts)
    tail = width - bounds[-1][1]
    if tail:
        parts.append(jnp.zeros((parts[0].shape[0], tail), parts[0].dtype))
    return (jnp.concatenate(parts, axis=1),)


split_cols.defvjp(_split_cols_fwd, _split_cols_bwd)


def _visibility_id(pos, kind, l_real):
    if kind == "fox":
        return pos
    pad_chunk = 2 + (l_real - N_META) // CHUNK
    frame_chunk = 1 + jnp.right_shift(pos - N_META, 6)
    return jnp.where(pos < N_META, 0, jnp.where(pos < l_real, frame_chunk, pad_chunk))


def _raw_scores(kind, a1, b1, a2, b2):
    s = lax.dot_general(a1.astype(BF16), b1.astype(BF16), NT_DIMS, preferred_element_type=F32)
    if kind == "mla":
        s = s + lax.dot_general(a2.astype(BF16), b2.astype(BF16), NT_DIMS, preferred_element_type=F32)
    return s


def _block_pairs(nb, look, by_key):
    outer, inner = [], []
    for a in range(nb):
        rng = range(max(a - look, 0), nb) if by_key else range(0, min(a + look, nb - 1) + 1)
        for b in rng:
            outer.append(a)
            inner.append(b)
    return jnp.asarray(outer, jnp.int32), jnp.asarray(inner, jnp.int32)


def _attention_call(body, n_pairs, in_specs, out_specs, out_shape, scratch, name, tables, operands):
    return pl.pallas_call(
        body,
        grid_spec=pltpu.PrefetchScalarGridSpec(num_scalar_prefetch=2, grid=(HEADS, n_pairs), in_specs=in_specs,
                                               out_specs=out_specs, scratch_shapes=scratch),
        out_shape=out_shape, compiler_params=_params("parallel", "arbitrary"), name=name,
    )(*tables, *operands)


def _attention_forward(kind, l_real, q1, k1, v, extras):
    L = q1.shape[0]
    T = _pick(L, _TILE_ATT)
    nb = L // T
    look = 1 if kind == "mla" else 0
    scale = (HEAD_DIM + ROPE_DIM) ** -0.5 if kind == "mla" else HEAD_DIM ** -0.5
    tables = _block_pairs(nb, look, False)

    q_tile = pl.BlockSpec((T, HEAD_DIM), lambda h, t, it, jt: (it[t], h))
    k_tile = pl.BlockSpec((T, HEAD_DIM), lambda h, t, it, jt: (jt[t], h))
    row_stat = pl.BlockSpec((None, T, 1), lambda h, t, it, jt: (h, it[t], 0))
    if kind == "mla":
        extra_specs = [pl.BlockSpec((None, T, ROPE_DIM), lambda h, t, it, jt: (h, it[t], 0)),
                       pl.BlockSpec((T, ROPE_DIM), lambda h, t, it, jt: (jt[t], 0))]
    else:
        extra_specs = [pl.BlockSpec((None, 1, T), lambda h, t, it, jt: (h, 0, jt[t]))]

    def body(it_ref, jt_ref, q1_ref, k1_ref, v_ref, *rest):
        if kind == "mla":
            eq_ref, ek_ref, o_ref, lse_ref, m_ref, l_ref, acc_ref = rest
        else:
            ek_ref, o_ref, lse_ref, m_ref, l_ref, acc_ref = rest
        t = pl.program_id(1)
        i, j = it_ref[t], jt_ref[t]

        @pl.when(j == 0)
        def _():
            m_ref[...] = jnp.full_like(m_ref, NEG)
            l_ref[...] = jnp.zeros_like(l_ref)
            acc_ref[...] = jnp.zeros_like(acc_ref)

        def block(masked):
            if kind == "mla":
                s = _raw_scores(kind, q1_ref[...] * scale, k1_ref[...], eq_ref[...] * scale, ek_ref[...])
            else:
                s = _raw_scores(kind, q1_ref[...] * scale, k1_ref[...], None, None) - ek_ref[...]
            if masked:
                pos_q = i * T + lax.broadcasted_iota(jnp.int32, (T, 1), 0)
                pos_k = j * T + lax.broadcasted_iota(jnp.int32, (1, T), 1)
                s = jnp.where(_visibility_id(pos_k, kind, l_real) <= _visibility_id(pos_q, kind, l_real), s, NEG)
            m_prev = m_ref[...]
            m_new = jnp.maximum(m_prev, jnp.max(s, axis=1, keepdims=True))
            alpha = jnp.exp(m_prev - m_new)
            p = jnp.exp(s - m_new)
            l_ref[...] = alpha * l_ref[...] + jnp.sum(p, axis=1, keepdims=True)
            m_ref[...] = m_new
            vb = v_ref[...].astype(BF16)
            p_hi = p.astype(BF16)
            pv = lax.dot_general(p_hi, vb, NN_DIMS, preferred_element_type=F32)
            if kind == "fox":
                p_lo = (p - p_hi.astype(F32)).astype(BF16)
                pv = pv + lax.dot_general(p_lo, vb, NN_DIMS, preferred_element_type=F32)
            acc_ref[...] = alpha * acc_ref[...] + pv

        @pl.when(j < i)
        def _():
            block(False)

        @pl.when(j >= i)
        def _():
            block(True)

        @pl.when(j == jnp.minimum(i + look, nb - 1))
        def _():
            o_ref[...] = acc_ref[...] / l_ref[...]
            lse_ref[...] = m_ref[...] + jnp.log(l_ref[...])

    return _attention_call(
        body, tables[0].shape[0], [q_tile, k_tile, k_tile] + extra_specs, [q_tile, row_stat],
        [jax.ShapeDtypeStruct((L, HEADS * HEAD_DIM), F32), jax.ShapeDtypeStruct((HEADS, L, 1), F32)],
        [pltpu.VMEM((T, 1), F32), pltpu.VMEM((T, 1), F32), pltpu.VMEM((T, HEAD_DIM), F32)],
        kind + "_attn_fwd", tables, (q1, k1, v, *extras))


def _attention_delta(o, do):
    L = o.shape[0]
    T = _pick(L, _TILE_ATT)
    tile = pl.BlockSpec((T, HEADS * HEAD_DIM), lambda i: (i, 0))

    def body(o_ref, do_ref, delta_ref):
        for h in range(HEADS):
            cols = slice(h * HEAD_DIM, (h + 1) * HEAD_DIM)
            delta_ref[h] = jnp.sum(do_ref[:, cols].astype(BF16).astype(F32) * o_ref[:, cols], axis=1, keepdims=True)

    return pl.pallas_call(
        body, grid=(L // T,), in_specs=[tile, tile],
        out_specs=pl.BlockSpec((HEADS, T, 1), lambda i: (0, i, 0)),
        out_shape=jax.ShapeDtypeStruct((HEADS, L, 1), F32),
        compiler_params=_params("parallel"), name="attn_delta",
    )(o, do)


def _attention_backward(kind, l_real, q1, k1, v, extras, do, lse_row, delta_row):
    L = q1.shape[0]
    T = _pick(L, _TILE_ATT)
    nb = L // T
    look = 1 if kind == "mla" else 0
    scale = (HEAD_DIM + ROPE_DIM) ** -0.5 if kind == "mla" else HEAD_DIM ** -0.5
    tables = _block_pairs(nb, look, True)

    k_tile = pl.BlockSpec((T, HEAD_DIM), lambda h, t, jt, it: (jt[t], h))
    q_tile = pl.BlockSpec((T, HEAD_DIM), lambda h, t, jt, it: (it[t], h))
    q_row = pl.BlockSpec((None, 1, T), lambda h, t, jt, it: (h, 0, it[t]))
    if kind == "mla":
        extra_specs = [pl.BlockSpec((None, T, ROPE_DIM), lambda h, t, jt, it: (h, it[t], 0)),
                       pl.BlockSpec((T, ROPE_DIM), lambda h, t, jt, it: (jt[t], 0))]
        third_spec = pl.BlockSpec((None, T, ROPE_DIM), lambda h, t, jt, it: (h, jt[t], 0))
        third_shape = jax.ShapeDtypeStruct((HEADS, L, ROPE_DIM), F32)
        third_scratch = pltpu.VMEM((T, ROPE_DIM), F32)
    else:
        extra_specs = [pl.BlockSpec((None, T, 1), lambda h, t, jt, it: (h, jt[t], 0))]
        third_spec = pl.BlockSpec((None, T, 1), lambda h, t, jt, it: (h, jt[t], 0))
        third_shape = jax.ShapeDtypeStruct((HEADS, L, 1), F32)
        third_scratch = pltpu.VMEM((T, 1), F32)
    in_specs = [q_tile, k_tile, k_tile] + extra_specs + [q_tile, q_row, q_row]
    n_pairs = tables[0].shape[0]
    out_specs = [k_tile, k_tile, third_spec, pl.BlockSpec((L, HEAD_DIM), lambda h, t, jt, it: (0, h))]
    out_shape = [jax.ShapeDtypeStruct((L, HEADS * HEAD_DIM), F32), jax.ShapeDtypeStruct((L, HEADS * HEAD_DIM), F32),
                 third_shape, jax.ShapeDtypeStruct((L, HEADS * HEAD_DIM), F32)]
    if kind == "mla":
        out_specs.append(pl.BlockSpec((None, L, ROPE_DIM), lambda h, t, jt, it: (h, 0, 0)))
        out_shape.append(jax.ShapeDtypeStruct((HEADS, L, ROPE_DIM), F32))

    def body(jt_ref, it_ref, q1_ref, k1_ref, v_ref, *rest):
        if kind == "mla":
            (eq_ref, ek_ref, do_ref, lse_ref, delta_ref, dk1_ref, dv_ref, third_ref, dq1_ref, dq2_ref,
             acck_ref, accv_ref, acc3_ref) = rest
        else:
            (ek_ref, do_ref, lse_ref, delta_ref, dk1_ref, dv_ref, third_ref, dq1_ref,
             acck_ref, accv_ref, acc3_ref) = rest
        t = pl.program_id(1)
        j, i = jt_ref[t], it_ref[t]
        q_rows = pl.ds(pl.multiple_of(i * T, T), T)

        @pl.when(t == 0)
        def _():
            dq1_ref[...] = jnp.zeros_like(dq1_ref)
            if kind == "mla":
                dq2_ref[...] = jnp.zeros_like(dq2_ref)

        @pl.when(i == jnp.maximum(j - look, 0))
        def _():
            acck_ref[...] = jnp.zeros_like(acck_ref)
            accv_ref[...] = jnp.zeros_like(accv_ref)
            acc3_ref[...] = jnp.zeros_like(acc3_ref)

        def block(masked):
            if kind == "mla":
                kb = jnp.concatenate([k1_ref[...].astype(BF16), ek_ref[...].astype(BF16)], axis=1)
                qb = jnp.concatenate([(q1_ref[...] * scale).astype(BF16), (eq_ref[...] * scale).astype(BF16)],
                                     axis=1)
            else:
                kb, qb = k1_ref[...].astype(BF16), (q1_ref[...] * scale).astype(BF16)
            st = lax.dot_general(kb, qb, NT_DIMS, preferred_element_type=F32)
            if kind == "fox":
                st = st - ek_ref[...]
            if masked:
                pos_k = j * T + lax.broadcasted_iota(jnp.int32, (T, 1), 0)
                pos_q = i * T + lax.broadcasted_iota(jnp.int32, (1, T), 1)
                st = jnp.where(_visibility_id(pos_k, kind, l_real) <= _visibility_id(pos_q, kind, l_real), st, NEG)
            pt = jnp.exp(st - lse_ref[...])
            dob = do_ref[...].astype(BF16)
            accv_ref[...] += lax.dot_general(pt.astype(BF16), dob, NN_DIMS, preferred_element_type=F32)
            dpt = lax.dot_general(v_ref[...].astype(BF16), dob, NT_DIMS, preferred_element_type=F32)
            dst = pt * (dpt - delta_ref[...])
            dsb = dst.astype(BF16)
            dk = lax.dot_general(dsb, qb, NN_DIMS, preferred_element_type=F32)
            dq = lax.dot_general(dsb, kb, TN_DIMS, preferred_element_type=F32)
            if kind == "mla":
                acck_ref[...] += dk[:, :HEAD_DIM]
                acc3_ref[...] += dk[:, HEAD_DIM:]
                dq1_ref[q_rows, :] += dq[:, :HEAD_DIM]
                dq2_ref[q_rows, :] += dq[:, HEAD_DIM:]
            else:
                acck_ref[...] += dk
                dq1_ref[q_rows, :] += dq
                acc3_ref[...] -= jnp.sum(dst, axis=1, keepdims=True)

        @pl.when(i > j)
        def _():
            block(False)

        @pl.when(i <= j)
        def _():
            block(True)

        @pl.when(i == nb - 1)
        def _():
            dk1_ref[...] = acck_ref[...]
            dv_ref[...] = accv_ref[...]
            third_ref[...] = acc3_ref[...]

        @pl.when(t == n_pairs - 1)
        def _():
            dq1_ref[...] = dq1_ref[...] * scale
            if kind == "mla":
                dq2_ref[...] = dq2_ref[...] * scale

    return _attention_call(
        body, n_pairs, in_specs, out_specs, out_shape,
        [pltpu.VMEM((T, HEAD_DIM), F32), pltpu.VMEM((T, HEAD_DIM), F32), third_scratch],
        kind + "_attn_bwd", tables, (q1, k1, v, *extras, do, lse_row, delta_row))


def _as_row(col):
    return col.reshape(col.shape[0], 1, col.shape[1])


@functools.partial(jax.custom_vjp, nondiff_argnums=(0,))
def mla_attention(l_real, qn, qr, kn, kr, v):
    return _attention_forward("mla", l_real, qn, kn, v, (qr, kr))[0]


def _mla_attention_fwd(l_real, qn, qr, kn, kr, v):
    o, lse = _attention_forward("mla", l_real, qn, kn, v, (qr, kr))
    return o, (qn, qr, kn, kr, v, o, lse)


def _mla_attention_bwd(l_real, res, do):
    qn, qr, kn, kr, v, o, lse = res
    delta = _attention_delta(o, do)
    dkn, dv, dkr_heads, dqn, dqr = _attention_backward("mla", l_real, qn, kn, v, (qr, kr), do, _as_row(lse),
                                                       _as_row(delta))
    return dqn, dqr, dkn, jnp.sum(dkr_heads, axis=0), dv


mla_attention.defvjp(_mla_attention_fwd, _mla_attention_bwd)


@functools.partial(jax.custom_vjp, nondiff_argnums=(0,))
def fox_attention(l_real, q, k, v, c):
    return _attention_forward("fox", l_real, q, k, v, (_as_row(c),))[0]


def _fox_attention_fwd(l_real, q, k, v, c):
    o, lse = _attention_forward("fox", l_real, q, k, v, (_as_row(c),))
    return o, (q, k, v, c, o, lse)


def _fox_attention_bwd(l_real, res, do):
    q, k, v, c, o, lse = res
    delta = _attention_delta(o, do)
    dk, dv, dc, dq = _attention_backward("fox", l_real, q, k, v, (c,), do, _as_row(lse), _as_row(delta))
    return dq, dk, dv, dc


fox_attention.defvjp(_fox_attention_fwd, _fox_attention_bwd)


GELU_C0 = 0.7978845608028654
GELU_C1 = 0.044715


def _shift_rows(x, prev, s):
    r = pltpu.roll(x, s, 0)
    pr = pltpu.roll(prev, s, 0)
    row = lax.broadcasted_iota(jnp.int32, prev.shape, 0)
    top = jnp.where(row < s, pr, r[0:SUBLANES])
    return jnp.concatenate([top, r[SUBLANES:]], axis=0)


def _conv_tiles(L, f):
    return _pick(L, _TILE_ATT), _pick(f, _TILE_FF)


def _conv_gate_forward(u, w, b, out_dtype):
    L, f2 = u.shape
    f = f2 // 2
    tm, tn = _conv_tiles(L, f)
    rb = tm // SUBLANES

    def body(u_ref, up_ref, w_ref, b_ref, o_ref):
        i = pl.program_id(1)
        x = u_ref[...]
        prev = jnp.where(i > 0, up_ref[...], 0.0)
        wv = w_ref[...]
        hc = b_ref[...] + ((wv[0:1] * _shift_rows(x, prev, 2) + wv[1:2] * _shift_rows(x, prev, 1)) + wv[2:3] * x)
        g = hc[:, :tn]
        gelu = 0.5 * g * (1.0 + jnp.tanh(GELU_C0 * (g + GELU_C1 * g * g * g)))
        o_ref[...] = (gelu * hc[:, tn:]).astype(o_ref.dtype)

    return pl.pallas_call(
        body, grid=(f // tn, L // tm),
        in_specs=[pl.BlockSpec((tm, 2 * tn), lambda j, i: (i, j)),
                  pl.BlockSpec((SUBLANES, 2 * tn), lambda j, i: (jnp.maximum(i * rb - 1, 0), j)),
                  pl.BlockSpec((3, 2 * tn), lambda j, i: (0, j)),
                  pl.BlockSpec((1, 2 * tn), lambda j, i: (0, j))],
        out_specs=pl.BlockSpec((tm, tn), lambda j, i: (i, j)),
        out_shape=jax.ShapeDtypeStruct((L, f), out_dtype),
        compiler_params=_params("parallel", "parallel"), name="conv_gate_fwd",
    )(u, u, w, b)


def _conv_gate_backward(u, w, b, dact, du_dtype):
    L, f2 = u.shape
    f = f2 // 2
    tm, tn = _conv_tiles(L, f)
    rb = tm // SUBLANES
    n_row_blocks = L // SUBLANES
    n_i = L // tm
    ext = tm + SUBLANES

    def next_rows(i):
        return jnp.minimum((i + 1) * rb, n_row_blocks - 1)

    def body(u_ref, up_ref, un_ref, da_ref, dan_ref, w_ref, b_ref, du_ref, dwb_ref):
        i = pl.program_id(1)
        is_last = i == n_i - 1
        prev = jnp.where(i > 0, up_ref[...], 0.0)
        xe = jnp.concatenate([u_ref[...], jnp.where(is_last, 0.0, un_ref[...])], axis=0)
        x1 = _shift_rows(xe, prev, 1)
        x2 = _shift_rows(xe, prev, 2)
        wv = w_ref[...]
        hc = b_ref[...] + ((wv[0:1] * x2 + wv[1:2] * x1) + wv[2:3] * xe)
        g, up = hc[:, :tn], hc[:, tn:]
        da = jnp.concatenate([da_ref[...], jnp.where(is_last, 0.0, dan_ref[...])], axis=0)
        t = jnp.tanh(GELU_C0 * (g + GELU_C1 * g * g * g))
        gelu = 0.5 * g * (1.0 + t)
        dgelu = 0.5 * (1.0 + t) + 0.5 * g * (1.0 - t * t) * (GELU_C0 * (1.0 + 3.0 * GELU_C1 * g * g))
        dh = jnp.concatenate([da * up * dgelu, da * gelu], axis=1)
        dh1 = pltpu.roll(dh, ext - 1, 0)
        dh2 = pltpu.roll(dh, ext - 2, 0)
        du_ref[...] = ((wv[2:3] * dh + wv[1:2] * dh1) + wv[0:1] * dh2)[:tm].astype(du_ref.dtype)
        dw0 = jnp.sum((dh * x2)[:tm], axis=0, keepdims=True)
        dw1 = jnp.sum((dh * x1)[:tm], axis=0, keepdims=True)
        dw2 = jnp.sum((dh * xe)[:tm], axis=0, keepdims=True)
        db = jnp.sum(dh[:tm], axis=0, keepdims=True)
        row = lax.broadcasted_iota(jnp.int32, (SUBLANES, 2 * tn), 0)
        upd = jnp.where(row == 0, dw0, jnp.where(row == 1, dw1, jnp.where(row == 2, dw2,
                        jnp.where(row == 3, db, 0.0))))

        @pl.when(i == 0)
        def _():
            dwb_ref[...] = jnp.zeros_like(dwb_ref)

        dwb_ref[...] += upd

    return pl.pallas_call(
        body, grid=(f // tn, n_i),
        in_specs=[pl.BlockSpec((tm, 2 * tn), lambda j, i: (i, j)),
                  pl.BlockSpec((SUBLANES, 2 * tn), lambda j, i: (jnp.maximum(i * rb - 1, 0), j)),
                  pl.BlockSpec((SUBLANES, 2 * tn), lambda j, i: (next_rows(i), j)),
                  pl.BlockSpec((tm, tn), lambda j, i: (i, j)),
                  pl.BlockSpec((SUBLANES, tn), lambda j, i: (next_rows(i), j)),
                  pl.BlockSpec((3, 2 * tn), lambda j, i: (0, j)),
                  pl.BlockSpec((1, 2 * tn), lambda j, i: (0, j))],
        out_specs=[pl.BlockSpec((tm, 2 * tn), lambda j, i: (i, j)),
                   pl.BlockSpec((SUBLANES, 2 * tn), lambda j, i: (0, j))],
        out_shape=[jax.ShapeDtypeStruct((L, f2), du_dtype), jax.ShapeDtypeStruct((SUBLANES, f2), F32)],
        compiler_params=_params("parallel", "arbitrary"), name="conv_gate_bwd",
    )(u, u, u, dact, dact, w, b)


@jax.custom_vjp
def conv_ffn(h, g, w_up, w_conv, b_conv, w_down):
    return _conv_ffn_fwd(h, g, w_up, w_conv, b_conv, w_down)[0]


def _conv_ffn_fwd(h, g, w_up, w_conv, b_conv, w_down):
    y = _rms_forward(h, g, BF16)
    u = _matmul(y, w_up, "nn", F32, "linear_fwd")
    act = _conv_gate_forward(u, w_conv, b_conv.reshape(1, -1), BF16)
    return (h, _matmul(act, w_down, "nn", F32, "linear_fwd")), (h, g, w_up, w_conv, b_conv, w_down, y, u, act)


def _conv_ffn_bwd(res, cts):
    h, g, w_up, w_conv, b_conv, w_down, y, u, act = res
    dh_other, df = cts
    dact = _matmul(df, w_down, "nt", F32, "linear_dx")
    dw_down = _matmul(act, df, "tn", w_down.dtype, "linear_dw")
    du, dwb = _conv_gate_backward(u, w_conv, b_conv.reshape(1, -1), dact, BF16)
    dy = _matmul(du, w_up, "nt", F32, "linear_dx")
    dw_up = _matmul(y, du, "tn", w_up.dtype, "linear_dw")
    dh, dg = _rms_backward(h, g, dy, dh_other)
    return dh, dg, dw_up, dwb[0:3], dwb[3], dw_down


conv_ffn.defvjp(_conv_ffn_fwd, _conv_ffn_bwd)


def _loss_rows(y, target):
    rows, d = y.shape
    tr = _row_tile(rows, d)

    def body(y_ref, t_ref, loss_ref, dy_ref):
        err = y_ref[...] - t_ref[...]
        loss_ref[...] = 0.5 * jnp.mean(err * err, axis=-1, keepdims=True)
        dy_ref[...] = err * (1.0 / d)

    return pl.pallas_call(
        body, grid=(rows // tr,),
        in_specs=[pl.BlockSpec((tr, d), lambda i: (i, 0)), pl.BlockSpec((tr, d), lambda i: (i, 0))],
        out_specs=[pl.BlockSpec((tr, 1), lambda i: (i, 0)), pl.BlockSpec((tr, d), lambda i: (i, 0))],
        out_shape=[jax.ShapeDtypeStruct((rows, 1), F32), jax.ShapeDtypeStruct((rows, d), F32)],
        compiler_params=_params("parallel"), name="loss_head",
    )(y, target)


@jax.custom_vjp
def token_loss(y, target):
    return jnp.sum(_loss_rows(y, target)[0])


def _token_loss_fwd(y, target):
    rows, dy = _loss_rows(y, target)
    return jnp.sum(rows), dy


def _token_loss_bwd(dy, ct):
    return ct * dy, -ct * dy


token_loss.defvjp(_token_loss_fwd, _token_loss_bwd)


def _cols_from_devices(g):
    k = g.shape[1]
    return jnp.transpose(g, (1, 0, 2)).reshape(k, -1)


@functools.partial(jax.custom_vjp, nondiff_argnums=(1,))
def _interleave_gate_up(a, f):
    tn = _pick(f, _TILE_FF)
    parts = []
    for j in range(f // tn):
        parts += [a[..., j * tn:(j + 1) * tn], a[..., f + j * tn:f + (j + 1) * tn]]
    return jnp.concatenate(parts, axis=-1)


def _interleave_fwd(a, f):
    return _interleave_gate_up(a, f), None


def _interleave_bwd(f, _, ct):
    tn = _pick(f, _TILE_FF)
    gate = [ct[..., 2 * j * tn:(2 * j + 1) * tn] for j in range(f // tn)]
    up = [ct[..., (2 * j + 1) * tn:(2 * j + 2) * tn] for j in range(f // tn)]
    return (jnp.concatenate(gate + up, axis=-1),)


_interleave_gate_up.defvjp(_interleave_fwd, _interleave_bwd)


def _rope(x, cos, sin):
    half = x.shape[-1] // 2
    x1, x2 = x[..., :half], x[..., half:]
    return jnp.concatenate([x1 * cos - x2 * sin, x2 * cos + x1 * sin], axis=-1)


PROJ_BOUNDS = ((0, 512), (512, 1024), (1024, 2048), (2048, 3072), (3072, 4096), (4096, 5120), (5120, 5184),
               (5184, 5192))


def _layer(h, big, small, conv_w, l, l_real, cos, sin):
    L, d = h.shape
    w_in = _cols_from_devices(big["w_in"])
    w_in = jnp.concatenate([w_in[:, :1024], w_in[:, 1088:5184], w_in[:, 1024:1088], w_in[:, 5184:],
                            jnp.zeros((d, IN_COLS_PADDED - IN_COLS), w_in.dtype)], axis=1)
    w_q_up = _cols_from_devices(big["w_q_up"]).reshape(MLA_Q_LORA, HEADS, HEAD_DIM + ROPE_DIM)
    w_q_up = jnp.concatenate([w_q_up[:, :, :HEAD_DIM].reshape(MLA_Q_LORA, -1),
                              w_q_up[:, :, HEAD_DIM:].reshape(MLA_Q_LORA, -1)], axis=1)
    w_kv_up = _cols_from_devices(big["w_kv_up"]).reshape(MLA_KV_LORA, HEADS, 2 * HEAD_DIM)
    w_kv_up = jnp.concatenate([w_kv_up[:, :, :HEAD_DIM].reshape(MLA_KV_LORA, -1),
                               w_kv_up[:, :, HEAD_DIM:].reshape(MLA_KV_LORA, -1)], axis=1)
    w_out = big["w_out"].reshape(-1, d)
    f = big["w_ffn_down"].shape[0] * big["w_ffn_down"].shape[1]
    w_ffn_up = _interleave_gate_up(_cols_from_devices(big["w_ffn_up"]), f)
    w_ffn_down = big["w_ffn_down"].reshape(f, d)
    w_conv = _interleave_gate_up(conv_w, f)
    b_conv = _interleave_gate_up(small["b_ffn_conv"][l], f)

    h, proj = norm_linear(h, small["ln_mix_pre"][l], w_in)
    c_q, c_kv, fq, fk, fv, fg, k_rope, ff = split_cols(proj, PROJ_BOUNDS)

    q = norm_linear(c_q, small["g_q_latent"][l], w_q_up)[1]
    qn, qr = split_cols(q, ((0, HEADS * HEAD_DIM), (HEADS * HEAD_DIM, HEADS * (HEAD_DIM + ROPE_DIM))))
    kv = norm_linear(c_kv, small["g_kv_latent"][l], w_kv_up)[1]
    kn, v = split_cols(kv, ((0, HEADS * HEAD_DIM), (HEADS * HEAD_DIM, 2 * HEADS * HEAD_DIM)))
    qr = jnp.transpose(_rope(qr.reshape(L, HEADS, ROPE_DIM), cos[:, None, :], sin[:, None, :]), (1, 0, 2))
    kr = _rope(k_rope, cos, sin)
    a = mla_attention(l_real, qn, qr, kn, kr, v)

    fqn = rms_norm(fq, small["g_fox_q"][l])
    fkn = rms_norm(fk, small["g_fox_k"][l])
    log_f = jax.nn.log_sigmoid(ff + small["b_forget"][l])
    c = jnp.cumsum(log_f, axis=0).T[:, :, None]
    bmix = fox_attention(l_real, fqn, fkn, fv, c) * jax.nn.sigmoid(fg)

    mix = linear_pair(a, bmix, w_out)
    h = add_norm(h, mix, small["ln_mix_post"][l])

    h, f_out = conv_ffn(h, small["ln_ffn_pre"][l], w_ffn_up, w_conv, b_conv, w_ffn_down)
    h = add_norm(h, f_out, small["ln_ffn_post"][l])
    return h


def _local_loss(big, small, meta, conv_w, x, target):
    s, d = x.shape
    l_real = N_META + s
    l_pad = -(-l_real // Q_BLOCK) * Q_BLOCK
    h = jnp.concatenate([meta, x, jnp.zeros((l_pad - l_real, d), F32)], axis=0)
    half = ROPE_DIM // 2
    inv_freq = ROPE_THETA ** (-jnp.arange(half, dtype=F32) / half)
    ang = jnp.arange(l_pad, dtype=jnp.int32).astype(F32)[:, None] * inv_freq[None, :]
    cos, sin = jnp.cos(ang), jnp.sin(ang)
    for l in range(DEPTH):
        h = _layer(h, big[l], small, conv_w[l], l, l_real, cos, sin)
    return token_loss(h[N_META:l_real], target)


ANY_SPACE = pl.BlockSpec(memory_space=pl.ANY)


def _place():
    ix, iy, ic = lax.axis_index("x"), lax.axis_index("y"), lax.axis_index("c")
    return ix, iy, ic, [(1 - ix, iy), (ix, 1 - iy), (1 - ix, 1 - iy)]


def _comm_call(body, arrays, out_shapes, n_remote, n_local, name):
    return pl.pallas_call(
        body, out_shape=out_shapes, in_specs=[ANY_SPACE] * len(arrays), out_specs=[ANY_SPACE] * len(out_shapes),
        scratch_shapes=[pltpu.SemaphoreType.DMA((n_remote,)), pltpu.SemaphoreType.DMA((n_remote,)),
                        pltpu.SemaphoreType.DMA((n_local,))],
        name=name,
    )(*arrays)


def _gather(arrays, name):
    n = len(arrays)

    def body(*refs):
        xs, outs = refs[:n], refs[n:2 * n]
        send_sems, recv_sems, local_sems = refs[2 * n:]
        ix, iy, ic, chips = _place()
        me, sibling = (ix, iy, ic), (ix, iy, 1 - ic)

        def copy(a, k, block, to, src=None):
            dst = outs[a].at[4 * block[0] + 2 * block[1] + block[2]]
            return pltpu.make_async_remote_copy(
                src_ref=dst if src is None else src, dst_ref=dst, send_sem=send_sems.at[7 * a + k],
                recv_sem=recv_sems.at[7 * a + k], device_id=to, device_id_type=MESH_ID)

        local, sent = [], []
        for a in range(n):
            mine = pltpu.make_async_copy(xs[a], outs[a].at[4 * ix + 2 * iy + ic], local_sems.at[a])
            mine.start()
            local.append(mine)
            first = [copy(a, 0, me, sibling, src=xs[a])]
            first += [copy(a, 1 + j, me, (*chip, ic), src=xs[a]) for j, chip in enumerate(chips)]
            for cp in first:
                cp.start()
            sent += first
        for a in range(n):
            for j, chip in enumerate(chips):
                copy(a, 1 + j, (*chip, ic), me).wait_recv()
                passed = copy(a, 4 + j, (*chip, ic), sibling)
                passed.start()
                sent.append(passed)
        for a in range(n):
            copy(a, 0, sibling, me).wait_recv()
            for j, chip in enumerate(chips):
                copy(a, 4 + j, (*chip, 1 - ic), me).wait_recv()
        for cp in sent:
            cp.wait_send()
        for cp in local:
            cp.wait()

    out_shapes = [jax.ShapeDtypeStruct((N_DEV,) + a.shape, a.dtype) for a in arrays]
    return _comm_call(body, arrays, out_shapes, 7 * n, n, name)


def _swap_with_sibling(arrays, name):
    n = len(arrays)

    def body(*refs):
        xs, outs = refs[:n], refs[n:2 * n]
        send_sems, recv_sems, _ = refs[2 * n:]
        ix, iy, ic, _ = _place()
        copies = [pltpu.make_async_remote_copy(
            src_ref=xs[a], dst_ref=outs[a], send_sem=send_sems.at[a], recv_sem=recv_sems.at[a],
            device_id=(ix, iy, 1 - ic), device_id_type=MESH_ID) for a in range(n)]
        for cp in copies:
            cp.start()
        for cp in copies:
            cp.wait()

    out_shapes = [jax.ShapeDtypeStruct(a.shape, a.dtype) for a in arrays]
    return _comm_call(body, arrays, out_shapes, n, 1, name)


def _exchange_chips(arrays, name):
    n = len(arrays)

    def body(*refs):
        xs, outs = refs[:n], refs[n:2 * n]
        send_sems, recv_sems, local_sems = refs[2 * n:]
        ix, iy, ic, chips = _place()
        my_chip = 2 * ix + iy
        local, sent = [], []
        for a in range(n):
            mine = pltpu.make_async_copy(xs[a].at[my_chip], outs[a].at[my_chip], local_sems.at[a])
            mine.start()
            local.append(mine)
            for j, chip in enumerate(chips):
                cp = pltpu.make_async_remote_copy(
                    src_ref=xs[a].at[2 * chip[0] + chip[1]], dst_ref=outs[a].at[my_chip],
                    send_sem=send_sems.at[3 * a + j], recv_sem=recv_sems.at[3 * a + j],
                    device_id=(*chip, ic), device_id_type=MESH_ID)
                cp.start()
                sent.append(cp)
        for a in range(n):
            for j, chip in enumerate(chips):
                pltpu.make_async_remote_copy(
                    src_ref=xs[a].at[my_chip], dst_ref=outs[a].at[2 * chip[0] + chip[1]],
                    send_sem=send_sems.at[3 * a + j], recv_sem=recv_sems.at[3 * a + j],
                    device_id=(*chip, ic), device_id_type=MESH_ID).wait_recv()
        for cp in sent:
            cp.wait_send()
        for cp in local:
            cp.wait()

    out_shapes = [jax.ShapeDtypeStruct(a.shape, a.dtype) for a in arrays]
    return _comm_call(body, arrays, out_shapes, 3 * n, n, name)


def _sum_slots(x, out_dtype, name):
    slots, rows, cols = x.shape
    tr = _row_tile(rows, cols, (2 << 20) // slots, 16)

    def body(x_ref, o_ref):
        acc = x_ref[0].astype(F32)
        for s in range(1, slots):
            acc = acc + x_ref[s].astype(F32)
        o_ref[...] = acc.astype(o_ref.dtype)

    return pl.pallas_call(
        body, grid=(rows // tr,), in_specs=[pl.BlockSpec((slots, tr, cols), lambda i: (0, i, 0))],
        out_specs=pl.BlockSpec((tr, cols), lambda i: (i, 0)), out_shape=jax.ShapeDtypeStruct((rows, cols), out_dtype),
        compiler_params=_params("parallel"), name=name,
    )(x)


def _add_pairs(a, b, name):
    slots, rows, cols = a.shape
    tr = _row_tile(rows, cols, 1 << 20, 16)

    def body(a_ref, b_ref, o_ref):
        o_ref[...] = (a_ref[...].astype(F32) + b_ref[...].astype(F32)).astype(o_ref.dtype)

    spec = pl.BlockSpec((None, tr, cols), lambda s, i: (s, i, 0))
    return pl.pallas_call(
        body, grid=(slots, rows // tr), in_specs=[spec, spec], out_specs=spec,
        out_shape=jax.ShapeDtypeStruct(a.shape, BF16), compiler_params=_params("parallel", "parallel"), name=name,
    )(a, b)


def _reduce_scatter(grads, ic):
    by_chip = [g.reshape((4, 2) + g.shape[1:]) for g in grads]
    keep = [lax.dynamic_index_in_dim(g, ic, axis=1, keepdims=False) for g in by_chip]
    give = [lax.dynamic_index_in_dim(g, 1 - ic, axis=1, keepdims=False) for g in by_chip]
    got = _swap_with_sibling(give, "scatter_sibling")
    pairs = [_add_pairs(k, g, "add_pairs") for k, g in zip(keep, got)]
    received = _exchange_chips(pairs, "scatter_chips")
    return [_sum_slots(r, F32, "sum_grads") for r in received]


def _pack(arrays, dtype, row_multiple):
    flat = jnp.concatenate([a.astype(dtype).reshape(-1) for a in arrays])
    n = flat.shape[0]
    quantum = row_multiple * FLAT_COLS
    padded = -(-n // quantum) * quantum
    return jnp.pad(flat, (0, padded - n)).reshape(padded // FLAT_COLS, FLAT_COLS)


def _unpack(buf, shapes):
    flat = buf.reshape(-1)
    out, off = [], 0
    for shp in shapes:
        n = 1
        for s in shp:
            n *= s
        out.append(flat[off:off + n].reshape(tuple(shp)))
        off += n
    return out


def _adamw(w, g, m, v, name):
    shape = w.shape
    cols = shape[-1]
    w2, g2, m2, v2 = (a.reshape(-1, cols) for a in (w, g, m, v))
    rows = w2.shape[0]
    tr = _row_tile(rows, cols, 1 << 20)

    def body(w_ref, g_ref, m_ref, v_ref, d_ref, nm_ref, nv_ref):
        gv = g_ref[...]
        nm = ADAM_B1 * m_ref[...] + (1.0 - ADAM_B1) * gv
        nv = ADAM_B2 * v_ref[...] + (1.0 - ADAM_B2) * (gv * gv)
        m_hat = nm / (1.0 - ADAM_B1 ** ADAM_STEP)
        v_hat = nv / (1.0 - ADAM_B2 ** ADAM_STEP)
        d_ref[...] = -ADAM_LR * (m_hat / (jnp.sqrt(v_hat) + ADAM_EPS) + ADAM_WD * w_ref[...])
        nm_ref[...] = nm
        nv_ref[...] = nv

    spec = pl.BlockSpec((tr, cols), lambda i: (i, 0))
    outs = pl.pallas_call(
        body, grid=(rows // tr,), in_specs=[spec] * 4, out_specs=[spec] * 3,
        out_shape=[jax.ShapeDtypeStruct((rows, cols), F32)] * 3,
        compiler_params=_params("parallel"), name=name,
    )(w2, g2, m2, v2)
    return tuple(o.reshape(shape) for o in outs)


BIG = ("w_in", "w_q_up", "w_kv_up", "w_out", "w_ffn_up", "w_ffn_down")
REPLICATED = ("ln_mix_pre", "b_forget", "g_q_latent", "g_kv_latent", "g_fox_q", "g_fox_k", "ln_mix_post",
              "ln_ffn_pre", "b_ffn_conv", "ln_ffn_post")
WEIGHTS = ("meta_tokens", "ln_mix_pre", "w_in", "b_forget", "g_q_latent", "g_kv_latent", "w_q_up", "w_kv_up",
           "g_fox_q", "g_fox_k", "w_out", "ln_mix_post", "ln_ffn_pre", "w_ffn_up", "w_ffn_conv", "b_ffn_conv",
           "w_ffn_down", "ln_ffn_post")


def kernel(x, meta_tokens, ln_mix_pre, w_in, b_forget, g_q_latent, g_kv_latent, w_q_up, w_kv_up, g_fox_q, g_fox_k, w_out, ln_mix_post, ln_ffn_pre, w_ffn_up, w_ffn_conv, b_ffn_conv, w_ffn_down, ln_ffn_post, loss_target, m_meta_tokens, m_ln_mix_pre, m_w_in, m_b_forget, m_g_q_latent, m_g_kv_latent, m_w_q_up, m_w_kv_up, m_g_fox_q, m_g_fox_k, m_w_out, m_ln_mix_post, m_ln_ffn_pre, m_w_ffn_up, m_w_ffn_conv, m_b_ffn_conv, m_w_ffn_down, m_ln_ffn_post, v_meta_tokens, v_ln_mix_pre, v_w_in, v_b_forget, v_g_q_latent, v_g_kv_latent, v_w_q_up, v_w_kv_up, v_g_fox_q, v_g_fox_k, v_w_out, v_ln_mix_post, v_ln_ffn_pre, v_w_ffn_up, v_w_ffn_conv, v_b_ffn_conv, v_w_ffn_down, v_ln_ffn_post):
    w = dict(meta_tokens=meta_tokens, ln_mix_pre=ln_mix_pre, w_in=w_in, b_forget=b_forget, g_q_latent=g_q_latent,
             g_kv_latent=g_kv_latent, w_q_up=w_q_up, w_kv_up=w_kv_up, g_fox_q=g_fox_q, g_fox_k=g_fox_k, w_out=w_out,
             ln_mix_post=ln_mix_post, ln_ffn_pre=ln_ffn_pre, w_ffn_up=w_ffn_up, w_ffn_conv=w_ffn_conv,
             b_ffn_conv=b_ffn_conv, w_ffn_down=w_ffn_down, ln_ffn_post=ln_ffn_post)
    mom = dict(meta_tokens=m_meta_tokens, ln_mix_pre=m_ln_mix_pre, w_in=m_w_in, b_forget=m_b_forget,
               g_q_latent=m_g_q_latent, g_kv_latent=m_g_kv_latent, w_q_up=m_w_q_up, w_kv_up=m_w_kv_up,
               g_fox_q=m_g_fox_q, g_fox_k=m_g_fox_k, w_out=m_w_out, ln_mix_post=m_ln_mix_post,
               ln_ffn_pre=m_ln_ffn_pre, w_ffn_up=m_w_ffn_up, w_ffn_conv=m_w_ffn_conv, b_ffn_conv=m_b_ffn_conv,
               w_ffn_down=m_w_ffn_down, ln_ffn_post=m_ln_ffn_post)
    var = dict(meta_tokens=v_meta_tokens, ln_mix_pre=v_ln_mix_pre, w_in=v_w_in, b_forget=v_b_forget,
               g_q_latent=v_g_q_latent, g_kv_latent=v_g_kv_latent, w_q_up=v_w_q_up, w_kv_up=v_w_kv_up,
               g_fox_q=v_g_fox_q, g_fox_k=v_g_fox_k, w_out=v_w_out, ln_mix_post=v_ln_mix_post,
               ln_ffn_pre=v_ln_ffn_pre, w_ffn_up=v_w_ffn_up, w_ffn_conv=v_w_ffn_conv, b_ffn_conv=v_b_ffn_conv,
               w_ffn_down=v_w_ffn_down, ln_ffn_post=v_ln_ffn_post)
    ic = lax.axis_index("c")
    me = 4 * lax.axis_index("x") + 2 * lax.axis_index("y") + ic

    gathered = _gather([w[n].astype(BF16) for n in BIG] + [meta_tokens, w_ffn_conv], "gather_weights")
    big = [{n: gathered[k][:, l] for k, n in enumerate(BIG)} for l in range(DEPTH)]
    meta_shape, conv_shape = meta_tokens.shape, w_ffn_conv.shape
    meta_full = _cols_from_devices(gathered[len(BIG)])
    conv_full = jnp.transpose(gathered[len(BIG) + 1], (1, 2, 0, 3)).reshape(DEPTH, conv_shape[1], -1)
    small = {n: w[n] for n in REPLICATED}

    loss, grads = jax.value_and_grad(_local_loss, argnums=(0, 1, 2, 3, 4))(
        big, small, meta_full, [conv_full[l] for l in range(DEPTH)], x[0], loss_target[0])
    g_big, g_small, g_meta, g_conv, g_x = grads
    loss = lax.psum(loss, ("x", "y", "c"))

    grad = {}
    per_layer = [_reduce_scatter([g_big[l][n] for n in BIG], ic) for l in range(DEPTH)]
    for k, n in enumerate(BIG):
        grad[n] = jnp.stack([per_layer[l][k] for l in range(DEPTH)])

    small_arrays = [g_small[n] for n in REPLICATED] + [g_meta, jnp.stack(g_conv)]
    small_shapes = [a.shape for a in small_arrays]
    partials = _gather([_pack(small_arrays, F32, 16)], "gather_small_grads")[0]
    summed = _unpack(_sum_slots(partials, F32, "sum_small_grads"), small_shapes)
    for n, g in zip(REPLICATED, summed):
        grad[n] = g
    grad["meta_tokens"] = lax.dynamic_slice_in_dim(summed[-2], me * meta_shape[1], meta_shape[1], axis=1)
    grad["w_ffn_conv"] = lax.dynamic_slice_in_dim(summed[-1], me * conv_shape[2], conv_shape[2], axis=2)

    delta, new_m, new_v = {}, {}, {}
    for n in BIG:
        delta[n], new_m[n], new_v[n] = _adamw(w[n], grad[n], mom[n], var[n], "adamw_" + n)
    rest = [n for n in WEIGHTS if n not in BIG]
    rest_shapes = [w[n].shape for n in rest]
    flat = [_pack([src[n] for n in rest], F32, SUBLANES) for src in (w, grad, mom, var)]
    outs = _adamw(*flat, "adamw_small")
    for dst, buf in zip((delta, new_m, new_v), outs):
        for n, a in zip(rest, _unpack(buf, rest_shapes)):
            dst[n] = a

    return (loss, g_x[None], *[grad[n] for n in WEIGHTS], *[delta[n] for n in WEIGHTS],
            *[new_m[n] for n in WEIGHTS], *[new_v[n] for n in WEIGHTS])
```

```python
import functools

import jax
import jax.numpy as jnp
from jax import lax
from jax.experimental import pallas as pl
from jax.experimental.pallas import tpu as pltpu

F32 = jnp.float32
BF16 = jnp.bfloat16
MESH_ID = pl.DeviceIdType.MESH

N_DEV = 8
DEPTH = 4
N_META = 16
CHUNK = 64
Q_BLOCK = 128
HEADS = 8
HEAD_DIM = 128
ROPE_DIM = 64
MLA_Q_LORA = 512
MLA_KV_LORA = 512
FOX_W = HEADS * HEAD_DIM
ROPE_THETA = 10000.0
EPS = 1e-6
NEG = -1e30
IN_COLS = 5192
IN_COLS_PADDED = 5376

ADAM_LR = 0.001
ADAM_B1 = 0.9
ADAM_B2 = 0.999
ADAM_EPS = 1e-08
ADAM_WD = 0.01
ADAM_STEP = 10

LANES = 128
SUBLANES = 8
FLAT_COLS = 1024
VMEM_LIMIT_V7X = 52 * 1024 * 1024

NT_DIMS = (((1,), (1,)), ((), ()))
NN_DIMS = (((1,), (0,)), ((), ()))
TN_DIMS = (((0,), (0,)), ((), ()))

_TILE_ATT = (640, 512, 384, 256, 128)
_TILE_FF = (512, 256, 128)


def _pick(n, candidates):
    for c in candidates:
        if n % c == 0:
            return c
    return n


def _row_tile(rows, cols, budget_bytes=2 << 20, align=SUBLANES):
    best = None
    for t in range(align, rows + 1, align):
        if rows % t == 0 and t * cols * 4 <= budget_bytes:
            best = t
    return best if best is not None else rows


def _params(*semantics):
    return pltpu.CompilerParams(dimension_semantics=semantics, vmem_limit_bytes=VMEM_LIMIT_V7X)


MATMUL_VMEM_BUDGET = 36 << 20
MXU_FLOPS_V7X = 9.0e14
HBM_BYTES_PER_S_V7X = 2.5e12
GRID_STEP_S = 0.35e-6
MXU_DIM = 256


def _tile_candidates(n, cap):
    c = [t for t in range(LANES, min(n, cap) + 1, LANES) if n % t == 0]
    return c if c else [n]


def _matmul_tiles(m, n, c, a_bytes, b_bytes, o_bytes):
    best, best_cost = None, None
    for tc in _tile_candidates(c, 4096):
        steps = c // tc
        for tm in _tile_candidates(m, 2048):
            for tn in _tile_candidates(n, 2048):
                vmem = 2 * (tm * tc * a_bytes + tc * tn * b_bytes + tm * tn * o_bytes)
                vmem += tm * tn * 4 if steps > 1 else 0
                if vmem > MATMUL_VMEM_BUDGET:
                    continue
                traffic = m * c * a_bytes * (1 if steps == 1 else n // tn) + c * n * b_bytes * (m // tm)
                traffic += m * n * o_bytes
                grid = (m // tm) * (n // tn) * steps
                accumulate = 0 if steps == 1 else grid * tm * tn * 8 / 4.0e12
                fill = (-(-tn // MXU_DIM) * MXU_DIM / tn) * (-(-tc // MXU_DIM) * MXU_DIM / tc)
                cost = max(2.0 * m * n * c * fill / MXU_FLOPS_V7X, traffic / HBM_BYTES_PER_S_V7X)
                cost += grid * GRID_STEP_S + accumulate
                if best_cost is None or cost < best_cost:
                    best, best_cost = (tm, tn, tc), cost
    return best


def _matmul(a, b, mode, out_dtype, name):
    if mode == "nn":
        (m, c), (c2, n) = a.shape, b.shape
    elif mode == "nt":
        (m, c), (n, c2) = a.shape, b.shape
    else:
        (c, m), (c2, n) = a.shape, b.shape
    assert c == c2, (a.shape, b.shape, mode)
    tm, tn, tc = _matmul_tiles(m, n, c, a.dtype.itemsize, b.dtype.itemsize, jnp.dtype(out_dtype).itemsize)
    steps = c // tc
    if mode == "nn":
        a_spec = pl.BlockSpec((tm, tc), lambda i, j, k: (i, k))
        b_spec = pl.BlockSpec((tc, tn), lambda i, j, k: (k, j))
        dims = NN_DIMS
    elif mode == "nt":
        a_spec = pl.BlockSpec((tm, tc), lambda i, j, k: (i, k))
        b_spec = pl.BlockSpec((tn, tc), lambda i, j, k: (j, k))
        dims = NT_DIMS
    else:
        a_spec = pl.BlockSpec((tc, tm), lambda i, j, k: (k, i))
        b_spec = pl.BlockSpec((tc, tn), lambda i, j, k: (k, j))
        dims = TN_DIMS

    def body(a_ref, b_ref, o_ref, acc_ref):
        k = pl.program_id(2)

        @pl.when(k == 0)
        def _():
            acc_ref[...] = jnp.zeros_like(acc_ref)

        acc_ref[...] += lax.dot_general(a_ref[...].astype(BF16), b_ref[...].astype(BF16), dims,
                                        preferred_element_type=F32)

        @pl.when(k == steps - 1)
        def _():
            o_ref[...] = acc_ref[...].astype(o_ref.dtype)

    def body_whole(a_ref, b_ref, o_ref):
        o_ref[...] = lax.dot_general(a_ref[...].astype(BF16), b_ref[...].astype(BF16), dims,
                                     preferred_element_type=F32).astype(o_ref.dtype)

    return pl.pallas_call(
        body if steps > 1 else body_whole, grid=(m // tm, n // tn, steps), in_specs=[a_spec, b_spec],
        out_specs=pl.BlockSpec((tm, tn), lambda i, j, k: (i, j)),
        out_shape=jax.ShapeDtypeStruct((m, n), out_dtype),
        scratch_shapes=[pltpu.VMEM((tm, tn), F32)] if steps > 1 else [],
        compiler_params=_params("parallel", "parallel", "arbitrary"), name=name,
    )(a, b)


@jax.custom_vjp
def linear(x, w):
    return _matmul(x, w, "nn", F32, "linear_fwd")


def _linear_fwd(x, w):
    return _matmul(x, w, "nn", F32, "linear_fwd"), (x, w)


def _linear_bwd(res, dy):
    x, w = res
    dx = _matmul(dy, w, "nt", F32, "linear_dx")
    dw = _matmul(x, dy, "tn", w.dtype, "linear_dw")
    return dx, dw


linear.defvjp(_linear_fwd, _linear_bwd)


def _matmul_pair(a, b, wa, wb, name):
    m, ka = a.shape
    kb = b.shape[1]
    n = wa.shape[1]
    tm, tn, tc = _matmul_tiles(m, n, ka + kb, a.dtype.itemsize, wa.dtype.itemsize, 4)
    assert tc == ka + kb, (tc, ka, kb)

    def body(a_ref, b_ref, wa_ref, wb_ref, o_ref):
        o_ref[...] = (
            lax.dot_general(a_ref[...].astype(BF16), wa_ref[...].astype(BF16), NN_DIMS, preferred_element_type=F32)
            + lax.dot_general(b_ref[...].astype(BF16), wb_ref[...].astype(BF16), NN_DIMS, preferred_element_type=F32))

    return pl.pallas_call(
        body, grid=(m // tm, n // tn),
        in_specs=[pl.BlockSpec((tm, ka), lambda i, j: (i, 0)), pl.BlockSpec((tm, kb), lambda i, j: (i, 0)),
                  pl.BlockSpec((ka, tn), lambda i, j: (0, j)), pl.BlockSpec((kb, tn), lambda i, j: (0, j))],
        out_specs=pl.BlockSpec((tm, tn), lambda i, j: (i, j)), out_shape=jax.ShapeDtypeStruct((m, n), F32),
        compiler_params=_params("parallel", "parallel"), name=name,
    )(a, b, wa, wb)


@jax.custom_vjp
def linear_pair(a, b, w):
    ka = a.shape[1]
    return _matmul_pair(a, b, w[:ka], w[ka:], "linear_pair_fwd")


def _linear_pair_fwd(a, b, w):
    return linear_pair(a, b, w), (a, b, w)


def _linear_pair_bwd(res, dz):
    a, b, w = res
    ka = a.shape[1]
    da = _matmul(dz, w[:ka], "nt", F32, "linear_dx")
    db = _matmul(dz, w[ka:], "nt", F32, "linear_dx")
    dw = jnp.concatenate([_matmul(a, dz, "tn", w.dtype, "linear_dw"), _matmul(b, dz, "tn", w.dtype, "linear_dw")],
                         axis=0)
    return da, db, dw


linear_pair.defvjp(_linear_pair_fwd, _linear_pair_bwd)


def _rms_forward(x, g, out_dtype=F32, residual=None):
    rows, d = x.shape
    gd = g.shape[0]
    tr = _row_tile(rows, d, align=16)
    tile = pl.BlockSpec((tr, d), lambda i: (i, 0))

    def body(x_ref, g_ref, *rest):
        y_ref = rest[-1]
        for c0 in range(0, d, gd):
            xv = x_ref[:, c0:c0 + gd]
            r = lax.rsqrt(jnp.mean(xv * xv, axis=-1, keepdims=True) + EPS)
            y = (xv * r) * g_ref[...]
            if residual is not None:
                y = rest[0][:, c0:c0 + gd] + y
            y_ref[:, c0:c0 + gd] = y.astype(y_ref.dtype)

    extra = [] if residual is None else [residual]
    return pl.pallas_call(
        body, grid=(rows // tr,),
        in_specs=[tile, pl.BlockSpec((1, gd), lambda i: (0, 0))] + [tile] * len(extra),
        out_specs=tile, out_shape=jax.ShapeDtypeStruct((rows, d), out_dtype),
        compiler_params=_params("parallel"), name="rmsnorm_fwd",
    )(x, g.reshape(1, gd), *extra)


def _rms_backward(x, g, dy, residual=None):
    rows, d = x.shape
    gd = g.shape[0]
    tr = _row_tile(rows, d)
    tile = pl.BlockSpec((tr, d), lambda i: (i, 0))

    def body(x_ref, g_ref, dy_ref, *rest):
        dx_ref, dg_ref = rest[-2:]
        i = pl.program_id(0)

        @pl.when(i == 0)
        def _():
            dg_ref[...] = jnp.zeros_like(dg_ref)

        for c0 in range(0, d, gd):
            xv = x_ref[:, c0:c0 + gd]
            dyv = dy_ref[:, c0:c0 + gd]
            r = lax.rsqrt(jnp.mean(xv * xv, axis=-1, keepdims=True) + EPS)
            xh = xv * r
            t = dyv * g_ref[...]
            dx = r * (t - xh * jnp.mean(t * xh, axis=-1, keepdims=True))
            if residual is not None:
                dx = rest[0][:, c0:c0 + gd] + dx
            dx_ref[:, c0:c0 + gd] = dx
            dg_ref[...] += jnp.sum(dyv * xh, axis=0, keepdims=True)

    extra = [] if residual is None else [residual]
    dx, dg = pl.pallas_call(
        body, grid=(rows // tr,),
        in_specs=[tile, pl.BlockSpec((1, gd), lambda i: (0, 0)), tile] + [tile] * len(extra),
        out_specs=[tile, pl.BlockSpec((1, gd), lambda i: (0, 0))],
        out_shape=[jax.ShapeDtypeStruct((rows, d), F32), jax.ShapeDtypeStruct((1, gd), F32)],
        compiler_params=_params("arbitrary"), name="rmsnorm_bwd",
    )(x, g.reshape(1, gd), dy, *extra)
    return dx, dg.reshape(g.shape)


@jax.custom_vjp
def rms_norm(x, g):
    return _rms_forward(x, g)


def _rms_norm_fwd(x, g):
    return _rms_forward(x, g), (x, g)


def _rms_norm_bwd(res, dy):
    x, g = res
    return _rms_backward(x, g, dy)


rms_norm.defvjp(_rms_norm_fwd, _rms_norm_bwd)


@jax.custom_vjp
def add_norm(h, x, g):
    return _rms_forward(x, g, F32, h)


def _add_norm_fwd(h, x, g):
    return _rms_forward(x, g, F32, h), (x, g)


def _add_norm_bwd(res, dy):
    x, g = res
    dx, dg = _rms_backward(x, g, dy)
    return dy, dx, dg


add_norm.defvjp(_add_norm_fwd, _add_norm_bwd)


@functools.partial(jax.custom_vjp, nondiff_argnums=(3,))
def norm_linear_split(x, g, w, bounds):
    z = _matmul(_rms_forward(x, g, BF16), w, "nn", F32, "linear_fwd")
    return (x,) + tuple(z[:, lo:hi] for lo, hi in bounds)


def _norm_linear_split_fwd(x, g, w, bounds):
    y = _rms_forward(x, g, BF16)
    z = _matmul(y, w, "nn", F32, "linear_fwd")
    return (x,) + tuple(z[:, lo:hi] for lo, hi in bounds), (x, g, w, y)


def _norm_linear_split_bwd(bounds, res, cts):
    x, g, w, y = res
    dx_other, parts = cts[0], list(cts[1:])
    tail = w.shape[1] - bounds[-1][1]
    if tail:
        parts.append(jnp.zeros((x.shape[0], tail), F32))
    dz = jnp.concatenate(parts, axis=1).astype(BF16)
    dy = _matmul(dz, w, "nt", F32, "linear_dx")
    dw = _matmul(y, dz, "tn", w.dtype, "linear_dw")
    dx, dg = _rms_backward(x, g, dy, dx_other)
    return dx, dg, dw


norm_linear_split.defvjp(_norm_linear_split_fwd, _norm_linear_split_bwd)


def _visibility_id(pos, kind, l_real):
    if kind == "fox":
        return pos
    pad_chunk = 2 + (l_real - N_META) // CHUNK
    frame_chunk = 1 + jnp.right_shift(pos - N_META, 6)
    return jnp.where(pos < N_META, 0, jnp.where(pos < l_real, frame_chunk, pad_chunk))


def _raw_scores(kind, a1, b1, a2, b2):
    s = lax.dot_general(a1.astype(BF16), b1.astype(BF16), NT_DIMS, preferred_element_type=F32)
    if kind == "mla":
        s = s + lax.dot_general(a2.astype(BF16), b2.astype(BF16), NT_DIMS, preferred_element_type=F32)
    return s


def _block_pairs(nb, look, by_key):
    outer, inner = [], []
    for a in range(nb):
        rng = range(max(a - look, 0), nb) if by_key else range(0, min(a + look, nb - 1) + 1)
        for b in rng:
            outer.append(a)
            inner.append(b)
    return jnp.asarray(outer, jnp.int32), jnp.asarray(inner, jnp.int32)


def _attention_call(body, n_pairs, in_specs, out_specs, out_shape, scratch, name, tables, operands):
    return pl.pallas_call(
        body,
        grid_spec=pltpu.PrefetchScalarGridSpec(num_scalar_prefetch=2, grid=(HEADS, n_pairs), in_specs=in_specs,
                                               out_specs=out_specs, scratch_shapes=scratch),
        out_shape=out_shape, compiler_params=_params("parallel", "arbitrary"), name=name,
    )(*tables, *operands)


def _attention_forward(kind, l_real, q1, k1, v, extras):
    L = q1.shape[0]
    T = _pick(L, _TILE_ATT)
    nb = L // T
    look = 1 if kind == "mla" else 0
    scale = (HEAD_DIM + ROPE_DIM) ** -0.5 if kind == "mla" else HEAD_DIM ** -0.5
    tables = _block_pairs(nb, look, False)

    q_tile = pl.BlockSpec((T, HEAD_DIM), lambda h, t, it, jt: (it[t], h))
    k_tile = pl.BlockSpec((T, HEAD_DIM), lambda h, t, it, jt: (jt[t], h))
    row_stat = pl.BlockSpec((None, T, 1), lambda h, t, it, jt: (h, it[t], 0))
    if kind == "mla":
        extra_specs = [pl.BlockSpec((None, T, ROPE_DIM), lambda h, t, it, jt: (h, it[t], 0)),
                       pl.BlockSpec((T, ROPE_DIM), lambda h, t, it, jt: (jt[t], 0))]
    else:
        extra_specs = [pl.BlockSpec((None, 1, T), lambda h, t, it, jt: (h, 0, jt[t]))]

    def body(it_ref, jt_ref, q1_ref, k1_ref, v_ref, *rest):
        if kind == "mla":
            eq_ref, ek_ref, o_ref, lse_ref, m_ref, l_ref, acc_ref = rest
        else:
            ek_ref, o_ref, lse_ref, m_ref, l_ref, acc_ref = rest
        t = pl.program_id(1)
        i, j = it_ref[t], jt_ref[t]

        @pl.when(j == 0)
        def _():
            m_ref[...] = jnp.full_like(m_ref, NEG)
            l_ref[...] = jnp.zeros_like(l_ref)
            acc_ref[...] = jnp.zeros_like(acc_ref)

        def block(masked):
            if kind == "mla":
                s = _raw_scores(kind, q1_ref[...] * scale, k1_ref[...], eq_ref[...] * scale, ek_ref[...])
            else:
                s = _raw_scores(kind, q1_ref[...] * scale, k1_ref[...], None, None) - ek_ref[...]
            if masked:
                pos_q = i * T + lax.broadcasted_iota(jnp.int32, (T, 1), 0)
                pos_k = j * T + lax.broadcasted_iota(jnp.int32, (1, T), 1)
                s = jnp.where(_visibility_id(pos_k, kind, l_real) <= _visibility_id(pos_q, kind, l_real), s, NEG)
            m_prev = m_ref[...]
            m_new = jnp.maximum(m_prev, jnp.max(s, axis=1, keepdims=True))
            alpha = jnp.exp(m_prev - m_new)
            p = jnp.exp(s - m_new)
            l_ref[...] = alpha * l_ref[...] + jnp.sum(p, axis=1, keepdims=True)
            m_ref[...] = m_new
            vb = v_ref[...].astype(BF16)
            p_hi = p.astype(BF16)
            pv = lax.dot_general(p_hi, vb, NN_DIMS, preferred_element_type=F32)
            if kind == "fox":
                p_lo = (p - p_hi.astype(F32)).astype(BF16)
                pv = pv + lax.dot_general(p_lo, vb, NN_DIMS, preferred_element_type=F32)
            acc_ref[...] = alpha * acc_ref[...] + pv

        @pl.when(j < i)
        def _():
            block(False)

        @pl.when(j >= i)
        def _():
            block(True)

        @pl.when(j == jnp.minimum(i + look, nb - 1))
        def _():
            o_ref[...] = acc_ref[...] / l_ref[...]
            lse_ref[...] = m_ref[...] + jnp.log(l_ref[...])

    return _attention_call(
        body, tables[0].shape[0], [q_tile, k_tile, k_tile] + extra_specs, [q_tile, row_stat],
        [jax.ShapeDtypeStruct((L, HEADS * HEAD_DIM), F32), jax.ShapeDtypeStruct((HEADS, L, 1), F32)],
        [pltpu.VMEM((T, 1), F32), pltpu.VMEM((T, 1), F32), pltpu.VMEM((T, HEAD_DIM), F32)],
        kind + "_attn_fwd", tables, (q1, k1, v, *extras))


def _attention_delta(o, do):
    L = o.shape[0]
    T = _pick(L, _TILE_ATT)
    tile = pl.BlockSpec((T, HEADS * HEAD_DIM), lambda i: (i, 0))

    def body(o_ref, do_ref, delta_ref):
        for h in range(HEADS):
            cols = slice(h * HEAD_DIM, (h + 1) * HEAD_DIM)
            delta_ref[h] = jnp.sum(do_ref[:, cols].astype(BF16).astype(F32) * o_ref[:, cols], axis=1, keepdims=True)

    return pl.pallas_call(
        body, grid=(L // T,), in_specs=[tile, tile],
        out_specs=pl.BlockSpec((HEADS, T, 1), lambda i: (0, i, 0)),
        out_shape=jax.ShapeDtypeStruct((HEADS, L, 1), F32),
        compiler_params=_params("parallel"), name="attn_delta",
    )(o, do)


def _attention_backward(kind, l_real, q1, k1, v, extras, do, lse_row, delta_row):
    L = q1.shape[0]
    T = _pick(L, _TILE_ATT)
    nb = L // T
    look = 1 if kind == "mla" else 0
    scale = (HEAD_DIM + ROPE_DIM) ** -0.5 if kind == "mla" else HEAD_DIM ** -0.5
    tables = _block_pairs(nb, look, True)

    k_tile = pl.BlockSpec((T, HEAD_DIM), lambda h, t, jt, it: (jt[t], h))
    q_tile = pl.BlockSpec((T, HEAD_DIM), lambda h, t, jt, it: (it[t], h))
    q_row = pl.BlockSpec((None, 1, T), lambda h, t, jt, it: (h, 0, it[t]))
    if kind == "mla":
        extra_specs = [pl.BlockSpec((None, T, ROPE_DIM), lambda h, t, jt, it: (h, it[t], 0)),
                       pl.BlockSpec((T, ROPE_DIM), lambda h, t, jt, it: (jt[t], 0))]
        third_spec = pl.BlockSpec((None, T, ROPE_DIM), lambda h, t, jt, it: (h, jt[t], 0))
        third_shape = jax.ShapeDtypeStruct((HEADS, L, ROPE_DIM), F32)
        third_scratch = pltpu.VMEM((T, ROPE_DIM), F32)
    else:
        extra_specs = [pl.BlockSpec((None, T, 1), lambda h, t, jt, it: (h, jt[t], 0))]
        third_spec = pl.BlockSpec((None, T, 1), lambda h, t, jt, it: (h, jt[t], 0))
        third_shape = jax.ShapeDtypeStruct((HEADS, L, 1), F32)
        third_scratch = pltpu.VMEM((T, 1), F32)
    in_specs = [q_tile, k_tile, k_tile] + extra_specs + [q_tile, q_row, q_row]
    n_pairs = tables[0].shape[0]
    out_specs = [k_tile, k_tile, third_spec, pl.BlockSpec((L, HEAD_DIM), lambda h, t, jt, it: (0, h))]
    out_shape = [jax.ShapeDtypeStruct((L, HEADS * HEAD_DIM), F32), jax.ShapeDtypeStruct((L, HEADS * HEAD_DIM), F32),
                 third_shape, jax.ShapeDtypeStruct((L, HEADS * HEAD_DIM), F32)]
    if kind == "mla":
        out_specs.append(pl.BlockSpec((None, L, ROPE_DIM), lambda h, t, jt, it: (h, 0, 0)))
        out_shape.append(jax.ShapeDtypeStruct((HEADS, L, ROPE_DIM), F32))

    def body(jt_ref, it_ref, q1_ref, k1_ref, v_ref, *rest):
        if kind == "mla":
            (eq_ref, ek_ref, do_ref, lse_ref, delta_ref, dk1_ref, dv_ref, third_ref, dq1_ref, dq2_ref,
             acck_ref, accv_ref, acc3_ref) = rest
        else:
            (ek_ref, do_ref, lse_ref, delta_ref, dk1_ref, dv_ref, third_ref, dq1_ref,
             acck_ref, accv_ref, acc3_ref) = rest
        t = pl.program_id(1)
        j, i = jt_ref[t], it_ref[t]
        q_rows = pl.ds(pl.multiple_of(i * T, T), T)

        @pl.when(t == 0)
        def _():
            dq1_ref[...] = jnp.zeros_like(dq1_ref)
            if kind == "mla":
                dq2_ref[...] = jnp.zeros_like(dq2_ref)

        @pl.when(i == jnp.maximum(j - look, 0))
        def _():
            acck_ref[...] = jnp.zeros_like(acck_ref)
            accv_ref[...] = jnp.zeros_like(accv_ref)
            acc3_ref[...] = jnp.zeros_like(acc3_ref)

        def block(masked):
            if kind == "mla":
                kb = jnp.concatenate([k1_ref[...].astype(BF16), ek_ref[...].astype(BF16)], axis=1)
                qb = jnp.concatenate([(q1_ref[...] * scale).astype(BF16), (eq_ref[...] * scale).astype(BF16)],
                                     axis=1)
            else:
                kb, qb = k1_ref[...].astype(BF16), (q1_ref[...] * scale).astype(BF16)
            st = lax.dot_general(kb, qb, NT_DIMS, preferred_element_type=F32)
            if kind == "fox":
                st = st - ek_ref[...]
            if masked:
                pos_k = j * T + lax.broadcasted_iota(jnp.int32, (T, 1), 0)
                pos_q = i * T + lax.broadcasted_iota(jnp.int32, (1, T), 1)
                st = jnp.where(_visibility_id(pos_k, kind, l_real) <= _visibility_id(pos_q, kind, l_real), st, NEG)
            pt = jnp.exp(st - lse_ref[...])
            dob = do_ref[...].astype(BF16)
            accv_ref[...] += lax.dot_general(pt.astype(BF16), dob, NN_DIMS, preferred_element_type=F32)
            dpt = lax.dot_general(v_ref[...].astype(BF16), dob, NT_DIMS, preferred_element_type=F32)
            dst = pt * (dpt - delta_ref[...])
            dsb = dst.astype(BF16)
            dk = lax.dot_general(dsb, qb, NN_DIMS, preferred_element_type=F32)
            dq = lax.dot_general(dsb, kb, TN_DIMS, preferred_element_type=F32)
            if kind == "mla":
                acck_ref[...] += dk[:, :HEAD_DIM]
                acc3_ref[...] += dk[:, HEAD_DIM:]
                dq1_ref[q_rows, :] += dq[:, :HEAD_DIM]
                dq2_ref[q_rows, :] += dq[:, HEAD_DIM:]
            else:
                acck_ref[...] += dk
                dq1_ref[q_rows, :] += dq
                acc3_ref[...] -= jnp.sum(dst, axis=1, keepdims=True)

        @pl.when(i > j)
        def _():
            block(False)

        @pl.when(i <= j)
        def _():
            block(True)

        @pl.when(i == nb - 1)
        def _():
            dk1_ref[...] = acck_ref[...]
            dv_ref[...] = accv_ref[...]
            third_ref[...] = acc3_ref[...]

        @pl.when(t == n_pairs - 1)
        def _():
            dq1_ref[...] = dq1_ref[...] * scale
            if kind == "mla":
                dq2_ref[...] = dq2_ref[...] * scale

    return _attention_call(
        body, n_pairs, in_specs, out_specs, out_shape,
        [pltpu.VMEM((T, HEAD_DIM), F32), pltpu.VMEM((T, HEAD_DIM), F32), third_scratch],
        kind + "_attn_bwd", tables, (q1, k1, v, *extras, do, lse_row, delta_row))


def _as_row(col):
    return col.reshape(col.shape[0], 1, col.shape[1])


@functools.partial(jax.custom_vjp, nondiff_argnums=(0,))
def mla_attention(l_real, qn, qr, kn, kr, v):
    return _attention_forward("mla", l_real, qn, kn, v, (qr, kr))[0]


def _mla_attention_fwd(l_real, qn, qr, kn, kr, v):
    o, lse = _attention_forward("mla", l_real, qn, kn, v, (qr, kr))
    return o, (qn, qr, kn, kr, v, o, lse)


def _mla_attention_bwd(l_real, res, do):
    qn, qr, kn, kr, v, o, lse = res
    delta = _attention_delta(o, do)
    dkn, dv, dkr_heads, dqn, dqr = _attention_backward("mla", l_real, qn, kn, v, (qr, kr), do, _as_row(lse),
                                                       _as_row(delta))
    return dqn, dqr, dkn, jnp.sum(dkr_heads, axis=0), dv


mla_attention.defvjp(_mla_attention_fwd, _mla_attention_bwd)


@functools.partial(jax.custom_vjp, nondiff_argnums=(0,))
def fox_attention(l_real, q, k, v, c):
    return _attention_forward("fox", l_real, q, k, v, (_as_row(c),))[0]


def _fox_attention_fwd(l_real, q, k, v, c):
    o, lse = _attention_forward("fox", l_real, q, k, v, (_as_row(c),))
    return o, (q, k, v, c, o, lse)


def _fox_attention_bwd(l_real, res, do):
    q, k, v, c, o, lse = res
    delta = _attention_delta(o, do)
    dk, dv, dc, dq = _attention_backward("fox", l_real, q, k, v, (c,), do, _as_row(lse), _as_row(delta))
    return dq, dk, dv, dc


fox_attention.defvjp(_fox_attention_fwd, _fox_attention_bwd)


GELU_C0 = 0.7978845608028654
GELU_C1 = 0.044715


def _shift_rows(x, prev, s):
    r = pltpu.roll(x, s, 0)
    pr = pltpu.roll(prev, s, 0)
    row = lax.broadcasted_iota(jnp.int32, prev.shape, 0)
    top = jnp.where(row < s, pr, r[0:SUBLANES])
    return jnp.concatenate([top, r[SUBLANES:]], axis=0)


def _conv_tiles(L, f):
    return _pick(L, _TILE_ATT), _pick(f, _TILE_FF)


def _conv_gate_forward(u, w, b, out_dtype):
    L, f2 = u.shape
    f = f2 // 2
    tm, tn = _conv_tiles(L, f)
    rb = tm // SUBLANES

    def body(u_ref, up_ref, w_ref, b_ref, o_ref):
        i = pl.program_id(1)
        x = u_ref[...]
        prev = jnp.where(i > 0, up_ref[...], 0.0)
        wv = w_ref[...]
        hc = b_ref[...] + ((wv[0:1] * _shift_rows(x, prev, 2) + wv[1:2] * _shift_rows(x, prev, 1)) + wv[2:3] * x)
        g = hc[:, :tn]
        gelu = 0.5 * g * (1.0 + jnp.tanh(GELU_C0 * (g + GELU_C1 * g * g * g)))
        o_ref[...] = (gelu * hc[:, tn:]).astype(o_ref.dtype)

    return pl.pallas_call(
        body, grid=(f // tn, L // tm),
        in_specs=[pl.BlockSpec((tm, 2 * tn), lambda j, i: (i, j)),
                  pl.BlockSpec((SUBLANES, 2 * tn), lambda j, i: (jnp.maximum(i * rb - 1, 0), j)),
                  pl.BlockSpec((3, 2 * tn), lambda j, i: (0, j)),
                  pl.BlockSpec((1, 2 * tn), lambda j, i: (0, j))],
        out_specs=pl.BlockSpec((tm, tn), lambda j, i: (i, j)),
        out_shape=jax.ShapeDtypeStruct((L, f), out_dtype),
        compiler_params=_params("parallel", "parallel"), name="conv_gate_fwd",
    )(u, u, w, b)


def _conv_gate_backward(u, w, b, dact, du_dtype):
    L, f2 = u.shape
    f = f2 // 2
    tm, tn = _conv_tiles(L, f)
    rb = tm // SUBLANES
    n_row_blocks = L // SUBLANES
    n_i = L // tm
    ext = tm + SUBLANES

    def next_rows(i):
        return jnp.minimum((i + 1) * rb, n_row_blocks - 1)

    def body(u_ref, up_ref, un_ref, da_ref, dan_ref, w_ref, b_ref, du_ref, dwb_ref):
        i = pl.program_id(1)
        is_last = i == n_i - 1
        prev = jnp.where(i > 0, up_ref[...], 0.0)
        xe = jnp.concatenate([u_ref[...], jnp.where(is_last, 0.0, un_ref[...])], axis=0)
        x1 = _shift_rows(xe, prev, 1)
        x2 = _shift_rows(xe, prev, 2)
        wv = w_ref[...]
        hc = b_ref[...] + ((wv[0:1] * x2 + wv[1:2] * x1) + wv[2:3] * xe)
        g, up = hc[:, :tn], hc[:, tn:]
        da = jnp.concatenate([da_ref[...], jnp.where(is_last, 0.0, dan_ref[...])], axis=0)
        t = jnp.tanh(GELU_C0 * (g + GELU_C1 * g * g * g))
        gelu = 0.5 * g * (1.0 + t)
        dgelu = 0.5 * (1.0 + t) + 0.5 * g * (1.0 - t * t) * (GELU_C0 * (1.0 + 3.0 * GELU_C1 * g * g))
        dh = jnp.concatenate([da * up * dgelu, da * gelu], axis=1)
        dh1 = pltpu.roll(dh, ext - 1, 0)
        dh2 = pltpu.roll(dh, ext - 2, 0)
        du_ref[...] = ((wv[2:3] * dh + wv[1:2] * dh1) + wv[0:1] * dh2)[:tm].astype(du_ref.dtype)
        dw0 = jnp.sum((dh * x2)[:tm], axis=0, keepdims=True)
        dw1 = jnp.sum((dh * x1)[:tm], axis=0, keepdims=True)
        dw2 = jnp.sum((dh * xe)[:tm], axis=0, keepdims=True)
        db = jnp.sum(dh[:tm], axis=0, keepdims=True)
        row = lax.broadcasted_iota(jnp.int32, (SUBLANES, 2 * tn), 0)
        upd = jnp.where(row == 0, dw0, jnp.where(row == 1, dw1, jnp.where(row == 2, dw2,
                        jnp.where(row == 3, db, 0.0))))

        @pl.when(i == 0)
        def _():
            dwb_ref[...] = jnp.zeros_like(dwb_ref)

        dwb_ref[...] += upd

    return pl.pallas_call(
        body, grid=(f // tn, n_i),
        in_specs=[pl.BlockSpec((tm, 2 * tn), lambda j, i: (i, j)),
                  pl.BlockSpec((SUBLANES, 2 * tn), lambda j, i: (jnp.maximum(i * rb - 1, 0), j)),
                  pl.BlockSpec((SUBLANES, 2 * tn), lambda j, i: (next_rows(i), j)),
                  pl.BlockSpec((tm, tn), lambda j, i: (i, j)),
                  pl.BlockSpec((SUBLANES, tn), lambda j, i: (next_rows(i), j)),
                  pl.BlockSpec((3, 2 * tn), lambda j, i: (0, j)),
                  pl.BlockSpec((1, 2 * tn), lambda j, i: (0, j))],
        out_specs=[pl.BlockSpec((tm, 2 * tn), lambda j, i: (i, j)),
                   pl.BlockSpec((SUBLANES, 2 * tn), lambda j, i: (0, j))],
        out_shape=[jax.ShapeDtypeStruct((L, f2), du_dtype), jax.ShapeDtypeStruct((SUBLANES, f2), F32)],
        compiler_params=_params("parallel", "arbitrary"), name="conv_gate_bwd",
    )(u, u, u, dact, dact, w, b)


@jax.custom_vjp
def conv_ffn(h, g, w_up, w_conv, b_conv, w_down):
    return _conv_ffn_fwd(h, g, w_up, w_conv, b_conv, w_down)[0]


def _conv_ffn_fwd(h, g, w_up, w_conv, b_conv, w_down):
    y = _rms_forward(h, g, BF16)
    u = _matmul(y, w_up, "nn", F32, "linear_fwd")
    act = _conv_gate_forward(u, w_conv, b_conv.reshape(1, -1), BF16)
    return (h, _matmul(act, w_down, "nn", F32, "linear_fwd")), (h, g, w_up, w_conv, b_conv, w_down, y, u, act)


def _conv_ffn_bwd(res, cts):
    h, g, w_up, w_conv, b_conv, w_down, y, u, act = res
    dh_other, df = cts
    dact = _matmul(df, w_down, "nt", F32, "linear_dx")
    dw_down = _matmul(act, df, "tn", w_down.dtype, "linear_dw")
    du, dwb = _conv_gate_backward(u, w_conv, b_conv.reshape(1, -1), dact, BF16)
    dy = _matmul(du, w_up, "nt", F32, "linear_dx")
    dw_up = _matmul(y, du, "tn", w_up.dtype, "linear_dw")
    dh, dg = _rms_backward(h, g, dy, dh_other)
    return dh, dg, dw_up, dwb[0:3], dwb[3], dw_down


conv_ffn.defvjp(_conv_ffn_fwd, _conv_ffn_bwd)


def _loss_rows(y, target):
    rows, d = y.shape
    tr = _row_tile(rows, d)

    def body(y_ref, t_ref, loss_ref, dy_ref):
        err = y_ref[...] - t_ref[...]
        loss_ref[...] = 0.5 * jnp.mean(err * err, axis=-1, keepdims=True)
        dy_ref[...] = err * (1.0 / d)

    return pl.pallas_call(
        body, grid=(rows // tr,),
        in_specs=[pl.BlockSpec((tr, d), lambda i: (i, 0)), pl.BlockSpec((tr, d), lambda i: (i, 0))],
        out_specs=[pl.BlockSpec((tr, 1), lambda i: (i, 0)), pl.BlockSpec((tr, d), lambda i: (i, 0))],
        out_shape=[jax.ShapeDtypeStruct((rows, 1), F32), jax.ShapeDtypeStruct((rows, d), F32)],
        compiler_params=_params("parallel"), name="loss_head",
    )(y, target)


@jax.custom_vjp
def token_loss(y, target):
    return jnp.sum(_loss_rows(y, target)[0])


def _token_loss_fwd(y, target):
    rows, dy = _loss_rows(y, target)
    return jnp.sum(rows), dy


def _token_loss_bwd(dy, ct):
    return ct * dy, -ct * dy


token_loss.defvjp(_token_loss_fwd, _token_loss_bwd)


def _cols_from_devices(g):
    k = g.shape[1]
    return jnp.transpose(g, (1, 0, 2)).reshape(k, -1)


@functools.partial(jax.custom_vjp, nondiff_argnums=(1,))
def _interleave_gate_up(a, f):
    tn = _pick(f, _TILE_FF)
    parts = []
    for j in range(f // tn):
        parts += [a[..., j * tn:(j + 1) * tn], a[..., f + j * tn:f + (j + 1) * tn]]
    return jnp.concatenate(parts, axis=-1)


def _interleave_fwd(a, f):
    return _interleave_gate_up(a, f), None


def _interleave_bwd(f, _, ct):
    tn = _pick(f, _TILE_FF)
    gate = [ct[..., 2 * j * tn:(2 * j + 1) * tn] for j in range(f // tn)]
    up = [ct[..., (2 * j + 1) * tn:(2 * j + 2) * tn] for j in range(f // tn)]
    return (jnp.concatenate(gate + up, axis=-1),)


_interleave_gate_up.defvjp(_interleave_fwd, _interleave_bwd)


def _rope(x, cos, sin):
    half = x.shape[-1] // 2
    x1, x2 = x[..., :half], x[..., half:]
    return jnp.concatenate([x1 * cos - x2 * sin, x2 * cos + x1 * sin], axis=-1)


PROJ_BOUNDS = ((0, 512), (512, 1024), (1024, 2048), (2048, 3072), (3072, 4096), (4096, 5120), (5120, 5184),
               (5184, 5192))
Q_BOUNDS = ((0, HEADS * HEAD_DIM), (HEADS * HEAD_DIM, HEADS * (HEAD_DIM + ROPE_DIM)))
KV_BOUNDS = ((0, HEADS * HEAD_DIM), (HEADS * HEAD_DIM, 2 * HEADS * HEAD_DIM))


def _layer(h, big, small, conv_w, l, l_real, cos, sin):
    L, d = h.shape
    w_in = _cols_from_devices(big["w_in"])
    w_in = jnp.concatenate([w_in[:, :1024], w_in[:, 1088:5184], w_in[:, 1024:1088], w_in[:, 5184:],
                            jnp.zeros((d, IN_COLS_PADDED - IN_COLS), w_in.dtype)], axis=1)
    w_q_up = _cols_from_devices(big["w_q_up"]).reshape(MLA_Q_LORA, HEADS, HEAD_DIM + ROPE_DIM)
    w_q_up = jnp.concatenate([w_q_up[:, :, :HEAD_DIM].reshape(MLA_Q_LORA, -1),
                              w_q_up[:, :, HEAD_DIM:].reshape(MLA_Q_LORA, -1)], axis=1)
    w_kv_up = _cols_from_devices(big["w_kv_up"]).reshape(MLA_KV_LORA, HEADS, 2 * HEAD_DIM)
    w_kv_up = jnp.concatenate([w_kv_up[:, :, :HEAD_DIM].reshape(MLA_KV_LORA, -1),
                               w_kv_up[:, :, HEAD_DIM:].reshape(MLA_KV_LORA, -1)], axis=1)
    w_out = big["w_out"].reshape(-1, d)
    f = big["w_ffn_down"].shape[0] * big["w_ffn_down"].shape[1]
    w_ffn_up = _interleave_gate_up(_cols_from_devices(big["w_ffn_up"]), f)
    w_ffn_down = big["w_ffn_down"].reshape(f, d)
    w_conv = _interleave_gate_up(conv_w, f)
    b_conv = _interleave_gate_up(small["b_ffn_conv"][l], f)

    h, c_q, c_kv, fq, fk, fv, fg, k_rope, ff = norm_linear_split(h, small["ln_mix_pre"][l], w_in, PROJ_BOUNDS)
    _, qn, qr = norm_linear_split(c_q, small["g_q_latent"][l], w_q_up, Q_BOUNDS)
    _, kn, v = norm_linear_split(c_kv, small["g_kv_latent"][l], w_kv_up, KV_BOUNDS)
    qr = jnp.transpose(_rope(qr.reshape(L, HEADS, ROPE_DIM), cos[:, None, :], sin[:, None, :]), (1, 0, 2))
    kr = _rope(k_rope, cos, sin)
    a = mla_attention(l_real, qn, qr, kn, kr, v)

    fqn = rms_norm(fq, small["g_fox_q"][l])
    fkn = rms_norm(fk, small["g_fox_k"][l])
    log_f = jax.nn.log_sigmoid(ff + small["b_forget"][l])
    c = jnp.cumsum(log_f, axis=0).T[:, :, None]
    bmix = fox_attention(l_real, fqn, fkn, fv, c) * jax.nn.sigmoid(fg)

    mix = linear_pair(a, bmix, w_out)
    h = add_norm(h, mix, small["ln_mix_post"][l])

    h, f_out = conv_ffn(h, small["ln_ffn_pre"][l], w_ffn_up, w_conv, b_conv, w_ffn_down)
    h = add_norm(h, f_out, small["ln_ffn_post"][l])
    return h


def _local_loss(big, small, meta, conv_w, x, target):
    s, d = x.shape
    l_real = N_META + s
    l_pad = -(-l_real // Q_BLOCK) * Q_BLOCK
    h = jnp.concatenate([meta, x, jnp.zeros((l_pad - l_real, d), F32)], axis=0)
    half = ROPE_DIM // 2
    inv_freq = ROPE_THETA ** (-jnp.arange(half, dtype=F32) / half)
    ang = jnp.arange(l_pad, dtype=jnp.int32).astype(F32)[:, None] * inv_freq[None, :]
    cos, sin = jnp.cos(ang), jnp.sin(ang)
    for l in range(DEPTH):
        h = _layer(h, big[l], small, conv_w[l], l, l_real, cos, sin)
    return token_loss(h[N_META:l_real], target)


ANY_SPACE = pl.BlockSpec(memory_space=pl.ANY)


def _place():
    ix, iy, ic = lax.axis_index("x"), lax.axis_index("y"), lax.axis_index("c")
    return ix, iy, ic, [(1 - ix, iy), (ix, 1 - iy), (1 - ix, 1 - iy)]


def _comm_call(body, arrays, out_shapes, n_remote, n_local, name):
    return pl.pallas_call(
        body, out_shape=out_shapes, in_specs=[ANY_SPACE] * len(arrays), out_specs=[ANY_SPACE] * len(out_shapes),
        scratch_shapes=[pltpu.SemaphoreType.DMA((n_remote,)), pltpu.SemaphoreType.DMA((n_remote,)),
                        pltpu.SemaphoreType.DMA((n_local,))],
        name=name,
    )(*arrays)


def _gather(arrays, name):
    n = len(arrays)

    def body(*refs):
        xs, outs = refs[:n], refs[n:2 * n]
        send_sems, recv_sems, local_sems = refs[2 * n:]
        ix, iy, ic, chips = _place()
        me, sibling = (ix, iy, ic), (ix, iy, 1 - ic)

        def copy(a, k, block, to, src=None):
            dst = outs[a].at[4 * block[0] + 2 * block[1] + block[2]]
            return pltpu.make_async_remote_copy(
                src_ref=dst if src is None else src, dst_ref=dst, send_sem=send_sems.at[7 * a + k],
                recv_sem=recv_sems.at[7 * a + k], device_id=to, device_id_type=MESH_ID)

        local, sent = [], []
        for a in range(n):
            mine = pltpu.make_async_copy(xs[a], outs[a].at[4 * ix + 2 * iy + ic], local_sems.at[a])
            mine.start()
            local.append(mine)
            first = [copy(a, 0, me, sibling, src=xs[a])]
            first += [copy(a, 1 + j, me, (*chip, ic), src=xs[a]) for j, chip in enumerate(chips)]
            for cp in first:
                cp.start()
            sent += first
        for a in range(n):
            for j, chip in enumerate(chips):
                copy(a, 1 + j, (*chip, ic), me).wait_recv()
                passed = copy(a, 4 + j, (*chip, ic), sibling)
                passed.start()
                sent.append(passed)
        for a in range(n):
            copy(a, 0, sibling, me).wait_recv()
            for j, chip in enumerate(chips):
                copy(a, 4 + j, (*chip, 1 - ic), me).wait_recv()
        for cp in sent:
            cp.wait_send()
        for cp in local:
            cp.wait()

    out_shapes = [jax.ShapeDtypeStruct((N_DEV,) + a.shape, a.dtype) for a in arrays]
    return _comm_call(body, arrays, out_shapes, 7 * n, n, name)


def _swap_with_sibling(arrays, name):
    n = len(arrays)

    def body(*refs):
        xs, outs = refs[:n], refs[n:2 * n]
        send_sems, recv_sems, _ = refs[2 * n:]
        ix, iy, ic, _ = _place()
        copies = [pltpu.make_async_remote_copy(
            src_ref=xs[a], dst_ref=outs[a], send_sem=send_sems.at[a], recv_sem=recv_sems.at[a],
            device_id=(ix, iy, 1 - ic), device_id_type=MESH_ID) for a in range(n)]
        for cp in copies:
            cp.start()
        for cp in copies:
            cp.wait()

    out_shapes = [jax.ShapeDtypeStruct(a.shape, a.dtype) for a in arrays]
    return _comm_call(body, arrays, out_shapes, n, 1, name)


def _exchange_chips(arrays, name):
    n = len(arrays)

    def body(*refs):
        xs, outs = refs[:n], refs[n:2 * n]
        send_sems, recv_sems, local_sems = refs[2 * n:]
        ix, iy, ic, chips = _place()
        my_chip = 2 * ix + iy
        local, sent = [], []
        for a in range(n):
            mine = pltpu.make_async_copy(xs[a].at[my_chip], outs[a].at[my_chip], local_sems.at[a])
            mine.start()
            local.append(mine)
            for j, chip in enumerate(chips):
                cp = pltpu.make_async_remote_copy(
                    src_ref=xs[a].at[2 * chip[0] + chip[1]], dst_ref=outs[a].at[my_chip],
                    send_sem=send_sems.at[3 * a + j], recv_sem=recv_sems.at[3 * a + j],
                    device_id=(*chip, ic), device_id_type=MESH_ID)
                cp.start()
                sent.append(cp)
        for a in range(n):
            for j, chip in enumerate(chips):
                pltpu.make_async_remote_copy(
                    src_ref=xs[a].at[my_chip], dst_ref=outs[a].at[2 * chip[0] + chip[1]],
                    send_sem=send_sems.at[3 * a + j], recv_sem=recv_sems.at[3 * a + j],
                    device_id=(*chip, ic), device_id_type=MESH_ID).wait_recv()
        for cp in sent:
            cp.wait_send()
        for cp in local:
            cp.wait()

    out_shapes = [jax.ShapeDtypeStruct(a.shape, a.dtype) for a in arrays]
    return _comm_call(body, arrays, out_shapes, 3 * n, n, name)


def _sum_slots(x, out_dtype, name):
    slots, rows, cols = x.shape
    tr = _row_tile(rows, cols, (2 << 20) // slots, 16)

    def body(x_ref, o_ref):
        acc = x_ref[0].astype(F32)
        for s in range(1, slots):
            acc = acc + x_ref[s].astype(F32)
        o_ref[...] = acc.astype(o_ref.dtype)

    return pl.pallas_call(
        body, grid=(rows // tr,), in_specs=[pl.BlockSpec((slots, tr, cols), lambda i: (0, i, 0))],
        out_specs=pl.BlockSpec((tr, cols), lambda i: (i, 0)), out_shape=jax.ShapeDtypeStruct((rows, cols), out_dtype),
        compiler_params=_params("parallel"), name=name,
    )(x)


def _add_pairs(a, b, name):
    slots, rows, cols = a.shape
    tr = _row_tile(rows, cols, 1 << 20, 16)

    def body(a_ref, b_ref, o_ref):
        o_ref[...] = (a_ref[...].astype(F32) + b_ref[...].astype(F32)).astype(o_ref.dtype)

    spec = pl.BlockSpec((None, tr, cols), lambda s, i: (s, i, 0))
    return pl.pallas_call(
        body, grid=(slots, rows // tr), in_specs=[spec, spec], out_specs=spec,
        out_shape=jax.ShapeDtypeStruct(a.shape, BF16), compiler_params=_params("parallel", "parallel"), name=name,
    )(a, b)


def _reduce_scatter(grads, ic):
    by_chip = [g.reshape((4, 2) + g.shape[1:]) for g in grads]
    keep = [lax.dynamic_index_in_dim(g, ic, axis=1, keepdims=False) for g in by_chip]
    give = [lax.dynamic_index_in_dim(g, 1 - ic, axis=1, keepdims=False) for g in by_chip]
    got = _swap_with_sibling(give, "scatter_sibling")
    pairs = [_add_pairs(k, g, "add_pairs") for k, g in zip(keep, got)]
    received = _exchange_chips(pairs, "scatter_chips")
    return [_sum_slots(r, F32, "sum_grads") for r in received]


def _pack(arrays, dtype, row_multiple):
    flat = jnp.concatenate([a.astype(dtype).reshape(-1) for a in arrays])
    n = flat.shape[0]
    quantum = row_multiple * FLAT_COLS
    padded = -(-n // quantum) * quantum
    return jnp.pad(flat, (0, padded - n)).reshape(padded // FLAT_COLS, FLAT_COLS)


def _unpack(buf, shapes):
    flat = buf.reshape(-1)
    out, off = [], 0
    for shp in shapes:
        n = 1
        for s in shp:
            n *= s
        out.append(flat[off:off + n].reshape(tuple(shp)))
        off += n
    return out


def _adamw(w, g, m, v, name):
    shape = w.shape
    cols = shape[-1]
    w2, g2, m2, v2 = (a.reshape(-1, cols) for a in (w, g, m, v))
    rows = w2.shape[0]
    tr = _row_tile(rows, cols, 1 << 20)

    def body(w_ref, g_ref, m_ref, v_ref, d_ref, nm_ref, nv_ref):
        gv = g_ref[...]
        nm = ADAM_B1 * m_ref[...] + (1.0 - ADAM_B1) * gv
        nv = ADAM_B2 * v_ref[...] + (1.0 - ADAM_B2) * (gv * gv)
        m_hat = nm / (1.0 - ADAM_B1 ** ADAM_STEP)
        v_hat = nv / (1.0 - ADAM_B2 ** ADAM_STEP)
        d_ref[...] = -ADAM_LR * (m_hat / (jnp.sqrt(v_hat) + ADAM_EPS) + ADAM_WD * w_ref[...])
        nm_ref[...] = nm
        nv_ref[...] = nv

    spec = pl.BlockSpec((tr, cols), lambda i: (i, 0))
    outs = pl.pallas_call(
        body, grid=(rows // tr,), in_specs=[spec] * 4, out_specs=[spec] * 3,
        out_shape=[jax.ShapeDtypeStruct((rows, cols), F32)] * 3,
        compiler_params=_params("parallel"), name=name,
    )(w2, g2, m2, v2)
    return tuple(o.reshape(shape) for o in outs)


BIG = ("w_in", "w_q_up", "w_kv_up", "w_out", "w_ffn_up", "w_ffn_down")
REPLICATED = ("ln_mix_pre", "b_forget", "g_q_latent", "g_kv_latent", "g_fox_q", "g_fox_k", "ln_mix_post",
              "ln_ffn_pre", "b_ffn_conv", "ln_ffn_post")
WEIGHTS = ("meta_tokens", "ln_mix_pre", "w_in", "b_forget", "g_q_latent", "g_kv_latent", "w_q_up", "w_kv_up",
           "g_fox_q", "g_fox_k", "w_out", "ln_mix_post", "ln_ffn_pre", "w_ffn_up", "w_ffn_conv", "b_ffn_conv",
           "w_ffn_down", "ln_ffn_post")


def kernel(x, meta_tokens, ln_mix_pre, w_in, b_forget, g_q_latent, g_kv_latent, w_q_up, w_kv_up, g_fox_q, g_fox_k, w_out, ln_mix_post, ln_ffn_pre, w_ffn_up, w_ffn_conv, b_ffn_conv, w_ffn_down, ln_ffn_post, loss_target, m_meta_tokens, m_ln_mix_pre, m_w_in, m_b_forget, m_g_q_latent, m_g_kv_latent, m_w_q_up, m_w_kv_up, m_g_fox_q, m_g_fox_k, m_w_out, m_ln_mix_post, m_ln_ffn_pre, m_w_ffn_up, m_w_ffn_conv, m_b_ffn_conv, m_w_ffn_down, m_ln_ffn_post, v_meta_tokens, v_ln_mix_pre, v_w_in, v_b_forget, v_g_q_latent, v_g_kv_latent, v_w_q_up, v_w_kv_up, v_g_fox_q, v_g_fox_k, v_w_out, v_ln_mix_post, v_ln_ffn_pre, v_w_ffn_up, v_w_ffn_conv, v_b_ffn_conv, v_w_ffn_down, v_ln_ffn_post):
    w = dict(meta_tokens=meta_tokens, ln_mix_pre=ln_mix_pre, w_in=w_in, b_forget=b_forget, g_q_latent=g_q_latent,
             g_kv_latent=g_kv_latent, w_q_up=w_q_up, w_kv_up=w_kv_up, g_fox_q=g_fox_q, g_fox_k=g_fox_k, w_out=w_out,
             ln_mix_post=ln_mix_post, ln_ffn_pre=ln_ffn_pre, w_ffn_up=w_ffn_up, w_ffn_conv=w_ffn_conv,
             b_ffn_conv=b_ffn_conv, w_ffn_down=w_ffn_down, ln_ffn_post=ln_ffn_post)
    mom = dict(meta_tokens=m_meta_tokens, ln_mix_pre=m_ln_mix_pre, w_in=m_w_in, b_forget=m_b_forget,
               g_q_latent=m_g_q_latent, g_kv_latent=m_g_kv_latent, w_q_up=m_w_q_up, w_kv_up=m_w_kv_up,
               g_fox_q=m_g_fox_q, g_fox_k=m_g_fox_k, w_out=m_w_out, ln_mix_post=m_ln_mix_post,
               ln_ffn_pre=m_ln_ffn_pre, w_ffn_up=m_w_ffn_up, w_ffn_conv=m_w_ffn_conv, b_ffn_conv=m_b_ffn_conv,
               w_ffn_down=m_w_ffn_down, ln_ffn_post=m_ln_ffn_post)
    var = dict(meta_tokens=v_meta_tokens, ln_mix_pre=v_ln_mix_pre, w_in=v_w_in, b_forget=v_b_forget,
               g_q_latent=v_g_q_latent, g_kv_latent=v_g_kv_latent, w_q_up=v_w_q_up, w_kv_up=v_w_kv_up,
               g_fox_q=v_g_fox_q, g_fox_k=v_g_fox_k, w_out=v_w_out, ln_mix_post=v_ln_mix_post,
               ln_ffn_pre=v_ln_ffn_pre, w_ffn_up=v_w_ffn_up, w_ffn_conv=v_w_ffn_conv, b_ffn_conv=v_b_ffn_conv,
               w_ffn_down=v_w_ffn_down, ln_ffn_post=v_ln_ffn_post)
    ic = lax.axis_index("c")
    me = 4 * lax.axis_index("x") + 2 * lax.axis_index("y") + ic

    gathered = _gather([w[n].astype(BF16) for n in BIG] + [meta_tokens, w_ffn_conv], "gather_weights")
    big = [{n: gathered[k][:, l] for k, n in enumerate(BIG)} for l in range(DEPTH)]
    meta_shape, conv_shape = meta_tokens.shape, w_ffn_conv.shape
    meta_full = _cols_from_devices(gathered[len(BIG)])
    conv_full = jnp.transpose(gathered[len(BIG) + 1], (1, 2, 0, 3)).reshape(DEPTH, conv_shape[1], -1)
    small = {n: w[n] for n in REPLICATED}

    loss, grads = jax.value_and_grad(_local_loss, argnums=(0, 1, 2, 3, 4))(
        big, small, meta_full, [conv_full[l] for l in range(DEPTH)], x[0], loss_target[0])
    g_big, g_small, g_meta, g_conv, g_x = grads
    loss = lax.psum(loss, ("x", "y", "c"))

    grad = {}
    per_layer = [_reduce_scatter([g_big[l][n] for n in BIG], ic) for l in range(DEPTH)]
    for k, n in enumerate(BIG):
        grad[n] = jnp.stack([per_layer[l][k] for l in range(DEPTH)])

    small_arrays = [g_small[n] for n in REPLICATED] + [g_meta, jnp.stack(g_conv)]
    small_shapes = [a.shape for a in small_arrays]
    partials = _gather([_pack(small_arrays, F32, 16)], "gather_small_grads")[0]
    summed = _unpack(_sum_slots(partials, F32, "sum_small_grads"), small_shapes)
    for n, g in zip(REPLICATED, summed):
        grad[n] = g
    grad["meta_tokens"] = lax.dynamic_slice_in_dim(summed[-2], me * meta_shape[1], meta_shape[1], axis=1)
    grad["w_ffn_conv"] = lax.dynamic_slice_in_dim(summed[-1], me * conv_shape[2], conv_shape[2], axis=2)

    delta, new_m, new_v = {}, {}, {}
    for n in BIG:
        delta[n], new_m[n], new_v[n] = _adamw(w[n], grad[n], mom[n], var[n], "adamw_" + n)
    rest = [n for n in WEIGHTS if n not in BIG]
    rest_shapes = [w[n].shape for n in rest]
    flat = [_pack([src[n] for n in rest], F32, SUBLANES) for src in (w, grad, mom, var)]
    outs = _adamw(*flat, "adamw_small")
    for dst, buf in zip((delta, new_m, new_v), outs):
        for n, a in zip(rest, _unpack(buf, rest_shapes)):
            dst[n] = a

    return (loss, g_x[None], *[grad[n] for n in WEIGHTS], *[delta[n] for n in WEIGHTS],
            *[new_m[n] for n in WEIGHTS], *[new_v[n] for n in WEIGHTS])
```

```python
import functools

import jax
import jax.numpy as jnp
from jax import lax
from jax.experimental import pallas as pl
from jax.experimental.pallas import tpu as pltpu

F32 = jnp.float32
BF16 = jnp.bfloat16
MESH_ID = pl.DeviceIdType.MESH

N_DEV = 8
DEPTH = 4
N_META = 16
CHUNK = 64
Q_BLOCK = 128
HEADS = 8
HEAD_DIM = 128
ROPE_DIM = 64
MLA_Q_LORA = 512
MLA_KV_LORA = 512
FOX_W = HEADS * HEAD_DIM
ROPE_THETA = 10000.0
EPS = 1e-6
NEG = -1e30
IN_COLS = 5192
IN_COLS_PADDED = 5376

ADAM_LR = 0.001
ADAM_B1 = 0.9
ADAM_B2 = 0.999
ADAM_EPS = 1e-08
ADAM_WD = 0.01
ADAM_STEP = 10

LANES = 128
SUBLANES = 8
FLAT_COLS = 1024
VMEM_LIMIT_V7X = 52 * 1024 * 1024

NT_DIMS = (((1,), (1,)), ((), ()))
NN_DIMS = (((1,), (0,)), ((), ()))
TN_DIMS = (((0,), (0,)), ((), ()))

_TILE_ATT = (640, 512, 384, 256, 128)
_TILE_FF = (512, 256, 128)


def _pick(n, candidates):
    for c in candidates:
        if n % c == 0:
            return c
    return n


def _row_tile(rows, cols, budget_bytes=2 << 20, align=SUBLANES):
    best = None
    for t in range(align, rows + 1, align):
        if rows % t == 0 and t * cols * 4 <= budget_bytes:
            best = t
    return best if best is not None else rows


def _params(*semantics):
    return pltpu.CompilerParams(dimension_semantics=semantics, vmem_limit_bytes=VMEM_LIMIT_V7X)


MATMUL_VMEM_BUDGET = 36 << 20
MXU_FLOPS_V7X = 9.0e14
HBM_BYTES_PER_S_V7X = 2.5e12
GRID_STEP_S = 0.35e-6
MXU_DIM = 256


def _tile_candidates(n, cap):
    c = [t for t in range(LANES, min(n, cap) + 1, LANES) if n % t == 0]
    return c if c else [n]


def _matmul_tiles(m, n, c, a_bytes, b_bytes, o_bytes):
    best, best_cost = None, None
    for tc in _tile_candidates(c, 4096):
        steps = c // tc
        for tm in _tile_candidates(m, 2048):
            for tn in _tile_candidates(n, 2048):
                vmem = 2 * (tm * tc * a_bytes + tc * tn * b_bytes + tm * tn * o_bytes)
                vmem += tm * tn * 4 if steps > 1 else 0
                if vmem > MATMUL_VMEM_BUDGET:
                    continue
                traffic = m * c * a_bytes * (1 if steps == 1 else n // tn) + c * n * b_bytes * (m // tm)
                traffic += m * n * o_bytes
                grid = (m // tm) * (n // tn) * steps
                accumulate = 0 if steps == 1 else grid * tm * tn * 8 / 4.0e12
                fill = (-(-tn // MXU_DIM) * MXU_DIM / tn) * (-(-tc // MXU_DIM) * MXU_DIM / tc)
                cost = max(2.0 * m * n * c * fill / MXU_FLOPS_V7X, traffic / HBM_BYTES_PER_S_V7X)
                cost += grid * GRID_STEP_S + accumulate
                if best_cost is None or cost < best_cost:
                    best, best_cost = (tm, tn, tc), cost
    return best


def _matmul(a, b, mode, out_dtype, name):
    if mode == "nn":
        (m, c), (c2, n) = a.shape, b.shape
    elif mode == "nt":
        (m, c), (n, c2) = a.shape, b.shape
    else:
        (c, m), (c2, n) = a.shape, b.shape
    assert c == c2, (a.shape, b.shape, mode)
    tm, tn, tc = _matmul_tiles(m, n, c, a.dtype.itemsize, b.dtype.itemsize, jnp.dtype(out_dtype).itemsize)
    steps = c // tc
    if mode == "nn":
        a_spec = pl.BlockSpec((tm, tc), lambda i, j, k: (i, k))
        b_spec = pl.BlockSpec((tc, tn), lambda i, j, k: (k, j))
        dims = NN_DIMS
    elif mode == "nt":
        a_spec = pl.BlockSpec((tm, tc), lambda i, j, k: (i, k))
        b_spec = pl.BlockSpec((tn, tc), lambda i, j, k: (j, k))
        dims = NT_DIMS
    else:
        a_spec = pl.BlockSpec((tc, tm), lambda i, j, k: (k, i))
        b_spec = pl.BlockSpec((tc, tn), lambda i, j, k: (k, j))
        dims = TN_DIMS

    def body(a_ref, b_ref, o_ref, acc_ref):
        k = pl.program_id(2)

        @pl.when(k == 0)
        def _():
            acc_ref[...] = jnp.zeros_like(acc_ref)

        acc_ref[...] += lax.dot_general(a_ref[...].astype(BF16), b_ref[...].astype(BF16), dims,
                                        preferred_element_type=F32)

        @pl.when(k == steps - 1)
        def _():
            o_ref[...] = acc_ref[...].astype(o_ref.dtype)

    def body_whole(a_ref, b_ref, o_ref):
        o_ref[...] = lax.dot_general(a_ref[...].astype(BF16), b_ref[...].astype(BF16), dims,
                                     preferred_element_type=F32).astype(o_ref.dtype)

    return pl.pallas_call(
        body if steps > 1 else body_whole, grid=(m // tm, n // tn, steps), in_specs=[a_spec, b_spec],
        out_specs=pl.BlockSpec((tm, tn), lambda i, j, k: (i, j)),
        out_shape=jax.ShapeDtypeStruct((m, n), out_dtype),
        scratch_shapes=[pltpu.VMEM((tm, tn), F32)] if steps > 1 else [],
        compiler_params=_params("parallel", "parallel", "arbitrary"), name=name,
    )(a, b)


@jax.custom_vjp
def linear(x, w):
    return _matmul(x, w, "nn", F32, "linear_fwd")


def _linear_fwd(x, w):
    return _matmul(x, w, "nn", F32, "linear_fwd"), (x, w)


def _linear_bwd(res, dy):
    x, w = res
    dx = _matmul(dy, w, "nt", F32, "linear_dx")
    dw = _matmul(x, dy, "tn", w.dtype, "linear_dw")
    return dx, dw


linear.defvjp(_linear_fwd, _linear_bwd)


def _matmul_pair(a, b, wa, wb, name):
    m, ka = a.shape
    kb = b.shape[1]
    n = wa.shape[1]
    tm, tn, tc = _matmul_tiles(m, n, ka + kb, a.dtype.itemsize, wa.dtype.itemsize, 4)
    assert tc == ka + kb, (tc, ka, kb)

    def body(a_ref, b_ref, wa_ref, wb_ref, o_ref):
        o_ref[...] = (
            lax.dot_general(a_ref[...].astype(BF16), wa_ref[...].astype(BF16), NN_DIMS, preferred_element_type=F32)
            + lax.dot_general(b_ref[...].astype(BF16), wb_ref[...].astype(BF16), NN_DIMS, preferred_element_type=F32))

    return pl.pallas_call(
        body, grid=(m // tm, n // tn),
        in_specs=[pl.BlockSpec((tm, ka), lambda i, j: (i, 0)), pl.BlockSpec((tm, kb), lambda i, j: (i, 0)),
                  pl.BlockSpec((ka, tn), lambda i, j: (0, j)), pl.BlockSpec((kb, tn), lambda i, j: (0, j))],
        out_specs=pl.BlockSpec((tm, tn), lambda i, j: (i, j)), out_shape=jax.ShapeDtypeStruct((m, n), F32),
        compiler_params=_params("parallel", "parallel"), name=name,
    )(a, b, wa, wb)


@jax.custom_vjp
def linear_pair(a, b, w):
    ka = a.shape[1]
    return _matmul_pair(a, b, w[:ka], w[ka:], "linear_pair_fwd")


def _linear_pair_fwd(a, b, w):
    return linear_pair(a, b, w), (a, b, w)


def _linear_pair_bwd(res, dz):
    a, b, w = res
    ka = a.shape[1]
    da = _matmul(dz, w[:ka], "nt", F32, "linear_dx")
    db = _matmul(dz, w[ka:], "nt", F32, "linear_dx")
    dw = jnp.concatenate([_matmul(a, dz, "tn", w.dtype, "linear_dw"), _matmul(b, dz, "tn", w.dtype, "linear_dw")],
                         axis=0)
    return da, db, dw


linear_pair.defvjp(_linear_pair_fwd, _linear_pair_bwd)


def _rms_forward(x, g, out_dtype=F32, residual=None):
    rows, d = x.shape
    gd = g.shape[0]
    tr = _row_tile(rows, d, align=16)
    tile = pl.BlockSpec((tr, d), lambda i: (i, 0))

    def body(x_ref, g_ref, *rest):
        y_ref = rest[-1]
        for c0 in range(0, d, gd):
            xv = x_ref[:, c0:c0 + gd]
            r = lax.rsqrt(jnp.mean(xv * xv, axis=-1, keepdims=True) + EPS)
            y = (xv * r) * g_ref[...]
            if residual is not None:
                y = rest[0][:, c0:c0 + gd] + y
            y_ref[:, c0:c0 + gd] = y.astype(y_ref.dtype)

    extra = [] if residual is None else [residual]
    return pl.pallas_call(
        body, grid=(rows // tr,),
        in_specs=[tile, pl.BlockSpec((1, gd), lambda i: (0, 0))] + [tile] * len(extra),
        out_specs=tile, out_shape=jax.ShapeDtypeStruct((rows, d), out_dtype),
        compiler_params=_params("parallel"), name="rmsnorm_fwd",
    )(x, g.reshape(1, gd), *extra)


def _rms_backward(x, g, dy, residual=None):
    rows, d = x.shape
    gd = g.shape[0]
    tr = _row_tile(rows, d)
    tile = pl.BlockSpec((tr, d), lambda i: (i, 0))

    def body(x_ref, g_ref, dy_ref, *rest):
        dx_ref, dg_ref = rest[-2:]
        i = pl.program_id(0)

        @pl.when(i == 0)
        def _():
            dg_ref[...] = jnp.zeros_like(dg_ref)

        for c0 in range(0, d, gd):
            xv = x_ref[:, c0:c0 + gd]
            dyv = dy_ref[:, c0:c0 + gd]
            r = lax.rsqrt(jnp.mean(xv * xv, axis=-1, keepdims=True) + EPS)
            xh = xv * r
            t = dyv * g_ref[...]
            dx = r * (t - xh * jnp.mean(t * xh, axis=-1, keepdims=True))
            if residual is not None:
                dx = rest[0][:, c0:c0 + gd] + dx
            dx_ref[:, c0:c0 + gd] = dx
            dg_ref[...] += jnp.sum(dyv * xh, axis=0, keepdims=True)

    extra = [] if residual is None else [residual]
    dx, dg = pl.pallas_call(
        body, grid=(rows // tr,),
        in_specs=[tile, pl.BlockSpec((1, gd), lambda i: (0, 0)), tile] + [tile] * len(extra),
        out_specs=[tile, pl.BlockSpec((1, gd), lambda i: (0, 0))],
        out_shape=[jax.ShapeDtypeStruct((rows, d), F32), jax.ShapeDtypeStruct((1, gd), F32)],
        compiler_params=_params("arbitrary"), name="rmsnorm_bwd",
    )(x, g.reshape(1, gd), dy, *extra)
    return dx, dg.reshape(g.shape)


@jax.custom_vjp
def rms_norm(x, g):
    return _rms_forward(x, g)


def _rms_norm_fwd(x, g):
    return _rms_forward(x, g), (x, g)


def _rms_norm_bwd(res, dy):
    x, g = res
    return _rms_backward(x, g, dy)


rms_norm.defvjp(_rms_norm_fwd, _rms_norm_bwd)


@jax.custom_vjp
def add_norm(h, x, g):
    return _rms_forward(x, g, F32, h)


def _add_norm_fwd(h, x, g):
    return _rms_forward(x, g, F32, h), (x, g)


def _add_norm_bwd(res, dy):
    x, g = res
    dx, dg = _rms_backward(x, g, dy)
    return dy, dx, dg


add_norm.defvjp(_add_norm_fwd, _add_norm_bwd)


@functools.partial(jax.custom_vjp, nondiff_argnums=(3,))
def norm_linear_split(x, g, w, bounds):
    z = _matmul(_rms_forward(x, g, BF16), w, "nn", F32, "linear_fwd")
    return (x,) + tuple(z[:, lo:hi] for lo, hi in bounds)


def _norm_linear_split_fwd(x, g, w, bounds):
    y = _rms_forward(x, g, BF16)
    z = _matmul(y, w, "nn", F32, "linear_fwd")
    return (x,) + tuple(z[:, lo:hi] for lo, hi in bounds), (x, g, w, y)


def _norm_linear_split_bwd(bounds, res, cts):
    x, g, w, y = res
    dx_other, parts = cts[0], [p.astype(BF16) for p in cts[1:]]
    tail = w.shape[1] - bounds[-1][1]
    if tail:
        parts.append(jnp.zeros((x.shape[0], tail), BF16))
    dz = jnp.concatenate(parts, axis=1)
    dy = _matmul(dz, w, "nt", F32, "linear_dx")
    dw = _matmul(y, dz, "tn", w.dtype, "linear_dw")
    dx, dg = _rms_backward(x, g, dy, dx_other)
    return dx, dg, dw


norm_linear_split.defvjp(_norm_linear_split_fwd, _norm_linear_split_bwd)


def _visibility_id(pos, kind, l_real):
    if kind == "fox":
        return pos
    pad_chunk = 2 + (l_real - N_META) // CHUNK
    frame_chunk = 1 + jnp.right_shift(pos - N_META, 6)
    return jnp.where(pos < N_META, 0, jnp.where(pos < l_real, frame_chunk, pad_chunk))


def _raw_scores(kind, a1, b1, a2, b2):
    s = lax.dot_general(a1.astype(BF16), b1.astype(BF16), NT_DIMS, preferred_element_type=F32)
    if kind == "mla":
        s = s + lax.dot_general(a2.astype(BF16), b2.astype(BF16), NT_DIMS, preferred_element_type=F32)
    return s


def _block_pairs(nb, look, by_key):
    outer, inner = [], []
    for a in range(nb):
        rng = range(max(a - look, 0), nb) if by_key else range(0, min(a + look, nb - 1) + 1)
        for b in rng:
            outer.append(a)
            inner.append(b)
    return jnp.asarray(outer, jnp.int32), jnp.asarray(inner, jnp.int32)


def _attention_call(body, n_pairs, in_specs, out_specs, out_shape, scratch, name, tables, operands):
    return pl.pallas_call(
        body,
        grid_spec=pltpu.PrefetchScalarGridSpec(num_scalar_prefetch=2, grid=(HEADS, n_pairs), in_specs=in_specs,
                                               out_specs=out_specs, scratch_shapes=scratch),
        out_shape=out_shape, compiler_params=_params("parallel", "arbitrary"), name=name,
    )(*tables, *operands)


def _attention_forward(kind, l_real, q1, k1, v, extras):
    L = q1.shape[0]
    T = _pick(L, _TILE_ATT)
    nb = L // T
    look = 1 if kind == "mla" else 0
    scale = (HEAD_DIM + ROPE_DIM) ** -0.5 if kind == "mla" else HEAD_DIM ** -0.5
    tables = _block_pairs(nb, look, False)

    q_tile = pl.BlockSpec((T, HEAD_DIM), lambda h, t, it, jt: (it[t], h))
    k_tile = pl.BlockSpec((T, HEAD_DIM), lambda h, t, it, jt: (jt[t], h))
    row_stat = pl.BlockSpec((None, T, 1), lambda h, t, it, jt: (h, it[t], 0))
    if kind == "mla":
        extra_specs = [pl.BlockSpec((None, T, ROPE_DIM), lambda h, t, it, jt: (h, it[t], 0)),
                       pl.BlockSpec((T, ROPE_DIM), lambda h, t, it, jt: (jt[t], 0))]
    else:
        extra_specs = [pl.BlockSpec((None, 1, T), lambda h, t, it, jt: (h, 0, jt[t]))]

    def body(it_ref, jt_ref, q1_ref, k1_ref, v_ref, *rest):
        if kind == "mla":
            eq_ref, ek_ref, o_ref, lse_ref, m_ref, l_ref, acc_ref = rest
        else:
            ek_ref, o_ref, lse_ref, m_ref, l_ref, acc_ref = rest
        t = pl.program_id(1)
        i, j = it_ref[t], jt_ref[t]

        @pl.when(j == 0)
        def _():
            m_ref[...] = jnp.full_like(m_ref, NEG)
            l_ref[...] = jnp.zeros_like(l_ref)
            acc_ref[...] = jnp.zeros_like(acc_ref)

        def block(masked):
            if kind == "mla":
                s = _raw_scores(kind, q1_ref[...] * scale, k1_ref[...], eq_ref[...] * scale, ek_ref[...])
            else:
                s = _raw_scores(kind, q1_ref[...] * scale, k1_ref[...], None, None) - ek_ref[...]
            if masked:
                pos_q = i * T + lax.broadcasted_iota(jnp.int32, (T, 1), 0)
                pos_k = j * T + lax.broadcasted_iota(jnp.int32, (1, T), 1)
                s = jnp.where(_visibility_id(pos_k, kind, l_real) <= _visibility_id(pos_q, kind, l_real), s, NEG)
            m_prev = m_ref[...]
            m_new = jnp.maximum(m_prev, jnp.max(s, axis=1, keepdims=True))
            alpha = jnp.exp(m_prev - m_new)
            p = jnp.exp(s - m_new)
            l_ref[...] = alpha * l_ref[...] + jnp.sum(p, axis=1, keepdims=True)
            m_ref[...] = m_new
            vb = v_ref[...].astype(BF16)
            p_hi = p.astype(BF16)
            pv = lax.dot_general(p_hi, vb, NN_DIMS, preferred_element_type=F32)
            if kind == "fox":
                p_lo = (p - p_hi.astype(F32)).astype(BF16)
                pv = pv + lax.dot_general(p_lo, vb, NN_DIMS, preferred_element_type=F32)
            acc_ref[...] = alpha * acc_ref[...] + pv

        @pl.when(j < i)
        def _():
            block(False)

        @pl.when(j >= i)
        def _():
            block(True)

        @pl.when(j == jnp.minimum(i + look, nb - 1))
        def _():
            o_ref[...] = acc_ref[...] / l_ref[...]
            lse_ref[...] = m_ref[...] + jnp.log(l_ref[...])

    return _attention_call(
        body, tables[0].shape[0], [q_tile, k_tile, k_tile] + extra_specs, [q_tile, row_stat],
        [jax.ShapeDtypeStruct((L, HEADS * HEAD_DIM), F32), jax.ShapeDtypeStruct((HEADS, L, 1), F32)],
        [pltpu.VMEM((T, 1), F32), pltpu.VMEM((T, 1), F32), pltpu.VMEM((T, HEAD_DIM), F32)],
        kind + "_attn_fwd", tables, (q1, k1, v, *extras))


def _attention_delta(o, do):
    L = o.shape[0]
    T = _pick(L, _TILE_ATT)
    tile = pl.BlockSpec((T, HEADS * HEAD_DIM), lambda i: (i, 0))

    def body(o_ref, do_ref, delta_ref):
        for h in range(HEADS):
            cols = slice(h * HEAD_DIM, (h + 1) * HEAD_DIM)
            delta_ref[h] = jnp.sum(do_ref[:, cols].astype(BF16).astype(F32) * o_ref[:, cols], axis=1, keepdims=True)

    return pl.pallas_call(
        body, grid=(L // T,), in_specs=[tile, tile],
        out_specs=pl.BlockSpec((HEADS, T, 1), lambda i: (0, i, 0)),
        out_shape=jax.ShapeDtypeStruct((HEADS, L, 1), F32),
        compiler_params=_params("parallel"), name="attn_delta",
    )(o, do)


def _attention_backward(kind, l_real, q1, k1, v, extras, do, lse_row, delta_row):
    L = q1.shape[0]
    T = _pick(L, _TILE_ATT)
    nb = L // T
    look = 1 if kind == "mla" else 0
    scale = (HEAD_DIM + ROPE_DIM) ** -0.5 if kind == "mla" else HEAD_DIM ** -0.5
    tables = _block_pairs(nb, look, True)

    k_tile = pl.BlockSpec((T, HEAD_DIM), lambda h, t, jt, it: (jt[t], h))
    q_tile = pl.BlockSpec((T, HEAD_DIM), lambda h, t, jt, it: (it[t], h))
    q_row = pl.BlockSpec((None, 1, T), lambda h, t, jt, it: (h, 0, it[t]))
    if kind == "mla":
        extra_specs = [pl.BlockSpec((None, T, ROPE_DIM), lambda h, t, jt, it: (h, it[t], 0)),
                       pl.BlockSpec((T, ROPE_DIM), lambda h, t, jt, it: (jt[t], 0))]
        third_spec = pl.BlockSpec((None, T, ROPE_DIM), lambda h, t, jt, it: (h, jt[t], 0))
        third_shape = jax.ShapeDtypeStruct((HEADS, L, ROPE_DIM), F32)
        third_scratch = pltpu.VMEM((T, ROPE_DIM), F32)
    else:
        extra_specs = [pl.BlockSpec((None, T, 1), lambda h, t, jt, it: (h, jt[t], 0))]
        third_spec = pl.BlockSpec((None, T, 1), lambda h, t, jt, it: (h, jt[t], 0))
        third_shape = jax.ShapeDtypeStruct((HEADS, L, 1), F32)
        third_scratch = pltpu.VMEM((T, 1), F32)
    in_specs = [q_tile, k_tile, k_tile] + extra_specs + [q_tile, q_row, q_row]
    n_pairs = tables[0].shape[0]
    out_specs = [k_tile, k_tile, third_spec, pl.BlockSpec((L, HEAD_DIM), lambda h, t, jt, it: (0, h))]
    out_shape = [jax.ShapeDtypeStruct((L, HEADS * HEAD_DIM), F32), jax.ShapeDtypeStruct((L, HEADS * HEAD_DIM), F32),
                 third_shape, jax.ShapeDtypeStruct((L, HEADS * HEAD_DIM), F32)]
    if kind == "mla":
        out_specs.append(pl.BlockSpec((None, L, ROPE_DIM), lambda h, t, jt, it: (h, 0, 0)))
        out_shape.append(jax.ShapeDtypeStruct((HEADS, L, ROPE_DIM), F32))

    def body(jt_ref, it_ref, q1_ref, k1_ref, v_ref, *rest):
        if kind == "mla":
            (eq_ref, ek_ref, do_ref, lse_ref, delta_ref, dk1_ref, dv_ref, third_ref, dq1_ref, dq2_ref,
             acck_ref, accv_ref, acc3_ref) = rest
        else:
            (ek_ref, do_ref, lse_ref, delta_ref, dk1_ref, dv_ref, third_ref, dq1_ref,
             acck_ref, accv_ref, acc3_ref) = rest
        t = pl.program_id(1)
        j, i = jt_ref[t], it_ref[t]
        q_rows = pl.ds(pl.multiple_of(i * T, T), T)

        @pl.when(t == 0)
        def _():
            dq1_ref[...] = jnp.zeros_like(dq1_ref)
            if kind == "mla":
                dq2_ref[...] = jnp.zeros_like(dq2_ref)

        @pl.when(i == jnp.maximum(j - look, 0))
        def _():
            acck_ref[...] = jnp.zeros_like(acck_ref)
            accv_ref[...] = jnp.zeros_like(accv_ref)
            acc3_ref[...] = jnp.zeros_like(acc3_ref)

        def block(masked):
            if kind == "mla":
                kb = jnp.concatenate([k1_ref[...].astype(BF16), ek_ref[...].astype(BF16)], axis=1)
                qb = jnp.concatenate([(q1_ref[...] * scale).astype(BF16), (eq_ref[...] * scale).astype(BF16)],
                                     axis=1)
            else:
                kb, qb = k1_ref[...].astype(BF16), (q1_ref[...] * scale).astype(BF16)
            st = lax.dot_general(kb, qb, NT_DIMS, preferred_element_type=F32)
            if kind == "fox":
                st = st - ek_ref[...]
            if masked:
                pos_k = j * T + lax.broadcasted_iota(jnp.int32, (T, 1), 0)
                pos_q = i * T + lax.broadcasted_iota(jnp.int32, (1, T), 1)
                st = jnp.where(_visibility_id(pos_k, kind, l_real) <= _visibility_id(pos_q, kind, l_real), st, NEG)
            pt = jnp.exp(st - lse_ref[...])
            dob = do_ref[...].astype(BF16)
            accv_ref[...] += lax.dot_general(pt.astype(BF16), dob, NN_DIMS, preferred_element_type=F32)
            dpt = lax.dot_general(v_ref[...].astype(BF16), dob, NT_DIMS, preferred_element_type=F32)
            dst = pt * (dpt - delta_ref[...])
            dsb = dst.astype(BF16)
            dk = lax.dot_general(dsb, qb, NN_DIMS, preferred_element_type=F32)
            dq = lax.dot_general(dsb, kb, TN_DIMS, preferred_element_type=F32)
            if kind == "mla":
                acck_ref[...] += dk[:, :HEAD_DIM]
                acc3_ref[...] += dk[:, HEAD_DIM:]
                dq1_ref[q_rows, :] += dq[:, :HEAD_DIM]
                dq2_ref[q_rows, :] += dq[:, HEAD_DIM:]
            else:
                acck_ref[...] += dk
                dq1_ref[q_rows, :] += dq
                acc3_ref[...] -= jnp.sum(dst, axis=1, keepdims=True)

        @pl.when(i > j)
        def _():
            block(False)

        @pl.when(i <= j)
        def _():
            block(True)

        @pl.when(i == nb - 1)
        def _():
            dk1_ref[...] = acck_ref[...]
            dv_ref[...] = accv_ref[...]
            third_ref[...] = acc3_ref[...]

        @pl.when(t == n_pairs - 1)
        def _():
            dq1_ref[...] = dq1_ref[...] * scale
            if kind == "mla":
                dq2_ref[...] = dq2_ref[...] * scale

    return _attention_call(
        body, n_pairs, in_specs, out_specs, out_shape,
        [pltpu.VMEM((T, HEAD_DIM), F32), pltpu.VMEM((T, HEAD_DIM), F32), third_scratch],
        kind + "_attn_bwd", tables, (q1, k1, v, *extras, do, lse_row, delta_row))


def _as_row(col):
    return col.reshape(col.shape[0], 1, col.shape[1])


@functools.partial(jax.custom_vjp, nondiff_argnums=(0,))
def mla_attention(l_real, qn, qr, kn, kr, v):
    return _attention_forward("mla", l_real, qn, kn, v, (qr, kr))[0]


def _mla_attention_fwd(l_real, qn, qr, kn, kr, v):
    o, lse = _attention_forward("mla", l_real, qn, kn, v, (qr, kr))
    return o, (qn, qr, kn, kr, v, o, lse)


def _mla_attention_bwd(l_real, res, do):
    qn, qr, kn, kr, v, o, lse = res
    delta = _attention_delta(o, do)
    dkn, dv, dkr_heads, dqn, dqr = _attention_backward("mla", l_real, qn, kn, v, (qr, kr), do, _as_row(lse),
                                                       _as_row(delta))
    return dqn, dqr, dkn, jnp.sum(dkr_heads, axis=0), dv


mla_attention.defvjp(_mla_attention_fwd, _mla_attention_bwd)


@functools.partial(jax.custom_vjp, nondiff_argnums=(0,))
def fox_attention(l_real, q, k, v, c):
    return _attention_forward("fox", l_real, q, k, v, (_as_row(c),))[0]


def _fox_attention_fwd(l_real, q, k, v, c):
    o, lse = _attention_forward("fox", l_real, q, k, v, (_as_row(c),))
    return o, (q, k, v, c, o, lse)


def _fox_attention_bwd(l_real, res, do):
    q, k, v, c, o, lse = res
    delta = _attention_delta(o, do)
    dk, dv, dc, dq = _attention_backward("fox", l_real, q, k, v, (c,), do, _as_row(lse), _as_row(delta))
    return dq, dk, dv, dc


fox_attention.defvjp(_fox_attention_fwd, _fox_attention_bwd)


GELU_C0 = 0.7978845608028654
GELU_C1 = 0.044715


def _shift_rows(x, prev, s):
    r = pltpu.roll(x, s, 0)
    pr = pltpu.roll(prev, s, 0)
    row = lax.broadcasted_iota(jnp.int32, prev.shape, 0)
    top = jnp.where(row < s, pr, r[0:SUBLANES])
    return jnp.concatenate([top, r[SUBLANES:]], axis=0)


def _conv_tiles(L, f):
    return _pick(L, _TILE_ATT), _pick(f, _TILE_FF)


def _conv_gate_forward(u, w, b, out_dtype):
    L, f2 = u.shape
    f = f2 // 2
    tm, tn = _conv_tiles(L, f)
    rb = tm // SUBLANES

    def body(u_ref, up_ref, w_ref, b_ref, o_ref):
        i = pl.program_id(1)
        x = u_ref[...]
        prev = jnp.where(i > 0, up_ref[...], 0.0)
        wv = w_ref[...]
        hc = b_ref[...] + ((wv[0:1] * _shift_rows(x, prev, 2) + wv[1:2] * _shift_rows(x, prev, 1)) + wv[2:3] * x)
        g = hc[:, :tn]
        gelu = 0.5 * g * (1.0 + jnp.tanh(GELU_C0 * (g + GELU_C1 * g * g * g)))
        o_ref[...] = (gelu * hc[:, tn:]).astype(o_ref.dtype)

    return pl.pallas_call(
        body, grid=(f // tn, L // tm),
        in_specs=[pl.BlockSpec((tm, 2 * tn), lambda j, i: (i, j)),
                  pl.BlockSpec((SUBLANES, 2 * tn), lambda j, i: (jnp.maximum(i * rb - 1, 0), j)),
                  pl.BlockSpec((3, 2 * tn), lambda j, i: (0, j)),
                  pl.BlockSpec((1, 2 * tn), lambda j, i: (0, j))],
        out_specs=pl.BlockSpec((tm, tn), lambda j, i: (i, j)),
        out_shape=jax.ShapeDtypeStruct((L, f), out_dtype),
        compiler_params=_params("parallel", "parallel"), name="conv_gate_fwd",
    )(u, u, w, b)


def _conv_gate_backward(u, w, b, dact, du_dtype):
    L, f2 = u.shape
    f = f2 // 2
    tm, tn = _conv_tiles(L, f)
    rb = tm // SUBLANES
    n_row_blocks = L // SUBLANES
    n_i = L // tm
    ext = tm + SUBLANES

    def next_rows(i):
        return jnp.minimum((i + 1) * rb, n_row_blocks - 1)

    def body(u_ref, up_ref, un_ref, da_ref, dan_ref, w_ref, b_ref, du_ref, dwb_ref):
        i = pl.program_id(1)
        is_last = i == n_i - 1
        prev = jnp.where(i > 0, up_ref[...], 0.0)
        xe = jnp.concatenate([u_ref[...], jnp.where(is_last, 0.0, un_ref[...])], axis=0)
        x1 = _shift_rows(xe, prev, 1)
        x2 = _shift_rows(xe, prev, 2)
        wv = w_ref[...]
        hc = b_ref[...] + ((wv[0:1] * x2 + wv[1:2] * x1) + wv[2:3] * xe)
        g, up = hc[:, :tn], hc[:, tn:]
        da = jnp.concatenate([da_ref[...], jnp.where(is_last, 0.0, dan_ref[...])], axis=0)
        t = jnp.tanh(GELU_C0 * (g + GELU_C1 * g * g * g))
        gelu = 0.5 * g * (1.0 + t)
        dgelu = 0.5 * (1.0 + t) + 0.5 * g * (1.0 - t * t) * (GELU_C0 * (1.0 + 3.0 * GELU_C1 * g * g))
        dh = jnp.concatenate([da * up * dgelu, da * gelu], axis=1)
        dh1 = pltpu.roll(dh, ext - 1, 0)
        dh2 = pltpu.roll(dh, ext - 2, 0)
        du_ref[...] = ((wv[2:3] * dh + wv[1:2] * dh1) + wv[0:1] * dh2)[:tm].astype(du_ref.dtype)
        dw0 = jnp.sum((dh * x2)[:tm], axis=0, keepdims=True)
        dw1 = jnp.sum((dh * x1)[:tm], axis=0, keepdims=True)
        dw2 = jnp.sum((dh * xe)[:tm], axis=0, keepdims=True)
        db = jnp.sum(dh[:tm], axis=0, keepdims=True)
        row = lax.broadcasted_iota(jnp.int32, (SUBLANES, 2 * tn), 0)
        upd = jnp.where(row == 0, dw0, jnp.where(row == 1, dw1, jnp.where(row == 2, dw2,
                        jnp.where(row == 3, db, 0.0))))

        @pl.when(i == 0)
        def _():
            dwb_ref[...] = jnp.zeros_like(dwb_ref)

        dwb_ref[...] += upd

    return pl.pallas_call(
        body, grid=(f // tn, n_i),
        in_specs=[pl.BlockSpec((tm, 2 * tn), lambda j, i: (i, j)),
                  pl.BlockSpec((SUBLANES, 2 * tn), lambda j, i: (jnp.maximum(i * rb - 1, 0), j)),
                  pl.BlockSpec((SUBLANES, 2 * tn), lambda j, i: (next_rows(i), j)),
                  pl.BlockSpec((tm, tn), lambda j, i: (i, j)),
                  pl.BlockSpec((SUBLANES, tn), lambda j, i: (next_rows(i), j)),
                  pl.BlockSpec((3, 2 * tn), lambda j, i: (0, j)),
                  pl.BlockSpec((1, 2 * tn), lambda j, i: (0, j))],
        out_specs=[pl.BlockSpec((tm, 2 * tn), lambda j, i: (i, j)),
                   pl.BlockSpec((SUBLANES, 2 * tn), lambda j, i: (0, j))],
        out_shape=[jax.ShapeDtypeStruct((L, f2), du_dtype), jax.ShapeDtypeStruct((SUBLANES, f2), F32)],
        compiler_params=_params("parallel", "arbitrary"), name="conv_gate_bwd",
    )(u, u, u, dact, dact, w, b)


@jax.custom_vjp
def conv_ffn(h, g, w_up, w_conv, b_conv, w_down):
    return _conv_ffn_fwd(h, g, w_up, w_conv, b_conv, w_down)[0]


def _conv_ffn_fwd(h, g, w_up, w_conv, b_conv, w_down):
    y = _rms_forward(h, g, BF16)
    u = _matmul(y, w_up, "nn", F32, "linear_fwd")
    act = _conv_gate_forward(u, w_conv, b_conv.reshape(1, -1), BF16)
    return (h, _matmul(act, w_down, "nn", F32, "linear_fwd")), (h, g, w_up, w_conv, b_conv, w_down, y, u, act)


def _conv_ffn_bwd(res, cts):
    h, g, w_up, w_conv, b_conv, w_down, y, u, act = res
    dh_other, df = cts
    dact = _matmul(df, w_down, "nt", F32, "linear_dx")
    dw_down = _matmul(act, df, "tn", w_down.dtype, "linear_dw")
    du, dwb = _conv_gate_backward(u, w_conv, b_conv.reshape(1, -1), dact, BF16)
    dy = _matmul(du, w_up, "nt", F32, "linear_dx")
    dw_up = _matmul(y, du, "tn", w_up.dtype, "linear_dw")
    dh, dg = _rms_backward(h, g, dy, dh_other)
    return dh, dg, dw_up, dwb[0:3], dwb[3], dw_down


conv_ffn.defvjp(_conv_ffn_fwd, _conv_ffn_bwd)


def _loss_rows(y, target):
    rows, d = y.shape
    tr = _row_tile(rows, d)

    def body(y_ref, t_ref, loss_ref, dy_ref):
        err = y_ref[...] - t_ref[...]
        loss_ref[...] = 0.5 * jnp.mean(err * err, axis=-1, keepdims=True)
        dy_ref[...] = err * (1.0 / d)

    return pl.pallas_call(
        body, grid=(rows // tr,),
        in_specs=[pl.BlockSpec((tr, d), lambda i: (i, 0)), pl.BlockSpec((tr, d), lambda i: (i, 0))],
        out_specs=[pl.BlockSpec((tr, 1), lambda i: (i, 0)), pl.BlockSpec((tr, d), lambda i: (i, 0))],
        out_shape=[jax.ShapeDtypeStruct((rows, 1), F32), jax.ShapeDtypeStruct((rows, d), F32)],
        compiler_params=_params("parallel"), name="loss_head",
    )(y, target)


@jax.custom_vjp
def token_loss(y, target):
    return jnp.sum(_loss_rows(y, target)[0])


def _token_loss_fwd(y, target):
    rows, dy = _loss_rows(y, target)
    return jnp.sum(rows), dy


def _token_loss_bwd(dy, ct):
    return ct * dy, -ct * dy


token_loss.defvjp(_token_loss_fwd, _token_loss_bwd)


def _cols_from_devices(g):
    k = g.shape[1]
    return jnp.transpose(g, (1, 0, 2)).reshape(k, -1)


@functools.partial(jax.custom_vjp, nondiff_argnums=(1,))
def _interleave_gate_up(a, f):
    tn = _pick(f, _TILE_FF)
    parts = []
    for j in range(f // tn):
        parts += [a[..., j * tn:(j + 1) * tn], a[..., f + j * tn:f + (j + 1) * tn]]
    return jnp.concatenate(parts, axis=-1)


def _interleave_fwd(a, f):
    return _interleave_gate_up(a, f), None


def _interleave_bwd(f, _, ct):
    tn = _pick(f, _TILE_FF)
    gate = [ct[..., 2 * j * tn:(2 * j + 1) * tn] for j in range(f // tn)]
    up = [ct[..., (2 * j + 1) * tn:(2 * j + 2) * tn] for j in range(f // tn)]
    return (jnp.concatenate(gate + up, axis=-1),)


_interleave_gate_up.defvjp(_interleave_fwd, _interleave_bwd)


def _rope(x, cos, sin):
    half = x.shape[-1] // 2
    x1, x2 = x[..., :half], x[..., half:]
    return jnp.concatenate([x1 * cos - x2 * sin, x2 * cos + x1 * sin], axis=-1)


PROJ_BOUNDS = ((0, 512), (512, 1024), (1024, 2048), (2048, 3072), (3072, 4096), (4096, 5120), (5120, 5184),
               (5184, 5192))
Q_BOUNDS = ((0, HEADS * HEAD_DIM), (HEADS * HEAD_DIM, HEADS * (HEAD_DIM + ROPE_DIM)))
KV_BOUNDS = ((0, HEADS * HEAD_DIM), (HEADS * HEAD_DIM, 2 * HEADS * HEAD_DIM))


def _layer(h, big, small, conv_w, l, l_real, cos, sin):
    L, d = h.shape
    w_in = _cols_from_devices(big["w_in"])
    w_in = jnp.concatenate([w_in[:, :1024], w_in[:, 1088:5184], w_in[:, 1024:1088], w_in[:, 5184:],
                            jnp.zeros((d, IN_COLS_PADDED - IN_COLS), w_in.dtype)], axis=1)
    w_q_up = _cols_from_devices(big["w_q_up"]).reshape(MLA_Q_LORA, HEADS, HEAD_DIM + ROPE_DIM)
    w_q_up = jnp.concatenate([w_q_up[:, :, :HEAD_DIM].reshape(MLA_Q_LORA, -1),
                              w_q_up[:, :, HEAD_DIM:].reshape(MLA_Q_LORA, -1)], axis=1)
    w_kv_up = _cols_from_devices(big["w_kv_up"]).reshape(MLA_KV_LORA, HEADS, 2 * HEAD_DIM)
    w_kv_up = jnp.concatenate([w_kv_up[:, :, :HEAD_DIM].reshape(MLA_KV_LORA, -1),
                               w_kv_up[:, :, HEAD_DIM:].reshape(MLA_KV_LORA, -1)], axis=1)
    w_out = big["w_out"].reshape(-1, d)
    f = big["w_ffn_down"].shape[0] * big["w_ffn_down"].shape[1]
    w_ffn_up = _interleave_gate_up(_cols_from_devices(big["w_ffn_up"]), f)
    w_ffn_down = big["w_ffn_down"].reshape(f, d)
    w_conv = _interleave_gate_up(conv_w, f)
    b_conv = _interleave_gate_up(small["b_ffn_conv"][l], f)

    h, c_q, c_kv, fq, fk, fv, fg, k_rope, ff = norm_linear_split(h, small["ln_mix_pre"][l], w_in, PROJ_BOUNDS)
    _, qn, qr = norm_linear_split(c_q, small["g_q_latent"][l], w_q_up, Q_BOUNDS)
    _, kn, v = norm_linear_split(c_kv, small["g_kv_latent"][l], w_kv_up, KV_BOUNDS)
    qr = jnp.transpose(_rope(qr.reshape(L, HEADS, ROPE_DIM), cos[:, None, :], sin[:, None, :]), (1, 0, 2))
    kr = _rope(k_rope, cos, sin)
    a = mla_attention(l_real, qn, qr, kn, kr, v)

    fqn = rms_norm(fq, small["g_fox_q"][l])
    fkn = rms_norm(fk, small["g_fox_k"][l])
    log_f = jax.nn.log_sigmoid(ff + small["b_forget"][l])
    c = jnp.cumsum(log_f, axis=0).T[:, :, None]
    bmix = fox_attention(l_real, fqn, fkn, fv, c) * jax.nn.sigmoid(fg)

    mix = linear_pair(a, bmix, w_out)
    h = add_norm(h, mix, small["ln_mix_post"][l])

    h, f_out = conv_ffn(h, small["ln_ffn_pre"][l], w_ffn_up, w_conv, b_conv, w_ffn_down)
    h = add_norm(h, f_out, small["ln_ffn_post"][l])
    return h


def _local_loss(big, small, meta, conv_w, x, target):
    s, d = x.shape
    l_real = N_META + s
    l_pad = -(-l_real // Q_BLOCK) * Q_BLOCK
    h = jnp.concatenate([meta, x, jnp.zeros((l_pad - l_real, d), F32)], axis=0)
    half = ROPE_DIM // 2
    inv_freq = ROPE_THETA ** (-jnp.arange(half, dtype=F32) / half)
    ang = jnp.arange(l_pad, dtype=jnp.int32).astype(F32)[:, None] * inv_freq[None, :]
    cos, sin = jnp.cos(ang), jnp.sin(ang)
    for l in range(DEPTH):
        h = _layer(h, big[l], small, conv_w[l], l, l_real, cos, sin)
    return token_loss(h[N_META:l_real], target)


ANY_SPACE = pl.BlockSpec(memory_space=pl.ANY)


def _place():
    ix, iy, ic = lax.axis_index("x"), lax.axis_index("y"), lax.axis_index("c")
    return ix, iy, ic, [(1 - ix, iy), (ix, 1 - iy), (1 - ix, 1 - iy)]


def _comm_call(body, arrays, out_shapes, n_remote, n_local, name):
    return pl.pallas_call(
        body, out_shape=out_shapes, in_specs=[ANY_SPACE] * len(arrays), out_specs=[ANY_SPACE] * len(out_shapes),
        scratch_shapes=[pltpu.SemaphoreType.DMA((n_remote,)), pltpu.SemaphoreType.DMA((n_remote,)),
                        pltpu.SemaphoreType.DMA((n_local,))],
        name=name,
    )(*arrays)


def _gather(arrays, name):
    n = len(arrays)

    def body(*refs):
        xs, outs = refs[:n], refs[n:2 * n]
        send_sems, recv_sems, local_sems = refs[2 * n:]
        ix, iy, ic, chips = _place()
        me, sibling = (ix, iy, ic), (ix, iy, 1 - ic)

        def copy(a, k, block, to, src=None):
            dst = outs[a].at[4 * block[0] + 2 * block[1] + block[2]]
            return pltpu.make_async_remote_copy(
                src_ref=dst if src is None else src, dst_ref=dst, send_sem=send_sems.at[7 * a + k],
                recv_sem=recv_sems.at[7 * a + k], device_id=to, device_id_type=MESH_ID)

        local, sent = [], []
        for a in range(n):
            mine = pltpu.make_async_copy(xs[a], outs[a].at[4 * ix + 2 * iy + ic], local_sems.at[a])
            mine.start()
            local.append(mine)
            first = [copy(a, 0, me, sibling, src=xs[a])]
            first += [copy(a, 1 + j, me, (*chip, ic), src=xs[a]) for j, chip in enumerate(chips)]
            for cp in first:
                cp.start()
            sent += first
        for a in range(n):
            for j, chip in enumerate(chips):
                copy(a, 1 + j, (*chip, ic), me).wait_recv()
                passed = copy(a, 4 + j, (*chip, ic), sibling)
                passed.start()
                sent.append(passed)
        for a in range(n):
            copy(a, 0, sibling, me).wait_recv()
            for j, chip in enumerate(chips):
                copy(a, 4 + j, (*chip, 1 - ic), me).wait_recv()
        for cp in sent:
            cp.wait_send()
        for cp in local:
            cp.wait()

    out_shapes = [jax.ShapeDtypeStruct((N_DEV,) + a.shape, a.dtype) for a in arrays]
    return _comm_call(body, arrays, out_shapes, 7 * n, n, name)


def _swap_with_sibling(arrays, name):
    n = len(arrays)

    def body(*refs):
        xs, outs = refs[:n], refs[n:2 * n]
        send_sems, recv_sems, _ = refs[2 * n:]
        ix, iy, ic, _ = _place()
        copies = [pltpu.make_async_remote_copy(
            src_ref=xs[a], dst_ref=outs[a], send_sem=send_sems.at[a], recv_sem=recv_sems.at[a],
            device_id=(ix, iy, 1 - ic), device_id_type=MESH_ID) for a in range(n)]
        for cp in copies:
            cp.start()
        for cp in copies:
            cp.wait()

    out_shapes = [jax.ShapeDtypeStruct(a.shape, a.dtype) for a in arrays]
    return _comm_call(body, arrays, out_shapes, n, 1, name)


def _exchange_chips(arrays, name):
    n = len(arrays)

    def body(*refs):
        xs, outs = refs[:n], refs[n:2 * n]
        send_sems, recv_sems, local_sems = refs[2 * n:]
        ix, iy, ic, chips = _place()
        my_chip = 2 * ix + iy
        local, sent = [], []
        for a in range(n):
            mine = pltpu.make_async_copy(xs[a].at[my_chip], outs[a].at[my_chip], local_sems.at[a])
            mine.start()
            local.append(mine)
            for j, chip in enumerate(chips):
                cp = pltpu.make_async_remote_copy(
                    src_ref=xs[a].at[2 * chip[0] + chip[1]], dst_ref=outs[a].at[my_chip],
                    send_sem=send_sems.at[3 * a + j], recv_sem=recv_sems.at[3 * a + j],
                    device_id=(*chip, ic), device_id_type=MESH_ID)
                cp.start()
                sent.append(cp)
        for a in range(n):
            for j, chip in enumerate(chips):
                pltpu.make_async_remote_copy(
                    src_ref=xs[a].at[my_chip], dst_ref=outs[a].at[2 * chip[0] + chip[1]],
                    send_sem=send_sems.at[3 * a + j], recv_sem=recv_sems.at[3 * a + j],
                    device_id=(*chip, ic), device_id_type=MESH_ID).wait_recv()
        for cp in sent:
            cp.wait_send()
        for cp in local:
            cp.wait()

    out_shapes = [jax.ShapeDtypeStruct(a.shape, a.dtype) for a in arrays]
    return _comm_call(body, arrays, out_shapes, 3 * n, n, name)


def _sum_slots(x, out_dtype, name):
    slots, rows, cols = x.shape
    tr = _row_tile(rows, cols, (2 << 20) // slots, 16)

    def body(x_ref, o_ref):
        acc = x_ref[0].astype(F32)
        for s in range(1, slots):
            acc = acc + x_ref[s].astype(F32)
        o_ref[...] = acc.astype(o_ref.dtype)

    return pl.pallas_call(
        body, grid=(rows // tr,), in_specs=[pl.BlockSpec((slots, tr, cols), lambda i: (0, i, 0))],
        out_specs=pl.BlockSpec((tr, cols), lambda i: (i, 0)), out_shape=jax.ShapeDtypeStruct((rows, cols), out_dtype),
        compiler_params=_params("parallel"), name=name,
    )(x)


def _add_pairs(a, b, name):
    slots, rows, cols = a.shape
    tr = _row_tile(rows, cols, 1 << 20, 16)

    def body(a_ref, b_ref, o_ref):
        o_ref[...] = (a_ref[...].astype(F32) + b_ref[...].astype(F32)).astype(o_ref.dtype)

    spec = pl.BlockSpec((None, tr, cols), lambda s, i: (s, i, 0))
    return pl.pallas_call(
        body, grid=(slots, rows // tr), in_specs=[spec, spec], out_specs=spec,
        out_shape=jax.ShapeDtypeStruct(a.shape, BF16), compiler_params=_params("parallel", "parallel"), name=name,
    )(a, b)


def _reduce_scatter(grads, ic):
    by_chip = [g.reshape((4, 2) + g.shape[1:]) for g in grads]
    keep = [lax.dynamic_index_in_dim(g, ic, axis=1, keepdims=False) for g in by_chip]
    give = [lax.dynamic_index_in_dim(g, 1 - ic, axis=1, keepdims=False) for g in by_chip]
    got = _swap_with_sibling(give, "scatter_sibling")
    pairs = [_add_pairs(k, g, "add_pairs") for k, g in zip(keep, got)]
    received = _exchange_chips(pairs, "scatter_chips")
    return [_sum_slots(r, F32, "sum_grads") for r in received]


def _pack(arrays, dtype, row_multiple):
    flat = jnp.concatenate([a.astype(dtype).reshape(-1) for a in arrays])
    n = flat.shape[0]
    quantum = row_multiple * FLAT_COLS
    padded = -(-n // quantum) * quantum
    return jnp.pad(flat, (0, padded - n)).reshape(padded // FLAT_COLS, FLAT_COLS)


def _unpack(buf, shapes):
    flat = buf.reshape(-1)
    out, off = [], 0
    for shp in shapes:
        n = 1
        for s in shp:
            n *= s
        out.append(flat[off:off + n].reshape(tuple(shp)))
        off += n
    return out


def _adamw(w, g, m, v, name):
    shape = w.shape
    cols = shape[-1]
    w2, g2, m2, v2 = (a.reshape(-1, cols) for a in (w, g, m, v))
    rows = w2.shape[0]
    tr = _row_tile(rows, cols, 1 << 20)

    def body(w_ref, g_ref, m_ref, v_ref, d_ref, nm_ref, nv_ref):
        gv = g_ref[...]
        nm = ADAM_B1 * m_ref[...] + (1.0 - ADAM_B1) * gv
        nv = ADAM_B2 * v_ref[...] + (1.0 - ADAM_B2) * (gv * gv)
        m_hat = nm / (1.0 - ADAM_B1 ** ADAM_STEP)
        v_hat = nv / (1.0 - ADAM_B2 ** ADAM_STEP)
        d_ref[...] = -ADAM_LR * (m_hat / (jnp.sqrt(v_hat) + ADAM_EPS) + ADAM_WD * w_ref[...])
        nm_ref[...] = nm
        nv_ref[...] = nv

    spec = pl.BlockSpec((tr, cols), lambda i: (i, 0))
    outs = pl.pallas_call(
        body, grid=(rows // tr,), in_specs=[spec] * 4, out_specs=[spec] * 3,
        out_shape=[jax.ShapeDtypeStruct((rows, cols), F32)] * 3,
        compiler_params=_params("parallel"), name=name,
    )(w2, g2, m2, v2)
    return tuple(o.reshape(shape) for o in outs)


BIG = ("w_in", "w_q_up", "w_kv_up", "w_out", "w_ffn_up", "w_ffn_down")
REPLICATED = ("ln_mix_pre", "b_forget", "g_q_latent", "g_kv_latent", "g_fox_q", "g_fox_k", "ln_mix_post",
              "ln_ffn_pre", "b_ffn_conv", "ln_ffn_post")
WEIGHTS = ("meta_tokens", "ln_mix_pre", "w_in", "b_forget", "g_q_latent", "g_kv_latent", "w_q_up", "w_kv_up",
           "g_fox_q", "g_fox_k", "w_out", "ln_mix_post", "ln_ffn_pre", "w_ffn_up", "w_ffn_conv", "b_ffn_conv",
           "w_ffn_down", "ln_ffn_post")


def kernel(x, meta_tokens, ln_mix_pre, w_in, b_forget, g_q_latent, g_kv_latent, w_q_up, w_kv_up, g_fox_q, g_fox_k, w_out, ln_mix_post, ln_ffn_pre, w_ffn_up, w_ffn_conv, b_ffn_conv, w_ffn_down, ln_ffn_post, loss_target, m_meta_tokens, m_ln_mix_pre, m_w_in, m_b_forget, m_g_q_latent, m_g_kv_latent, m_w_q_up, m_w_kv_up, m_g_fox_q, m_g_fox_k, m_w_out, m_ln_mix_post, m_ln_ffn_pre, m_w_ffn_up, m_w_ffn_conv, m_b_ffn_conv, m_w_ffn_down, m_ln_ffn_post, v_meta_tokens, v_ln_mix_pre, v_w_in, v_b_forget, v_g_q_latent, v_g_kv_latent, v_w_q_up, v_w_kv_up, v_g_fox_q, v_g_fox_k, v_w_out, v_ln_mix_post, v_ln_ffn_pre, v_w_ffn_up, v_w_ffn_conv, v_b_ffn_conv, v_w_ffn_down, v_ln_ffn_post):
    w = dict(meta_tokens=meta_tokens, ln_mix_pre=ln_mix_pre, w_in=w_in, b_forget=b_forget, g_q_latent=g_q_latent,
             g_kv_latent=g_kv_latent, w_q_up=w_q_up, w_kv_up=w_kv_up, g_fox_q=g_fox_q, g_fox_k=g_fox_k, w_out=w_out,
             ln_mix_post=ln_mix_post, ln_ffn_pre=ln_ffn_pre, w_ffn_up=w_ffn_up, w_ffn_conv=w_ffn_conv,
             b_ffn_conv=b_ffn_conv, w_ffn_down=w_ffn_down, ln_ffn_post=ln_ffn_post)
    mom = dict(meta_tokens=m_meta_tokens, ln_mix_pre=m_ln_mix_pre, w_in=m_w_in, b_forget=m_b_forget,
               g_q_latent=m_g_q_latent, g_kv_latent=m_g_kv_latent, w_q_up=m_w_q_up, w_kv_up=m_w_kv_up,
               g_fox_q=m_g_fox_q, g_fox_k=m_g_fox_k, w_out=m_w_out, ln_mix_post=m_ln_mix_post,
               ln_ffn_pre=m_ln_ffn_pre, w_ffn_up=m_w_ffn_up, w_ffn_conv=m_w_ffn_conv, b_ffn_conv=m_b_ffn_conv,
               w_ffn_down=m_w_ffn_down, ln_ffn_post=m_ln_ffn_post)
    var = dict(meta_tokens=v_meta_tokens, ln_mix_pre=v_ln_mix_pre, w_in=v_w_in, b_forget=v_b_forget,
               g_q_latent=v_g_q_latent, g_kv_latent=v_g_kv_latent, w_q_up=v_w_q_up, w_kv_up=v_w_kv_up,
               g_fox_q=v_g_fox_q, g_fox_k=v_g_fox_k, w_out=v_w_out, ln_mix_post=v_ln_mix_post,
               ln_ffn_pre=v_ln_ffn_pre, w_ffn_up=v_w_ffn_up, w_ffn_conv=v_w_ffn_conv, b_ffn_conv=v_b_ffn_conv,
               w_ffn_down=v_w_ffn_down, ln_ffn_post=v_ln_ffn_post)
    ic = lax.axis_index("c")
    me = 4 * lax.axis_index("x") + 2 * lax.axis_index("y") + ic

    gathered = _gather([w[n].astype(BF16) for n in BIG] + [meta_tokens, w_ffn_conv], "gather_weights")
    big = [{n: gathered[k][:, l] for k, n in enumerate(BIG)} for l in range(DEPTH)]
    meta_shape, conv_shape = meta_tokens.shape, w_ffn_conv.shape
    meta_full = _cols_from_devices(gathered[len(BIG)])
    conv_full = jnp.transpose(gathered[len(BIG) + 1], (1, 2, 0, 3)).reshape(DEPTH, conv_shape[1], -1)
    small = {n: w[n] for n in REPLICATED}

    loss, grads = jax.value_and_grad(_local_loss, argnums=(0, 1, 2, 3, 4))(
        big, small, meta_full, [conv_full[l] for l in range(DEPTH)], x[0], loss_target[0])
    g_big, g_small, g_meta, g_conv, g_x = grads
    loss = lax.psum(loss, ("x", "y", "c"))

    grad = {}
    per_layer = [_reduce_scatter([g_big[l][n] for n in BIG], ic) for l in range(DEPTH)]
    for k, n in enumerate(BIG):
        grad[n] = jnp.stack([per_layer[l][k] for l in range(DEPTH)])

    small_arrays = [g_small[n] for n in REPLICATED] + [g_meta, jnp.stack(g_conv)]
    small_shapes = [a.shape for a in small_arrays]
    partials = _gather([_pack(small_arrays, F32, 16)], "gather_small_grads")[0]
    summed = _unpack(_sum_slots(partials, F32, "sum_small_grads"), small_shapes)
    for n, g in zip(REPLICATED, summed):
        grad[n] = g
    grad["meta_tokens"] = lax.dynamic_slice_in_dim(summed[-2], me * meta_shape[1], meta_shape[1], axis=1)
    grad["w_ffn_conv"] = lax.dynamic_slice_in_dim(summed[-1], me * conv_shape[2], conv_shape[2], axis=2)

    delta, new_m, new_v = {}, {}, {}
    for n in BIG:
        delta[n], new_m[n], new_v[n] = _adamw(w[n], grad[n], mom[n], var[n], "adamw_" + n)
    rest = [n for n in WEIGHTS if n not in BIG]
    rest_shapes = [w[n].shape for n in rest]
    flat = [_pack([src[n] for n in rest], F32, SUBLANES) for src in (w, grad, mom, var)]
    outs = _adamw(*flat, "adamw_small")
    for dst, buf in zip((delta, new_m, new_v), outs):
        for n, a in zip(rest, _unpack(buf, rest_shapes)):
            dst[n] = a

    return (loss, g_x[None], *[grad[n] for n in WEIGHTS], *[delta[n] for n in WEIGHTS],
            *[new_m[n] for n in WEIGHTS], *[new_v[n] for n in WEIGHTS])
```

```python
import functools

import jax
import jax.numpy as jnp
from jax import lax
from jax.experimental import pallas as pl
from jax.experimental.pallas import tpu as pltpu

F32 = jnp.float32
BF16 = jnp.bfloat16
MESH_ID = pl.DeviceIdType.MESH

N_DEV = 8
DEPTH = 4
N_META = 16
CHUNK = 64
Q_BLOCK = 128
HEADS = 8
HEAD_DIM = 128
ROPE_DIM = 64
MLA_Q_LORA = 512
MLA_KV_LORA = 512
FOX_W = HEADS * HEAD_DIM
ROPE_THETA = 10000.0
EPS = 1e-6
NEG = -1e30
IN_COLS = 5192
IN_COLS_PADDED = 5376

ADAM_LR = 0.001
ADAM_B1 = 0.9
ADAM_B2 = 0.999
ADAM_EPS = 1e-08
ADAM_WD = 0.01
ADAM_STEP = 10

LANES = 128
SUBLANES = 8
FLAT_COLS = 1024
VMEM_LIMIT_V7X = 52 * 1024 * 1024

NT_DIMS = (((1,), (1,)), ((), ()))
NN_DIMS = (((1,), (0,)), ((), ()))
TN_DIMS = (((0,), (0,)), ((), ()))

_TILE_ATT = (640, 512, 384, 256, 128)
_TILE_FF = (512, 256, 128)


def _pick(n, candidates):
    for c in candidates:
        if n % c == 0:
            return c
    return n


def _row_tile(rows, cols, budget_bytes=4 << 20, align=SUBLANES):
    best = None
    for t in range(align, rows + 1, align):
        if rows % t == 0 and t * cols * 4 <= budget_bytes:
            best = t
    return best if best is not None else rows


def _params(*semantics):
    return pltpu.CompilerParams(dimension_semantics=semantics, vmem_limit_bytes=VMEM_LIMIT_V7X)


MATMUL_VMEM_BUDGET = 36 << 20
MXU_FLOPS_V7X = 9.0e14
HBM_BYTES_PER_S_V7X = 2.5e12
GRID_STEP_S = 0.35e-6
MXU_DIM = 256


def _tile_candidates(n, cap):
    c = [t for t in range(LANES, min(n, cap) + 1, LANES) if n % t == 0]
    return c if c else [n]


def _matmul_tiles(m, n, c, a_bytes, b_bytes, o_bytes):
    best, best_cost = None, None
    for tc in _tile_candidates(c, 4096):
        steps = c // tc
        for tm in _tile_candidates(m, 2048):
            for tn in _tile_candidates(n, 2048):
                vmem = 2 * (tm * tc * a_bytes + tc * tn * b_bytes + tm * tn * o_bytes)
                vmem += tm * tn * 4 if steps > 1 else 0
                if vmem > MATMUL_VMEM_BUDGET:
                    continue
                traffic = m * c * a_bytes * (1 if steps == 1 else n // tn) + c * n * b_bytes * (m // tm)
                traffic += m * n * o_bytes
                grid = (m // tm) * (n // tn) * steps
                accumulate = 0 if steps == 1 else grid * tm * tn * 8 / 4.0e12
                fill = (-(-tn // MXU_DIM) * MXU_DIM / tn) * (-(-tc // MXU_DIM) * MXU_DIM / tc)
                cost = max(2.0 * m * n * c * fill / MXU_FLOPS_V7X, traffic / HBM_BYTES_PER_S_V7X)
                cost += grid * GRID_STEP_S + accumulate
                if best_cost is None or cost < best_cost:
                    best, best_cost = (tm, tn, tc), cost
    return best


def _matmul(a, b, mode, out_dtype, name):
    if mode == "nn":
        (m, c), (c2, n) = a.shape, b.shape
    elif mode == "nt":
        (m, c), (n, c2) = a.shape, b.shape
    else:
        (c, m), (c2, n) = a.shape, b.shape
    assert c == c2, (a.shape, b.shape, mode)
    tm, tn, tc = _matmul_tiles(m, n, c, a.dtype.itemsize, b.dtype.itemsize, jnp.dtype(out_dtype).itemsize)
    steps = c // tc
    if mode == "nn":
        a_spec = pl.BlockSpec((tm, tc), lambda i, j, k: (i, k))
        b_spec = pl.BlockSpec((tc, tn), lambda i, j, k: (k, j))
        dims = NN_DIMS
    elif mode == "nt":
        a_spec = pl.BlockSpec((tm, tc), lambda i, j, k: (i, k))
        b_spec = pl.BlockSpec((tn, tc), lambda i, j, k: (j, k))
        dims = NT_DIMS
    else:
        a_spec = pl.BlockSpec((tc, tm), lambda i, j, k: (k, i))
        b_spec = pl.BlockSpec((tc, tn), lambda i, j, k: (k, j))
        dims = TN_DIMS

    def body(a_ref, b_ref, o_ref, acc_ref):
        k = pl.program_id(2)

        @pl.when(k == 0)
        def _():
            acc_ref[...] = jnp.zeros_like(acc_ref)

        acc_ref[...] += lax.dot_general(a_ref[...].astype(BF16), b_ref[...].astype(BF16), dims,
                                        preferred_element_type=F32)

        @pl.when(k == steps - 1)
        def _():
            o_ref[...] = acc_ref[...].astype(o_ref.dtype)

    def body_whole(a_ref, b_ref, o_ref):
        o_ref[...] = lax.dot_general(a_ref[...].astype(BF16), b_ref[...].astype(BF16), dims,
                                     preferred_element_type=F32).astype(o_ref.dtype)

    return pl.pallas_call(
        body if steps > 1 else body_whole, grid=(m // tm, n // tn, steps), in_specs=[a_spec, b_spec],
        out_specs=pl.BlockSpec((tm, tn), lambda i, j, k: (i, j)),
        out_shape=jax.ShapeDtypeStruct((m, n), out_dtype),
        scratch_shapes=[pltpu.VMEM((tm, tn), F32)] if steps > 1 else [],
        compiler_params=_params("parallel", "parallel", "arbitrary"), name=name,
    )(a, b)


@jax.custom_vjp
def linear(x, w):
    return _matmul(x, w, "nn", F32, "linear_fwd")


def _linear_fwd(x, w):
    return _matmul(x, w, "nn", F32, "linear_fwd"), (x, w)


def _linear_bwd(res, dy):
    x, w = res
    dx = _matmul(dy, w, "nt", F32, "linear_dx")
    dw = _matmul(x, dy, "tn", w.dtype, "linear_dw")
    return dx, dw


linear.defvjp(_linear_fwd, _linear_bwd)


def _matmul_pair(a, b, wa, wb, name):
    m, ka = a.shape
    kb = b.shape[1]
    n = wa.shape[1]
    tm, tn, tc = _matmul_tiles(m, n, ka + kb, a.dtype.itemsize, wa.dtype.itemsize, 4)
    assert tc == ka + kb, (tc, ka, kb)

    def body(a_ref, b_ref, wa_ref, wb_ref, o_ref):
        o_ref[...] = (
            lax.dot_general(a_ref[...].astype(BF16), wa_ref[...].astype(BF16), NN_DIMS, preferred_element_type=F32)
            + lax.dot_general(b_ref[...].astype(BF16), wb_ref[...].astype(BF16), NN_DIMS, preferred_element_type=F32))

    return pl.pallas_call(
        body, grid=(m // tm, n // tn),
        in_specs=[pl.BlockSpec((tm, ka), lambda i, j: (i, 0)), pl.BlockSpec((tm, kb), lambda i, j: (i, 0)),
                  pl.BlockSpec((ka, tn), lambda i, j: (0, j)), pl.BlockSpec((kb, tn), lambda i, j: (0, j))],
        out_specs=pl.BlockSpec((tm, tn), lambda i, j: (i, j)), out_shape=jax.ShapeDtypeStruct((m, n), F32),
        compiler_params=_params("parallel", "parallel"), name=name,
    )(a, b, wa, wb)


@jax.custom_vjp
def linear_pair(a, b, w):
    ka = a.shape[1]
    return _matmul_pair(a, b, w[:ka], w[ka:], "linear_pair_fwd")


def _linear_pair_fwd(a, b, w):
    return linear_pair(a, b, w), (a, b, w)


def _linear_pair_bwd(res, dz):
    a, b, w = res
    ka = a.shape[1]
    da = _matmul(dz, w[:ka], "nt", F32, "linear_dx")
    db = _matmul(dz, w[ka:], "nt", F32, "linear_dx")
    dw = jnp.concatenate([_matmul(a, dz, "tn", w.dtype, "linear_dw"), _matmul(b, dz, "tn", w.dtype, "linear_dw")],
                         axis=0)
    return da, db, dw


linear_pair.defvjp(_linear_pair_fwd, _linear_pair_bwd)


def _rms_forward(x, g, out_dtype=F32, residual=None):
    rows, d = x.shape
    gd = g.shape[0]
    tr = _row_tile(rows, d, align=16)
    tile = pl.BlockSpec((tr, d), lambda i: (i, 0))

    def body(x_ref, g_ref, *rest):
        y_ref = rest[-1]
        for c0 in range(0, d, gd):
            xv = x_ref[:, c0:c0 + gd]
            r = lax.rsqrt(jnp.mean(xv * xv, axis=-1, keepdims=True) + EPS)
            y = (xv * r) * g_ref[...]
            if residual is not None:
                y = rest[0][:, c0:c0 + gd] + y
            y_ref[:, c0:c0 + gd] = y.astype(y_ref.dtype)

    extra = [] if residual is None else [residual]
    return pl.pallas_call(
        body, grid=(rows // tr,),
        in_specs=[tile, pl.BlockSpec((1, gd), lambda i: (0, 0))] + [tile] * len(extra),
        out_specs=tile, out_shape=jax.ShapeDtypeStruct((rows, d), out_dtype),
        compiler_params=_params("parallel"), name="rmsnorm_fwd",
    )(x, g.reshape(1, gd), *extra)


def _rms_backward(x, g, dy, residual=None):
    rows, d = x.shape
    gd = g.shape[0]
    tr = _row_tile(rows, d)
    tile = pl.BlockSpec((tr, d), lambda i: (i, 0))

    def body(x_ref, g_ref, dy_ref, *rest):
        dx_ref, dg_ref = rest[-2:]
        i = pl.program_id(0)

        @pl.when(i == 0)
        def _():
            dg_ref[...] = jnp.zeros_like(dg_ref)

        for c0 in range(0, d, gd):
            xv = x_ref[:, c0:c0 + gd]
            dyv = dy_ref[:, c0:c0 + gd]
            r = lax.rsqrt(jnp.mean(xv * xv, axis=-1, keepdims=True) + EPS)
            xh = xv * r
            t = dyv * g_ref[...]
            dx = r * (t - xh * jnp.mean(t * xh, axis=-1, keepdims=True))
            if residual is not None:
                dx = rest[0][:, c0:c0 + gd] + dx
            dx_ref[:, c0:c0 + gd] = dx
            dg_ref[...] += jnp.sum(dyv * xh, axis=0, keepdims=True)

    extra = [] if residual is None else [residual]
    dx, dg = pl.pallas_call(
        body, grid=(rows // tr,),
        in_specs=[tile, pl.BlockSpec((1, gd), lambda i: (0, 0)), tile] + [tile] * len(extra),
        out_specs=[tile, pl.BlockSpec((1, gd), lambda i: (0, 0))],
        out_shape=[jax.ShapeDtypeStruct((rows, d), F32), jax.ShapeDtypeStruct((1, gd), F32)],
        compiler_params=_params("arbitrary"), name="rmsnorm_bwd",
    )(x, g.reshape(1, gd), dy, *extra)
    return dx, dg.reshape(g.shape)


@jax.custom_vjp
def rms_norm(x, g):
    return _rms_forward(x, g)


def _rms_norm_fwd(x, g):
    return _rms_forward(x, g), (x, g)


def _rms_norm_bwd(res, dy):
    x, g = res
    return _rms_backward(x, g, dy)


rms_norm.defvjp(_rms_norm_fwd, _rms_norm_bwd)


@jax.custom_vjp
def add_norm(h, x, g):
    return _rms_forward(x, g, F32, h)


def _add_norm_fwd(h, x, g):
    return _rms_forward(x, g, F32, h), (x, g)


def _add_norm_bwd(res, dy):
    x, g = res
    dx, dg = _rms_backward(x, g, dy)
    return dy, dx, dg


add_norm.defvjp(_add_norm_fwd, _add_norm_bwd)


@functools.partial(jax.custom_vjp, nondiff_argnums=(3,))
def norm_linear_split(x, g, w, bounds):
    z = _matmul(_rms_forward(x, g, BF16), w, "nn", F32, "linear_fwd")
    return (x,) + tuple(z[:, lo:hi] for lo, hi in bounds)


def _norm_linear_split_fwd(x, g, w, bounds):
    y = _rms_forward(x, g, BF16)
    z = _matmul(y, w, "nn", F32, "linear_fwd")
    return (x,) + tuple(z[:, lo:hi] for lo, hi in bounds), (x, g, w, y)


def _norm_linear_split_bwd(bounds, res, cts):
    x, g, w, y = res
    dx_other, parts = cts[0], [p.astype(BF16) for p in cts[1:]]
    tail = w.shape[1] - bounds[-1][1]
    if tail:
        parts.append(jnp.zeros((x.shape[0], tail), BF16))
    dz = jnp.concatenate(parts, axis=1)
    dy = _matmul(dz, w, "nt", F32, "linear_dx")
    dw = _matmul(y, dz, "tn", w.dtype, "linear_dw")
    dx, dg = _rms_backward(x, g, dy, dx_other)
    return dx, dg, dw


norm_linear_split.defvjp(_norm_linear_split_fwd, _norm_linear_split_bwd)


def _visibility_id(pos, kind, l_real):
    if kind == "fox":
        return pos
    pad_chunk = 2 + (l_real - N_META) // CHUNK
    frame_chunk = 1 + jnp.right_shift(pos - N_META, 6)
    return jnp.where(pos < N_META, 0, jnp.where(pos < l_real, frame_chunk, pad_chunk))


def _raw_scores(kind, a1, b1, a2, b2):
    s = lax.dot_general(a1.astype(BF16), b1.astype(BF16), NT_DIMS, preferred_element_type=F32)
    if kind == "mla":
        s = s + lax.dot_general(a2.astype(BF16), b2.astype(BF16), NT_DIMS, preferred_element_type=F32)
    return s


def _block_pairs(nb, look, by_key):
    outer, inner = [], []
    for a in range(nb):
        rng = range(max(a - look, 0), nb) if by_key else range(0, min(a + look, nb - 1) + 1)
        for b in rng:
            outer.append(a)
            inner.append(b)
    return jnp.asarray(outer, jnp.int32), jnp.asarray(inner, jnp.int32)


def _attention_call(body, n_pairs, in_specs, out_specs, out_shape, scratch, name, tables, operands):
    return pl.pallas_call(
        body,
        grid_spec=pltpu.PrefetchScalarGridSpec(num_scalar_prefetch=2, grid=(HEADS, n_pairs), in_specs=in_specs,
                                               out_specs=out_specs, scratch_shapes=scratch),
        out_shape=out_shape, compiler_params=_params("parallel", "arbitrary"), name=name,
    )(*tables, *operands)


def _attention_forward(kind, l_real, q1, k1, v, extras):
    L = q1.shape[0]
    T = _pick(L, _TILE_ATT)
    nb = L // T
    look = 1 if kind == "mla" else 0
    scale = (HEAD_DIM + ROPE_DIM) ** -0.5 if kind == "mla" else HEAD_DIM ** -0.5
    tables = _block_pairs(nb, look, False)

    q_tile = pl.BlockSpec((T, HEAD_DIM), lambda h, t, it, jt: (it[t], h))
    k_tile = pl.BlockSpec((T, HEAD_DIM), lambda h, t, it, jt: (jt[t], h))
    row_stat = pl.BlockSpec((None, T, 1), lambda h, t, it, jt: (h, it[t], 0))
    if kind == "mla":
        extra_specs = [pl.BlockSpec((None, T, ROPE_DIM), lambda h, t, it, jt: (h, it[t], 0)),
                       pl.BlockSpec((T, ROPE_DIM), lambda h, t, it, jt: (jt[t], 0))]
    else:
        extra_specs = [pl.BlockSpec((None, 1, T), lambda h, t, it, jt: (h, 0, jt[t]))]

    def body(it_ref, jt_ref, q1_ref, k1_ref, v_ref, *rest):
        if kind == "mla":
            eq_ref, ek_ref, o_ref, lse_ref, m_ref, l_ref, acc_ref = rest
        else:
            ek_ref, o_ref, lse_ref, m_ref, l_ref, acc_ref = rest
        t = pl.program_id(1)
        i, j = it_ref[t], jt_ref[t]

        @pl.when(j == 0)
        def _():
            m_ref[...] = jnp.full_like(m_ref, NEG)
            l_ref[...] = jnp.zeros_like(l_ref)
            acc_ref[...] = jnp.zeros_like(acc_ref)

        def block(masked):
            if kind == "mla":
                s = _raw_scores(kind, q1_ref[...] * scale, k1_ref[...], eq_ref[...] * scale, ek_ref[...])
            else:
                s = _raw_scores(kind, q1_ref[...] * scale, k1_ref[...], None, None) - ek_ref[...]
            if masked:
                pos_q = i * T + lax.broadcasted_iota(jnp.int32, (T, 1), 0)
                pos_k = j * T + lax.broadcasted_iota(jnp.int32, (1, T), 1)
                s = jnp.where(_visibility_id(pos_k, kind, l_real) <= _visibility_id(pos_q, kind, l_real), s, NEG)
            m_prev = m_ref[...]
            m_new = jnp.maximum(m_prev, jnp.max(s, axis=1, keepdims=True))
            alpha = jnp.exp(m_prev - m_new)
            p = jnp.exp(s - m_new)
            l_ref[...] = alpha * l_ref[...] + jnp.sum(p, axis=1, keepdims=True)
            m_ref[...] = m_new
            vb = v_ref[...].astype(BF16)
            p_hi = p.astype(BF16)
            pv = lax.dot_general(p_hi, vb, NN_DIMS, preferred_element_type=F32)
            if kind == "fox":
                p_lo = (p - p_hi.astype(F32)).astype(BF16)
                pv = pv + lax.dot_general(p_lo, vb, NN_DIMS, preferred_element_type=F32)
            acc_ref[...] = alpha * acc_ref[...] + pv

        @pl.when(j < i)
        def _():
            block(False)

        @pl.when(j >= i)
        def _():
            block(True)

        @pl.when(j == jnp.minimum(i + look, nb - 1))
        def _():
            o_ref[...] = acc_ref[...] / l_ref[...]
            lse_ref[...] = m_ref[...] + jnp.log(l_ref[...])

    return _attention_call(
        body, tables[0].shape[0], [q_tile, k_tile, k_tile] + extra_specs, [q_tile, row_stat],
        [jax.ShapeDtypeStruct((L, HEADS * HEAD_DIM), F32), jax.ShapeDtypeStruct((HEADS, L, 1), F32)],
        [pltpu.VMEM((T, 1), F32), pltpu.VMEM((T, 1), F32), pltpu.VMEM((T, HEAD_DIM), F32)],
        kind + "_attn_fwd", tables, (q1, k1, v, *extras))


def _attention_delta(o, do):
    L = o.shape[0]
    T = _pick(L, _TILE_ATT)
    tile = pl.BlockSpec((T, HEADS * HEAD_DIM), lambda i: (i, 0))

    def body(o_ref, do_ref, delta_ref):
        for h in range(HEADS):
            cols = slice(h * HEAD_DIM, (h + 1) * HEAD_DIM)
            delta_ref[h] = jnp.sum(do_ref[:, cols].astype(BF16).astype(F32) * o_ref[:, cols], axis=1, keepdims=True)

    return pl.pallas_call(
        body, grid=(L // T,), in_specs=[tile, tile],
        out_specs=pl.BlockSpec((HEADS, T, 1), lambda i: (0, i, 0)),
        out_shape=jax.ShapeDtypeStruct((HEADS, L, 1), F32),
        compiler_params=_params("parallel"), name="attn_delta",
    )(o, do)


def _attention_backward(kind, l_real, q1, k1, v, extras, do, lse_row, delta_row):
    L = q1.shape[0]
    T = _pick(L, _TILE_ATT)
    nb = L // T
    look = 1 if kind == "mla" else 0
    scale = (HEAD_DIM + ROPE_DIM) ** -0.5 if kind == "mla" else HEAD_DIM ** -0.5
    tables = _block_pairs(nb, look, True)

    k_tile = pl.BlockSpec((T, HEAD_DIM), lambda h, t, jt, it: (jt[t], h))
    q_tile = pl.BlockSpec((T, HEAD_DIM), lambda h, t, jt, it: (it[t], h))
    q_row = pl.BlockSpec((None, 1, T), lambda h, t, jt, it: (h, 0, it[t]))
    if kind == "mla":
        extra_specs = [pl.BlockSpec((None, T, ROPE_DIM), lambda h, t, jt, it: (h, it[t], 0)),
                       pl.BlockSpec((T, ROPE_DIM), lambda h, t, jt, it: (jt[t], 0))]
        third_spec = pl.BlockSpec((None, T, ROPE_DIM), lambda h, t, jt, it: (h, jt[t], 0))
        third_shape = jax.ShapeDtypeStruct((HEADS, L, ROPE_DIM), F32)
        third_scratch = pltpu.VMEM((T, ROPE_DIM), F32)
    else:
        extra_specs = [pl.BlockSpec((None, T, 1), lambda h, t, jt, it: (h, jt[t], 0))]
        third_spec = pl.BlockSpec((None, T, 1), lambda h, t, jt, it: (h, jt[t], 0))
        third_shape = jax.ShapeDtypeStruct((HEADS, L, 1), F32)
        third_scratch = pltpu.VMEM((T, 1), F32)
    in_specs = [q_tile, k_tile, k_tile] + extra_specs + [q_tile, q_row, q_row]
    n_pairs = tables[0].shape[0]
    out_specs = [k_tile, k_tile, third_spec, pl.BlockSpec((L, HEAD_DIM), lambda h, t, jt, it: (0, h))]
    out_shape = [jax.ShapeDtypeStruct((L, HEADS * HEAD_DIM), F32), jax.ShapeDtypeStruct((L, HEADS * HEAD_DIM), F32),
                 third_shape, jax.ShapeDtypeStruct((L, HEADS * HEAD_DIM), F32)]
    if kind == "mla":
        out_specs.append(pl.BlockSpec((None, L, ROPE_DIM), lambda h, t, jt, it: (h, 0, 0)))
        out_shape.append(jax.ShapeDtypeStruct((HEADS, L, ROPE_DIM), F32))

    def body(jt_ref, it_ref, q1_ref, k1_ref, v_ref, *rest):
        if kind == "mla":
            (eq_ref, ek_ref, do_ref, lse_ref, delta_ref, dk1_ref, dv_ref, third_ref, dq1_ref, dq2_ref,
             acck_ref, accv_ref, acc3_ref) = rest
        else:
            (ek_ref, do_ref, lse_ref, delta_ref, dk1_ref, dv_ref, third_ref, dq1_ref,
             acck_ref, accv_ref, acc3_ref) = rest
        t = pl.program_id(1)
        j, i = jt_ref[t], it_ref[t]
        q_rows = pl.ds(pl.multiple_of(i * T, T), T)

        @pl.when(t == 0)
        def _():
            dq1_ref[...] = jnp.zeros_like(dq1_ref)
            if kind == "mla":
                dq2_ref[...] = jnp.zeros_like(dq2_ref)

        @pl.when(i == jnp.maximum(j - look, 0))
        def _():
            acck_ref[...] = jnp.zeros_like(acck_ref)
            accv_ref[...] = jnp.zeros_like(accv_ref)
            acc3_ref[...] = jnp.zeros_like(acc3_ref)

        def block(masked):
            if kind == "mla":
                kb = jnp.concatenate([k1_ref[...].astype(BF16), ek_ref[...].astype(BF16)], axis=1)
                qb = jnp.concatenate([(q1_ref[...] * scale).astype(BF16), (eq_ref[...] * scale).astype(BF16)],
                                     axis=1)
            else:
                kb, qb = k1_ref[...].astype(BF16), (q1_ref[...] * scale).astype(BF16)
            st = lax.dot_general(kb, qb, NT_DIMS, preferred_element_type=F32)
            if kind == "fox":
                st = st - ek_ref[...]
            if masked:
                pos_k = j * T + lax.broadcasted_iota(jnp.int32, (T, 1), 0)
                pos_q = i * T + lax.broadcasted_iota(jnp.int32, (1, T), 1)
                st = jnp.where(_visibility_id(pos_k, kind, l_real) <= _visibility_id(pos_q, kind, l_real), st, NEG)
            pt = jnp.exp(st - lse_ref[...])
            dob = do_ref[...].astype(BF16)
            accv_ref[...] += lax.dot_general(pt.astype(BF16), dob, NN_DIMS, preferred_element_type=F32)
            dpt = lax.dot_general(v_ref[...].astype(BF16), dob, NT_DIMS, preferred_element_type=F32)
            dst = pt * (dpt - delta_ref[...])
            dsb = dst.astype(BF16)
            dk = lax.dot_general(dsb, qb, NN_DIMS, preferred_element_type=F32)
            dq = lax.dot_general(dsb, kb, TN_DIMS, preferred_element_type=F32)
            if kind == "mla":
                acck_ref[...] += dk[:, :HEAD_DIM]
                acc3_ref[...] += dk[:, HEAD_DIM:]
                dq1_ref[q_rows, :] += dq[:, :HEAD_DIM]
                dq2_ref[q_rows, :] += dq[:, HEAD_DIM:]
            else:
                acck_ref[...] += dk
                dq1_ref[q_rows, :] += dq
                acc3_ref[...] -= jnp.sum(dst, axis=1, keepdims=True)

        @pl.when(i > j)
        def _():
            block(False)

        @pl.when(i <= j)
        def _():
            block(True)

        @pl.when(i == nb - 1)
        def _():
            dk1_ref[...] = acck_ref[...]
            dv_ref[...] = accv_ref[...]
            third_ref[...] = acc3_ref[...]

        @pl.when(t == n_pairs - 1)
        def _():
            dq1_ref[...] = dq1_ref[...] * scale
            if kind == "mla":
                dq2_ref[...] = dq2_ref[...] * scale

    return _attention_call(
        body, n_pairs, in_specs, out_specs, out_shape,
        [pltpu.VMEM((T, HEAD_DIM), F32), pltpu.VMEM((T, HEAD_DIM), F32), third_scratch],
        kind + "_attn_bwd", tables, (q1, k1, v, *extras, do, lse_row, delta_row))


def _as_row(col):
    return col.reshape(col.shape[0], 1, col.shape[1])


@functools.partial(jax.custom_vjp, nondiff_argnums=(0,))
def mla_attention(l_real, qn, qr, kn, kr, v):
    return _attention_forward("mla", l_real, qn, kn, v, (qr, kr))[0]


def _mla_attention_fwd(l_real, qn, qr, kn, kr, v):
    o, lse = _attention_forward("mla", l_real, qn, kn, v, (qr, kr))
    return o, (qn, qr, kn, kr, v, o, lse)


def _mla_attention_bwd(l_real, res, do):
    qn, qr, kn, kr, v, o, lse = res
    delta = _attention_delta(o, do)
    dkn, dv, dkr_heads, dqn, dqr = _attention_backward("mla", l_real, qn, kn, v, (qr, kr), do, _as_row(lse),
                                                       _as_row(delta))
    return dqn, dqr, dkn, jnp.sum(dkr_heads, axis=0), dv


mla_attention.defvjp(_mla_attention_fwd, _mla_attention_bwd)


@functools.partial(jax.custom_vjp, nondiff_argnums=(0,))
def fox_attention(l_real, q, k, v, c):
    return _attention_forward("fox", l_real, q, k, v, (_as_row(c),))[0]


def _fox_attention_fwd(l_real, q, k, v, c):
    o, lse = _attention_forward("fox", l_real, q, k, v, (_as_row(c),))
    return o, (q, k, v, c, o, lse)


def _fox_attention_bwd(l_real, res, do):
    q, k, v, c, o, lse = res
    delta = _attention_delta(o, do)
    dk, dv, dc, dq = _attention_backward("fox", l_real, q, k, v, (c,), do, _as_row(lse), _as_row(delta))
    return dq, dk, dv, dc


fox_attention.defvjp(_fox_attention_fwd, _fox_attention_bwd)


GELU_C0 = 0.7978845608028654
GELU_C1 = 0.044715


def _shift_rows(x, prev, s):
    r = pltpu.roll(x, s, 0)
    pr = pltpu.roll(prev, s, 0)
    row = lax.broadcasted_iota(jnp.int32, prev.shape, 0)
    top = jnp.where(row < s, pr, r[0:SUBLANES])
    return jnp.concatenate([top, r[SUBLANES:]], axis=0)


def _conv_tiles(L, f):
    return _pick(L, _TILE_ATT), _pick(f, _TILE_FF)


def _conv_gate_forward(u, w, b, out_dtype):
    L, f2 = u.shape
    f = f2 // 2
    tm, tn = _conv_tiles(L, f)
    rb = tm // SUBLANES

    def body(u_ref, up_ref, w_ref, b_ref, o_ref):
        i = pl.program_id(1)
        x = u_ref[...]
        prev = jnp.where(i > 0, up_ref[...], 0.0)
        wv = w_ref[...]
        hc = b_ref[...] + ((wv[0:1] * _shift_rows(x, prev, 2) + wv[1:2] * _shift_rows(x, prev, 1)) + wv[2:3] * x)
        g = hc[:, :tn]
        gelu = 0.5 * g * (1.0 + jnp.tanh(GELU_C0 * (g + GELU_C1 * g * g * g)))
        o_ref[...] = (gelu * hc[:, tn:]).astype(o_ref.dtype)

    return pl.pallas_call(
        body, grid=(f // tn, L // tm),
        in_specs=[pl.BlockSpec((tm, 2 * tn), lambda j, i: (i, j)),
                  pl.BlockSpec((SUBLANES, 2 * tn), lambda j, i: (jnp.maximum(i * rb - 1, 0), j)),
                  pl.BlockSpec((3, 2 * tn), lambda j, i: (0, j)),
                  pl.BlockSpec((1, 2 * tn), lambda j, i: (0, j))],
        out_specs=pl.BlockSpec((tm, tn), lambda j, i: (i, j)),
        out_shape=jax.ShapeDtypeStruct((L, f), out_dtype),
        compiler_params=_params("parallel", "parallel"), name="conv_gate_fwd",
    )(u, u, w, b)


def _conv_gate_backward(u, w, b, dact, du_dtype):
    L, f2 = u.shape
    f = f2 // 2
    tm, tn = _conv_tiles(L, f)
    rb = tm // SUBLANES
    n_row_blocks = L // SUBLANES
    n_i = L // tm
    ext = tm + SUBLANES

    def next_rows(i):
        return jnp.minimum((i + 1) * rb, n_row_blocks - 1)

    def body(u_ref, up_ref, un_ref, da_ref, dan_ref, w_ref, b_ref, du_ref, dwb_ref):
        i = pl.program_id(1)
        is_last = i == n_i - 1
        prev = jnp.where(i > 0, up_ref[...], 0.0)
        xe = jnp.concatenate([u_ref[...], jnp.where(is_last, 0.0, un_ref[...])], axis=0)
        x1 = _shift_rows(xe, prev, 1)
        x2 = _shift_rows(xe, prev, 2)
        wv = w_ref[...]
        hc = b_ref[...] + ((wv[0:1] * x2 + wv[1:2] * x1) + wv[2:3] * xe)
        g, up = hc[:, :tn], hc[:, tn:]
        da = jnp.concatenate([da_ref[...], jnp.where(is_last, 0.0, dan_ref[...])], axis=0)
        t = jnp.tanh(GELU_C0 * (g + GELU_C1 * g * g * g))
        gelu = 0.5 * g * (1.0 + t)
        dgelu = 0.5 * (1.0 + t) + 0.5 * g * (1.0 - t * t) * (GELU_C0 * (1.0 + 3.0 * GELU_C1 * g * g))
        dh = jnp.concatenate([da * up * dgelu, da * gelu], axis=1)
        dh1 = pltpu.roll(dh, ext - 1, 0)
        dh2 = pltpu.roll(dh, ext - 2, 0)
        du_ref[...] = ((wv[2:3] * dh + wv[1:2] * dh1) + wv[0:1] * dh2)[:tm].astype(du_ref.dtype)
        dw0 = jnp.sum((dh * x2)[:tm], axis=0, keepdims=True)
        dw1 = jnp.sum((dh * x1)[:tm], axis=0, keepdims=True)
        dw2 = jnp.sum((dh * xe)[:tm], axis=0, keepdims=True)
        db = jnp.sum(dh[:tm], axis=0, keepdims=True)
        row = lax.broadcasted_iota(jnp.int32, (SUBLANES, 2 * tn), 0)
        upd = jnp.where(row == 0, dw0, jnp.where(row == 1, dw1, jnp.where(row == 2, dw2,
                        jnp.where(row == 3, db, 0.0))))

        @pl.when(i == 0)
        def _():
            dwb_ref[...] = jnp.zeros_like(dwb_ref)

        dwb_ref[...] += upd

    return pl.pallas_call(
        body, grid=(f // tn, n_i),
        in_specs=[pl.BlockSpec((tm, 2 * tn), lambda j, i: (i, j)),
                  pl.BlockSpec((SUBLANES, 2 * tn), lambda j, i: (jnp.maximum(i * rb - 1, 0), j)),
                  pl.BlockSpec((SUBLANES, 2 * tn), lambda j, i: (next_rows(i), j)),
                  pl.BlockSpec((tm, tn), lambda j, i: (i, j)),
                  pl.BlockSpec((SUBLANES, tn), lambda j, i: (next_rows(i), j)),
                  pl.BlockSpec((3, 2 * tn), lambda j, i: (0, j)),
                  pl.BlockSpec((1, 2 * tn), lambda j, i: (0, j))],
        out_specs=[pl.BlockSpec((tm, 2 * tn), lambda j, i: (i, j)),
                   pl.BlockSpec((SUBLANES, 2 * tn), lambda j, i: (0, j))],
        out_shape=[jax.ShapeDtypeStruct((L, f2), du_dtype), jax.ShapeDtypeStruct((SUBLANES, f2), F32)],
        compiler_params=_params("parallel", "arbitrary"), name="conv_gate_bwd",
    )(u, u, u, dact, dact, w, b)


@jax.custom_vjp
def conv_ffn(h, g, w_up, w_conv, b_conv, w_down):
    return _conv_ffn_fwd(h, g, w_up, w_conv, b_conv, w_down)[0]


def _conv_ffn_fwd(h, g, w_up, w_conv, b_conv, w_down):
    y = _rms_forward(h, g, BF16)
    u = _matmul(y, w_up, "nn", F32, "linear_fwd")
    act = _conv_gate_forward(u, w_conv, b_conv.reshape(1, -1), BF16)
    return (h, _matmul(act, w_down, "nn", F32, "linear_fwd")), (h, g, w_up, w_conv, b_conv, w_down, y, u, act)


def _conv_ffn_bwd(res, cts):
    h, g, w_up, w_conv, b_conv, w_down, y, u, act = res
    dh_other, df = cts
    dact = _matmul(df, w_down, "nt", F32, "linear_dx")
    dw_down = _matmul(act, df, "tn", w_down.dtype, "linear_dw")
    du, dwb = _conv_gate_backward(u, w_conv, b_conv.reshape(1, -1), dact, BF16)
    dy = _matmul(du, w_up, "nt", F32, "linear_dx")
    dw_up = _matmul(y, du, "tn", w_up.dtype, "linear_dw")
    dh, dg = _rms_backward(h, g, dy, dh_other)
    return dh, dg, dw_up, dwb[0:3], dwb[3], dw_down


conv_ffn.defvjp(_conv_ffn_fwd, _conv_ffn_bwd)


def _loss_rows(y, target):
    rows, d = y.shape
    tr = _row_tile(rows, d)

    def body(y_ref, t_ref, loss_ref, dy_ref):
        err = y_ref[...] - t_ref[...]
        loss_ref[...] = 0.5 * jnp.mean(err * err, axis=-1, keepdims=True)
        dy_ref[...] = err * (1.0 / d)

    return pl.pallas_call(
        body, grid=(rows // tr,),
        in_specs=[pl.BlockSpec((tr, d), lambda i: (i, 0)), pl.BlockSpec((tr, d), lambda i: (i, 0))],
        out_specs=[pl.BlockSpec((tr, 1), lambda i: (i, 0)), pl.BlockSpec((tr, d), lambda i: (i, 0))],
        out_shape=[jax.ShapeDtypeStruct((rows, 1), F32), jax.ShapeDtypeStruct((rows, d), F32)],
        compiler_params=_params("parallel"), name="loss_head",
    )(y, target)


@jax.custom_vjp
def token_loss(y, target):
    return jnp.sum(_loss_rows(y, target)[0])


def _token_loss_fwd(y, target):
    rows, dy = _loss_rows(y, target)
    return jnp.sum(rows), dy


def _token_loss_bwd(dy, ct):
    return ct * dy, -ct * dy


token_loss.defvjp(_token_loss_fwd, _token_loss_bwd)


def _cols_from_devices(g):
    k = g.shape[1]
    return jnp.transpose(g, (1, 0, 2)).reshape(k, -1)


@functools.partial(jax.custom_vjp, nondiff_argnums=(1,))
def _interleave_gate_up(a, f):
    tn = _pick(f, _TILE_FF)
    parts = []
    for j in range(f // tn):
        parts += [a[..., j * tn:(j + 1) * tn], a[..., f + j * tn:f + (j + 1) * tn]]
    return jnp.concatenate(parts, axis=-1)


def _interleave_fwd(a, f):
    return _interleave_gate_up(a, f), None


def _interleave_bwd(f, _, ct):
    tn = _pick(f, _TILE_FF)
    gate = [ct[..., 2 * j * tn:(2 * j + 1) * tn] for j in range(f // tn)]
    up = [ct[..., (2 * j + 1) * tn:(2 * j + 2) * tn] for j in range(f // tn)]
    return (jnp.concatenate(gate + up, axis=-1),)


_interleave_gate_up.defvjp(_interleave_fwd, _interleave_bwd)


def _rope(x, cos, sin):
    half = x.shape[-1] // 2
    x1, x2 = x[..., :half], x[..., half:]
    return jnp.concatenate([x1 * cos - x2 * sin, x2 * cos + x1 * sin], axis=-1)


PROJ_BOUNDS = ((0, 512), (512, 1024), (1024, 2048), (2048, 3072), (3072, 4096), (4096, 5120), (5120, 5184),
               (5184, 5192))
Q_BOUNDS = ((0, HEADS * HEAD_DIM), (HEADS * HEAD_DIM, HEADS * (HEAD_DIM + ROPE_DIM)))
KV_BOUNDS = ((0, HEADS * HEAD_DIM), (HEADS * HEAD_DIM, 2 * HEADS * HEAD_DIM))


def _layer(h, big, small, conv_w, l, l_real, cos, sin):
    L, d = h.shape
    w_in = _cols_from_devices(big["w_in"])
    w_in = jnp.concatenate([w_in[:, :1024], w_in[:, 1088:5184], w_in[:, 1024:1088], w_in[:, 5184:],
                            jnp.zeros((d, IN_COLS_PADDED - IN_COLS), w_in.dtype)], axis=1)
    w_q_up = _cols_from_devices(big["w_q_up"]).reshape(MLA_Q_LORA, HEADS, HEAD_DIM + ROPE_DIM)
    w_q_up = jnp.concatenate([w_q_up[:, :, :HEAD_DIM].reshape(MLA_Q_LORA, -1),
                              w_q_up[:, :, HEAD_DIM:].reshape(MLA_Q_LORA, -1)], axis=1)
    w_kv_up = _cols_from_devices(big["w_kv_up"]).reshape(MLA_KV_LORA, HEADS, 2 * HEAD_DIM)
    w_kv_up = jnp.concatenate([w_kv_up[:, :, :HEAD_DIM].reshape(MLA_KV_LORA, -1),
                               w_kv_up[:, :, HEAD_DIM:].reshape(MLA_KV_LORA, -1)], axis=1)
    w_out = big["w_out"].reshape(-1, d)
    f = big["w_ffn_down"].shape[0] * big["w_ffn_down"].shape[1]
    w_ffn_up = _interleave_gate_up(_cols_from_devices(big["w_ffn_up"]), f)
    w_ffn_down = big["w_ffn_down"].reshape(f, d)
    w_conv = _interleave_gate_up(conv_w, f)
    b_conv = _interleave_gate_up(small["b_ffn_conv"][l], f)

    h, c_q, c_kv, fq, fk, fv, fg, k_rope, ff = norm_linear_split(h, small["ln_mix_pre"][l], w_in, PROJ_BOUNDS)
    _, qn, qr = norm_linear_split(c_q, small["g_q_latent"][l], w_q_up, Q_BOUNDS)
    _, kn, v = norm_linear_split(c_kv, small["g_kv_latent"][l], w_kv_up, KV_BOUNDS)
    qr = jnp.transpose(_rope(qr.reshape(L, HEADS, ROPE_DIM), cos[:, None, :], sin[:, None, :]), (1, 0, 2))
    kr = _rope(k_rope, cos, sin)
    a = mla_attention(l_real, qn, qr, kn, kr, v)

    fqn = rms_norm(fq, small["g_fox_q"][l])
    fkn = rms_norm(fk, small["g_fox_k"][l])
    log_f = jax.nn.log_sigmoid(ff + small["b_forget"][l])
    c = jnp.cumsum(log_f, axis=0).T[:, :, None]
    bmix = fox_attention(l_real, fqn, fkn, fv, c) * jax.nn.sigmoid(fg)

    mix = linear_pair(a, bmix, w_out)
    h = add_norm(h, mix, small["ln_mix_post"][l])

    h, f_out = conv_ffn(h, small["ln_ffn_pre"][l], w_ffn_up, w_conv, b_conv, w_ffn_down)
    h = add_norm(h, f_out, small["ln_ffn_post"][l])
    return h


def _local_loss(big, small, meta, conv_w, x, target):
    s, d = x.shape
    l_real = N_META + s
    l_pad = -(-l_real // Q_BLOCK) * Q_BLOCK
    h = jnp.concatenate([meta, x, jnp.zeros((l_pad - l_real, d), F32)], axis=0)
    half = ROPE_DIM // 2
    inv_freq = ROPE_THETA ** (-jnp.arange(half, dtype=F32) / half)
    ang = jnp.arange(l_pad, dtype=jnp.int32).astype(F32)[:, None] * inv_freq[None, :]
    cos, sin = jnp.cos(ang), jnp.sin(ang)
    for l in range(DEPTH):
        h = _layer(h, big[l], small, conv_w[l], l, l_real, cos, sin)
    return token_loss(h[N_META:l_real], target)


ANY_SPACE = pl.BlockSpec(memory_space=pl.ANY)


def _place():
    ix, iy, ic = lax.axis_index("x"), lax.axis_index("y"), lax.axis_index("c")
    return ix, iy, ic, [(1 - ix, iy), (ix, 1 - iy), (1 - ix, 1 - iy)]


def _comm_call(body, arrays, out_shapes, n_remote, n_local, name):
    return pl.pallas_call(
        body, out_shape=out_shapes, in_specs=[ANY_SPACE] * len(arrays), out_specs=[ANY_SPACE] * len(out_shapes),
        scratch_shapes=[pltpu.SemaphoreType.DMA((n_remote,)), pltpu.SemaphoreType.DMA((n_remote,)),
                        pltpu.SemaphoreType.DMA((n_local,))],
        name=name,
    )(*arrays)


def _gather(arrays, name):
    n = len(arrays)

    def body(*refs):
        xs, outs = refs[:n], refs[n:2 * n]
        send_sems, recv_sems, local_sems = refs[2 * n:]
        ix, iy, ic, chips = _place()
        me, sibling = (ix, iy, ic), (ix, iy, 1 - ic)

        def copy(a, k, block, to, src=None):
            dst = outs[a].at[4 * block[0] + 2 * block[1] + block[2]]
            return pltpu.make_async_remote_copy(
                src_ref=dst if src is None else src, dst_ref=dst, send_sem=send_sems.at[7 * a + k],
                recv_sem=recv_sems.at[7 * a + k], device_id=to, device_id_type=MESH_ID)

        local, sent = [], []
        for a in range(n):
            mine = pltpu.make_async_copy(xs[a], outs[a].at[4 * ix + 2 * iy + ic], local_sems.at[a])
            mine.start()
            local.append(mine)
            first = [copy(a, 0, me, sibling, src=xs[a])]
            first += [copy(a, 1 + j, me, (*chip, ic), src=xs[a]) for j, chip in enumerate(chips)]
            for cp in first:
                cp.start()
            sent += first
        for a in range(n):
            for j, chip in enumerate(chips):
                copy(a, 1 + j, (*chip, ic), me).wait_recv()
                passed = copy(a, 4 + j, (*chip, ic), sibling)
                passed.start()
                sent.append(passed)
        for a in range(n):
            copy(a, 0, sibling, me).wait_recv()
            for j, chip in enumerate(chips):
                copy(a, 4 + j, (*chip, 1 - ic), me).wait_recv()
        for cp in sent:
            cp.wait_send()
        for cp in local:
            cp.wait()

    out_shapes = [jax.ShapeDtypeStruct((N_DEV,) + a.shape, a.dtype) for a in arrays]
    return _comm_call(body, arrays, out_shapes, 7 * n, n, name)


def _swap_with_sibling(arrays, name):
    n = len(arrays)

    def body(*refs):
        xs, outs = refs[:n], refs[n:2 * n]
        send_sems, recv_sems, _ = refs[2 * n:]
        ix, iy, ic, _ = _place()
        copies = [pltpu.make_async_remote_copy(
            src_ref=xs[a], dst_ref=outs[a], send_sem=send_sems.at[a], recv_sem=recv_sems.at[a],
            device_id=(ix, iy, 1 - ic), device_id_type=MESH_ID) for a in range(n)]
        for cp in copies:
            cp.start()
        for cp in copies:
            cp.wait()

    out_shapes = [jax.ShapeDtypeStruct(a.shape, a.dtype) for a in arrays]
    return _comm_call(body, arrays, out_shapes, n, 1, name)


def _exchange_chips(arrays, name):
    n = len(arrays)

    def body(*refs):
        xs, outs = refs[:n], refs[n:2 * n]
        send_sems, recv_sems, local_sems = refs[2 * n:]
        ix, iy, ic, chips = _place()
        my_chip = 2 * ix + iy
        local, sent = [], []
        for a in range(n):
            mine = pltpu.make_async_copy(xs[a].at[my_chip], outs[a].at[my_chip], local_sems.at[a])
            mine.start()
            local.append(mine)
            for j, chip in enumerate(chips):
                cp = pltpu.make_async_remote_copy(
                    src_ref=xs[a].at[2 * chip[0] + chip[1]], dst_ref=outs[a].at[my_chip],
                    send_sem=send_sems.at[3 * a + j], recv_sem=recv_sems.at[3 * a + j],
                    device_id=(*chip, ic), device_id_type=MESH_ID)
                cp.start()
                sent.append(cp)
        for a in range(n):
            for j, chip in enumerate(chips):
                pltpu.make_async_remote_copy(
                    src_ref=xs[a].at[my_chip], dst_ref=outs[a].at[2 * chip[0] + chip[1]],
                    send_sem=send_sems.at[3 * a + j], recv_sem=recv_sems.at[3 * a + j],
                    device_id=(*chip, ic), device_id_type=MESH_ID).wait_recv()
        for cp in sent:
            cp.wait_send()
        for cp in local:
            cp.wait()

    out_shapes = [jax.ShapeDtypeStruct(a.shape, a.dtype) for a in arrays]
    return _comm_call(body, arrays, out_shapes, 3 * n, n, name)


def _sum_slots(x, out_dtype, name):
    slots, rows, cols = x.shape
    tr = _row_tile(rows, cols, (2 << 20) // slots, 16)

    def body(x_ref, o_ref):
        acc = x_ref[0].astype(F32)
        for s in range(1, slots):
            acc = acc + x_ref[s].astype(F32)
        o_ref[...] = acc.astype(o_ref.dtype)

    return pl.pallas_call(
        body, grid=(rows // tr,), in_specs=[pl.BlockSpec((slots, tr, cols), lambda i: (0, i, 0))],
        out_specs=pl.BlockSpec((tr, cols), lambda i: (i, 0)), out_shape=jax.ShapeDtypeStruct((rows, cols), out_dtype),
        compiler_params=_params("parallel"), name=name,
    )(x)


def _add_pairs(a, b, name):
    slots, rows, cols = a.shape
    tr = _row_tile(rows, cols, 1 << 20, 16)

    def body(a_ref, b_ref, o_ref):
        o_ref[...] = (a_ref[...].astype(F32) + b_ref[...].astype(F32)).astype(o_ref.dtype)

    spec = pl.BlockSpec((None, tr, cols), lambda s, i: (s, i, 0))
    return pl.pallas_call(
        body, grid=(slots, rows // tr), in_specs=[spec, spec], out_specs=spec,
        out_shape=jax.ShapeDtypeStruct(a.shape, BF16), compiler_params=_params("parallel", "parallel"), name=name,
    )(a, b)


def _reduce_scatter(grads, ic):
    by_chip = [g.reshape((4, 2) + g.shape[1:]) for g in grads]
    keep = [lax.dynamic_index_in_dim(g, ic, axis=1, keepdims=False) for g in by_chip]
    give = [lax.dynamic_index_in_dim(g, 1 - ic, axis=1, keepdims=False) for g in by_chip]
    got = _swap_with_sibling(give, "scatter_sibling")
    pairs = [_add_pairs(k, g, "add_pairs") for k, g in zip(keep, got)]
    received = _exchange_chips(pairs, "scatter_chips")
    return [_sum_slots(r, F32, "sum_grads") for r in received]


def _pack(arrays, dtype, row_multiple):
    flat = jnp.concatenate([a.astype(dtype).reshape(-1) for a in arrays])
    n = flat.shape[0]
    quantum = row_multiple * FLAT_COLS
    padded = -(-n // quantum) * quantum
    return jnp.pad(flat, (0, padded - n)).reshape(padded // FLAT_COLS, FLAT_COLS)


def _unpack(buf, shapes):
    flat = buf.reshape(-1)
    out, off = [], 0
    for shp in shapes:
        n = 1
        for s in shp:
            n *= s
        out.append(flat[off:off + n].reshape(tuple(shp)))
        off += n
    return out


def _adamw(w, g, m, v, name):
    shape = w.shape
    cols = shape[-1]
    w2, g2, m2, v2 = (a.reshape(-1, cols) for a in (w, g, m, v))
    rows = w2.shape[0]
    tr = _row_tile(rows, cols, 1 << 20)

    def body(w_ref, g_ref, m_ref, v_ref, d_ref, nm_ref, nv_ref):
        gv = g_ref[...]
        nm = ADAM_B1 * m_ref[...] + (1.0 - ADAM_B1) * gv
        nv = ADAM_B2 * v_ref[...] + (1.0 - ADAM_B2) * (gv * gv)
        m_hat = nm / (1.0 - ADAM_B1 ** ADAM_STEP)
        v_hat = nv / (1.0 - ADAM_B2 ** ADAM_STEP)
        d_ref[...] = -ADAM_LR * (m_hat / (jnp.sqrt(v_hat) + ADAM_EPS) + ADAM_WD * w_ref[...])
        nm_ref[...] = nm
        nv_ref[...] = nv

    spec = pl.BlockSpec((tr, cols), lambda i: (i, 0))
    outs = pl.pallas_call(
        body, grid=(rows // tr,), in_specs=[spec] * 4, out_specs=[spec] * 3,
        out_shape=[jax.ShapeDtypeStruct((rows, cols), F32)] * 3,
        compiler_params=_params("parallel"), name=name,
    )(w2, g2, m2, v2)
    return tuple(o.reshape(shape) for o in outs)


BIG = ("w_in", "w_q_up", "w_kv_up", "w_out", "w_ffn_up", "w_ffn_down")
REPLICATED = ("ln_mix_pre", "b_forget", "g_q_latent", "g_kv_latent", "g_fox_q", "g_fox_k", "ln_mix_post",
              "ln_ffn_pre", "b_ffn_conv", "ln_ffn_post")
WEIGHTS = ("meta_tokens", "ln_mix_pre", "w_in", "b_forget", "g_q_latent", "g_kv_latent", "w_q_up", "w_kv_up",
           "g_fox_q", "g_fox_k", "w_out", "ln_mix_post", "ln_ffn_pre", "w_ffn_up", "w_ffn_conv", "b_ffn_conv",
           "w_ffn_down", "ln_ffn_post")


def kernel(x, meta_tokens, ln_mix_pre, w_in, b_forget, g_q_latent, g_kv_latent, w_q_up, w_kv_up, g_fox_q, g_fox_k, w_out, ln_mix_post, ln_ffn_pre, w_ffn_up, w_ffn_conv, b_ffn_conv, w_ffn_down, ln_ffn_post, loss_target, m_meta_tokens, m_ln_mix_pre, m_w_in, m_b_forget, m_g_q_latent, m_g_kv_latent, m_w_q_up, m_w_kv_up, m_g_fox_q, m_g_fox_k, m_w_out, m_ln_mix_post, m_ln_ffn_pre, m_w_ffn_up, m_w_ffn_conv, m_b_ffn_conv, m_w_ffn_down, m_ln_ffn_post, v_meta_tokens, v_ln_mix_pre, v_w_in, v_b_forget, v_g_q_latent, v_g_kv_latent, v_w_q_up, v_w_kv_up, v_g_fox_q, v_g_fox_k, v_w_out, v_ln_mix_post, v_ln_ffn_pre, v_w_ffn_up, v_w_ffn_conv, v_b_ffn_conv, v_w_ffn_down, v_ln_ffn_post):
    w = dict(meta_tokens=meta_tokens, ln_mix_pre=ln_mix_pre, w_in=w_in, b_forget=b_forget, g_q_latent=g_q_latent,
             g_kv_latent=g_kv_latent, w_q_up=w_q_up, w_kv_up=w_kv_up, g_fox_q=g_fox_q, g_fox_k=g_fox_k, w_out=w_out,
             ln_mix_post=ln_mix_post, ln_ffn_pre=ln_ffn_pre, w_ffn_up=w_ffn_up, w_ffn_conv=w_ffn_conv,
             b_ffn_conv=b_ffn_conv, w_ffn_down=w_ffn_down, ln_ffn_post=ln_ffn_post)
    mom = dict(meta_tokens=m_meta_tokens, ln_mix_pre=m_ln_mix_pre, w_in=m_w_in, b_forget=m_b_forget,
               g_q_latent=m_g_q_latent, g_kv_latent=m_g_kv_latent, w_q_up=m_w_q_up, w_kv_up=m_w_kv_up,
               g_fox_q=m_g_fox_q, g_fox_k=m_g_fox_k, w_out=m_w_out, ln_mix_post=m_ln_mix_post,
               ln_ffn_pre=m_ln_ffn_pre, w_ffn_up=m_w_ffn_up, w_ffn_conv=m_w_ffn_conv, b_ffn_conv=m_b_ffn_conv,
               w_ffn_down=m_w_ffn_down, ln_ffn_post=m_ln_ffn_post)
    var = dict(meta_tokens=v_meta_tokens, ln_mix_pre=v_ln_mix_pre, w_in=v_w_in, b_forget=v_b_forget,
               g_q_latent=v_g_q_latent, g_kv_latent=v_g_kv_latent, w_q_up=v_w_q_up, w_kv_up=v_w_kv_up,
               g_fox_q=v_g_fox_q, g_fox_k=v_g_fox_k, w_out=v_w_out, ln_mix_post=v_ln_mix_post,
               ln_ffn_pre=v_ln_ffn_pre, w_ffn_up=v_w_ffn_up, w_ffn_conv=v_w_ffn_conv, b_ffn_conv=v_b_ffn_conv,
               w_ffn_down=v_w_ffn_down, ln_ffn_post=v_ln_ffn_post)
    ic = lax.axis_index("c")
    me = 4 * lax.axis_index("x") + 2 * lax.axis_index("y") + ic

    gathered = _gather([w[n].astype(BF16) for n in BIG] + [meta_tokens, w_ffn_conv], "gather_weights")
    big = [{n: gathered[k][:, l] for k, n in enumerate(BIG)} for l in range(DEPTH)]
    meta_shape, conv_shape = meta_tokens.shape, w_ffn_conv.shape
    meta_full = _cols_from_devices(gathered[len(BIG)])
    conv_full = jnp.transpose(gathered[len(BIG) + 1], (1, 2, 0, 3)).reshape(DEPTH, conv_shape[1], -1)
    small = {n: w[n] for n in REPLICATED}

    loss, grads = jax.value_and_grad(_local_loss, argnums=(0, 1, 2, 3, 4))(
        big, small, meta_full, [conv_full[l] for l in range(DEPTH)], x[0], loss_target[0])
    g_big, g_small, g_meta, g_conv, g_x = grads
    loss = lax.psum(loss, ("x", "y", "c"))

    grad = {}
    per_layer = [_reduce_scatter([g_big[l][n] for n in BIG], ic) for l in range(DEPTH)]
    for k, n in enumerate(BIG):
        grad[n] = jnp.stack([per_layer[l][k] for l in range(DEPTH)])

    small_arrays = [g_small[n] for n in REPLICATED] + [g_meta, jnp.stack(g_conv)]
    small_shapes = [a.shape for a in small_arrays]
    partials = _gather([_pack(small_arrays, F32, 16)], "gather_small_grads")[0]
    summed = _unpack(_sum_slots(partials, F32, "sum_small_grads"), small_shapes)
    for n, g in zip(REPLICATED, summed):
        grad[n] = g
    grad["meta_tokens"] = lax.dynamic_slice_in_dim(summed[-2], me * meta_shape[1], meta_shape[1], axis=1)
    grad["w_ffn_conv"] = lax.dynamic_slice_in_dim(summed[-1], me * conv_shape[2], conv_shape[2], axis=2)

    delta, new_m, new_v = {}, {}, {}
    for n in BIG:
        delta[n], new_m[n], new_v[n] = _adamw(w[n], grad[n], mom[n], var[n], "adamw_" + n)
    rest = [n for n in WEIGHTS if n not in BIG]
    rest_shapes = [w[n].shape for n in rest]
    flat = [_pack([src[n] for n in rest], F32, SUBLANES) for src in (w, grad, mom, var)]
    outs = _adamw(*flat, "adamw_small")
    for dst, buf in zip((delta, new_m, new_v), outs):
        for n, a in zip(rest, _unpack(buf, rest_shapes)):
            dst[n] = a

    return (loss, g_x[None], *[grad[n] for n in WEIGHTS], *[delta[n] for n in WEIGHTS],
            *[new_m[n] for n in WEIGHTS], *[new_v[n] for n in WEIGHTS])
```
